```python
import jax, jax.numpy as jnp
from jax import lax
import numpy as np

D_MODEL = 1024
BATCH = 32
SEQ = 256
DEPTH = 2
DEC_BATCH = 2
DEC_SEQ = 1024
PAST_LEN = 256

GRID_W = 64
HEAD_DIM = 64
A_HEADS = 4
A_KV_HEADS = 2
B_HEADS = 6
B_KV_HEADS = 2
C_HEADS = 6
A_DIM = A_HEADS * HEAD_DIM
B_DIM = B_HEADS * HEAD_DIM
C_DIM = C_HEADS * HEAD_DIM
MIX_DIM = A_DIM + B_DIM + C_DIM
WINDOW = 128
Q_BLK = 128
W_RANK = 64
A_RANK = 64
G_RANK = 128
FF_DIM = -(-8 * D_MODEL // (3 * 256)) * 256
ROPE_THETA = 10000.0
ROPE_PAIRS_AXIS = HEAD_DIM // 4
NORM_EPS = 1e-6
GN_EPS = 64e-5
NEG_INF = -1e30
IN_SPLITS = (A_DIM, A_KV_HEADS * HEAD_DIM, A_KV_HEADS * HEAD_DIM,
             B_DIM, B_KV_HEADS * HEAD_DIM, B_KV_HEADS * HEAD_DIM,
             C_DIM, C_DIM, C_DIM, W_RANK, A_RANK, G_RANK)
IN_COLS = sum(IN_SPLITS)

kernel_name = 'hybrid_diffusion_prefix_trunk_step'


def rms_norm(x, g):
    xf = x.astype(jnp.float32)
    y = xf * lax.rsqrt(jnp.mean(xf * xf, axis=-1, keepdims=True) + NORM_EPS)
    return y.astype(x.dtype) * g


def rope_2d(x):
    T = x.shape[1]
    rows = T // GRID_W
    f32 = jnp.float32
    row = jnp.repeat(jnp.arange(rows), GRID_W).astype(f32)
    col = (jnp.arange(T) % GRID_W).astype(f32)
    freqs = ROPE_THETA ** (-jnp.arange(ROPE_PAIRS_AXIS, dtype=f32) / ROPE_PAIRS_AXIS)
    ang = jnp.concatenate([row[:, None] * freqs, col[:, None] * freqs], axis=-1)[None, :, None, :]
    cos, sin = jnp.cos(ang), jnp.sin(ang)
    xf = x.astype(f32)
    half = HEAD_DIM // 2
    x1, x2 = xf[..., :half], xf[..., half:]
    return jnp.concatenate([x1 * cos - x2 * sin, x1 * sin + x2 * cos], axis=-1).astype(x.dtype)


def dense_attn(q, k, v, sink=None):
    B, T, Hq, D = q.shape
    Hkv = k.shape[2]
    G = Hq // Hkv
    nb = T // Q_BLK
    scale = D ** -0.5
    qb = jnp.moveaxis(q.reshape(B, nb, Q_BLK, Hkv, G, D), 1, 0)

    def one_block(qi):
        s = jnp.einsum('bqhgd,bshd->bhgqs', qi, k).astype(jnp.float32) * scale
        if sink is None:
            p = jax.nn.softmax(s, axis=-1)
        else:
            sl = jnp.broadcast_to(sink.astype(jnp.float32).reshape(Hkv, G)[None, :, :, None, None],
                                  s.shape[:-1] + (1,))
            p = jax.nn.softmax(jnp.concatenate([s, sl], axis=-1), axis=-1)[..., :-1]
        return jnp.einsum('bhgqs,bshd->bqhgd', p.astype(v.dtype), v)

    o = lax.map(one_block, qb)
    return jnp.moveaxis(o, 0, 1).reshape(B, T, Hq, D)


def window_attn(q, k, v, k_ctx, v_ctx, sink):
    B, T, Hq, D = q.shape
    Hkv = k.shape[2]
    G = Hq // Hkv
    nb = T // Q_BLK
    scale = D ** -0.5
    qb = q.reshape(B, nb, Q_BLK, Hkv, G, D)

    def band(t):
        tp = jnp.pad(t, ((0, 0), (Q_BLK, Q_BLK), (0, 0), (0, 0))).reshape(B, nb + 2, Q_BLK, Hkv, D)
        return jnp.concatenate([tp[:, :-2], tp[:, 1:-1], tp[:, 2:]], axis=2)

    kb, vb = band(k), band(v)
    blk = jnp.arange(nb)[:, None] * Q_BLK
    qpos = blk + jnp.arange(Q_BLK)[None, :]
    kpos = blk - Q_BLK + jnp.arange(3 * Q_BLK)[None, :]
    mask = ((jnp.abs(kpos[:, None, :] - qpos[:, :, None]) <= WINDOW)
            & (kpos[:, None, :] >= 0) & (kpos[:, None, :] < T))
    s_loc = jnp.einsum('bnqhgd,bnkhd->bnhgqk', qb, kb).astype(jnp.float32) * scale
    s_loc = jnp.where(mask[None, :, None, None], s_loc, NEG_INF)
    s_ctx = jnp.einsum('bnqhgd,bshd->bnhgqs', qb, k_ctx).astype(jnp.float32) * scale
    s_sink = jnp.broadcast_to(sink.astype(jnp.float32).reshape(Hkv, G)[None, None, :, :, None, None],
                              s_loc.shape[:-1] + (1,))
    p = jax.nn.softmax(jnp.concatenate([s_loc, s_ctx, s_sink], axis=-1), axis=-1)
    n_loc = 3 * Q_BLK
    n_ctx = k_ctx.shape[1]
    p_loc = p[..., :n_loc].astype(v.dtype)
    p_ctx = p[..., n_loc:n_loc + n_ctx].astype(v.dtype)
    o = (jnp.einsum('bnhgqk,bnkhd->bnqhgd', p_loc, vb)
         + jnp.einsum('bnhgqs,bshd->bnqhgd', p_ctx, v_ctx))
    return o.reshape(B, T, Hq, D)


def wkv_step(S, inp):
    r, w, k, v, a, b = inp
    sa = jnp.einsum('bhvk,bhk->bhv', S, a)
    S = S * w[:, :, None, :] + sa[..., None] * b[:, :, None, :] + v[..., None] * k[:, :, None, :]
    y = jnp.einsum('bhvk,bhk->bhv', S, r)
    return S, y


def rwkv_mix(r, k, v, xw, xa, xg, P, l, init_state):
    dtype = r.dtype
    f32 = jnp.float32
    B, T, _ = r.shape
    r, k, v, xw, xa, xg = (t.astype(f32) for t in (r, k, v, xw, xa, xg))

    def heads(t):
        return t.reshape(B, T, C_HEADS, HEAD_DIM)

    g = jax.nn.sigmoid(xg) @ P['c_g_up'][l].astype(f32)
    tw = jnp.tanh(xw)
    kk = heads(k * P['c_k_k'][l].astype(f32))
    kk = kk * lax.rsqrt(jnp.sum(kk * kk, axis=-1, keepdims=True) + 1e-12)
    rh, vh = heads(r), heads(v)
    r_k = P['c_r_k'][l].astype(f32)
    y_sum = jnp.zeros_like(rh)
    bonus = jnp.zeros_like(rh)
    finals = []
    for d in range(2):
        wlog = -jax.nn.softplus(-(P['c_w0'][l, d].astype(f32) + tw @ P['c_w_up'][l, d].astype(f32))) - 0.5
        decay = jnp.exp(-jnp.exp(wlog))
        a = jax.nn.sigmoid(P['c_a0'][l, d].astype(f32) + xa @ P['c_a_up'][l, d].astype(f32))
        kt = heads(k * (1.0 + (a - 1.0) * P['c_k_a'][l].astype(f32)))
        ah = heads(a)
        xs = tuple(jnp.moveaxis(t, 1, 0) for t in (rh, heads(decay), kt, vh, -kk, kk * ah))
        S_fin, y = lax.scan(wkv_step, init_state[:, d].astype(f32), xs, reverse=(d == 1))
        y_sum = y_sum + jnp.moveaxis(y, 0, 1)
        bonus = bonus + jnp.sum(rh * kt * r_k, axis=-1, keepdims=True) * vh
        finals.append(S_fin)
    mu = jnp.mean(y_sum, axis=-1, keepdims=True)
    var = jnp.mean(jnp.square(y_sum - mu), axis=-1, keepdims=True)
    yn = ((y_sum - mu) * lax.rsqrt(var + GN_EPS)).reshape(B, T, C_DIM)
    o = (yn * P['c_ln_w'][l].astype(f32) + P['c_ln_b'][l].astype(f32) + bonus.reshape(B, T, C_DIM)) * g
    return o.astype(dtype), jnp.stack(finals, axis=1)


def mixer(h, P, l, ctx_cache):
    B, T, _ = h.shape
    z = h @ P['w_in'][l]
    split_points = np.cumsum(IN_SPLITS)[:-1].tolist()
    aq, ak, av, bq, bk, bv, cr, ck, cv, cw, ca, cg = jnp.split(z, split_points, axis=-1)
    aq = aq.reshape(B, T, A_HEADS, HEAD_DIM)
    ak = ak.reshape(B, T, A_KV_HEADS, HEAD_DIM)
    av = av.reshape(B, T, A_KV_HEADS, HEAD_DIM)
    bq = rms_norm(bq.reshape(B, T, B_HEADS, HEAD_DIM), P['b_q_norm'][l])
    bk = rms_norm(bk.reshape(B, T, B_KV_HEADS, HEAD_DIM), P['b_k_norm'][l])
    bv = bv.reshape(B, T, B_KV_HEADS, HEAD_DIM)
    if ctx_cache is None:
        oa = dense_attn(aq, ak, av, P['a_sink'][l])
        ob = dense_attn(bq, bk, bv)
        init = jnp.zeros((B, 2, C_HEADS, HEAD_DIM, HEAD_DIM), jnp.float32)
        oc, st = rwkv_mix(cr, ck, cv, cw, ca, cg, P, l, init)
        cache = (ak, av, bk, bv, st)
    else:
        ka_c, va_c, kb_c, vb_c, st_c = ctx_cache
        oa = window_attn(rope_2d(aq), rope_2d(ak), av, ka_c, va_c, P['a_sink'][l])
        ob = dense_attn(rope_2d(bq),
                        jnp.concatenate([kb_c, rope_2d(bk)], axis=1),
                        jnp.concatenate([vb_c, bv], axis=1))
        oc, _ = rwkv_mix(cr, ck, cv, cw, ca, cg, P, l, st_c)
        cache = None
    o = jnp.concatenate([oa.reshape(B, T, A_DIM), ob.reshape(B, T, B_DIM), oc], axis=-1) @ P['w_out'][l]
    return o, cache


def trunk_layer(x, mod, P, l, ctx_cache):
    shift1, scale1, gate1, shift2, scale2, gate2 = jnp.split(mod, 6, axis=-1)
    h = rms_norm(x, P['norm_mix_pre'][l]) * (1.0 + scale1) + shift1
    o, cache = mixer(h, P, l, ctx_cache)
    x = x + gate1 * rms_norm(o, P['norm_mix_post'][l])
    h = rms_norm(x, P['norm_ffn_pre'][l]) * (1.0 + scale2) + shift2
    gu = h @ P['w_gu'][l]
    f = (jax.nn.silu(gu[..., :FF_DIM]) * gu[..., FF_DIM:]) @ P['w_down'][l]
    x = x + gate2 * rms_norm(f, P['norm_ffn_post'][l])
    return x, cache


def setup_inputs(seed: int = 0) -> dict:
    key = jax.random.key(seed)
    ks = iter(jax.random.split(key, 48))
    f32 = jnp.float32

    def nrm(shape, scale):
        return jax.random.normal(next(ks), shape, f32) * scale

    return {
        'x_prompt': nrm((BATCH, SEQ, D_MODEL), 1.0),
        'x_sample': nrm((DEC_BATCH, DEC_SEQ, D_MODEL), 1.0),
        'cache_a_k': nrm((DEC_BATCH, DEPTH, PAST_LEN, A_KV_HEADS, HEAD_DIM), 1.0),
        'cache_a_v': nrm((DEC_BATCH, DEPTH, PAST_LEN, A_KV_HEADS, HEAD_DIM), 1.0),
        'cache_b_k': nrm((DEC_BATCH, DEPTH, PAST_LEN, B_KV_HEADS, HEAD_DIM), 1.0),
        'cache_b_v': nrm((DEC_BATCH, DEPTH, PAST_LEN, B_KV_HEADS, HEAD_DIM), 1.0),
        'state_c': nrm((DEC_BATCH, DEPTH, 2, C_HEADS, HEAD_DIM, HEAD_DIM), 0.3),
        'c': nrm((DEC_BATCH, D_MODEL), 1.0),
        'c_ctx': nrm((D_MODEL,), 1.0),
        'w_mod': nrm((DEPTH, D_MODEL, 6 * D_MODEL), D_MODEL ** -0.5),
        'b_mod': nrm((DEPTH, 6 * D_MODEL), 0.02),
        'norm_mix_pre': 1.0 + nrm((DEPTH, D_MODEL), 0.05),
        'norm_mix_post': 1.0 + nrm((DEPTH, D_MODEL), 0.05),
        'norm_ffn_pre': 1.0 + nrm((DEPTH, D_MODEL), 0.05),
        'norm_ffn_post': 1.0 + nrm((DEPTH, D_MODEL), 0.05),
        'w_in': nrm((DEPTH, D_MODEL, IN_COLS), D_MODEL ** -0.5),
        'w_out': nrm((DEPTH, MIX_DIM, D_MODEL), MIX_DIM ** -0.5),
        'a_sink': nrm((DEPTH, A_HEADS), 0.5),
        'b_q_norm': 1.0 + nrm((DEPTH, HEAD_DIM), 0.05),
        'b_k_norm': 1.0 + nrm((DEPTH, HEAD_DIM), 0.05),
        'c_w0': jax.random.uniform(next(ks), (DEPTH, 2, C_DIM), f32, -5.0, 0.5),
        'c_w_up': nrm((DEPTH, 2, W_RANK, C_DIM), 0.1),
        'c_a0': nrm((DEPTH, 2, C_DIM), 0.1),
        'c_a_up': nrm((DEPTH, 2, A_RANK, C_DIM), 0.1),
        'c_g_up': nrm((DEPTH, G_RANK, C_DIM), G_RANK ** -0.5),
        'c_k_k': 0.85 + nrm((DEPTH, C_DIM), 0.05),
        'c_k_a': 1.0 + nrm((DEPTH, C_DIM), 0.05),
        'c_r_k': nrm((DEPTH, C_HEADS, HEAD_DIM), 0.1),
        'c_ln_w': 1.0 + nrm((DEPTH, C_DIM), 0.05),
        'c_ln_b': nrm((DEPTH, C_DIM), 0.02),
        'w_gu': nrm((DEPTH, D_MODEL, 2 * FF_DIM), D_MODEL ** -0.5),
        'w_down': nrm((DEPTH, FF_DIM, D_MODEL), FF_DIM ** -0.5),
    }


def reference(x_prompt, x_sample, cache_a_k, cache_a_v, cache_b_k, cache_b_v, state_c, c, c_ctx,
              w_mod, b_mod, norm_mix_pre, norm_mix_post, norm_ffn_pre, norm_ffn_post, w_in, w_out,
              a_sink, b_q_norm, b_k_norm, c_w0, c_w_up, c_a0, c_a_up, c_g_up, c_k_k, c_k_a, c_r_k,
              c_ln_w, c_ln_b, w_gu, w_down):
    P = {
        'norm_mix_pre': norm_mix_pre, 'norm_mix_post': norm_mix_post,
        'norm_ffn_pre': norm_ffn_pre, 'norm_ffn_post': norm_ffn_post,
        'w_in': w_in, 'w_out': w_out, 'a_sink': a_sink, 'b_q_norm': b_q_norm, 'b_k_norm': b_k_norm,
        'c_w0': c_w0, 'c_w_up': c_w_up, 'c_a0': c_a0, 'c_a_up': c_a_up, 'c_g_up': c_g_up,
        'c_k_k': c_k_k, 'c_k_a': c_k_a, 'c_r_k': c_r_k, 'c_ln_w': c_ln_w, 'c_ln_b': c_ln_b,
        'w_gu': w_gu, 'w_down': w_down,
    }
    xp = x_prompt
    ak_l, av_l, bk_l, bv_l, st_l = [], [], [], [], []
    for l in range(DEPTH):
        mod = (jax.nn.silu(c_ctx) @ w_mod[l] + b_mod[l])[None, None, :]
        xp, (ak, av, bk, bv, st) = trunk_layer(xp, mod, P, l, None)
        ak_l.append(ak)
        av_l.append(av)
        bk_l.append(bk)
        bv_l.append(bv)
        st_l.append(st.astype(x_prompt.dtype))
    xs = x_sample
    for l in range(DEPTH):
        mod = (jax.nn.silu(c) @ w_mod[l] + b_mod[l])[:, None, :]
        ctx = (cache_a_k[:, l], cache_a_v[:, l], cache_b_k[:, l], cache_b_v[:, l], state_c[:, l])
        xs, _ = trunk_layer(xs, mod, P, l, ctx)
    new_a_k = jnp.stack(ak_l, axis=1)
    new_a_v = jnp.stack(av_l, axis=1)
    new_b_k = jnp.stack(bk_l, axis=1)
    new_b_v = jnp.stack(bv_l, axis=1)
    new_state_c = jnp.stack(st_l, axis=1)
    return (xp, xs, new_a_k, new_a_v, new_b_k, new_b_v, new_state_c)
```

```python
import functools
import math

import jax
import jax.numpy as jnp
from jax import lax
from jax.experimental import pallas as pl
from jax.experimental.pallas import tpu as pltpu

D_MODEL = 1024
BATCH = 32
SEQ = 256
DEPTH = 2
DEC_BATCH = 2
DEC_SEQ = 1024
PAST_LEN = 256
GRID_W = 64
HEAD_DIM = 64
A_HEADS = 4
A_KV_HEADS = 2
B_HEADS = 6
B_KV_HEADS = 2
C_HEADS = 6
A_DIM = A_HEADS * HEAD_DIM
B_DIM = B_HEADS * HEAD_DIM
C_DIM = C_HEADS * HEAD_DIM
MIX_DIM = A_DIM + B_DIM + C_DIM
WINDOW = 128
Q_BLK = 128
W_RANK = 64
A_RANK = 64
G_RANK = 128
FF_DIM = -(-8 * D_MODEL // (3 * 256)) * 256
ROPE_THETA = 10000.0
ROPE_PAIRS_AXIS = HEAD_DIM // 4
NORM_EPS = 1e-6
GN_EPS = 64e-5
NEG_INF = -1e30

ATT_COLS = A_DIM + 2 * A_KV_HEADS * HEAD_DIM + B_DIM + 2 * B_KV_HEADS * HEAD_DIM
RWKV_COLS = 3 * C_DIM + W_RANK + A_RANK + G_RANK
IN_COLS = ATT_COLS + RWKV_COLS
ATT_OUT = A_DIM + B_DIM
MOD_COLS = 6 * D_MODEL
MOD_ROWS = 8
CHUNK = 64
QK_SCALE = HEAD_DIM ** -0.5
DECAY_RATE = math.exp(-0.5)
ROW_TILE = 512
RWKV_ROWS = 256
FF_CHUNK = FF_DIM // 2
VMEM_LIMIT = 56 * 1024 * 1024

F32 = jnp.float32
BF16 = jnp.bfloat16


def _bdot(a, b):
    return jnp.dot(a.astype(BF16), b.astype(BF16), preferred_element_type=F32)


def _bdot_nt(a, b):
    return lax.dot_general(a.astype(BF16), b.astype(BF16), (((1,), (1,)), ((), ())), preferred_element_type=F32)


def _bdot_tn(a, b):
    return lax.dot_general(a.astype(BF16), b.astype(BF16), (((0,), (0,)), ((), ())), preferred_element_type=F32)


def _split(x):
    hi = x.astype(BF16)
    lo = (x - hi.astype(F32)).astype(BF16)
    return hi, lo


def _split_dot(x, m):
    hi, lo = _split(x)
    return jnp.dot(hi, m, preferred_element_type=F32) + jnp.dot(lo, m, preferred_element_type=F32)


def _split_dot_left(m, x):
    hi, lo = _split(x)
    return jnp.dot(m, hi, preferred_element_type=F32) + jnp.dot(m, lo, preferred_element_type=F32)


def _head_ones(n):
    r = lax.shift_right_logical(lax.broadcasted_iota(jnp.int32, (n, n), 0), jnp.int32(6))
    c = lax.shift_right_logical(lax.broadcasted_iota(jnp.int32, (n, n), 1), jnp.int32(6))
    return jnp.where(r == c, 1.0, 0.0).astype(BF16)


def _rms(x, g):
    ms = jnp.mean(x * x, axis=-1, keepdims=True)
    return x * lax.rsqrt(ms + NORM_EPS) * g


def _head_rms(x, g, ones):
    ms = _split_dot(x * x, ones) * (1.0 / HEAD_DIM)
    return x * lax.rsqrt(ms + NORM_EPS) * g


def _rope(x, cos, sin):
    t = x.shape[0]
    lane = lax.broadcasted_iota(jnp.int32, (t, 128), 1)
    first = (lane & 32) == 0
    outs = []
    for i in range(x.shape[1] // 128):
        xs = x[:, i * 128:(i + 1) * 128]
        swapped = jnp.where(first, pltpu.roll(xs, 96, axis=1), pltpu.roll(xs, 32, axis=1))
        outs.append(xs * cos + swapped * sin)
    return outs[0] if len(outs) == 1 else jnp.concatenate(outs, axis=1)


def _softmax_pv(scores, values, sink=None):
    m = scores[0].max(axis=-1, keepdims=True)
    for s in scores[1:]:
        m = jnp.maximum(m, s.max(axis=-1, keepdims=True))
    if sink is not None:
        m = jnp.maximum(m, sink)
    denom = None
    acc = None
    for s, v in zip(scores, values):
        p = jnp.exp(s - m)
        d = p.sum(axis=-1, keepdims=True)
        o = jnp.dot(p.astype(BF16), v, preferred_element_type=F32)
        denom = d if denom is None else denom + d
        acc = o if acc is None else acc + o
    if sink is not None:
        denom = denom + jnp.exp(sink - m)
    return acc / denom


def _mod_body(c_ref, w_ref, b_ref, o_ref):
    cv = c_ref[...]
    s = cv * jax.nn.sigmoid(cv)
    o_ref[0] = _bdot(s, w_ref[0]) + b_ref[0]


def _mod_call(cvec, w_mod, b_mod):
    bn = MOD_COLS // 4
    return pl.pallas_call(
        _mod_body,
        out_shape=jax.ShapeDtypeStruct((DEPTH, MOD_ROWS, MOD_COLS), F32),
        grid=(DEPTH, MOD_COLS // bn),
        in_specs=[
            pl.BlockSpec((MOD_ROWS, D_MODEL), lambda l, j: (0, 0)),
            pl.BlockSpec((1, D_MODEL, bn), lambda l, j: (l, 0, j)),
            pl.BlockSpec((1, 1, bn), lambda l, j: (l, 0, j)),
        ],
        out_specs=pl.BlockSpec((1, MOD_ROWS, bn), lambda l, j: (l, 0, j)),
        compiler_params=pltpu.CompilerParams(vmem_limit_bytes=VMEM_LIMIT),
        name="mod_vectors",
    )(cvec, w_mod, b_mod.reshape(DEPTH, 1, MOD_COLS))


def _proj_in_body(x_ref, mod_ref, g_ref, w_ref, za_ref, zc_ref):
    mod = mod_ref[0]
    shift1 = mod[:, 0:D_MODEL]
    scale1 = mod[:, D_MODEL:2 * D_MODEL]
    h = (_rms(x_ref[...], g_ref[0:1]) * (1.0 + scale1) + shift1).astype(BF16)
    za_ref[...] = jnp.dot(h, w_ref[:, 0:ATT_COLS], preferred_element_type=F32)
    zc_ref[...] = jnp.dot(h, w_ref[:, ATT_COLS:IN_COLS], preferred_element_type=F32)


def _mod_spec(n_mod, rows_per_mod):
    if n_mod == 1:
        return pl.BlockSpec((1, 1, MOD_COLS), lambda i: (0, 0, 0))
    tiles = rows_per_mod // ROW_TILE
    return pl.BlockSpec((1, 1, MOD_COLS), lambda i: (i // tiles, 0, 0))


def _const_spec(shape):
    zeros = (0,) * len(shape)
    return pl.BlockSpec(shape, lambda *_: zeros, pipeline_mode=pl.Buffered(1))


def _proj_in_call(x, mod, gvec, w_in_b, rows_per_mod):
    n = x.shape[0]
    return pl.pallas_call(
        _proj_in_body,
        out_shape=(jax.ShapeDtypeStruct((n, ATT_COLS), F32), jax.ShapeDtypeStruct((n, RWKV_COLS), F32)),
        grid=(n // ROW_TILE,),
        in_specs=[
            pl.BlockSpec((ROW_TILE, D_MODEL), lambda i: (i, 0)),
            _mod_spec(mod.shape[0], rows_per_mod),
            _const_spec((8, D_MODEL)),
            _const_spec((D_MODEL, IN_COLS)),
        ],
        out_specs=(pl.BlockSpec((ROW_TILE, ATT_COLS), lambda i: (i, 0)),
                   pl.BlockSpec((ROW_TILE, RWKV_COLS), lambda i: (i, 0))),
        compiler_params=pltpu.CompilerParams(vmem_limit_bytes=VMEM_LIMIT),
        name="proj_in",
    )(x, mod, gvec, w_in_b)


def _attn_prompt_body(sink_ref, za_ref, gq_ref, gk_ref, o_ref, kv_ref):
    aq = (za_ref[:, 0:A_DIM] * QK_SCALE).astype(BF16)
    ak = za_ref[:, 256:384]
    av = za_ref[:, 384:512]
    akb = ak.astype(BF16)
    avb = av.astype(BF16)
    outs = []
    for h in range(A_HEADS):
        kv = h // (A_HEADS // A_KV_HEADS)
        s = _bdot_nt(aq[:, h * 64:(h + 1) * 64], akb[:, kv * 64:(kv + 1) * 64])
        outs.append(_softmax_pv([s], [avb[:, kv * 64:(kv + 1) * 64]], sink_ref[h]))
    bq = _head_rms(za_ref[:, 512:896], gq_ref[...], _head_ones(B_DIM))
    bk = _head_rms(za_ref[:, 896:1024], gk_ref[...], _head_ones(128))
    bv = za_ref[:, 1024:1152]
    bqb = (bq * QK_SCALE).astype(BF16)
    bkb = bk.astype(BF16)
    bvb = bv.astype(BF16)
    for h in range(B_HEADS):
        kv = h // (B_HEADS // B_KV_HEADS)
        s = _bdot_nt(bqb[:, h * 64:(h + 1) * 64], bkb[:, kv * 64:(kv + 1) * 64])
        outs.append(_softmax_pv([s], [bvb[:, kv * 64:(kv + 1) * 64]]))
    o_ref[...] = jnp.concatenate(outs, axis=1)
    kv_ref[...] = jnp.concatenate([ak, av, bk, bv], axis=1)


def _attn_prompt_call(za, sink, gq, gk):
    nb = za.shape[0] // SEQ
    return pl.pallas_call(
        _attn_prompt_body,
        out_shape=(jax.ShapeDtypeStruct((nb * SEQ, ATT_OUT), F32), jax.ShapeDtypeStruct((nb * SEQ, 512), F32)),
        grid=(nb,),
        in_specs=[
            pl.BlockSpec(memory_space=pltpu.SMEM),
            pl.BlockSpec((SEQ, ATT_COLS), lambda b: (b, 0)),
            pl.BlockSpec((1, B_DIM), lambda b: (0, 0)),
            pl.BlockSpec((1, 128), lambda b: (0, 0)),
        ],
        out_specs=(pl.BlockSpec((SEQ, ATT_OUT), lambda b: (b, 0)), pl.BlockSpec((SEQ, 512), lambda b: (b, 0))),
        compiler_params=pltpu.CompilerParams(vmem_limit_bytes=VMEM_LIMIT),
        name="attn_prompt",
    )(sink, za, gq, gk)


def _attn_sample_body(sink_ref, za_ref, cak_ref, cav_ref, cbk_ref, cbv_ref, cos_ref, sin_ref, gq_ref, gk_ref,
                      o_ref, qa_s, ka_s, va_s, qb_s, kb_s, vb_s):
    t = DEC_SEQ
    cos = cos_ref[...]
    sin = sin_ref[...]
    qa_s[...] = (_rope(za_ref[:, 0:A_DIM], cos, sin) * QK_SCALE).astype(BF16)
    zpad = jnp.zeros((Q_BLK, 128), BF16)
    ka_s[0:Q_BLK] = zpad
    ka_s[Q_BLK + t:] = zpad
    va_s[0:Q_BLK] = zpad
    va_s[Q_BLK + t:] = zpad
    ka_s[Q_BLK:Q_BLK + t] = _rope(za_ref[:, 256:384], cos, sin).astype(BF16)
    va_s[Q_BLK:Q_BLK + t] = za_ref[:, 384:512].astype(BF16)
    bq = _head_rms(za_ref[:, 512:896], gq_ref[...], _head_ones(B_DIM))
    qb_s[...] = (_rope(bq, cos, sin) * QK_SCALE).astype(BF16)
    bk = _head_rms(za_ref[:, 896:1024], gk_ref[...], _head_ones(128))
    kb_s[0:PAST_LEN] = cbk_ref[0].astype(BF16)
    kb_s[PAST_LEN:] = _rope(bk, cos, sin).astype(BF16)
    vb_s[0:PAST_LEN] = cbv_ref[0].astype(BF16)
    vb_s[PAST_LEN:] = za_ref[:, 1024:1152].astype(BF16)

    kca = cak_ref[0].astype(BF16)
    vca = cav_ref[0].astype(BF16)

    def a_block(n, carry):
        r0 = pl.multiple_of(n.astype(jnp.int32) * Q_BLK, Q_BLK)
        q = qa_s[pl.ds(r0, Q_BLK), :]
        kl = ka_s[pl.ds(r0, 3 * Q_BLK), :]
        vl = va_s[pl.ds(r0, 3 * Q_BLK), :]
        qpos = r0 + lax.broadcasted_iota(jnp.int32, (Q_BLK, 3 * Q_BLK), 0)
        kpos = r0 - Q_BLK + lax.broadcasted_iota(jnp.int32, (Q_BLK, 3 * Q_BLK), 1)
        mask = (jnp.abs(kpos - qpos) <= WINDOW) & (kpos >= 0) & (kpos < t)
        outs = []
        for h in range(A_HEADS):
            kv = h // (A_HEADS // A_KV_HEADS)
            hs = slice(kv * 64, (kv + 1) * 64)
            qh = q[:, h * 64:(h + 1) * 64]
            s_loc = jnp.where(mask, _bdot_nt(qh, kl[:, hs]), NEG_INF)
            s_ctx = _bdot_nt(qh, kca[:, hs])
            outs.append(_softmax_pv([s_loc, s_ctx], [vl[:, hs], vca[:, hs]], sink_ref[h]))
        o_ref[pl.ds(r0, Q_BLK), 0:A_DIM] = jnp.concatenate(outs, axis=1)
        return carry

    lax.fori_loop(0, t // Q_BLK, a_block, 0)

    qrows = 256

    def b_block(n, carry):
        r0 = pl.multiple_of(n.astype(jnp.int32) * qrows, qrows)
        q = qb_s[pl.ds(r0, qrows), :]
        outs = []
        for h in range(B_HEADS):
            kv = h // (B_HEADS // B_KV_HEADS)
            hs = slice(kv * 64, (kv + 1) * 64)
            s = _bdot_nt(q[:, h * 64:(h + 1) * 64], kb_s[:, hs])
            outs.append(_softmax_pv([s], [vb_s[:, hs]]))
        o_ref[pl.ds(r0, qrows), A_DIM:ATT_OUT] = jnp.concatenate(outs, axis=1)
        return carry

    lax.fori_loop(0, t // qrows, b_block, 0)


def _attn_sample_call(za, sink, caches, cos, sin, gq, gk):
    nb = za.shape[0] // DEC_SEQ
    t = DEC_SEQ
    cache_spec = pl.BlockSpec((1, PAST_LEN, 128), lambda b: (b, 0, 0))
    return pl.pallas_call(
        _attn_sample_body,
        out_shape=jax.ShapeDtypeStruct((nb * t, ATT_OUT), F32),
        grid=(nb,),
        in_specs=[
            pl.BlockSpec(memory_space=pltpu.SMEM),
            pl.BlockSpec((t, ATT_COLS), lambda b: (b, 0)),
            cache_spec, cache_spec, cache_spec, cache_spec,
            pl.BlockSpec((t, 128), lambda b: (0, 0)),
            pl.BlockSpec((t, 128), lambda b: (0, 0)),
            pl.BlockSpec((1, B_DIM), lambda b: (0, 0)),
            pl.BlockSpec((1, 128), lambda b: (0, 0)),
        ],
        out_specs=pl.BlockSpec((t, ATT_OUT), lambda b: (b, 0)),
        scratch_shapes=[
            pltpu.VMEM((t, A_DIM), BF16),
            pltpu.VMEM((t + 2 * Q_BLK, 128), BF16),
            pltpu.VMEM((t + 2 * Q_BLK, 128), BF16),
            pltpu.VMEM((t, B_DIM), BF16),
            pltpu.VMEM((t + PAST_LEN, 128), BF16),
            pltpu.VMEM((t + PAST_LEN, 128), BF16),
        ],
        compiler_params=pltpu.CompilerParams(vmem_limit_bytes=VMEM_LIMIT),
        name="attn_sample",
    )(sink, za, *caches, cos, sin, gq, gk)


def _rwkv_body(*refs, t, has_init, want_state):
    refs = list(refs)
    zc_ref, cv_ref, wup_ref, aup_ref, gup_ref = refs[:5]
    pos = 5
    st0_ref = None
    if has_init:
        st0_ref = refs[pos]
        pos += 1
    oc_ref = refs[pos]
    pos += 1
    stout_ref = None
    if want_state:
        stout_ref = refs[pos]
        pos += 1
    aa_s, ld_s, b_s, kt_s, bon_s, y_s, st_s = refs[pos:]

    nchunks = t // CHUNK
    ones = _head_ones(C_DIM)
    k_k = cv_ref[0:1]
    k_a = cv_ref[1:2]
    r_k = cv_ref[2:3]
    ln_w = cv_ref[3:4]
    ln_b = cv_ref[4:5]
    w0 = (cv_ref[5:6], cv_ref[6:7])
    a0 = (cv_ref[7:8], cv_ref[8:9])
    wup = (wup_ref[0].astype(BF16), wup_ref[1].astype(BF16))
    aup = (aup_ref[0].astype(BF16), aup_ref[1].astype(BF16))

    for rb in range(t // RWKV_ROWS):
        rows = slice(rb * RWKV_ROWS, (rb + 1) * RWKV_ROWS)
        zr = zc_ref[rows, 0:384]
        zk = zc_ref[rows, 384:768]
        zv = zc_ref[rows, 768:1152]
        tw = jnp.tanh(zc_ref[rows, 1152:1216])
        xa = zc_ref[rows, 1216:1280]
        kkr = zk * k_k
        kk = kkr * lax.rsqrt(_split_dot(kkr * kkr, ones) + 1e-12)
        aa_s[rows] = -kk
        bon = None
        for d in range(2):
            wl = w0[d] + _bdot(tw, wup[d])
            ld_s[d, rows] = (-DECAY_RATE) * jax.nn.sigmoid(wl)
            a = jax.nn.sigmoid(a0[d] + _bdot(xa, aup[d]))
            kt = zk * (1.0 + (a - 1.0) * k_a)
            kt_s[d, rows] = kt
            b_s[d, rows] = kk * a
            bd = _split_dot(zr * kt * r_k, ones) * zv
            bon = bd if bon is None else bon + bd
        bon_s[rows] = bon
        y_s[rows] = jnp.zeros((RWKV_ROWS, C_DIM), F32)

    ir = lax.broadcasted_iota(jnp.int32, (CHUNK, CHUNK), 0)
    ic = lax.broadcasted_iota(jnp.int32, (CHUNK, CHUNK), 1)
    eye = jnp.where(ir == ic, 1.0, 0.0).astype(F32)
    pair_masks = []
    for j in range(CHUNK.bit_length() - 1):
        same_big = lax.shift_right_logical(ir, jnp.int32(j + 1)) == lax.shift_right_logical(ic, jnp.int32(j + 1))
        same_small = lax.shift_right_logical(ir, jnp.int32(j)) == lax.shift_right_logical(ic, jnp.int32(j))
        pair_masks.append(same_big & jnp.logical_not(same_small))

    for d in range(2):
        rev = d == 1
        incl = (ic >= ir) if rev else (ic <= ir)
        strict = (ic > ir) if rev else (ic < ir)
        tri = jnp.where(incl, 1.0, 0.0).astype(BF16)
        for h in range(C_HEADS):
            st_s[h] = st0_ref[0, d, h] if has_init else jnp.zeros((HEAD_DIM, HEAD_DIM), F32)

        def chunk(c, carry, d=d, rev=rev, incl=incl, strict=strict, tri=tri):
            c = c.astype(jnp.int32)
            ci = (nchunks - 1 - c) if rev else c
            r0 = pl.multiple_of(ci * CHUNK, CHUNK)
            rows = pl.ds(r0, CHUNK)
            ld = ld_s[d, rows]
            cinc = _split_dot_left(tri, ld)
            cexc = cinc - ld
            ctot = cinc[0:1] if rev else cinc[CHUNK - 1:CHUNK]
            e_ninc = jnp.exp(-cinc)
            e_rem = jnp.exp(ctot - cinc)
            e_tot = jnp.exp(ctot)
            bb = b_s[d, rows]
            kt = kt_s[d, rows]
            at = (aa_s[rows] * jnp.exp(cexc)).astype(BF16)
            rt = (zc_ref[rows, 0:384] * jnp.exp(cinc)).astype(BF16)
            bt = (bb * e_ninc).astype(BF16)
            ktt = (kt * e_ninc).astype(BF16)
            bh = (bb * e_rem).astype(BF16)
            kh = (kt * e_rem).astype(BF16)
            vm = zc_ref[rows, 768:1152].astype(BF16)
            ys = []
            for h in range(C_HEADS):
                hs = slice(h * 64, (h + 1) * 64)
                ar = jnp.concatenate([at[:, hs], rt[:, hs]], axis=0)
                sb = _bdot_nt(ar, bt[:, hs])
                sk = _bdot_nt(ar, ktt[:, hs])
                lmat = jnp.where(strict, sb[0:CHUNK], 0.0)
                mak = jnp.where(strict, sk[0:CHUNK], 0.0)
                nrb = jnp.where(incl, sb[CHUNK:], 0.0)
                nrk = jnp.where(incl, sk[CHUNK:], 0.0)
                tm = eye + jnp.where(pair_masks[0], lmat, 0.0)
                for pm in pair_masks[1:]:
                    tm = tm + _bdot(_bdot(tm, jnp.where(pm, lmat, 0.0)), tm)
                ta = _bdot(tm, at[:, hs])
                uin = _bdot(tm, _bdot(mak, vm[:, hs]))
                s0 = st_s[h]
                x = _bdot_nt(jnp.concatenate([ta.astype(BF16), rt[:, hs]], axis=0), s0)
                u = x[0:CHUNK] + uin
                ys.append(x[CHUNK:] + _bdot(nrb, u) + _bdot(nrk, vm[:, hs]))
                st_s[h] = s0 * e_tot[:, hs] + _bdot_tn(
                    jnp.concatenate([u.astype(BF16), vm[:, hs]], axis=0),
                    jnp.concatenate([bh[:, hs], kh[:, hs]], axis=0))
            y_s[rows] = y_s[rows] + jnp.concatenate(ys, axis=1)
            return carry

        lax.fori_loop(0, nchunks, chunk, 0)
        if want_state:
            for h in range(C_HEADS):
                stout_ref[0, d, h] = st_s[h]

    gup = gup_ref[...].astype(BF16)
    for rb in range(t // RWKV_ROWS):
        rows = slice(rb * RWKV_ROWS, (rb + 1) * RWKV_ROWS)
        y = y_s[rows]
        mu = _split_dot(y, ones) * (1.0 / HEAD_DIM)
        yc = y - mu
        var = _split_dot(yc * yc, ones) * (1.0 / HEAD_DIM)
        yn = yc * lax.rsqrt(var + GN_EPS)
        g = _bdot(jax.nn.sigmoid(zc_ref[rows, 1280:1408]), gup)
        oc_ref[rows] = (yn * ln_w + ln_b + bon_s[rows]) * g


def _rwkv_call(zc, cvec, wup, aup, gup, t, init_state, want_state):
    nb = zc.shape[0] // t
    has_init = init_state is not None
    state_spec = pl.BlockSpec((1, 2, C_HEADS, HEAD_DIM, HEAD_DIM), lambda b: (b, 0, 0, 0, 0))
    in_specs = [
        pl.BlockSpec((t, RWKV_COLS), lambda b: (b, 0)),
        pl.BlockSpec((16, C_DIM), lambda b: (0, 0)),
        pl.BlockSpec((2, W_RANK, C_DIM), lambda b: (0, 0, 0)),
        pl.BlockSpec((2, A_RANK, C_DIM), lambda b: (0, 0, 0)),
        pl.BlockSpec((G_RANK, C_DIM), lambda b: (0, 0)),
    ]
    args = [zc, cvec, wup, aup, gup]
    if has_init:
        in_specs.append(state_spec)
        args.append(init_state)
    out_shape = [jax.ShapeDtypeStruct((nb * t, C_DIM), F32)]
    out_specs = [pl.BlockSpec((t, C_DIM), lambda b: (b, 0))]
    if want_state:
        out_shape.append(jax.ShapeDtypeStruct((nb, 2, C_HEADS, HEAD_DIM, HEAD_DIM), F32))
        out_specs.append(state_spec)
    res = pl.pallas_call(
        functools.partial(_rwkv_body, t=t, has_init=has_init, want_state=want_state),
        out_shape=tuple(out_shape),
        grid=(nb,),
        in_specs=in_specs,
        out_specs=tuple(out_specs),
        scratch_shapes=[
            pltpu.VMEM((t, C_DIM), F32),
            pltpu.VMEM((2, t, C_DIM), F32),
            pltpu.VMEM((2, t, C_DIM), F32),
            pltpu.VMEM((2, t, C_DIM), F32),
            pltpu.VMEM((t, C_DIM), F32),
            pltpu.VMEM((t, C_DIM), F32),
            pltpu.VMEM((C_HEADS, HEAD_DIM, HEAD_DIM), F32),
        ],
        compiler_params=pltpu.CompilerParams(vmem_limit_bytes=VMEM_LIMIT),
        name="rwkv_prompt" if want_state else "rwkv_sample",
    )(*args)
    return res


def _post_body(oa_ref, oc_ref, x_ref, mod_ref, g_ref, wout_ref, wgu_ref, wdn_ref, xo_ref):
    mod = mod_ref[0]
    gate1 = mod[:, 2 * D_MODEL:3 * D_MODEL]
    shift2 = mod[:, 3 * D_MODEL:4 * D_MODEL]
    scale2 = mod[:, 4 * D_MODEL:5 * D_MODEL]
    gate2 = mod[:, 5 * D_MODEL:6 * D_MODEL]
    o = (jnp.dot(oa_ref[...].astype(BF16), wout_ref[0:ATT_OUT], preferred_element_type=F32)
         + jnp.dot(oc_ref[...].astype(BF16), wout_ref[ATT_OUT:MIX_DIM], preferred_element_type=F32))
    x1 = x_ref[...] + gate1 * _rms(o, g_ref[1:2])
    h2 = (_rms(x1, g_ref[2:3]) * (1.0 + scale2) + shift2).astype(BF16)
    acc = None
    for j in range(FF_DIM // FF_CHUNK):
        lo = j * FF_CHUNK
        g = jnp.dot(h2, wgu_ref[:, lo:lo + FF_CHUNK], preferred_element_type=F32)
        u = jnp.dot(h2, wgu_ref[:, FF_DIM + lo:FF_DIM + lo + FF_CHUNK], preferred_element_type=F32)
        act = (g * jax.nn.sigmoid(g) * u).astype(BF16)
        part = jnp.dot(act, wdn_ref[lo:lo + FF_CHUNK, :], preferred_element_type=F32)
        acc = part if acc is None else acc + part
    xo_ref[...] = x1 + gate2 * _rms(acc, g_ref[3:4])


def _post_call(oatt, oc, x, mod, gvec, wout_b, wgu_b, wdn_b, rows_per_mod):
    n = x.shape[0]
    return pl.pallas_call(
        _post_body,
        out_shape=jax.ShapeDtypeStruct((n, D_MODEL), F32),
        grid=(n // ROW_TILE,),
        in_specs=[
            pl.BlockSpec((ROW_TILE, ATT_OUT), lambda i: (i, 0)),
            pl.BlockSpec((ROW_TILE, C_DIM), lambda i: (i, 0)),
            pl.BlockSpec((ROW_TILE, D_MODEL), lambda i: (i, 0)),
            _mod_spec(mod.shape[0], rows_per_mod),
            _const_spec((8, D_MODEL)),
            _const_spec((MIX_DIM, D_MODEL)),
            _const_spec((D_MODEL, 2 * FF_DIM)),
            _const_spec((FF_DIM, D_MODEL)),
        ],
        out_specs=pl.BlockSpec((ROW_TILE, D_MODEL), lambda i: (i, 0)),
        compiler_params=pltpu.CompilerParams(vmem_limit_bytes=VMEM_LIMIT),
        name="post_ffn",
    )(oatt, oc, x, mod, gvec, wout_b, wgu_b, wdn_b)


def _rope_tables(t):
    pos = jnp.arange(t)
    row = (pos // GRID_W).astype(F32)
    col = (pos % GRID_W).astype(F32)
    freqs = ROPE_THETA ** (-jnp.arange(ROPE_PAIRS_AXIS, dtype=F32) / ROPE_PAIRS_AXIS)
    ang = jnp.concatenate([row[:, None] * freqs, col[:, None] * freqs], axis=-1)
    cos = jnp.tile(jnp.cos(ang), (1, 4))
    sin = jnp.sin(ang)
    sin = jnp.tile(jnp.concatenate([-sin, sin], axis=-1), (1, 2))
    return cos, sin


def kernel(x_prompt, x_sample, cache_a_k, cache_a_v, cache_b_k, cache_b_v, state_c, c, c_ctx, w_mod, b_mod, norm_mix_pre, norm_mix_post, norm_ffn_pre, norm_ffn_post, w_in, w_out, a_sink, b_q_norm, b_k_norm, c_w0, c_w_up, c_a0, c_a_up, c_g_up, c_k_k, c_k_a, c_r_k, c_ln_w, c_ln_b, w_gu, w_down):
    cvec = jnp.zeros((MOD_ROWS, D_MODEL), F32).at[0].set(c_ctx).at[1:1 + DEC_BATCH].set(c)
    mods = _mod_call(cvec, w_mod, b_mod)
    cos, sin = _rope_tables(DEC_SEQ)

    xp = x_prompt.reshape(BATCH * SEQ, D_MODEL)
    xs = x_sample.reshape(DEC_BATCH * DEC_SEQ, D_MODEL)
    kv_l, st_l = [], []
    for l in range(DEPTH):
        w_in_b = w_in[l].astype(BF16)
        wout_b = w_out[l].astype(BF16)
        wgu_b = w_gu[l].astype(BF16)
        wdn_b = w_down[l].astype(BF16)
        gvec = jnp.zeros((8, D_MODEL), F32)
        gvec = gvec.at[0].set(norm_mix_pre[l]).at[1].set(norm_mix_post[l])
        gvec = gvec.at[2].set(norm_ffn_pre[l]).at[3].set(norm_ffn_post[l])
        rvec = jnp.zeros((16, C_DIM), F32)
        rvec = rvec.at[0].set(c_k_k[l]).at[1].set(c_k_a[l]).at[2].set(c_r_k[l].reshape(C_DIM))
        rvec = rvec.at[3].set(c_ln_w[l]).at[4].set(c_ln_b[l])
        rvec = rvec.at[5:7].set(c_w0[l]).at[7:9].set(c_a0[l])
        gq = jnp.tile(b_q_norm[l], B_HEADS)[None, :]
        gk = jnp.tile(b_k_norm[l], B_KV_HEADS)[None, :]
        sink = a_sink[l]
        mod_p = mods[l, 0:1].reshape(1, 1, MOD_COLS)
        mod_s = mods[l, 1:1 + DEC_BATCH].reshape(DEC_BATCH, 1, MOD_COLS)

        za, zc = _proj_in_call(xp, mod_p, gvec, w_in_b, BATCH * SEQ)
        oatt, kv = _attn_prompt_call(za, sink, gq, gk)
        oc, st = _rwkv_call(zc, rvec, c_w_up[l], c_a_up[l], c_g_up[l], SEQ, None, True)
        xp = _post_call(oatt, oc, xp, mod_p, gvec, wout_b, wgu_b, wdn_b, BATCH * SEQ)
        kv_l.append(kv)
        st_l.append(st)

        caches = tuple(t[:, l].reshape(DEC_BATCH, PAST_LEN, 128)
                       for t in (cache_a_k, cache_a_v, cache_b_k, cache_b_v))
        za, zc = _proj_in_call(xs, mod_s, gvec, w_in_b, DEC_SEQ)
        oatt = _attn_sample_call(za, sink, caches, cos, sin, gq, gk)
        (oc,) = _rwkv_call(zc, rvec, c_w_up[l], c_a_up[l], c_g_up[l], DEC_SEQ, state_c[:, l], False)
        xs = _post_call(oatt, oc, xs, mod_s, gvec, wout_b, wgu_b, wdn_b, DEC_SEQ)

    def cache_out(lo):
        return jnp.stack([kv[:, lo:lo + 128].reshape(BATCH, SEQ, 2, HEAD_DIM) for kv in kv_l], axis=1)

    return (xp.reshape(BATCH, SEQ, D_MODEL), xs.reshape(DEC_BATCH, DEC_SEQ, D_MODEL),
            cache_out(0), cache_out(128), cache_out(256), cache_out(384), jnp.stack(st_l, axis=1))
```

```python
import functools
import math

import jax
import jax.numpy as jnp
from jax import lax
from jax.experimental import pallas as pl
from jax.experimental.pallas import tpu as pltpu

D_MODEL = 1024
BATCH = 32
SEQ = 256
DEPTH = 2
DEC_BATCH = 2
DEC_SEQ = 1024
PAST_LEN = 256
GRID_W = 64
HEAD_DIM = 64
A_HEADS = 4
A_KV_HEADS = 2
B_HEADS = 6
B_KV_HEADS = 2
C_HEADS = 6
A_DIM = A_HEADS * HEAD_DIM
B_DIM = B_HEADS * HEAD_DIM
C_DIM = C_HEADS * HEAD_DIM
MIX_DIM = A_DIM + B_DIM + C_DIM
WINDOW = 128
Q_BLK = 128
W_RANK = 64
A_RANK = 64
G_RANK = 128
FF_DIM = -(-8 * D_MODEL // (3 * 256)) * 256
ROPE_THETA = 10000.0
ROPE_PAIRS_AXIS = HEAD_DIM // 4
NORM_EPS = 1e-6
GN_EPS = 64e-5
NEG_INF = -1e30

ATT_COLS = A_DIM + 2 * A_KV_HEADS * HEAD_DIM + B_DIM + 2 * B_KV_HEADS * HEAD_DIM
RWKV_COLS = 3 * C_DIM + W_RANK + A_RANK + G_RANK
IN_COLS = ATT_COLS + RWKV_COLS
ATT_OUT = A_DIM + B_DIM
MOD_COLS = 6 * D_MODEL
MOD_ROWS = 8
CHUNK = 64
QK_SCALE = HEAD_DIM ** -0.5
DECAY_RATE = math.exp(-0.5)
ROW_TILE = 512
RWKV_ROWS = 256
FF_CHUNK = FF_DIM // 2
VMEM_LIMIT = 56 * 1024 * 1024

F32 = jnp.float32
BF16 = jnp.bfloat16


def _bdot(a, b):
    return jnp.dot(a.astype(BF16), b.astype(BF16), preferred_element_type=F32)


def _bdot_nt(a, b):
    return lax.dot_general(a.astype(BF16), b.astype(BF16), (((1,), (1,)), ((), ())), preferred_element_type=F32)


def _bdot_tn(a, b):
    return lax.dot_general(a.astype(BF16), b.astype(BF16), (((0,), (0,)), ((), ())), preferred_element_type=F32)


def _split(x):
    hi = x.astype(BF16)
    lo = (x - hi.astype(F32)).astype(BF16)
    return hi, lo


def _split_dot(x, m):
    hi, lo = _split(x)
    return jnp.dot(hi, m, preferred_element_type=F32) + jnp.dot(lo, m, preferred_element_type=F32)


def _split_dot_left(m, x):
    hi, lo = _split(x)
    return jnp.dot(m, hi, preferred_element_type=F32) + jnp.dot(m, lo, preferred_element_type=F32)


def _head_ones(n):
    r = lax.shift_right_logical(lax.broadcasted_iota(jnp.int32, (n, n), 0), jnp.int32(6))
    c = lax.shift_right_logical(lax.broadcasted_iota(jnp.int32, (n, n), 1), jnp.int32(6))
    return jnp.where(r == c, 1.0, 0.0).astype(BF16)


def _rms(x, g):
    ms = jnp.mean(x * x, axis=-1, keepdims=True)
    return x * lax.rsqrt(ms + NORM_EPS) * g


def _head_rms(x, g, ones):
    ms = _split_dot(x * x, ones) * (1.0 / HEAD_DIM)
    return x * lax.rsqrt(ms + NORM_EPS) * g


def _rope(x, cos, sin):
    t = x.shape[0]
    lane = lax.broadcasted_iota(jnp.int32, (t, 128), 1)
    first = (lane & 32) == 0
    outs = []
    for i in range(x.shape[1] // 128):
        xs = x[:, i * 128:(i + 1) * 128]
        swapped = jnp.where(first, pltpu.roll(xs, 96, axis=1), pltpu.roll(xs, 32, axis=1))
        outs.append(xs * cos + swapped * sin)
    return outs[0] if len(outs) == 1 else jnp.concatenate(outs, axis=1)


def _softmax_pv(scores, values, sink=None):
    m = scores[0].max(axis=-1, keepdims=True)
    for s in scores[1:]:
        m = jnp.maximum(m, s.max(axis=-1, keepdims=True))
    if sink is not None:
        m = jnp.maximum(m, sink)
    denom = None
    acc = None
    for s, v in zip(scores, values):
        p = jnp.exp(s - m)
        d = p.sum(axis=-1, keepdims=True)
        o = jnp.dot(p.astype(BF16), v, preferred_element_type=F32)
        denom = d if denom is None else denom + d
        acc = o if acc is None else acc + o
    if sink is not None:
        denom = denom + jnp.exp(sink - m)
    return acc / denom


def _mod_body(c_ref, w_ref, b_ref, o_ref):
    cv = c_ref[...]
    s = cv * jax.nn.sigmoid(cv)
    o_ref[0] = _bdot(s, w_ref[0]) + b_ref[0]


def _mod_call(cvec, w_mod, b_mod):
    bn = MOD_COLS // 4
    return pl.pallas_call(
        _mod_body,
        out_shape=jax.ShapeDtypeStruct((DEPTH, MOD_ROWS, MOD_COLS), F32),
        grid=(DEPTH, MOD_COLS // bn),
        in_specs=[
            pl.BlockSpec((MOD_ROWS, D_MODEL), lambda l, j: (0, 0)),
            pl.BlockSpec((1, D_MODEL, bn), lambda l, j: (l, 0, j)),
            pl.BlockSpec((1, 1, bn), lambda l, j: (l, 0, j)),
        ],
        out_specs=pl.BlockSpec((1, MOD_ROWS, bn), lambda l, j: (l, 0, j)),
        compiler_params=pltpu.CompilerParams(vmem_limit_bytes=VMEM_LIMIT),
        name="mod_vectors",
    )(cvec, w_mod, b_mod.reshape(DEPTH, 1, MOD_COLS))


def _proj_in_body(x_ref, mod_ref, g_ref, w_ref, za_ref, zc_ref):
    mod = mod_ref[0]
    shift1 = mod[:, 0:D_MODEL]
    scale1 = mod[:, D_MODEL:2 * D_MODEL]
    h = (_rms(x_ref[...], g_ref[0:1]) * (1.0 + scale1) + shift1).astype(BF16)
    za_ref[...] = jnp.dot(h, w_ref[:, 0:ATT_COLS], preferred_element_type=F32)
    zc_ref[...] = jnp.dot(h, w_ref[:, ATT_COLS:IN_COLS], preferred_element_type=F32)


def _mod_spec(n_mod, rows_per_mod):
    if n_mod == 1:
        return pl.BlockSpec((1, 1, MOD_COLS), lambda i: (0, 0, 0))
    tiles = rows_per_mod // ROW_TILE
    return pl.BlockSpec((1, 1, MOD_COLS), lambda i: (i // tiles, 0, 0))


def _const_spec(shape):
    zeros = (0,) * len(shape)
    return pl.BlockSpec(shape, lambda *_: zeros, pipeline_mode=pl.Buffered(1))


def _proj_in_call(x, mod, gvec, w_in_b, rows_per_mod):
    n = x.shape[0]
    return pl.pallas_call(
        _proj_in_body,
        out_shape=(jax.ShapeDtypeStruct((n, ATT_COLS), F32), jax.ShapeDtypeStruct((n, RWKV_COLS), F32)),
        grid=(n // ROW_TILE,),
        in_specs=[
            pl.BlockSpec((ROW_TILE, D_MODEL), lambda i: (i, 0)),
            _mod_spec(mod.shape[0], rows_per_mod),
            _const_spec((8, D_MODEL)),
            _const_spec((D_MODEL, IN_COLS)),
        ],
        out_specs=(pl.BlockSpec((ROW_TILE, ATT_COLS), lambda i: (i, 0)),
                   pl.BlockSpec((ROW_TILE, RWKV_COLS), lambda i: (i, 0))),
        compiler_params=pltpu.CompilerParams(vmem_limit_bytes=VMEM_LIMIT),
        name="proj_in",
    )(x, mod, gvec, w_in_b)


def _attn_prompt_body(sink_ref, za_ref, gq_ref, gk_ref, o_ref, kv_ref):
    aq = (za_ref[:, 0:A_DIM] * QK_SCALE).astype(BF16)
    ak = za_ref[:, 256:384]
    av = za_ref[:, 384:512]
    akb = ak.astype(BF16)
    avb = av.astype(BF16)
    outs = []
    for h in range(A_HEADS):
        kv = h // (A_HEADS // A_KV_HEADS)
        s = _bdot_nt(aq[:, h * 64:(h + 1) * 64], akb[:, kv * 64:(kv + 1) * 64])
        outs.append(_softmax_pv([s], [avb[:, kv * 64:(kv + 1) * 64]], sink_ref[h]))
    bq = _head_rms(za_ref[:, 512:896], gq_ref[...], _head_ones(B_DIM))
    bk = _head_rms(za_ref[:, 896:1024], gk_ref[...], _head_ones(128))
    bv = za_ref[:, 1024:1152]
    bqb = (bq * QK_SCALE).astype(BF16)
    bkb = bk.astype(BF16)
    bvb = bv.astype(BF16)
    for h in range(B_HEADS):
        kv = h // (B_HEADS // B_KV_HEADS)
        s = _bdot_nt(bqb[:, h * 64:(h + 1) * 64], bkb[:, kv * 64:(kv + 1) * 64])
        outs.append(_softmax_pv([s], [bvb[:, kv * 64:(kv + 1) * 64]]))
    o_ref[...] = jnp.concatenate(outs, axis=1)
    kv_ref[...] = jnp.concatenate([ak, av, bk, bv], axis=1)


def _attn_prompt_call(za, sink, gq, gk):
    nb = za.shape[0] // SEQ
    return pl.pallas_call(
        _attn_prompt_body,
        out_shape=(jax.ShapeDtypeStruct((nb * SEQ, ATT_OUT), F32), jax.ShapeDtypeStruct((nb * SEQ, 512), F32)),
        grid=(nb,),
        in_specs=[
            pl.BlockSpec(memory_space=pltpu.SMEM),
            pl.BlockSpec((SEQ, ATT_COLS), lambda b: (b, 0)),
            pl.BlockSpec((1, B_DIM), lambda b: (0, 0)),
            pl.BlockSpec((1, 128), lambda b: (0, 0)),
        ],
        out_specs=(pl.BlockSpec((SEQ, ATT_OUT), lambda b: (b, 0)), pl.BlockSpec((SEQ, 512), lambda b: (b, 0))),
        compiler_params=pltpu.CompilerParams(vmem_limit_bytes=VMEM_LIMIT),
        name="attn_prompt",
    )(sink, za, gq, gk)


def _attn_sample_body(sink_ref, za_ref, cak_ref, cav_ref, cbk_ref, cbv_ref, cos_ref, sin_ref, gq_ref, gk_ref,
                      o_ref, qa_s, ka_s, va_s, qb_s, kb_s, vb_s):
    t = DEC_SEQ
    cos = cos_ref[...]
    sin = sin_ref[...]
    qa_s[...] = (_rope(za_ref[:, 0:A_DIM], cos, sin) * QK_SCALE).astype(BF16)
    zpad = jnp.zeros((Q_BLK, 128), BF16)
    ka_s[0:Q_BLK] = zpad
    ka_s[Q_BLK + t:] = zpad
    va_s[0:Q_BLK] = zpad
    va_s[Q_BLK + t:] = zpad
    ka_s[Q_BLK:Q_BLK + t] = _rope(za_ref[:, 256:384], cos, sin).astype(BF16)
    va_s[Q_BLK:Q_BLK + t] = za_ref[:, 384:512].astype(BF16)
    bq = _head_rms(za_ref[:, 512:896], gq_ref[...], _head_ones(B_DIM))
    qb_s[...] = (_rope(bq, cos, sin) * QK_SCALE).astype(BF16)
    bk = _head_rms(za_ref[:, 896:1024], gk_ref[...], _head_ones(128))
    kb_s[0:PAST_LEN] = cbk_ref[0].astype(BF16)
    kb_s[PAST_LEN:] = _rope(bk, cos, sin).astype(BF16)
    vb_s[0:PAST_LEN] = cbv_ref[0].astype(BF16)
    vb_s[PAST_LEN:] = za_ref[:, 1024:1152].astype(BF16)

    kca = cak_ref[0].astype(BF16)
    vca = cav_ref[0].astype(BF16)

    def a_block(n, carry):
        r0 = pl.multiple_of(n.astype(jnp.int32) * Q_BLK, Q_BLK)
        q = qa_s[pl.ds(r0, Q_BLK), :]
        kl = ka_s[pl.ds(r0, 3 * Q_BLK), :]
        vl = va_s[pl.ds(r0, 3 * Q_BLK), :]
        qpos = r0 + lax.broadcasted_iota(jnp.int32, (Q_BLK, 3 * Q_BLK), 0)
        kpos = r0 - Q_BLK + lax.broadcasted_iota(jnp.int32, (Q_BLK, 3 * Q_BLK), 1)
        mask = (jnp.abs(kpos - qpos) <= WINDOW) & (kpos >= 0) & (kpos < t)
        outs = []
        for h in range(A_HEADS):
            kv = h // (A_HEADS // A_KV_HEADS)
            hs = slice(kv * 64, (kv + 1) * 64)
            qh = q[:, h * 64:(h + 1) * 64]
            s_loc = jnp.where(mask, _bdot_nt(qh, kl[:, hs]), NEG_INF)
            s_ctx = _bdot_nt(qh, kca[:, hs])
            outs.append(_softmax_pv([s_loc, s_ctx], [vl[:, hs], vca[:, hs]], sink_ref[h]))
        o_ref[pl.ds(r0, Q_BLK), 0:A_DIM] = jnp.concatenate(outs, axis=1)
        return carry

    lax.fori_loop(0, t // Q_BLK, a_block, 0)

    qrows = 256

    def b_block(n, carry):
        r0 = pl.multiple_of(n.astype(jnp.int32) * qrows, qrows)
        q = qb_s[pl.ds(r0, qrows), :]
        outs = []
        for h in range(B_HEADS):
            kv = h // (B_HEADS // B_KV_HEADS)
            hs = slice(kv * 64, (kv + 1) * 64)
            s = _bdot_nt(q[:, h * 64:(h + 1) * 64], kb_s[:, hs])
            outs.append(_softmax_pv([s], [vb_s[:, hs]]))
        o_ref[pl.ds(r0, qrows), A_DIM:ATT_OUT] = jnp.concatenate(outs, axis=1)
        return carry

    lax.fori_loop(0, t // qrows, b_block, 0)


def _attn_sample_call(za, sink, caches, cos, sin, gq, gk):
    nb = za.shape[0] // DEC_SEQ
    t = DEC_SEQ
    cache_spec = pl.BlockSpec((1, PAST_LEN, 128), lambda b: (b, 0, 0))
    return pl.pallas_call(
        _attn_sample_body,
        out_shape=jax.ShapeDtypeStruct((nb * t, ATT_OUT), F32),
        grid=(nb,),
        in_specs=[
            pl.BlockSpec(memory_space=pltpu.SMEM),
            pl.BlockSpec((t, ATT_COLS), lambda b: (b, 0)),
            cache_spec, cache_spec, cache_spec, cache_spec,
            pl.BlockSpec((t, 128), lambda b: (0, 0)),
            pl.BlockSpec((t, 128), lambda b: (0, 0)),
            pl.BlockSpec((1, B_DIM), lambda b: (0, 0)),
            pl.BlockSpec((1, 128), lambda b: (0, 0)),
        ],
        out_specs=pl.BlockSpec((t, ATT_OUT), lambda b: (b, 0)),
        scratch_shapes=[
            pltpu.VMEM((t, A_DIM), BF16),
            pltpu.VMEM((t + 2 * Q_BLK, 128), BF16),
            pltpu.VMEM((t + 2 * Q_BLK, 128), BF16),
            pltpu.VMEM((t, B_DIM), BF16),
            pltpu.VMEM((t + PAST_LEN, 128), BF16),
            pltpu.VMEM((t + PAST_LEN, 128), BF16),
        ],
        compiler_params=pltpu.CompilerParams(vmem_limit_bytes=VMEM_LIMIT),
        name="attn_sample",
    )(sink, za, *caches, cos, sin, gq, gk)


def _rwkv_body(*refs, t, has_init, want_state):
    refs = list(refs)
    zc_ref, cv_ref, wup_ref, aup_ref, gup_ref = refs[:5]
    pos = 5
    st0_ref = None
    if has_init:
        st0_ref = refs[pos]
        pos += 1
    oc_ref = refs[pos]
    pos += 1
    stout_ref = None
    if want_state:
        stout_ref = refs[pos]
        pos += 1
    aa_s, ld_s, b_s, kt_s, bon_s, y_s, st_s = refs[pos:]

    nchunks = t // CHUNK
    ones = _head_ones(C_DIM)
    k_k = cv_ref[0:1]
    k_a = cv_ref[1:2]
    r_k = cv_ref[2:3]
    ln_w = cv_ref[3:4]
    ln_b = cv_ref[4:5]
    w0 = (cv_ref[5:6], cv_ref[6:7])
    a0 = (cv_ref[7:8], cv_ref[8:9])
    wup = (wup_ref[0].astype(BF16), wup_ref[1].astype(BF16))
    aup = (aup_ref[0].astype(BF16), aup_ref[1].astype(BF16))

    for rb in range(t // RWKV_ROWS):
        rows = slice(rb * RWKV_ROWS, (rb + 1) * RWKV_ROWS)
        zr = zc_ref[rows, 0:384]
        zk = zc_ref[rows, 384:768]
        zv = zc_ref[rows, 768:1152]
        tw = jnp.tanh(zc_ref[rows, 1152:1216])
        xa = zc_ref[rows, 1216:1280]
        kkr = zk * k_k
        kk = kkr * lax.rsqrt(_split_dot(kkr * kkr, ones) + 1e-12)
        aa_s[rows] = -kk
        bon = None
        for d in range(2):
            wl = w0[d] + _bdot(tw, wup[d])
            ld_s[d, rows] = (-DECAY_RATE) * jax.nn.sigmoid(wl)
            a = jax.nn.sigmoid(a0[d] + _bdot(xa, aup[d]))
            kt = zk * (1.0 + (a - 1.0) * k_a)
            kt_s[d, rows] = kt
            b_s[d, rows] = kk * a
            bd = _split_dot(zr * kt * r_k, ones) * zv
            bon = bd if bon is None else bon + bd
        bon_s[rows] = bon
        y_s[rows] = jnp.zeros((RWKV_ROWS, C_DIM), F32)

    ir = lax.broadcasted_iota(jnp.int32, (CHUNK, CHUNK), 0)
    ic = lax.broadcasted_iota(jnp.int32, (CHUNK, CHUNK), 1)
    eye = jnp.where(ir == ic, 1.0, 0.0).astype(F32)
    pair_masks = []
    for j in range(CHUNK.bit_length() - 1):
        same_big = lax.shift_right_logical(ir, jnp.int32(j + 1)) == lax.shift_right_logical(ic, jnp.int32(j + 1))
        same_small = lax.shift_right_logical(ir, jnp.int32(j)) == lax.shift_right_logical(ic, jnp.int32(j))
        pair_masks.append(same_big & jnp.logical_not(same_small))

    incl = (ic <= ir, ic >= ir)
    strict = (ic < ir, ic > ir)
    tri = tuple(jnp.where(m, 1.0, 0.0).astype(BF16) for m in incl)
    for d in range(2):
        for h in range(C_HEADS):
            st_s[d, h] = st0_ref[0, d, h] if has_init else jnp.zeros((HEAD_DIM, HEAD_DIM), F32)

    def chunk(c, carry):
        c = c.astype(jnp.int32)
        rows = (pl.ds(pl.multiple_of(c * CHUNK, CHUNK), CHUNK),
                pl.ds(pl.multiple_of((nchunks - 1 - c) * CHUNK, CHUNK), CHUNK))
        items = []
        e_tot = []
        for d in range(2):
            ld = ld_s[d, rows[d]]
            cinc = _split_dot_left(tri[d], ld)
            cexc = cinc - ld
            ctot = cinc[0:1] if d == 1 else cinc[CHUNK - 1:CHUNK]
            e_ninc = jnp.exp(-cinc)
            e_rem = jnp.exp(ctot - cinc)
            e_tot.append(jnp.exp(ctot))
            bb = b_s[d, rows[d]]
            kt = kt_s[d, rows[d]]
            at = (aa_s[rows[d]] * jnp.exp(cexc)).astype(BF16)
            rt = (zc_ref[rows[d], 0:384] * jnp.exp(cinc)).astype(BF16)
            bt = (bb * e_ninc).astype(BF16)
            ktt = (kt * e_ninc).astype(BF16)
            bh = (bb * e_rem).astype(BF16)
            kh = (kt * e_rem).astype(BF16)
            vm = zc_ref[rows[d], 768:1152].astype(BF16)
            for h in range(C_HEADS):
                hs = slice(h * 64, (h + 1) * 64)
                items.append(dict(d=d, h=h, hs=hs, at=at[:, hs], rt=rt[:, hs], bt=bt[:, hs], kt=ktt[:, hs],
                                  bh=bh[:, hs], kh=kh[:, hs], vm=vm[:, hs]))
        for it in items:
            it["ar"] = jnp.concatenate([it["at"], it["rt"]], axis=0)
        for it in items:
            it["sb"] = _bdot_nt(it["ar"], it["bt"])
        for it in items:
            it["sk"] = _bdot_nt(it["ar"], it["kt"])
        for it in items:
            d = it["d"]
            it["l"] = jnp.where(strict[d], it["sb"][0:CHUNK], 0.0)
            it["mak"] = jnp.where(strict[d], it["sk"][0:CHUNK], 0.0).astype(BF16)
            it["nrb"] = jnp.where(incl[d], it["sb"][CHUNK:], 0.0).astype(BF16)
            it["nrk"] = jnp.where(incl[d], it["sk"][CHUNK:], 0.0).astype(BF16)
            it["tm"] = eye + jnp.where(pair_masks[0], it["l"], 0.0)
        for pm in pair_masks[1:]:
            for it in items:
                it["t1"] = _bdot(it["tm"], jnp.where(pm, it["l"], 0.0))
            for it in items:
                it["tm"] = it["tm"] + _bdot(it["t1"], it["tm"])
        for it in items:
            it["mv"] = _bdot(it["mak"], it["vm"])
        for it in items:
            it["ta"] = _bdot(it["tm"], it["at"])
        for it in items:
            it["uin"] = _bdot(it["tm"], it["mv"])
        for it in items:
            it["s0"] = st_s[it["d"], it["h"]]
            it["x"] = _bdot_nt(jnp.concatenate([it["ta"].astype(BF16), it["rt"]], axis=0), it["s0"])
        for it in items:
            it["u"] = it["x"][0:CHUNK] + it["uin"]
            it["yv"] = _bdot(it["nrk"], it["vm"])
        for it in items:
            it["y"] = it["x"][CHUNK:] + _bdot(it["nrb"], it["u"]) + it["yv"]
        for it in items:
            d = it["d"]
            st_s[d, it["h"]] = it["s0"] * e_tot[d][:, it["hs"]] + _bdot_tn(
                jnp.concatenate([it["u"].astype(BF16), it["vm"]], axis=0),
                jnp.concatenate([it["bh"], it["kh"]], axis=0))
        for d in range(2):
            ys = [it["y"] for it in items if it["d"] == d]
            y_s[rows[d]] = y_s[rows[d]] + jnp.concatenate(ys, axis=1)
        return carry

    lax.fori_loop(0, nchunks, chunk, 0)
    if want_state:
        for d in range(2):
            for h in range(C_HEADS):
                stout_ref[0, d, h] = st_s[d, h]

    gup = gup_ref[...].astype(BF16)
    for rb in range(t // RWKV_ROWS):
        rows = slice(rb * RWKV_ROWS, (rb + 1) * RWKV_ROWS)
        y = y_s[rows]
        mu = _split_dot(y, ones) * (1.0 / HEAD_DIM)
        yc = y - mu
        var = _split_dot(yc * yc, ones) * (1.0 / HEAD_DIM)
        yn = yc * lax.rsqrt(var + GN_EPS)
        g = _bdot(jax.nn.sigmoid(zc_ref[rows, 1280:1408]), gup)
        oc_ref[rows] = (yn * ln_w + ln_b + bon_s[rows]) * g


def _rwkv_call(zc, cvec, wup, aup, gup, t, init_state, want_state):
    nb = zc.shape[0] // t
    has_init = init_state is not None
    state_spec = pl.BlockSpec((1, 2, C_HEADS, HEAD_DIM, HEAD_DIM), lambda b: (b, 0, 0, 0, 0))
    in_specs = [
        pl.BlockSpec((t, RWKV_COLS), lambda b: (b, 0)),
        pl.BlockSpec((16, C_DIM), lambda b: (0, 0)),
        pl.BlockSpec((2, W_RANK, C_DIM), lambda b: (0, 0, 0)),
        pl.BlockSpec((2, A_RANK, C_DIM), lambda b: (0, 0, 0)),
        pl.BlockSpec((G_RANK, C_DIM), lambda b: (0, 0)),
    ]
    args = [zc, cvec, wup, aup, gup]
    if has_init:
        in_specs.append(state_spec)
        args.append(init_state)
    out_shape = [jax.ShapeDtypeStruct((nb * t, C_DIM), F32)]
    out_specs = [pl.BlockSpec((t, C_DIM), lambda b: (b, 0))]
    if want_state:
        out_shape.append(jax.ShapeDtypeStruct((nb, 2, C_HEADS, HEAD_DIM, HEAD_DIM), F32))
        out_specs.append(state_spec)
    res = pl.pallas_call(
        functools.partial(_rwkv_body, t=t, has_init=has_init, want_state=want_state),
        out_shape=tuple(out_shape),
        grid=(nb,),
        in_specs=in_specs,
        out_specs=tuple(out_specs),
        scratch_shapes=[
            pltpu.VMEM((t, C_DIM), F32),
            pltpu.VMEM((2, t, C_DIM), F32),
            pltpu.VMEM((2, t, C_DIM), F32),
            pltpu.VMEM((2, t, C_DIM), F32),
            pltpu.VMEM((t, C_DIM), F32),
            pltpu.VMEM((t, C_DIM), F32),
            pltpu.VMEM((2, C_HEADS, HEAD_DIM, HEAD_DIM), F32),
        ],
        compiler_params=pltpu.CompilerParams(vmem_limit_bytes=VMEM_LIMIT),
        name="rwkv_prompt" if want_state else "rwkv_sample",
    )(*args)
    return res


def _post_body(oa_ref, oc_ref, x_ref, mod_ref, g_ref, wout_ref, wgu_ref, wdn_ref, xo_ref):
    mod = mod_ref[0]
    gate1 = mod[:, 2 * D_MODEL:3 * D_MODEL]
    shift2 = mod[:, 3 * D_MODEL:4 * D_MODEL]
    scale2 = mod[:, 4 * D_MODEL:5 * D_MODEL]
    gate2 = mod[:, 5 * D_MODEL:6 * D_MODEL]
    o = (jnp.dot(oa_ref[...].astype(BF16), wout_ref[0:ATT_OUT], preferred_element_type=F32)
         + jnp.dot(oc_ref[...].astype(BF16), wout_ref[ATT_OUT:MIX_DIM], preferred_element_type=F32))
    x1 = x_ref[...] + gate1 * _rms(o, g_ref[1:2])
    h2 = (_rms(x1, g_ref[2:3]) * (1.0 + scale2) + shift2).astype(BF16)
    acc = None
    for j in range(FF_DIM // FF_CHUNK):
        lo = j * FF_CHUNK
        g = jnp.dot(h2, wgu_ref[:, lo:lo + FF_CHUNK], preferred_element_type=F32)
        u = jnp.dot(h2, wgu_ref[:, FF_DIM + lo:FF_DIM + lo + FF_CHUNK], preferred_element_type=F32)
        act = (g * jax.nn.sigmoid(g) * u).astype(BF16)
        part = jnp.dot(act, wdn_ref[lo:lo + FF_CHUNK, :], preferred_element_type=F32)
        acc = part if acc is None else acc + part
    xo_ref[...] = x1 + gate2 * _rms(acc, g_ref[3:4])


def _post_call(oatt, oc, x, mod, gvec, wout_b, wgu_b, wdn_b, rows_per_mod):
    n = x.shape[0]
    return pl.pallas_call(
        _post_body,
        out_shape=jax.ShapeDtypeStruct((n, D_MODEL), F32),
        grid=(n // ROW_TILE,),
        in_specs=[
            pl.BlockSpec((ROW_TILE, ATT_OUT), lambda i: (i, 0)),
            pl.BlockSpec((ROW_TILE, C_DIM), lambda i: (i, 0)),
            pl.BlockSpec((ROW_TILE, D_MODEL), lambda i: (i, 0)),
            _mod_spec(mod.shape[0], rows_per_mod),
            _const_spec((8, D_MODEL)),
            _const_spec((MIX_DIM, D_MODEL)),
            _const_spec((D_MODEL, 2 * FF_DIM)),
            _const_spec((FF_DIM, D_MODEL)),
        ],
        out_specs=pl.BlockSpec((ROW_TILE, D_MODEL), lambda i: (i, 0)),
        compiler_params=pltpu.CompilerParams(vmem_limit_bytes=VMEM_LIMIT),
        name="post_ffn",
    )(oatt, oc, x, mod, gvec, wout_b, wgu_b, wdn_b)


def _rope_tables(t):
    pos = jnp.arange(t)
    row = (pos // GRID_W).astype(F32)
    col = (pos % GRID_W).astype(F32)
    freqs = ROPE_THETA ** (-jnp.arange(ROPE_PAIRS_AXIS, dtype=F32) / ROPE_PAIRS_AXIS)
    ang = jnp.concatenate([row[:, None] * freqs, col[:, None] * freqs], axis=-1)
    cos = jnp.tile(jnp.cos(ang), (1, 4))
    sin = jnp.sin(ang)
    sin = jnp.tile(jnp.concatenate([-sin, sin], axis=-1), (1, 2))
    return cos, sin


def kernel(x_prompt, x_sample, cache_a_k, cache_a_v, cache_b_k, cache_b_v, state_c, c, c_ctx, w_mod, b_mod, norm_mix_pre, norm_mix_post, norm_ffn_pre, norm_ffn_post, w_in, w_out, a_sink, b_q_norm, b_k_norm, c_w0, c_w_up, c_a0, c_a_up, c_g_up, c_k_k, c_k_a, c_r_k, c_ln_w, c_ln_b, w_gu, w_down):
    cvec = jnp.zeros((MOD_ROWS, D_MODEL), F32).at[0].set(c_ctx).at[1:1 + DEC_BATCH].set(c)
    mods = _mod_call(cvec, w_mod, b_mod)
    cos, sin = _rope_tables(DEC_SEQ)

    xp = x_prompt.reshape(BATCH * SEQ, D_MODEL)
    xs = x_sample.reshape(DEC_BATCH * DEC_SEQ, D_MODEL)
    kv_l, st_l = [], []
    for l in range(DEPTH):
        w_in_b = w_in[l].astype(BF16)
        wout_b = w_out[l].astype(BF16)
        wgu_b = w_gu[l].astype(BF16)
        wdn_b = w_down[l].astype(BF16)
        gvec = jnp.zeros((8, D_MODEL), F32)
        gvec = gvec.at[0].set(norm_mix_pre[l]).at[1].set(norm_mix_post[l])
        gvec = gvec.at[2].set(norm_ffn_pre[l]).at[3].set(norm_ffn_post[l])
        rvec = jnp.zeros((16, C_DIM), F32)
        rvec = rvec.at[0].set(c_k_k[l]).at[1].set(c_k_a[l]).at[2].set(c_r_k[l].reshape(C_DIM))
        rvec = rvec.at[3].set(c_ln_w[l]).at[4].set(c_ln_b[l])
        rvec = rvec.at[5:7].set(c_w0[l]).at[7:9].set(c_a0[l])
        gq = jnp.tile(b_q_norm[l], B_HEADS)[None, :]
        gk = jnp.tile(b_k_norm[l], B_KV_HEADS)[None, :]
        sink = a_sink[l]
        mod_p = mods[l, 0:1].reshape(1, 1, MOD_COLS)
        mod_s = mods[l, 1:1 + DEC_BATCH].reshape(DEC_BATCH, 1, MOD_COLS)

        za, zc = _proj_in_call(xp, mod_p, gvec, w_in_b, BATCH * SEQ)
        oatt, kv = _attn_prompt_call(za, sink, gq, gk)
        oc, st = _rwkv_call(zc, rvec, c_w_up[l], c_a_up[l], c_g_up[l], SEQ, None, True)
        xp = _post_call(oatt, oc, xp, mod_p, gvec, wout_b, wgu_b, wdn_b, BATCH * SEQ)
        kv_l.append(kv)
        st_l.append(st)

        caches = tuple(t[:, l].reshape(DEC_BATCH, PAST_LEN, 128)
                       for t in (cache_a_k, cache_a_v, cache_b_k, cache_b_v))
        za, zc = _proj_in_call(xs, mod_s, gvec, w_in_b, DEC_SEQ)
        oatt = _attn_sample_call(za, sink, caches, cos, sin, gq, gk)
        (oc,) = _rwkv_call(zc, rvec, c_w_up[l], c_a_up[l], c_g_up[l], DEC_SEQ, state_c[:, l], False)
        xs = _post_call(oatt, oc, xs, mod_s, gvec, wout_b, wgu_b, wdn_b, DEC_SEQ)

    def cache_out(lo):
        return jnp.stack([kv[:, lo:lo + 128].reshape(BATCH, SEQ, 2, HEAD_DIM) for kv in kv_l], axis=1)

    return (xp.reshape(BATCH, SEQ, D_MODEL), xs.reshape(DEC_BATCH, DEC_SEQ, D_MODEL),
            cache_out(0), cache_out(128), cache_out(256), cache_out(384), jnp.stack(st_l, axis=1))
```

```python
import functools
import math

import jax
import jax.numpy as jnp
from jax import lax
from jax.experimental import pallas as pl
from jax.experimental.pallas import tpu as pltpu

D_MODEL = 1024
BATCH = 32
SEQ = 256
DEPTH = 2
DEC_BATCH = 2
DEC_SEQ = 1024
PAST_LEN = 256
GRID_W = 64
HEAD_DIM = 64
A_HEADS = 4
A_KV_HEADS = 2
B_HEADS = 6
B_KV_HEADS = 2
C_HEADS = 6
A_DIM = A_HEADS * HEAD_DIM
B_DIM = B_HEADS * HEAD_DIM
C_DIM = C_HEADS * HEAD_DIM
MIX_DIM = A_DIM + B_DIM + C_DIM
WINDOW = 128
Q_BLK = 128
W_RANK = 64
A_RANK = 64
G_RANK = 128
FF_DIM = -(-8 * D_MODEL // (3 * 256)) * 256
ROPE_THETA = 10000.0
ROPE_PAIRS_AXIS = HEAD_DIM // 4
NORM_EPS = 1e-6
GN_EPS = 64e-5
NEG_INF = -1e30

KV_DIM = 2 * HEAD_DIM
ATT_COLS = A_DIM + 2 * KV_DIM + B_DIM + 2 * KV_DIM
RWKV_COLS = 3 * C_DIM + W_RANK + A_RANK + G_RANK
IN_COLS = ATT_COLS + RWKV_COLS
ATT_OUT = A_DIM + B_DIM
MOD_COLS = 6 * D_MODEL
MOD_ROWS = 8
N_CACHE = 4
CHUNK = 64
QK_SCALE = HEAD_DIM ** -0.5
DECAY_RATE = math.exp(-0.5)
ROW_TILE = 512
RWKV_ROWS = 256
B_QROWS = 256
FF_CHUNK = FF_DIM // 2
VMEM_LIMIT = 56 * 1024 * 1024

F32 = jnp.float32
BF16 = jnp.bfloat16

assert DEPTH == 2


def _bdot(a, b):
    return jnp.dot(a.astype(BF16), b.astype(BF16), preferred_element_type=F32)


def _bdot_nt(a, b):
    return lax.dot_general(a.astype(BF16), b.astype(BF16), (((1,), (1,)), ((), ())), preferred_element_type=F32)


def _bdot_tn(a, b):
    return lax.dot_general(a.astype(BF16), b.astype(BF16), (((0,), (0,)), ((), ())), preferred_element_type=F32)


def _split(x):
    hi = x.astype(BF16)
    lo = (x - hi.astype(F32)).astype(BF16)
    return hi, lo


def _split_dot(x, m):
    hi, lo = _split(x)
    return jnp.dot(hi, m, preferred_element_type=F32) + jnp.dot(lo, m, preferred_element_type=F32)


def _split_dot_left(m, x):
    hi, lo = _split(x)
    return jnp.dot(m, hi, preferred_element_type=F32) + jnp.dot(m, lo, preferred_element_type=F32)


def _head_ones(n):
    r = lax.shift_right_logical(lax.broadcasted_iota(jnp.int32, (n, n), 0), jnp.int32(6))
    c = lax.shift_right_logical(lax.broadcasted_iota(jnp.int32, (n, n), 1), jnp.int32(6))
    return jnp.where(r == c, 1.0, 0.0).astype(BF16)


def _rms(x, g):
    ms = jnp.mean(x * x, axis=-1, keepdims=True)
    return x * lax.rsqrt(ms + NORM_EPS) * g


def _head_rms(x, g, ones):
    ms = _split_dot(x * x, ones) * (1.0 / HEAD_DIM)
    return x * lax.rsqrt(ms + NORM_EPS) * g


def _rope(x, cos, sin):
    t = x.shape[0]
    lane = lax.broadcasted_iota(jnp.int32, (t, 128), 1)
    first = (lane & 32) == 0
    outs = []
    for i in range(x.shape[1] // 128):
        xs = x[:, i * 128:(i + 1) * 128]
        swapped = jnp.where(first, pltpu.roll(xs, 96, axis=1), pltpu.roll(xs, 32, axis=1))
        outs.append(xs * cos + swapped * sin)
    return outs[0] if len(outs) == 1 else jnp.concatenate(outs, axis=1)


def _softmax_pv(scores, values, sink=None):
    m = scores[0].max(axis=-1, keepdims=True)
    for s in scores[1:]:
        m = jnp.maximum(m, s.max(axis=-1, keepdims=True))
    if sink is not None:
        m = jnp.maximum(m, sink)
    denom = None
    acc = None
    for s, v in zip(scores, values):
        p = jnp.exp(s - m)
        d = p.sum(axis=-1, keepdims=True)
        o = jnp.dot(p.astype(BF16), v, preferred_element_type=F32)
        denom = d if denom is None else denom + d
        acc = o if acc is None else acc + o
    if sink is not None:
        denom = denom + jnp.exp(sink - m)
    return acc / denom


def _attend_heads(score_fns, finish_fns):
    outs = []
    nxt = score_fns[0]()
    for i in range(len(score_fns)):
        cur = nxt
        if i + 1 < len(score_fns):
            nxt = score_fns[i + 1]()
        outs.append(finish_fns[i](cur))
    return outs


def _index32(i):
    return jnp.asarray(i, jnp.int32)


def _layer_spec(shape, layer):
    zeros = (0,) * len(shape)
    return pl.BlockSpec((None,) + shape, lambda *_: (layer,) + zeros, pipeline_mode=pl.Buffered(1))


def _mod_row(mod_ref, rows_per_mod):
    if rows_per_mod is None:
        return mod_ref[0:1]
    row = 1 + pl.program_id(0) // (rows_per_mod // ROW_TILE)
    return mod_ref[pl.ds(row, 1), :]


def _mod_body(c_ref, w_ref, b_ref, o_ref):
    cv = c_ref[...]
    s = cv * jax.nn.sigmoid(cv)
    o_ref[0] = _bdot(s, w_ref[0]) + b_ref[0]


def _mod_call(cvec, w_mod, b_mod):
    bn = MOD_COLS // 4
    return pl.pallas_call(
        _mod_body,
        out_shape=jax.ShapeDtypeStruct((DEPTH, MOD_ROWS, MOD_COLS), F32),
        grid=(DEPTH, MOD_COLS // bn),
        in_specs=[
            pl.BlockSpec((MOD_ROWS, D_MODEL), lambda l, j: (0, 0)),
            pl.BlockSpec((1, D_MODEL, bn), lambda l, j: (l, 0, j)),
            pl.BlockSpec((1, 1, bn), lambda l, j: (l, 0, j)),
        ],
        out_specs=pl.BlockSpec((1, MOD_ROWS, bn), lambda l, j: (l, 0, j)),
        compiler_params=pltpu.CompilerParams(vmem_limit_bytes=VMEM_LIMIT),
        name="mod_vectors",
    )(cvec, w_mod, b_mod.reshape(DEPTH, 1, MOD_COLS))


def _proj_in_body(x_ref, mod_ref, g_ref, w_ref, za_ref, zc_ref, *, rows_per_mod):
    mod = _mod_row(mod_ref, rows_per_mod)
    shift1 = mod[:, 0:D_MODEL]
    scale1 = mod[:, D_MODEL:2 * D_MODEL]
    h = (_rms(x_ref[...], g_ref[0:1]) * (1.0 + scale1) + shift1).astype(BF16)
    za_ref[...] = jnp.dot(h, w_ref[:, 0:ATT_COLS], preferred_element_type=F32)
    zc_ref[...] = jnp.dot(h, w_ref[:, ATT_COLS:IN_COLS], preferred_element_type=F32)


def _proj_in_call(x, mods, gvec, w_in_b, layer, rows_per_mod):
    n = x.shape[0]
    return pl.pallas_call(
        functools.partial(_proj_in_body, rows_per_mod=rows_per_mod),
        out_shape=(jax.ShapeDtypeStruct((n, ATT_COLS), F32), jax.ShapeDtypeStruct((n, RWKV_COLS), F32)),
        grid=(n // ROW_TILE,),
        in_specs=[
            pl.BlockSpec((ROW_TILE, D_MODEL), lambda i: (i, 0)),
            _layer_spec((MOD_ROWS, MOD_COLS), layer),
            _layer_spec((8, D_MODEL), layer),
            _layer_spec((D_MODEL, IN_COLS), layer),
        ],
        out_specs=(pl.BlockSpec((ROW_TILE, ATT_COLS), lambda i: (i, 0)),
                   pl.BlockSpec((ROW_TILE, RWKV_COLS), lambda i: (i, 0))),
        compiler_params=pltpu.CompilerParams(vmem_limit_bytes=VMEM_LIMIT),
        name="proj_in",
    )(x, mods, gvec, w_in_b)


def _attn_prompt_body(*refs, layer):
    sink_ref, za_ref, gq_ref, gk_ref = refs[:4]
    if layer == 0:
        o_ref, kv_ref = refs[4:]
    else:
        prev_ref, o_ref = refs[4:6]
        cache_refs = refs[6:]
    aq = (za_ref[:, 0:A_DIM] * QK_SCALE).astype(BF16)
    ak = za_ref[:, 256:384]
    av = za_ref[:, 384:512]
    akb = ak.astype(BF16)
    avb = av.astype(BF16)
    bq = _head_rms(za_ref[:, 512:896], gq_ref[...], _head_ones(B_DIM))
    bk = _head_rms(za_ref[:, 896:1024], gk_ref[...], _head_ones(KV_DIM))
    bv = za_ref[:, 1024:1152]
    bqb = (bq * QK_SCALE).astype(BF16)
    bkb = bk.astype(BF16)
    bvb = bv.astype(BF16)

    score_fns, finish_fns = [], []
    for h in range(A_HEADS):
        kv = h // (A_HEADS // A_KV_HEADS)
        hs = slice(kv * 64, (kv + 1) * 64)
        score_fns.append(lambda h=h, hs=hs: [_bdot_nt(aq[:, h * 64:(h + 1) * 64], akb[:, hs])])
        finish_fns.append(lambda sc, h=h, hs=hs: _softmax_pv(sc, [avb[:, hs]], sink_ref[layer, h]))
    for h in range(B_HEADS):
        kv = h // (B_HEADS // B_KV_HEADS)
        hs = slice(kv * 64, (kv + 1) * 64)
        score_fns.append(lambda h=h, hs=hs: [_bdot_nt(bqb[:, h * 64:(h + 1) * 64], bkb[:, hs])])
        finish_fns.append(lambda sc, hs=hs: _softmax_pv(sc, [bvb[:, hs]]))
    o_ref[...] = jnp.concatenate(_attend_heads(score_fns, finish_fns), axis=1)

    for j, piece in enumerate((ak, av, bk, bv)):
        if layer == 0:
            kv_ref[0, j] = piece.T
        else:
            cache_refs[j][0, 0] = prev_ref[0, j]
            cache_refs[j][0, 1] = piece.T


def _attn_prompt_call(za, sink, gq, gk, layer, prev_kv):
    nb = za.shape[0] // SEQ
    in_specs = [
        pl.BlockSpec(memory_space=pltpu.SMEM),
        pl.BlockSpec((SEQ, ATT_COLS), lambda b: (b, 0)),
        _layer_spec((1, B_DIM), layer),
        _layer_spec((1, KV_DIM), layer),
    ]
    args = [sink, za, gq, gk]
    out_shape = [jax.ShapeDtypeStruct((nb * SEQ, ATT_OUT), F32)]
    out_specs = [pl.BlockSpec((SEQ, ATT_OUT), lambda b: (b, 0))]
    slab_spec = pl.BlockSpec((1, N_CACHE, KV_DIM, SEQ), lambda b: (b, 0, 0, 0))
    if layer == 0:
        out_shape.append(jax.ShapeDtypeStruct((nb, N_CACHE, KV_DIM, SEQ), F32))
        out_specs.append(slab_spec)
    else:
        in_specs.append(slab_spec)
        args.append(prev_kv)
        for _ in range(N_CACHE):
            out_shape.append(jax.ShapeDtypeStruct((nb, DEPTH, KV_DIM, SEQ), F32))
            out_specs.append(pl.BlockSpec((1, DEPTH, KV_DIM, SEQ), lambda b: (b, 0, 0, 0)))
    return pl.pallas_call(
        functools.partial(_attn_prompt_body, layer=layer),
        out_shape=tuple(out_shape),
        grid=(nb,),
        in_specs=in_specs,
        out_specs=tuple(out_specs),
        compiler_params=pltpu.CompilerParams(vmem_limit_bytes=VMEM_LIMIT),
        name="attn_prompt",
    )(*args)


def _attn_sample_body(sink_ref, za_ref, cak_ref, cav_ref, cbk_ref, cbv_ref, cos_ref, sin_ref, gq_ref, gk_ref,
                      o_ref, qa_s, ka_s, va_s, qb_s, kb_s, vb_s, *, layer):
    t = DEC_SEQ
    cos = cos_ref[...]
    sin = sin_ref[...]
    qa_s[...] = (_rope(za_ref[:, 0:A_DIM], cos, sin) * QK_SCALE).astype(BF16)
    zpad = jnp.zeros((Q_BLK, KV_DIM), BF16)
    ka_s[0:Q_BLK] = zpad
    ka_s[Q_BLK + t:] = zpad
    va_s[0:Q_BLK] = zpad
    va_s[Q_BLK + t:] = zpad
    ka_s[Q_BLK:Q_BLK + t] = _rope(za_ref[:, 256:384], cos, sin).astype(BF16)
    va_s[Q_BLK:Q_BLK + t] = za_ref[:, 384:512].astype(BF16)
    bq = _head_rms(za_ref[:, 512:896], gq_ref[...], _head_ones(B_DIM))
    qb_s[...] = (_rope(bq, cos, sin) * QK_SCALE).astype(BF16)
    bk = _head_rms(za_ref[:, 896:1024], gk_ref[...], _head_ones(KV_DIM))
    kb_s[0:PAST_LEN] = cbk_ref[0].T.astype(BF16)
    kb_s[PAST_LEN:] = _rope(bk, cos, sin).astype(BF16)
    vb_s[0:PAST_LEN] = cbv_ref[0].T.astype(BF16)
    vb_s[PAST_LEN:] = za_ref[:, 1024:1152].astype(BF16)

    kca = cak_ref[0].T.astype(BF16)
    vca = cav_ref[0].T.astype(BF16)

    def a_block(n, carry):
        r0 = pl.multiple_of(_index32(n) * Q_BLK, Q_BLK)
        q = qa_s[pl.ds(r0, Q_BLK), :]
        kl = ka_s[pl.ds(r0, 3 * Q_BLK), :]
        vl = va_s[pl.ds(r0, 3 * Q_BLK), :]
        qpos = r0 + lax.broadcasted_iota(jnp.int32, (Q_BLK, 3 * Q_BLK), 0)
        kpos = r0 - Q_BLK + lax.broadcasted_iota(jnp.int32, (Q_BLK, 3 * Q_BLK), 1)
        mask = (jnp.abs(kpos - qpos) <= WINDOW) & (kpos >= 0) & (kpos < t)
        score_fns, finish_fns = [], []
        for h in range(A_HEADS):
            kv = h // (A_HEADS // A_KV_HEADS)
            hs = slice(kv * 64, (kv + 1) * 64)

            def scores(h=h, hs=hs):
                qh = q[:, h * 64:(h + 1) * 64]
                return [jnp.where(mask, _bdot_nt(qh, kl[:, hs]), NEG_INF), _bdot_nt(qh, kca[:, hs])]

            score_fns.append(scores)
            finish_fns.append(lambda sc, h=h, hs=hs: _softmax_pv(sc, [vl[:, hs], vca[:, hs]], sink_ref[layer, h]))
        o_ref[pl.ds(r0, Q_BLK), 0:A_DIM] = jnp.concatenate(_attend_heads(score_fns, finish_fns), axis=1)
        return carry

    lax.fori_loop(0, t // Q_BLK, a_block, 0)

    def b_block(n, carry):
        r0 = pl.multiple_of(_index32(n) * B_QROWS, B_QROWS)
        q = qb_s[pl.ds(r0, B_QROWS), :]
        score_fns, finish_fns = [], []
        for h in range(B_HEADS):
            kv = h // (B_HEADS // B_KV_HEADS)
            hs = slice(kv * 64, (kv + 1) * 64)
            score_fns.append(lambda h=h, hs=hs: [_bdot_nt(q[:, h * 64:(h + 1) * 64], kb_s[:, hs])])
            finish_fns.append(lambda sc, hs=hs: _softmax_pv(sc, [vb_s[:, hs]]))
        o_ref[pl.ds(r0, B_QROWS), A_DIM:ATT_OUT] = jnp.concatenate(_attend_heads(score_fns, finish_fns), axis=1)
        return carry

    lax.fori_loop(0, t // B_QROWS, b_block, 0)


def _attn_sample_call(za, sink, caches, cos, sin, gq, gk, layer):
    nb = za.shape[0] // DEC_SEQ
    t = DEC_SEQ
    cache_spec = pl.BlockSpec((1, None, KV_DIM, PAST_LEN), lambda b: (b, layer, 0, 0))
    return pl.pallas_call(
        functools.partial(_attn_sample_body, layer=layer),
        out_shape=jax.ShapeDtypeStruct((nb * t, ATT_OUT), F32),
        grid=(nb,),
        in_specs=[
            pl.BlockSpec(memory_space=pltpu.SMEM),
            pl.BlockSpec((t, ATT_COLS), lambda b: (b, 0)),
            cache_spec, cache_spec, cache_spec, cache_spec,
            pl.BlockSpec((t, 128), lambda b: (0, 0)),
            pl.BlockSpec((t, 128), lambda b: (0, 0)),
            _layer_spec((1, B_DIM), layer),
            _layer_spec((1, KV_DIM), layer),
        ],
        out_specs=pl.BlockSpec((t, ATT_OUT), lambda b: (b, 0)),
        scratch_shapes=[
            pltpu.VMEM((t, A_DIM), BF16),
            pltpu.VMEM((t + 2 * Q_BLK, KV_DIM), BF16),
            pltpu.VMEM((t + 2 * Q_BLK, KV_DIM), BF16),
            pltpu.VMEM((t, B_DIM), BF16),
            pltpu.VMEM((t + PAST_LEN, KV_DIM), BF16),
            pltpu.VMEM((t + PAST_LEN, KV_DIM), BF16),
        ],
        compiler_params=pltpu.CompilerParams(vmem_limit_bytes=VMEM_LIMIT),
        name="attn_sample",
    )(sink, za, *caches, cos, sin, gq, gk)


def _rwkv_body(*refs, t, has_init, want_state, has_prev):
    refs = list(refs)
    zc_ref, cv_ref, wup_ref, aup_ref, gup_ref = refs[:5]
    pos = 5
    st0_ref = prev_ref = stout_ref = None
    if has_init:
        st0_ref = refs[pos]
        pos += 1
    if has_prev:
        prev_ref = refs[pos]
        pos += 1
    oc_ref = refs[pos]
    pos += 1
    if want_state:
        stout_ref = refs[pos]
        pos += 1
    aa_s, ld_s, b_s, kt_s, bon_s, y_s, st_s = refs[pos:]

    nchunks = t // CHUNK
    ones = _head_ones(C_DIM)
    k_k = cv_ref[0:1]
    k_a = cv_ref[1:2]
    r_k = cv_ref[2:3]
    ln_w = cv_ref[3:4]
    ln_b = cv_ref[4:5]
    w0 = (cv_ref[5:6], cv_ref[6:7])
    a0 = (cv_ref[7:8], cv_ref[8:9])
    wup = (wup_ref[0].astype(BF16), wup_ref[1].astype(BF16))
    aup = (aup_ref[0].astype(BF16), aup_ref[1].astype(BF16))

    for rb in range(t // RWKV_ROWS):
        rows = slice(rb * RWKV_ROWS, (rb + 1) * RWKV_ROWS)
        zr = zc_ref[rows, 0:384]
        zk = zc_ref[rows, 384:768]
        zv = zc_ref[rows, 768:1152]
        tw = jnp.tanh(zc_ref[rows, 1152:1216])
        xa = zc_ref[rows, 1216:1280]
        kkr = zk * k_k
        kk = kkr * lax.rsqrt(_split_dot(kkr * kkr, ones) + 1e-12)
        aa_s[rows] = -kk
        bon = None
        for d in range(2):
            wl = w0[d] + _bdot(tw, wup[d])
            ld_s[d, rows] = (-DECAY_RATE) * jax.nn.sigmoid(wl)
            a = jax.nn.sigmoid(a0[d] + _bdot(xa, aup[d]))
            kt = zk * (1.0 + (a - 1.0) * k_a)
            kt_s[d, rows] = kt
            b_s[d, rows] = kk * a
            bd = _split_dot(zr * kt * r_k, ones) * zv
            bon = bd if bon is None else bon + bd
        bon_s[rows] = bon
        y_s[rows] = jnp.zeros((RWKV_ROWS, C_DIM), F32)

    ir = lax.broadcasted_iota(jnp.int32, (CHUNK, CHUNK), 0)
    ic = lax.broadcasted_iota(jnp.int32, (CHUNK, CHUNK), 1)
    eye = jnp.where(ir == ic, 1.0, 0.0).astype(F32)
    pair_masks = []
    for j in range(CHUNK.bit_length() - 1):
        same_big = lax.shift_right_logical(ir, jnp.int32(j + 1)) == lax.shift_right_logical(ic, jnp.int32(j + 1))
        same_small = lax.shift_right_logical(ir, jnp.int32(j)) == lax.shift_right_logical(ic, jnp.int32(j))
        pair_masks.append(same_big & jnp.logical_not(same_small))

    incl = (ic <= ir, ic >= ir)
    strict = (ic < ir, ic > ir)
    tri = tuple(jnp.where(m, 1.0, 0.0).astype(BF16) for m in incl)
    for d in range(2):
        for h in range(C_HEADS):
            st_s[d, h] = st0_ref[0, d, h] if has_init else jnp.zeros((HEAD_DIM, HEAD_DIM), F32)

    def chunk(c, carry):
        c = _index32(c)
        rows = (pl.ds(pl.multiple_of(c * CHUNK, CHUNK), CHUNK),
                pl.ds(pl.multiple_of((nchunks - 1 - c) * CHUNK, CHUNK), CHUNK))
        items = []
        e_tot = []
        for d in range(2):
            ld = ld_s[d, rows[d]]
            cinc = _split_dot_left(tri[d], ld)
            cexc = cinc - ld
            ctot = cinc[0:1] if d == 1 else cinc[CHUNK - 1:CHUNK]
            e_ninc = jnp.exp(-cinc)
            e_rem = jnp.exp(ctot - cinc)
            e_tot.append(jnp.exp(ctot))
            bb = b_s[d, rows[d]]
            kt = kt_s[d, rows[d]]
            at = (aa_s[rows[d]] * jnp.exp(cexc)).astype(BF16)
            rt = (zc_ref[rows[d], 0:384] * jnp.exp(cinc)).astype(BF16)
            bt = (bb * e_ninc).astype(BF16)
            ktt = (kt * e_ninc).astype(BF16)
            bh = (bb * e_rem).astype(BF16)
            kh = (kt * e_rem).astype(BF16)
            vm = zc_ref[rows[d], 768:1152].astype(BF16)
            for h in range(C_HEADS):
                hs = slice(h * 64, (h + 1) * 64)
                items.append(dict(d=d, h=h, hs=hs, at=at[:, hs], rt=rt[:, hs], bt=bt[:, hs], kt=ktt[:, hs],
                                  bh=bh[:, hs], kh=kh[:, hs], vm=vm[:, hs]))
        for it in items:
            it["ar"] = jnp.concatenate([it["at"], it["rt"]], axis=0)
        for it in items:
            it["sb"] = _bdot_nt(it["ar"], it["bt"])
        for it in items:
            it["sk"] = _bdot_nt(it["ar"], it["kt"])
        for it in items:
            d = it["d"]
            it["l"] = jnp.where(strict[d], it["sb"][0:CHUNK], 0.0)
            it["mak"] = jnp.where(strict[d], it["sk"][0:CHUNK], 0.0).astype(BF16)
            it["nrb"] = jnp.where(incl[d], it["sb"][CHUNK:], 0.0).astype(BF16)
            it["nrk"] = jnp.where(incl[d], it["sk"][CHUNK:], 0.0).astype(BF16)
            it["tm"] = eye + jnp.where(pair_masks[0], it["l"], 0.0)
        for pm in pair_masks[1:]:
            for it in items:
                it["t1"] = _bdot(it["tm"], jnp.where(pm, it["l"], 0.0))
            for it in items:
                it["tm"] = it["tm"] + _bdot(it["t1"], it["tm"])
        for it in items:
            it["mv"] = _bdot(it["mak"], it["vm"])
        for it in items:
            it["ta"] = _bdot(it["tm"], it["at"])
        for it in items:
            it["uin"] = _bdot(it["tm"], it["mv"])
        for it in items:
            it["s0"] = st_s[it["d"], it["h"]]
            it["x"] = _bdot_nt(jnp.concatenate([it["ta"].astype(BF16), it["rt"]], axis=0), it["s0"])
        for it in items:
            it["u"] = it["x"][0:CHUNK] + it["uin"]
            it["yv"] = _bdot(it["nrk"], it["vm"])
        for it in items:
            it["y"] = it["x"][CHUNK:] + _bdot(it["nrb"], it["u"]) + it["yv"]
        for it in items:
            d = it["d"]
            st_s[d, it["h"]] = it["s0"] * e_tot[d][:, it["hs"]] + _bdot_tn(
                jnp.concatenate([it["u"].astype(BF16), it["vm"]], axis=0),
                jnp.concatenate([it["bh"], it["kh"]], axis=0))
        for d in range(2):
            ys = [it["y"] for it in items if it["d"] == d]
            y_s[rows[d]] = y_s[rows[d]] + jnp.concatenate(ys, axis=1)
        return carry

    lax.fori_loop(0, nchunks, chunk, 0)
    if want_state:
        if has_prev:
            stout_ref[0, 0] = prev_ref[0]
        for d in range(2):
            for h in range(C_HEADS):
                if has_prev:
                    stout_ref[0, 1, d, h] = st_s[d, h]
                else:
                    stout_ref[0, d, h] = st_s[d, h]

    gup = gup_ref[...].astype(BF16)
    for rb in range(t // RWKV_ROWS):
        rows = slice(rb * RWKV_ROWS, (rb + 1) * RWKV_ROWS)
        y = y_s[rows]
        mu = _split_dot(y, ones) * (1.0 / HEAD_DIM)
        yc = y - mu
        var = _split_dot(yc * yc, ones) * (1.0 / HEAD_DIM)
        yn = yc * lax.rsqrt(var + GN_EPS)
        g = _bdot(jax.nn.sigmoid(zc_ref[rows, 1280:1408]), gup)
        oc_ref[rows] = (yn * ln_w + ln_b + bon_s[rows]) * g


def _rwkv_call(zc, rvec, wup, aup, gup, layer, t, init_state, want_state, prev_state):
    nb = zc.shape[0] // t
    has_init = init_state is not None
    has_prev = prev_state is not None
    st_shape = (2, C_HEADS, HEAD_DIM, HEAD_DIM)
    in_specs = [
        pl.BlockSpec((t, RWKV_COLS), lambda b: (b, 0)),
        _layer_spec((16, C_DIM), layer),
        _layer_spec((2, W_RANK, C_DIM), layer),
        _layer_spec((2, A_RANK, C_DIM), layer),
        _layer_spec((G_RANK, C_DIM), layer),
    ]
    args = [zc, rvec, wup, aup, gup]
    if has_init:
        in_specs.append(pl.BlockSpec((1, None) + st_shape, lambda b: (b, layer, 0, 0, 0, 0)))
        args.append(init_state)
    if has_prev:
        in_specs.append(pl.BlockSpec((1,) + st_shape, lambda b: (b, 0, 0, 0, 0)))
        args.append(prev_state)
    out_shape = [jax.ShapeDtypeStruct((nb * t, C_DIM), F32)]
    out_specs = [pl.BlockSpec((t, C_DIM), lambda b: (b, 0))]
    if want_state and has_prev:
        out_shape.append(jax.ShapeDtypeStruct((nb, DEPTH) + st_shape, F32))
        out_specs.append(pl.BlockSpec((1, DEPTH) + st_shape, lambda b: (b, 0, 0, 0, 0, 0)))
    elif want_state:
        out_shape.append(jax.ShapeDtypeStruct((nb,) + st_shape, F32))
        out_specs.append(pl.BlockSpec((1,) + st_shape, lambda b: (b, 0, 0, 0, 0)))
    return pl.pallas_call(
        functools.partial(_rwkv_body, t=t, has_init=has_init, want_state=want_state, has_prev=has_prev),
        out_shape=tuple(out_shape),
        grid=(nb,),
        in_specs=in_specs,
        out_specs=tuple(out_specs),
        scratch_shapes=[
            pltpu.VMEM((t, C_DIM), F32),
            pltpu.VMEM((2, t, C_DIM), F32),
            pltpu.VMEM((2, t, C_DIM), F32),
            pltpu.VMEM((2, t, C_DIM), F32),
            pltpu.VMEM((t, C_DIM), F32),
            pltpu.VMEM((t, C_DIM), F32),
            pltpu.VMEM(st_shape, F32),
        ],
        compiler_params=pltpu.CompilerParams(vmem_limit_bytes=VMEM_LIMIT),
        name="rwkv_prompt" if want_state else "rwkv_sample",
    )(*args)


def _post_body(oa_ref, oc_ref, x_ref, mod_ref, g_ref, wout_ref, wgu_ref, wdn_ref, xo_ref, *, rows_per_mod):
    mod = _mod_row(mod_ref, rows_per_mod)
    gate1 = mod[:, 2 * D_MODEL:3 * D_MODEL]
    shift2 = mod[:, 3 * D_MODEL:4 * D_MODEL]
    scale2 = mod[:, 4 * D_MODEL:5 * D_MODEL]
    gate2 = mod[:, 5 * D_MODEL:6 * D_MODEL]
    o = (jnp.dot(oa_ref[...].astype(BF16), wout_ref[0:ATT_OUT], preferred_element_type=F32)
         + jnp.dot(oc_ref[...].astype(BF16), wout_ref[ATT_OUT:MIX_DIM], preferred_element_type=F32))
    x1 = x_ref[...] + gate1 * _rms(o, g_ref[1:2])
    h2 = (_rms(x1, g_ref[2:3]) * (1.0 + scale2) + shift2).astype(BF16)
    acc = None
    for j in range(FF_DIM // FF_CHUNK):
        lo = j * FF_CHUNK
        g = jnp.dot(h2, wgu_ref[:, lo:lo + FF_CHUNK], preferred_element_type=F32)
        u = jnp.dot(h2, wgu_ref[:, FF_DIM + lo:FF_DIM + lo + FF_CHUNK], preferred_element_type=F32)
        act = (g * jax.nn.sigmoid(g) * u).astype(BF16)
        part = jnp.dot(act, wdn_ref[lo:lo + FF_CHUNK, :], preferred_element_type=F32)
        acc = part if acc is None else acc + part
    xo_ref[...] = x1 + gate2 * _rms(acc, g_ref[3:4])


def _post_call(oatt, oc, x, mods, gvec, wout_b, wgu_b, wdn_b, layer, rows_per_mod):
    n = x.shape[0]
    return pl.pallas_call(
        functools.partial(_post_body, rows_per_mod=rows_per_mod),
        out_shape=jax.ShapeDtypeStruct((n, D_MODEL), F32),
        grid=(n // ROW_TILE,),
        in_specs=[
            pl.BlockSpec((ROW_TILE, ATT_OUT), lambda i: (i, 0)),
            pl.BlockSpec((ROW_TILE, C_DIM), lambda i: (i, 0)),
            pl.BlockSpec((ROW_TILE, D_MODEL), lambda i: (i, 0)),
            _layer_spec((MOD_ROWS, MOD_COLS), layer),
            _layer_spec((8, D_MODEL), layer),
            _layer_spec((MIX_DIM, D_MODEL), layer),
            _layer_spec((D_MODEL, 2 * FF_DIM), layer),
            _layer_spec((FF_DIM, D_MODEL), layer),
        ],
        out_specs=pl.BlockSpec((ROW_TILE, D_MODEL), lambda i: (i, 0)),
        compiler_params=pltpu.CompilerParams(vmem_limit_bytes=VMEM_LIMIT),
        name="post_ffn",
    )(oatt, oc, x, mods, gvec, wout_b, wgu_b, wdn_b)


def _rope_tables(t):
    pos = jnp.arange(t)
    row = (pos // GRID_W).astype(F32)
    col = (pos % GRID_W).astype(F32)
    freqs = ROPE_THETA ** (-jnp.arange(ROPE_PAIRS_AXIS, dtype=F32) / ROPE_PAIRS_AXIS)
    ang = jnp.concatenate([row[:, None] * freqs, col[:, None] * freqs], axis=-1)
    cos = jnp.tile(jnp.cos(ang), (1, 4))
    sin = jnp.sin(ang)
    sin = jnp.tile(jnp.concatenate([-sin, sin], axis=-1), (1, 2))
    return cos, sin


def _to_time_minor(cache):
    b, l, t = cache.shape[:3]
    return jnp.transpose(cache, (0, 1, 3, 4, 2)).reshape(b, l, KV_DIM, t)


def _from_time_minor(slab):
    b, l, _, t = slab.shape
    return jnp.transpose(slab.reshape(b, l, 2, HEAD_DIM, t), (0, 1, 4, 2, 3))


def kernel(x_prompt, x_sample, cache_a_k, cache_a_v, cache_b_k, cache_b_v, state_c, c, c_ctx, w_mod, b_mod, norm_mix_pre, norm_mix_post, norm_ffn_pre, norm_ffn_post, w_in, w_out, a_sink, b_q_norm, b_k_norm, c_w0, c_w_up, c_a0, c_a_up, c_g_up, c_k_k, c_k_a, c_r_k, c_ln_w, c_ln_b, w_gu, w_down):
    cvec = jnp.zeros((MOD_ROWS, D_MODEL), F32).at[0].set(c_ctx).at[1:1 + DEC_BATCH].set(c)
    mods = _mod_call(cvec, w_mod, b_mod)
    cos, sin = _rope_tables(DEC_SEQ)

    w_in_b = w_in.astype(BF16)
    wout_b = w_out.astype(BF16)
    wgu_b = w_gu.astype(BF16)
    wdn_b = w_down.astype(BF16)
    gvec = jnp.zeros((DEPTH, 8, D_MODEL), F32)
    gvec = gvec.at[:, 0].set(norm_mix_pre).at[:, 1].set(norm_mix_post)
    gvec = gvec.at[:, 2].set(norm_ffn_pre).at[:, 3].set(norm_ffn_post)
    rvec = jnp.zeros((DEPTH, 16, C_DIM), F32)
    rvec = rvec.at[:, 0].set(c_k_k).at[:, 1].set(c_k_a).at[:, 2].set(c_r_k.reshape(DEPTH, C_DIM))
    rvec = rvec.at[:, 3].set(c_ln_w).at[:, 4].set(c_ln_b)
    rvec = rvec.at[:, 5:7].set(c_w0).at[:, 7:9].set(c_a0)
    gq = jnp.tile(b_q_norm, (1, B_HEADS))[:, None, :]
    gk = jnp.tile(b_k_norm, (1, B_KV_HEADS))[:, None, :]
    caches = tuple(_to_time_minor(t) for t in (cache_a_k, cache_a_v, cache_b_k, cache_b_v))

    xp = x_prompt.reshape(BATCH * SEQ, D_MODEL)
    xs = x_sample.reshape(DEC_BATCH * DEC_SEQ, D_MODEL)
    kv_prev = st_prev = None
    for l in range(DEPTH):
        za, zc = _proj_in_call(xp, mods, gvec, w_in_b, l, None)
        oatt, *kv_prev = _attn_prompt_call(za, a_sink, gq, gk, l, kv_prev)
        if l == 0:
            (kv_prev,) = kv_prev
        oc, st_prev = _rwkv_call(zc, rvec, c_w_up, c_a_up, c_g_up, l, SEQ, None, True, st_prev)
        xp = _post_call(oatt, oc, xp, mods, gvec, wout_b, wgu_b, wdn_b, l, None)

        za, zc = _proj_in_call(xs, mods, gvec, w_in_b, l, DEC_SEQ)
        oatt = _attn_sample_call(za, a_sink, caches, cos, sin, gq, gk, l)
        (oc,) = _rwkv_call(zc, rvec, c_w_up, c_a_up, c_g_up, l, DEC_SEQ, state_c, False, None)
        xs = _post_call(oatt, oc, xs, mods, gvec, wout_b, wgu_b, wdn_b, l, DEC_SEQ)

    new_caches = tuple(_from_time_minor(slab) for slab in kv_prev)
    return (xp.reshape(BATCH, SEQ, D_MODEL), xs.reshape(DEC_BATCH, DEC_SEQ, D_MODEL), *new_caches, st_prev)
```

```python
import functools
import math

import jax
import jax.numpy as jnp
from jax import lax
from jax.experimental import pallas as pl
from jax.experimental.pallas import tpu as pltpu

D_MODEL = 1024
BATCH = 32
SEQ = 256
DEPTH = 2
DEC_BATCH = 2
DEC_SEQ = 1024
PAST_LEN = 256
GRID_W = 64
HEAD_DIM = 64
A_HEADS = 4
A_KV_HEADS = 2
B_HEADS = 6
B_KV_HEADS = 2
C_HEADS = 6
A_DIM = A_HEADS * HEAD_DIM
B_DIM = B_HEADS * HEAD_DIM
C_DIM = C_HEADS * HEAD_DIM
MIX_DIM = A_DIM + B_DIM + C_DIM
WINDOW = 128
Q_BLK = 128
W_RANK = 64
A_RANK = 64
G_RANK = 128
FF_DIM = -(-8 * D_MODEL // (3 * 256)) * 256
ROPE_THETA = 10000.0
ROPE_PAIRS_AXIS = HEAD_DIM // 4
NORM_EPS = 1e-6
GN_EPS = 64e-5
NEG_INF = -1e30

KV_DIM = 2 * HEAD_DIM
ATT_COLS = A_DIM + 2 * KV_DIM + B_DIM + 2 * KV_DIM
RWKV_COLS = 3 * C_DIM + W_RANK + A_RANK + G_RANK
IN_COLS = ATT_COLS + RWKV_COLS
ATT_OUT = A_DIM + B_DIM
MOD_COLS = 6 * D_MODEL
MOD_ROWS = 8
N_CACHE = 4
CHUNK = 64
QK_SCALE = HEAD_DIM ** -0.5
DECAY_RATE = math.exp(-0.5)
ROW_TILE = 512
RWKV_ROWS = 256
RWKV_BATCH = 2
B_QROWS = 256
FF_CHUNK = FF_DIM // 2
VMEM_LIMIT = 56 * 1024 * 1024

F32 = jnp.float32
BF16 = jnp.bfloat16

assert DEPTH == 2


def _bdot(a, b):
    return jnp.dot(a.astype(BF16), b.astype(BF16), preferred_element_type=F32)


def _bdot_nt(a, b):
    return lax.dot_general(a.astype(BF16), b.astype(BF16), (((1,), (1,)), ((), ())), preferred_element_type=F32)


def _bdot_tn(a, b):
    return lax.dot_general(a.astype(BF16), b.astype(BF16), (((0,), (0,)), ((), ())), preferred_element_type=F32)


def _split(x):
    hi = x.astype(BF16)
    lo = (x - hi.astype(F32)).astype(BF16)
    return hi, lo


def _split_dot(x, m):
    hi, lo = _split(x)
    return jnp.dot(hi, m, preferred_element_type=F32) + jnp.dot(lo, m, preferred_element_type=F32)


def _split_dot_left(m, x):
    hi, lo = _split(x)
    return jnp.dot(m, hi, preferred_element_type=F32) + jnp.dot(m, lo, preferred_element_type=F32)


def _head_ones(n):
    r = lax.shift_right_logical(lax.broadcasted_iota(jnp.int32, (n, n), 0), jnp.int32(6))
    c = lax.shift_right_logical(lax.broadcasted_iota(jnp.int32, (n, n), 1), jnp.int32(6))
    return jnp.where(r == c, 1.0, 0.0).astype(BF16)


def _rms(x, g):
    ms = jnp.mean(x * x, axis=-1, keepdims=True)
    return x * lax.rsqrt(ms + NORM_EPS) * g


def _head_rms(x, g, ones):
    ms = _split_dot(x * x, ones) * (1.0 / HEAD_DIM)
    return x * lax.rsqrt(ms + NORM_EPS) * g


def _rope(x, cos, sin):
    t = x.shape[0]
    lane = lax.broadcasted_iota(jnp.int32, (t, 128), 1)
    first = (lane & 32) == 0
    outs = []
    for i in range(x.shape[1] // 128):
        xs = x[:, i * 128:(i + 1) * 128]
        swapped = jnp.where(first, pltpu.roll(xs, 96, axis=1), pltpu.roll(xs, 32, axis=1))
        outs.append(xs * cos + swapped * sin)
    return outs[0] if len(outs) == 1 else jnp.concatenate(outs, axis=1)


def _softmax_pv(scores, values, sink=None):
    m = scores[0].max(axis=-1, keepdims=True)
    for s in scores[1:]:
        m = jnp.maximum(m, s.max(axis=-1, keepdims=True))
    if sink is not None:
        m = jnp.maximum(m, sink)
    denom = None
    acc = None
    for s, v in zip(scores, values):
        p = jnp.exp(s - m)
        d = p.sum(axis=-1, keepdims=True)
        o = jnp.dot(p.astype(BF16), v, preferred_element_type=F32)
        denom = d if denom is None else denom + d
        acc = o if acc is None else acc + o
    if sink is not None:
        denom = denom + jnp.exp(sink - m)
    return acc / denom


def _attend_heads(score_fns, finish_fns):
    outs = []
    nxt = score_fns[0]()
    for i in range(len(score_fns)):
        cur = nxt
        if i + 1 < len(score_fns):
            nxt = score_fns[i + 1]()
        outs.append(finish_fns[i](cur))
    return outs


def _index32(i):
    return jnp.asarray(i, jnp.int32)


def _layer_spec(shape, layer):
    zeros = (0,) * len(shape)
    return pl.BlockSpec((None,) + shape, lambda *_: (layer,) + zeros, pipeline_mode=pl.Buffered(1))


def _mod_row(mod_ref, rows_per_mod):
    if rows_per_mod is None:
        return mod_ref[0:1]
    row = 1 + pl.program_id(0) // (rows_per_mod // ROW_TILE)
    return mod_ref[pl.ds(row, 1), :]


def _mod_body(c_ref, w_ref, b_ref, o_ref):
    cv = c_ref[...]
    s = cv * jax.nn.sigmoid(cv)
    o_ref[0] = _bdot(s, w_ref[0]) + b_ref[0]


def _mod_call(cvec, w_mod, b_mod):
    bn = MOD_COLS // 4
    return pl.pallas_call(
        _mod_body,
        out_shape=jax.ShapeDtypeStruct((DEPTH, MOD_ROWS, MOD_COLS), F32),
        grid=(DEPTH, MOD_COLS // bn),
        in_specs=[
            pl.BlockSpec((MOD_ROWS, D_MODEL), lambda l, j: (0, 0)),
            pl.BlockSpec((1, D_MODEL, bn), lambda l, j: (l, 0, j)),
            pl.BlockSpec((1, 1, bn), lambda l, j: (l, 0, j)),
        ],
        out_specs=pl.BlockSpec((1, MOD_ROWS, bn), lambda l, j: (l, 0, j)),
        compiler_params=pltpu.CompilerParams(vmem_limit_bytes=VMEM_LIMIT),
        name="mod_vectors",
    )(cvec, w_mod, b_mod.reshape(DEPTH, 1, MOD_COLS))


def _proj_in_body(x_ref, mod_ref, g_ref, w_ref, za_ref, zc_ref, *, rows_per_mod):
    mod = _mod_row(mod_ref, rows_per_mod)
    shift1 = mod[:, 0:D_MODEL]
    scale1 = mod[:, D_MODEL:2 * D_MODEL]
    h = (_rms(x_ref[...], g_ref[0:1]) * (1.0 + scale1) + shift1).astype(BF16)
    za_ref[...] = jnp.dot(h, w_ref[:, 0:ATT_COLS], preferred_element_type=F32)
    zc_ref[...] = jnp.dot(h, w_ref[:, ATT_COLS:IN_COLS], preferred_element_type=F32)


def _proj_in_call(x, mods, gvec, w_in_b, layer, rows_per_mod):
    n = x.shape[0]
    return pl.pallas_call(
        functools.partial(_proj_in_body, rows_per_mod=rows_per_mod),
        out_shape=(jax.ShapeDtypeStruct((n, ATT_COLS), F32), jax.ShapeDtypeStruct((n, RWKV_COLS), F32)),
        grid=(n // ROW_TILE,),
        in_specs=[
            pl.BlockSpec((ROW_TILE, D_MODEL), lambda i: (i, 0)),
            _layer_spec((MOD_ROWS, MOD_COLS), layer),
            _layer_spec((8, D_MODEL), layer),
            _layer_spec((D_MODEL, IN_COLS), layer),
        ],
        out_specs=(pl.BlockSpec((ROW_TILE, ATT_COLS), lambda i: (i, 0)),
                   pl.BlockSpec((ROW_TILE, RWKV_COLS), lambda i: (i, 0))),
        compiler_params=pltpu.CompilerParams(vmem_limit_bytes=VMEM_LIMIT),
        name="proj_in",
    )(x, mods, gvec, w_in_b)


def _attn_prompt_body(*refs, layer):
    sink_ref, za_ref, gq_ref, gk_ref = refs[:4]
    if layer == 0:
        o_ref, kv_ref = refs[4:]
    else:
        prev_ref, o_ref = refs[4:6]
        cache_refs = refs[6:]
    aq = (za_ref[:, 0:A_DIM] * QK_SCALE).astype(BF16)
    ak = za_ref[:, 256:384]
    av = za_ref[:, 384:512]
    akb = ak.astype(BF16)
    avb = av.astype(BF16)
    bq = _head_rms(za_ref[:, 512:896], gq_ref[...], _head_ones(B_DIM))
    bk = _head_rms(za_ref[:, 896:1024], gk_ref[...], _head_ones(KV_DIM))
    bv = za_ref[:, 1024:1152]
    bqb = (bq * QK_SCALE).astype(BF16)
    bkb = bk.astype(BF16)
    bvb = bv.astype(BF16)

    score_fns, finish_fns = [], []
    for h in range(A_HEADS):
        kv = h // (A_HEADS // A_KV_HEADS)
        hs = slice(kv * 64, (kv + 1) * 64)
        score_fns.append(lambda h=h, hs=hs: [_bdot_nt(aq[:, h * 64:(h + 1) * 64], akb[:, hs])])
        finish_fns.append(lambda sc, h=h, hs=hs: _softmax_pv(sc, [avb[:, hs]], sink_ref[layer, h]))
    for h in range(B_HEADS):
        kv = h // (B_HEADS // B_KV_HEADS)
        hs = slice(kv * 64, (kv + 1) * 64)
        score_fns.append(lambda h=h, hs=hs: [_bdot_nt(bqb[:, h * 64:(h + 1) * 64], bkb[:, hs])])
        finish_fns.append(lambda sc, hs=hs: _softmax_pv(sc, [bvb[:, hs]]))
    o_ref[...] = jnp.concatenate(_attend_heads(score_fns, finish_fns), axis=1)

    for j, piece in enumerate((ak, av, bk, bv)):
        if layer == 0:
            kv_ref[0, j] = piece.T
        else:
            cache_refs[j][0, 0] = prev_ref[0, j]
            cache_refs[j][0, 1] = piece.T


def _attn_prompt_call(za, sink, gq, gk, layer, prev_kv):
    nb = za.shape[0] // SEQ
    in_specs = [
        pl.BlockSpec(memory_space=pltpu.SMEM),
        pl.BlockSpec((SEQ, ATT_COLS), lambda b: (b, 0)),
        _layer_spec((1, B_DIM), layer),
        _layer_spec((1, KV_DIM), layer),
    ]
    args = [sink, za, gq, gk]
    out_shape = [jax.ShapeDtypeStruct((nb * SEQ, ATT_OUT), F32)]
    out_specs = [pl.BlockSpec((SEQ, ATT_OUT), lambda b: (b, 0))]
    slab_spec = pl.BlockSpec((1, N_CACHE, KV_DIM, SEQ), lambda b: (b, 0, 0, 0))
    if layer == 0:
        out_shape.append(jax.ShapeDtypeStruct((nb, N_CACHE, KV_DIM, SEQ), F32))
        out_specs.append(slab_spec)
    else:
        in_specs.append(slab_spec)
        args.append(prev_kv)
        for _ in range(N_CACHE):
            out_shape.append(jax.ShapeDtypeStruct((nb, DEPTH, KV_DIM, SEQ), F32))
            out_specs.append(pl.BlockSpec((1, DEPTH, KV_DIM, SEQ), lambda b: (b, 0, 0, 0)))
    return pl.pallas_call(
        functools.partial(_attn_prompt_body, layer=layer),
        out_shape=tuple(out_shape),
        grid=(nb,),
        in_specs=in_specs,
        out_specs=tuple(out_specs),
        compiler_params=pltpu.CompilerParams(vmem_limit_bytes=VMEM_LIMIT),
        name="attn_prompt",
    )(*args)


def _attn_sample_body(sink_ref, za_ref, cak_ref, cav_ref, cbk_ref, cbv_ref, cos_ref, sin_ref, gq_ref, gk_ref,
                      o_ref, qa_s, ka_s, va_s, qb_s, kb_s, vb_s, *, layer):
    t = DEC_SEQ
    cos = cos_ref[...]
    sin = sin_ref[...]
    qa_s[...] = (_rope(za_ref[:, 0:A_DIM], cos, sin) * QK_SCALE).astype(BF16)
    zpad = jnp.zeros((Q_BLK, KV_DIM), BF16)
    ka_s[0:Q_BLK] = zpad
    ka_s[Q_BLK + t:] = zpad
    va_s[0:Q_BLK] = zpad
    va_s[Q_BLK + t:] = zpad
    ka_s[Q_BLK:Q_BLK + t] = _rope(za_ref[:, 256:384], cos, sin).astype(BF16)
    va_s[Q_BLK:Q_BLK + t] = za_ref[:, 384:512].astype(BF16)
    bq = _head_rms(za_ref[:, 512:896], gq_ref[...], _head_ones(B_DIM))
    qb_s[...] = (_rope(bq, cos, sin) * QK_SCALE).astype(BF16)
    bk = _head_rms(za_ref[:, 896:1024], gk_ref[...], _head_ones(KV_DIM))
    kb_s[0:PAST_LEN] = cbk_ref[0].T.astype(BF16)
    kb_s[PAST_LEN:] = _rope(bk, cos, sin).astype(BF16)
    vb_s[0:PAST_LEN] = cbv_ref[0].T.astype(BF16)
    vb_s[PAST_LEN:] = za_ref[:, 1024:1152].astype(BF16)

    kca = cak_ref[0].T.astype(BF16)
    vca = cav_ref[0].T.astype(BF16)

    def a_block(n, carry):
        r0 = pl.multiple_of(_index32(n) * Q_BLK, Q_BLK)
        q = qa_s[pl.ds(r0, Q_BLK), :]
        kl = ka_s[pl.ds(r0, 3 * Q_BLK), :]
        vl = va_s[pl.ds(r0, 3 * Q_BLK), :]
        qpos = r0 + lax.broadcasted_iota(jnp.int32, (Q_BLK, 3 * Q_BLK), 0)
        kpos = r0 - Q_BLK + lax.broadcasted_iota(jnp.int32, (Q_BLK, 3 * Q_BLK), 1)
        mask = (jnp.abs(kpos - qpos) <= WINDOW) & (kpos >= 0) & (kpos < t)
        score_fns, finish_fns = [], []
        for h in range(A_HEADS):
            kv = h // (A_HEADS // A_KV_HEADS)
            hs = slice(kv * 64, (kv + 1) * 64)

            def scores(h=h, hs=hs):
                qh = q[:, h * 64:(h + 1) * 64]
                return [jnp.where(mask, _bdot_nt(qh, kl[:, hs]), NEG_INF), _bdot_nt(qh, kca[:, hs])]

            score_fns.append(scores)
            finish_fns.append(lambda sc, h=h, hs=hs: _softmax_pv(sc, [vl[:, hs], vca[:, hs]], sink_ref[layer, h]))
        o_ref[pl.ds(r0, Q_BLK), 0:A_DIM] = jnp.concatenate(_attend_heads(score_fns, finish_fns), axis=1)
        return carry

    lax.fori_loop(0, t // Q_BLK, a_block, 0)

    def b_block(n, carry):
        r0 = pl.multiple_of(_index32(n) * B_QROWS, B_QROWS)
        q = qb_s[pl.ds(r0, B_QROWS), :]
        score_fns, finish_fns = [], []
        for h in range(B_HEADS):
            kv = h // (B_HEADS // B_KV_HEADS)
            hs = slice(kv * 64, (kv + 1) * 64)
            score_fns.append(lambda h=h, hs=hs: [_bdot_nt(q[:, h * 64:(h + 1) * 64], kb_s[:, hs])])
            finish_fns.append(lambda sc, hs=hs: _softmax_pv(sc, [vb_s[:, hs]]))
        o_ref[pl.ds(r0, B_QROWS), A_DIM:ATT_OUT] = jnp.concatenate(_attend_heads(score_fns, finish_fns), axis=1)
        return carry

    lax.fori_loop(0, t // B_QROWS, b_block, 0)


def _attn_sample_call(za, sink, caches, cos, sin, gq, gk, layer):
    nb = za.shape[0] // DEC_SEQ
    t = DEC_SEQ
    cache_spec = pl.BlockSpec((1, None, KV_DIM, PAST_LEN), lambda b: (b, layer, 0, 0))
    return pl.pallas_call(
        functools.partial(_attn_sample_body, layer=layer),
        out_shape=jax.ShapeDtypeStruct((nb * t, ATT_OUT), F32),
        grid=(nb,),
        in_specs=[
            pl.BlockSpec(memory_space=pltpu.SMEM),
            pl.BlockSpec((t, ATT_COLS), lambda b: (b, 0)),
            cache_spec, cache_spec, cache_spec, cache_spec,
            pl.BlockSpec((t, 128), lambda b: (0, 0)),
            pl.BlockSpec((t, 128), lambda b: (0, 0)),
            _layer_spec((1, B_DIM), layer),
            _layer_spec((1, KV_DIM), layer),
        ],
        out_specs=pl.BlockSpec((t, ATT_OUT), lambda b: (b, 0)),
        scratch_shapes=[
            pltpu.VMEM((t, A_DIM), BF16),
            pltpu.VMEM((t + 2 * Q_BLK, KV_DIM), BF16),
            pltpu.VMEM((t + 2 * Q_BLK, KV_DIM), BF16),
            pltpu.VMEM((t, B_DIM), BF16),
            pltpu.VMEM((t + PAST_LEN, KV_DIM), BF16),
            pltpu.VMEM((t + PAST_LEN, KV_DIM), BF16),
        ],
        compiler_params=pltpu.CompilerParams(vmem_limit_bytes=VMEM_LIMIT),
        name="attn_sample",
    )(sink, za, *caches, cos, sin, gq, gk)


def _rwkv_body(*refs, t, nbatch, has_init, want_state, has_prev):
    refs = list(refs)
    zc_ref, cv_ref, wup_ref, aup_ref, gup_ref = refs[:5]
    pos = 5
    st0_ref = prev_ref = stout_ref = None
    if has_init:
        st0_ref = refs[pos]
        pos += 1
    if has_prev:
        prev_ref = refs[pos]
        pos += 1
    oc_ref = refs[pos]
    pos += 1
    if want_state:
        stout_ref = refs[pos]
        pos += 1
    aa_s, ld_s, b_s, kt_s, bon_s, y_s, st_s = refs[pos:]

    nchunks = t // CHUNK
    ones = _head_ones(C_DIM)
    k_k = cv_ref[0:1]
    k_a = cv_ref[1:2]
    r_k = cv_ref[2:3]
    ln_w = cv_ref[3:4]
    ln_b = cv_ref[4:5]
    w0 = (cv_ref[5:6], cv_ref[6:7])
    a0 = (cv_ref[7:8], cv_ref[8:9])
    wup = (wup_ref[0].astype(BF16), wup_ref[1].astype(BF16))
    aup = (aup_ref[0].astype(BF16), aup_ref[1].astype(BF16))

    for rb in range(nbatch * t // RWKV_ROWS):
        rows = slice(rb * RWKV_ROWS, (rb + 1) * RWKV_ROWS)
        zr = zc_ref[rows, 0:384]
        zk = zc_ref[rows, 384:768]
        zv = zc_ref[rows, 768:1152]
        tw = jnp.tanh(zc_ref[rows, 1152:1216])
        xa = zc_ref[rows, 1216:1280]
        kkr = zk * k_k
        kk = kkr * lax.rsqrt(_split_dot(kkr * kkr, ones) + 1e-12)
        aa_s[rows] = -kk
        bon = None
        for d in range(2):
            wl = w0[d] + _bdot(tw, wup[d])
            ld_s[d, rows] = (-DECAY_RATE) * jax.nn.sigmoid(wl)
            a = jax.nn.sigmoid(a0[d] + _bdot(xa, aup[d]))
            kt = zk * (1.0 + (a - 1.0) * k_a)
            kt_s[d, rows] = kt
            b_s[d, rows] = kk * a
            bd = _split_dot(zr * kt * r_k, ones) * zv
            bon = bd if bon is None else bon + bd
        bon_s[rows] = bon
        y_s[rows] = jnp.zeros((RWKV_ROWS, C_DIM), F32)

    ir = lax.broadcasted_iota(jnp.int32, (CHUNK, 128), 0)
    ic = lax.broadcasted_iota(jnp.int32, (CHUNK, 128), 1) & 63
    eye = jnp.where(ir == ic, 1.0, 0.0).astype(F32)
    pair_masks = []
    for j in range(CHUNK.bit_length() - 1):
        same_big = lax.shift_right_logical(ir, jnp.int32(j + 1)) == lax.shift_right_logical(ic, jnp.int32(j + 1))
        same_small = lax.shift_right_logical(ir, jnp.int32(j)) == lax.shift_right_logical(ic, jnp.int32(j))
        pair_masks.append(same_big & jnp.logical_not(same_small))
    incl = (ic <= ir, ic >= ir)
    strict = (ic < ir, ic > ir)
    sr = lax.broadcasted_iota(jnp.int32, (CHUNK, CHUNK), 0)
    sc = lax.broadcasted_iota(jnp.int32, (CHUNK, CHUNK), 1)
    tri = tuple(jnp.where(m, 1.0, 0.0).astype(BF16) for m in (sc <= sr, sc >= sr))
    br = lax.broadcasted_iota(jnp.int32, (128, 128), 0)
    bc = lax.broadcasted_iota(jnp.int32, (128, 128), 1)
    diag_blocks = (br < 64) == (bc < 64)
    left_half = lax.broadcasted_iota(jnp.int32, (CHUNK, 128), 1) < 64

    def blockdiag(x2):
        x2 = x2.astype(BF16)
        return jnp.where(diag_blocks, jnp.concatenate([x2, x2], axis=0), jnp.zeros((128, 128), BF16))

    npairs = C_HEADS // 2
    for bi in range(nbatch):
        for d in range(2):
            for p in range(npairs):
                if has_init:
                    st_s[bi, d, p] = jnp.concatenate([st0_ref[bi, d, 2 * p], st0_ref[bi, d, 2 * p + 1]], axis=1)
                else:
                    st_s[bi, d, p] = jnp.zeros((HEAD_DIM, 128), F32)

    def chunk(c, carry):
        c = _index32(c)
        items = []
        for bi, d in [(bi, d) for bi in range(nbatch) for d in range(2)]:
            r0 = bi * t + ((nchunks - 1 - c) if d == 1 else c) * CHUNK
            rows = pl.ds(pl.multiple_of(r0, CHUNK), CHUNK)
            ld = ld_s[d, rows]
            cinc = _split_dot_left(tri[d], ld)
            cexc = cinc - ld
            ctot = cinc[0:1] if d == 1 else cinc[CHUNK - 1:CHUNK]
            e_ninc = jnp.exp(-cinc)
            e_rem = jnp.exp(ctot - cinc)
            e_tot = jnp.exp(ctot)
            bb = b_s[d, rows]
            kt = kt_s[d, rows]
            at = (aa_s[rows] * jnp.exp(cexc)).astype(BF16)
            rt = (zc_ref[rows, 0:384] * jnp.exp(cinc)).astype(BF16)
            bt = (bb * e_ninc).astype(BF16)
            ktt = (kt * e_ninc).astype(BF16)
            bh = (bb * e_rem).astype(BF16)
            kh = (kt * e_rem).astype(BF16)
            vm = zc_ref[rows, 768:1152].astype(BF16)
            for p in range(npairs):
                ps = slice(p * 128, (p + 1) * 128)
                items.append(dict(bi=bi, d=d, p=p, rows=rows, e_tot=e_tot[:, ps], at=at[:, ps], rt=rt[:, ps],
                                  bt=bt[:, ps], kt=ktt[:, ps], bh=bh[:, ps], kh=kh[:, ps], vm=vm[:, ps]))
        for it in items:
            it["ar"] = jnp.concatenate([it["at"], it["rt"]], axis=0)
            it["vmd"] = blockdiag(it["vm"])
        for it in items:
            it["sb"] = _bdot_nt(it["ar"], blockdiag(it["bt"]))
        for it in items:
            it["sk"] = _bdot_nt(it["ar"], blockdiag(it["kt"]))
        for it in items:
            d = it["d"]
            it["l"] = jnp.where(strict[d], it["sb"][0:CHUNK], 0.0)
            it["mak"] = jnp.where(strict[d], it["sk"][0:CHUNK], 0.0).astype(BF16)
            it["nr"] = jnp.concatenate([jnp.where(incl[d], it["sb"][CHUNK:], 0.0).astype(BF16),
                                        jnp.where(incl[d], it["sk"][CHUNK:], 0.0).astype(BF16)], axis=1)
            it["tm"] = eye + jnp.where(pair_masks[0], it["l"], 0.0)
        for pm in pair_masks[1:]:
            for it in items:
                it["t1"] = _bdot(it["tm"], blockdiag(jnp.where(pm, it["l"], 0.0)))
            for it in items:
                it["tm"] = it["tm"] + _bdot(it["t1"], blockdiag(it["tm"]))
        for it in items:
            it["mv"] = _bdot(it["mak"], it["vmd"])
        for it in items:
            it["tau"] = _bdot(it["tm"], jnp.concatenate([blockdiag(it["at"]), blockdiag(it["mv"])], axis=1))
        for it in items:
            it["s0"] = st_s[it["bi"], it["d"], it["p"]]
            it["x"] = _bdot_nt(jnp.concatenate([it["tau"][:, 0:128].astype(BF16), it["rt"]], axis=0),
                               blockdiag(it["s0"]))
        for it in items:
            it["u"] = (it["x"][0:CHUNK] + it["tau"][:, 128:256]).astype(BF16)
        for it in items:
            it["y"] = it["x"][CHUNK:] + _bdot(it["nr"], jnp.concatenate([blockdiag(it["u"]), it["vmd"]], axis=0))
        for it in items:
            full = _bdot_tn(jnp.concatenate([it["u"], it["vm"]], axis=0),
                            jnp.concatenate([it["bh"], it["kh"]], axis=0))
            st_s[it["bi"], it["d"], it["p"]] = (it["s0"] * it["e_tot"]
                                                + jnp.where(left_half, full[0:64], full[64:128]))
        for i in range(0, len(items), npairs):
            rows = items[i]["rows"]
            y_s[rows] = y_s[rows] + jnp.concatenate([it["y"] for it in items[i:i + npairs]], axis=1)
        return carry

    lax.fori_loop(0, nchunks, chunk, 0)
    if want_state:
        for bi in range(nbatch):
            if has_prev:
                stout_ref[bi, 0] = prev_ref[bi]
            for d in range(2):
                for p in range(npairs):
                    s2 = st_s[bi, d, p]
                    for q in range(2):
                        if has_prev:
                            stout_ref[bi, 1, d, 2 * p + q] = s2[:, q * 64:(q + 1) * 64]
                        else:
                            stout_ref[bi, d, 2 * p + q] = s2[:, q * 64:(q + 1) * 64]

    gup = gup_ref[...].astype(BF16)
    for rb in range(nbatch * t // RWKV_ROWS):
        rows = slice(rb * RWKV_ROWS, (rb + 1) * RWKV_ROWS)
        y = y_s[rows]
        mu = _split_dot(y, ones) * (1.0 / HEAD_DIM)
        yc = y - mu
        var = _split_dot(yc * yc, ones) * (1.0 / HEAD_DIM)
        yn = yc * lax.rsqrt(var + GN_EPS)
        g = _bdot(jax.nn.sigmoid(zc_ref[rows, 1280:1408]), gup)
        oc_ref[rows] = (yn * ln_w + ln_b + bon_s[rows]) * g


def _rwkv_call(zc, rvec, wup, aup, gup, layer, t, init_state, want_state, prev_state):
    nb = zc.shape[0] // t
    nbatch = RWKV_BATCH
    rows = nbatch * t
    has_init = init_state is not None
    has_prev = prev_state is not None
    st_shape = (2, C_HEADS, HEAD_DIM, HEAD_DIM)
    in_specs = [
        pl.BlockSpec((rows, RWKV_COLS), lambda b: (b, 0), pipeline_mode=pl.Buffered(1 if nb == nbatch else 2)),
        _layer_spec((16, C_DIM), layer),
        _layer_spec((2, W_RANK, C_DIM), layer),
        _layer_spec((2, A_RANK, C_DIM), layer),
        _layer_spec((G_RANK, C_DIM), layer),
    ]
    args = [zc, rvec, wup, aup, gup]
    if has_init:
        in_specs.append(pl.BlockSpec((nbatch, None) + st_shape, lambda b: (b, layer, 0, 0, 0, 0)))
        args.append(init_state)
    if has_prev:
        in_specs.append(pl.BlockSpec((nbatch,) + st_shape, lambda b: (b, 0, 0, 0, 0)))
        args.append(prev_state)
    out_shape = [jax.ShapeDtypeStruct((nb * t, C_DIM), F32)]
    out_specs = [pl.BlockSpec((rows, C_DIM), lambda b: (b, 0))]
    if want_state and has_prev:
        out_shape.append(jax.ShapeDtypeStruct((nb, DEPTH) + st_shape, F32))
        out_specs.append(pl.BlockSpec((nbatch, DEPTH) + st_shape, lambda b: (b, 0, 0, 0, 0, 0)))
    elif want_state:
        out_shape.append(jax.ShapeDtypeStruct((nb,) + st_shape, F32))
        out_specs.append(pl.BlockSpec((nbatch,) + st_shape, lambda b: (b, 0, 0, 0, 0)))
    return pl.pallas_call(
        functools.partial(_rwkv_body, t=t, nbatch=nbatch, has_init=has_init, want_state=want_state,
                          has_prev=has_prev),
        out_shape=tuple(out_shape),
        grid=(nb // nbatch,),
        in_specs=in_specs,
        out_specs=tuple(out_specs),
        scratch_shapes=[
            pltpu.VMEM((rows, C_DIM), F32),
            pltpu.VMEM((2, rows, C_DIM), F32),
            pltpu.VMEM((2, rows, C_DIM), F32),
            pltpu.VMEM((2, rows, C_DIM), F32),
            pltpu.VMEM((rows, C_DIM), F32),
            pltpu.VMEM((rows, C_DIM), F32),
            pltpu.VMEM((nbatch, 2, C_HEADS // 2, HEAD_DIM, 128), F32),
        ],
        compiler_params=pltpu.CompilerParams(vmem_limit_bytes=VMEM_LIMIT),
        name="rwkv_prompt" if want_state else "rwkv_sample",
    )(*args)


def _post_body(oa_ref, oc_ref, x_ref, mod_ref, g_ref, wout_ref, wgu_ref, wdn_ref, xo_ref, *, rows_per_mod):
    mod = _mod_row(mod_ref, rows_per_mod)
    gate1 = mod[:, 2 * D_MODEL:3 * D_MODEL]
    shift2 = mod[:, 3 * D_MODEL:4 * D_MODEL]
    scale2 = mod[:, 4 * D_MODEL:5 * D_MODEL]
    gate2 = mod[:, 5 * D_MODEL:6 * D_MODEL]
    o = (jnp.dot(oa_ref[...].astype(BF16), wout_ref[0:ATT_OUT], preferred_element_type=F32)
         + jnp.dot(oc_ref[...].astype(BF16), wout_ref[ATT_OUT:MIX_DIM], preferred_element_type=F32))
    x1 = x_ref[...] + gate1 * _rms(o, g_ref[1:2])
    h2 = (_rms(x1, g_ref[2:3]) * (1.0 + scale2) + shift2).astype(BF16)
    acc = None
    for j in range(FF_DIM // FF_CHUNK):
        lo = j * FF_CHUNK
        g = jnp.dot(h2, wgu_ref[:, lo:lo + FF_CHUNK], preferred_element_type=F32)
        u = jnp.dot(h2, wgu_ref[:, FF_DIM + lo:FF_DIM + lo + FF_CHUNK], preferred_element_type=F32)
        act = (g * jax.nn.sigmoid(g) * u).astype(BF16)
        part = jnp.dot(act, wdn_ref[lo:lo + FF_CHUNK, :], preferred_element_type=F32)
        acc = part if acc is None else acc + part
    xo_ref[...] = x1 + gate2 * _rms(acc, g_ref[3:4])


def _post_call(oatt, oc, x, mods, gvec, wout_b, wgu_b, wdn_b, layer, rows_per_mod):
    n = x.shape[0]
    return pl.pallas_call(
        functools.partial(_post_body, rows_per_mod=rows_per_mod),
        out_shape=jax.ShapeDtypeStruct((n, D_MODEL), F32),
        grid=(n // ROW_TILE,),
        in_specs=[
            pl.BlockSpec((ROW_TILE, ATT_OUT), lambda i: (i, 0)),
            pl.BlockSpec((ROW_TILE, C_DIM), lambda i: (i, 0)),
            pl.BlockSpec((ROW_TILE, D_MODEL), lambda i: (i, 0)),
            _layer_spec((MOD_ROWS, MOD_COLS), layer),
            _layer_spec((8, D_MODEL), layer),
            _layer_spec((MIX_DIM, D_MODEL), layer),
            _layer_spec((D_MODEL, 2 * FF_DIM), layer),
            _layer_spec((FF_DIM, D_MODEL), layer),
        ],
        out_specs=pl.BlockSpec((ROW_TILE, D_MODEL), lambda i: (i, 0)),
        compiler_params=pltpu.CompilerParams(vmem_limit_bytes=VMEM_LIMIT),
        name="post_ffn",
    )(oatt, oc, x, mods, gvec, wout_b, wgu_b, wdn_b)


def _rope_tables(t):
    pos = jnp.arange(t)
    row = (pos // GRID_W).astype(F32)
    col = (pos % GRID_W).astype(F32)
    freqs = ROPE_THETA ** (-jnp.arange(ROPE_PAIRS_AXIS, dtype=F32) / ROPE_PAIRS_AXIS)
    ang = jnp.concatenate([row[:, None] * freqs, col[:, None] * freqs], axis=-1)
    cos = jnp.tile(jnp.cos(ang), (1, 4))
    sin = jnp.sin(ang)
    sin = jnp.tile(jnp.concatenate([-sin, sin], axis=-1), (1, 2))
    return cos, sin


def _to_time_minor(cache):
    b, l, t = cache.shape[:3]
    return jnp.transpose(cache, (0, 1, 3, 4, 2)).reshape(b, l, KV_DIM, t)


def _from_time_minor(slab):
    b, l, _, t = slab.shape
    return jnp.transpose(slab.reshape(b, l, 2, HEAD_DIM, t), (0, 1, 4, 2, 3))


def kernel(x_prompt, x_sample, cache_a_k, cache_a_v, cache_b_k, cache_b_v, state_c, c, c_ctx, w_mod, b_mod, norm_mix_pre, norm_mix_post, norm_ffn_pre, norm_ffn_post, w_in, w_out, a_sink, b_q_norm, b_k_norm, c_w0, c_w_up, c_a0, c_a_up, c_g_up, c_k_k, c_k_a, c_r_k, c_ln_w, c_ln_b, w_gu, w_down):
    cvec = jnp.zeros((MOD_ROWS, D_MODEL), F32).at[0].set(c_ctx).at[1:1 + DEC_BATCH].set(c)
    mods = _mod_call(cvec, w_mod, b_mod)
    cos, sin = _rope_tables(DEC_SEQ)

    w_in_b = w_in.astype(BF16)
    wout_b = w_out.astype(BF16)
    wgu_b = w_gu.astype(BF16)
    wdn_b = w_down.astype(BF16)
    gvec = jnp.zeros((DEPTH, 8, D_MODEL), F32)
    gvec = gvec.at[:, 0].set(norm_mix_pre).at[:, 1].set(norm_mix_post)
    gvec = gvec.at[:, 2].set(norm_ffn_pre).at[:, 3].set(norm_ffn_post)
    rvec = jnp.zeros((DEPTH, 16, C_DIM), F32)
    rvec = rvec.at[:, 0].set(c_k_k).at[:, 1].set(c_k_a).at[:, 2].set(c_r_k.reshape(DEPTH, C_DIM))
    rvec = rvec.at[:, 3].set(c_ln_w).at[:, 4].set(c_ln_b)
    rvec = rvec.at[:, 5:7].set(c_w0).at[:, 7:9].set(c_a0)
    gq = jnp.tile(b_q_norm, (1, B_HEADS))[:, None, :]
    gk = jnp.tile(b_k_norm, (1, B_KV_HEADS))[:, None, :]
    caches = tuple(_to_time_minor(t) for t in (cache_a_k, cache_a_v, cache_b_k, cache_b_v))

    xp = x_prompt.reshape(BATCH * SEQ, D_MODEL)
    xs = x_sample.reshape(DEC_BATCH * DEC_SEQ, D_MODEL)
    kv_prev = st_prev = None
    for l in range(DEPTH):
        za, zc = _proj_in_call(xp, mods, gvec, w_in_b, l, None)
        oatt, *kv_prev = _attn_prompt_call(za, a_sink, gq, gk, l, kv_prev)
        if l == 0:
            (kv_prev,) = kv_prev
        oc, st_prev = _rwkv_call(zc, rvec, c_w_up, c_a_up, c_g_up, l, SEQ, None, True, st_prev)
        xp = _post_call(oatt, oc, xp, mods, gvec, wout_b, wgu_b, wdn_b, l, None)

        za, zc = _proj_in_call(xs, mods, gvec, w_in_b, l, DEC_SEQ)
        oatt = _attn_sample_call(za, a_sink, caches, cos, sin, gq, gk, l)
        (oc,) = _rwkv_call(zc, rvec, c_w_up, c_a_up, c_g_up, l, DEC_SEQ, state_c, False, None)
        xs = _post_call(oatt, oc, xs, mods, gvec, wout_b, wgu_b, wdn_b, l, DEC_SEQ)

    new_caches = tuple(_from_time_minor(slab) for slab in kv_prev)
    return (xp.reshape(BATCH, SEQ, D_MODEL), xs.reshape(DEC_BATCH, DEC_SEQ, D_MODEL), *new_caches, st_prev)
```

```python
import functools
import math

import jax
import jax.numpy as jnp
from jax import lax
from jax.experimental import pallas as pl
from jax.experimental.pallas import tpu as pltpu

D_MODEL = 1024
BATCH = 32
SEQ = 256
DEPTH = 2
DEC_BATCH = 2
DEC_SEQ = 1024
PAST_LEN = 256
GRID_W = 64
HEAD_DIM = 64
A_HEADS = 4
A_KV_HEADS = 2
B_HEADS = 6
B_KV_HEADS = 2
C_HEADS = 6
A_DIM = A_HEADS * HEAD_DIM
B_DIM = B_HEADS * HEAD_DIM
C_DIM = C_HEADS * HEAD_DIM
MIX_DIM = A_DIM + B_DIM + C_DIM
WINDOW = 128
Q_BLK = 128
W_RANK = 64
A_RANK = 64
G_RANK = 128
FF_DIM = -(-8 * D_MODEL // (3 * 256)) * 256
ROPE_THETA = 10000.0
ROPE_PAIRS_AXIS = HEAD_DIM // 4
NORM_EPS = 1e-6
GN_EPS = 64e-5
NEG_INF = -1e30

KV_DIM = 2 * HEAD_DIM
ATT_COLS = A_DIM + 2 * KV_DIM + B_DIM + 2 * KV_DIM
RWKV_COLS = 3 * C_DIM + W_RANK + A_RANK + G_RANK
IN_COLS = ATT_COLS + RWKV_COLS
ATT_OUT = A_DIM + B_DIM
MOD_COLS = 6 * D_MODEL
MOD_ROWS = 8
N_CACHE = 4
CHUNK = 64
QK_SCALE = HEAD_DIM ** -0.5
DECAY_RATE = math.exp(-0.5)
ROW_TILE = 512
RWKV_ROWS = 256
RWKV_BATCH = 2
B_QROWS = 256
FF_CHUNK = FF_DIM // 2
VMEM_LIMIT = 56 * 1024 * 1024

F32 = jnp.float32
BF16 = jnp.bfloat16

assert DEPTH == 2


def _bdot(a, b):
    return jnp.dot(a.astype(BF16), b.astype(BF16), preferred_element_type=F32)


def _bdot_nt(a, b):
    return lax.dot_general(a.astype(BF16), b.astype(BF16), (((1,), (1,)), ((), ())), preferred_element_type=F32)


def _bdot_tn(a, b):
    return lax.dot_general(a.astype(BF16), b.astype(BF16), (((0,), (0,)), ((), ())), preferred_element_type=F32)


def _split(x):
    hi = x.astype(BF16)
    lo = (x - hi.astype(F32)).astype(BF16)
    return hi, lo


def _split_dot_left(m, x):
    hi, lo = _split(x)
    return jnp.dot(m, hi, preferred_element_type=F32) + jnp.dot(m, lo, preferred_element_type=F32)


def _head_sum(x, ones):
    return jnp.dot(x.astype(BF16), ones, preferred_element_type=F32)


def _sigmoid(x):
    return 0.5 * jnp.tanh(0.5 * x) + 0.5


def _head_ones(n):
    r = lax.shift_right_logical(lax.broadcasted_iota(jnp.int32, (n, n), 0), jnp.int32(6))
    c = lax.shift_right_logical(lax.broadcasted_iota(jnp.int32, (n, n), 1), jnp.int32(6))
    return jnp.where(r == c, 1.0, 0.0).astype(BF16)


def _rms(x, g):
    ms = jnp.mean(x * x, axis=-1, keepdims=True)
    return x * lax.rsqrt(ms + NORM_EPS) * g


def _head_rms(x, g, ones):
    ms = _head_sum(x * x, ones) * (1.0 / HEAD_DIM)
    return x * lax.rsqrt(ms + NORM_EPS) * g


def _rope(x, cos, sin):
    t = x.shape[0]
    lane = lax.broadcasted_iota(jnp.int32, (t, 128), 1)
    first = (lane & 32) == 0
    outs = []
    for i in range(x.shape[1] // 128):
        xs = x[:, i * 128:(i + 1) * 128]
        swapped = jnp.where(first, pltpu.roll(xs, 96, axis=1), pltpu.roll(xs, 32, axis=1))
        outs.append(xs * cos + swapped * sin)
    return outs[0] if len(outs) == 1 else jnp.concatenate(outs, axis=1)


def _softmax_pv(scores, values, sink=None):
    m = scores[0].max(axis=-1, keepdims=True)
    for s in scores[1:]:
        m = jnp.maximum(m, s.max(axis=-1, keepdims=True))
    if sink is not None:
        m = jnp.maximum(m, sink)
    denom = None
    acc = None
    for s, v in zip(scores, values):
        p = jnp.exp(s - m)
        d = p.sum(axis=-1, keepdims=True)
        o = jnp.dot(p.astype(BF16), v, preferred_element_type=F32)
        denom = d if denom is None else denom + d
        acc = o if acc is None else acc + o
    if sink is not None:
        denom = denom + jnp.exp(sink - m)
    return acc / denom


def _attend_heads(score_fns, finish_fns):
    outs = []
    nxt = score_fns[0]()
    for i in range(len(score_fns)):
        cur = nxt
        if i + 1 < len(score_fns):
            nxt = score_fns[i + 1]()
        outs.append(finish_fns[i](cur))
    return outs


def _index32(i):
    return jnp.asarray(i, jnp.int32)


def _layer_spec(shape, layer):
    zeros = (0,) * len(shape)
    return pl.BlockSpec((None,) + shape, lambda *_: (layer,) + zeros, pipeline_mode=pl.Buffered(1))


def _mod_row(mod_ref, rows_per_mod):
    if rows_per_mod is None:
        return mod_ref[0:1]
    row = 1 + pl.program_id(0) // (rows_per_mod // ROW_TILE)
    return mod_ref[pl.ds(row, 1), :]


def _mod_body(c_ref, w_ref, b_ref, o_ref):
    cv = c_ref[...]
    s = cv * _sigmoid(cv)
    o_ref[0] = _bdot(s, w_ref[0]) + b_ref[0]


def _mod_call(cvec, w_mod, b_mod):
    bn = MOD_COLS // 4
    return pl.pallas_call(
        _mod_body,
        out_shape=jax.ShapeDtypeStruct((DEPTH, MOD_ROWS, MOD_COLS), F32),
        grid=(DEPTH, MOD_COLS // bn),
        in_specs=[
            pl.BlockSpec((MOD_ROWS, D_MODEL), lambda l, j: (0, 0)),
            pl.BlockSpec((1, D_MODEL, bn), lambda l, j: (l, 0, j)),
            pl.BlockSpec((1, 1, bn), lambda l, j: (l, 0, j)),
        ],
        out_specs=pl.BlockSpec((1, MOD_ROWS, bn), lambda l, j: (l, 0, j)),
        compiler_params=pltpu.CompilerParams(vmem_limit_bytes=VMEM_LIMIT),
        name="mod_vectors",
    )(cvec, w_mod, b_mod.reshape(DEPTH, 1, MOD_COLS))


def _proj_in_body(x_ref, mod_ref, g_ref, w_ref, za_ref, zc_ref, *, rows_per_mod):
    mod = _mod_row(mod_ref, rows_per_mod)
    shift1 = mod[:, 0:D_MODEL]
    scale1 = mod[:, D_MODEL:2 * D_MODEL]
    h = (_rms(x_ref[...], g_ref[0:1]) * (1.0 + scale1) + shift1).astype(BF16)
    za_ref[...] = jnp.dot(h, w_ref[:, 0:ATT_COLS], preferred_element_type=F32)
    zc_ref[...] = jnp.dot(h, w_ref[:, ATT_COLS:IN_COLS], preferred_element_type=F32)


def _proj_in_call(x, mods, gvec, w_in_b, layer, rows_per_mod):
    n = x.shape[0]
    return pl.pallas_call(
        functools.partial(_proj_in_body, rows_per_mod=rows_per_mod),
        out_shape=(jax.ShapeDtypeStruct((n, ATT_COLS), F32), jax.ShapeDtypeStruct((n, RWKV_COLS), F32)),
        grid=(n // ROW_TILE,),
        in_specs=[
            pl.BlockSpec((ROW_TILE, D_MODEL), lambda i: (i, 0)),
            _layer_spec((MOD_ROWS, MOD_COLS), layer),
            _layer_spec((8, D_MODEL), layer),
            _layer_spec((D_MODEL, IN_COLS), layer),
        ],
        out_specs=(pl.BlockSpec((ROW_TILE, ATT_COLS), lambda i: (i, 0)),
                   pl.BlockSpec((ROW_TILE, RWKV_COLS), lambda i: (i, 0))),
        compiler_params=pltpu.CompilerParams(vmem_limit_bytes=VMEM_LIMIT),
        name="proj_in",
    )(x, mods, gvec, w_in_b)


def _attn_prompt_body(*refs, layer):
    sink_ref, za_ref, gq_ref, gk_ref = refs[:4]
    if layer == 0:
        o_ref, kv_ref = refs[4:]
    else:
        prev_ref, o_ref = refs[4:6]
        cache_refs = refs[6:]
    aq = (za_ref[:, 0:A_DIM] * QK_SCALE).astype(BF16)
    ak = za_ref[:, 256:384]
    av = za_ref[:, 384:512]
    akb = ak.astype(BF16)
    avb = av.astype(BF16)
    bq = _head_rms(za_ref[:, 512:896], gq_ref[...], _head_ones(B_DIM))
    bk = _head_rms(za_ref[:, 896:1024], gk_ref[...], _head_ones(KV_DIM))
    bv = za_ref[:, 1024:1152]
    bqb = (bq * QK_SCALE).astype(BF16)
    bkb = bk.astype(BF16)
    bvb = bv.astype(BF16)

    score_fns, finish_fns = [], []
    for h in range(A_HEADS):
        kv = h // (A_HEADS // A_KV_HEADS)
        hs = slice(kv * 64, (kv + 1) * 64)
        score_fns.append(lambda h=h, hs=hs: [_bdot_nt(aq[:, h * 64:(h + 1) * 64], akb[:, hs])])
        finish_fns.append(lambda sc, h=h, hs=hs: _softmax_pv(sc, [avb[:, hs]], sink_ref[layer, h]))
    for h in range(B_HEADS):
        kv = h // (B_HEADS // B_KV_HEADS)
        hs = slice(kv * 64, (kv + 1) * 64)
        score_fns.append(lambda h=h, hs=hs: [_bdot_nt(bqb[:, h * 64:(h + 1) * 64], bkb[:, hs])])
        finish_fns.append(lambda sc, hs=hs: _softmax_pv(sc, [bvb[:, hs]]))
    o_ref[...] = jnp.concatenate(_attend_heads(score_fns, finish_fns), axis=1)

    for j, piece in enumerate((ak, av, bk, bv)):
        if layer == 0:
            kv_ref[0, j] = piece.T
        else:
            cache_refs[j][0, 0] = prev_ref[0, j]
            cache_refs[j][0, 1] = piece.T


def _attn_prompt_call(za, sink, gq, gk, layer, prev_kv):
    nb = za.shape[0] // SEQ
    in_specs = [
        pl.BlockSpec(memory_space=pltpu.SMEM),
        pl.BlockSpec((SEQ, ATT_COLS), lambda b: (b, 0)),
        _layer_spec((1, B_DIM), layer),
        _layer_spec((1, KV_DIM), layer),
    ]
    args = [sink, za, gq, gk]
    out_shape = [jax.ShapeDtypeStruct((nb * SEQ, ATT_OUT), F32)]
    out_specs = [pl.BlockSpec((SEQ, ATT_OUT), lambda b: (b, 0))]
    slab_spec = pl.BlockSpec((1, N_CACHE, KV_DIM, SEQ), lambda b: (b, 0, 0, 0))
    if layer == 0:
        out_shape.append(jax.ShapeDtypeStruct((nb, N_CACHE, KV_DIM, SEQ), F32))
        out_specs.append(slab_spec)
    else:
        in_specs.append(slab_spec)
        args.append(prev_kv)
        for _ in range(N_CACHE):
            out_shape.append(jax.ShapeDtypeStruct((nb, DEPTH, KV_DIM, SEQ), F32))
            out_specs.append(pl.BlockSpec((1, DEPTH, KV_DIM, SEQ), lambda b: (b, 0, 0, 0)))
    return pl.pallas_call(
        functools.partial(_attn_prompt_body, layer=layer),
        out_shape=tuple(out_shape),
        grid=(nb,),
        in_specs=in_specs,
        out_specs=tuple(out_specs),
        compiler_params=pltpu.CompilerParams(vmem_limit_bytes=VMEM_LIMIT),
        name="attn_prompt",
    )(*args)


def _attn_sample_body(sink_ref, za_ref, cak_ref, cav_ref, cbk_ref, cbv_ref, cos_ref, sin_ref, gq_ref, gk_ref,
                      o_ref, qa_s, ka_s, va_s, qb_s, kb_s, vb_s, *, layer):
    t = DEC_SEQ
    cos = cos_ref[...]
    sin = sin_ref[...]
    qa_s[...] = (_rope(za_ref[:, 0:A_DIM], cos, sin) * QK_SCALE).astype(BF16)
    zpad = jnp.zeros((Q_BLK, KV_DIM), BF16)
    ka_s[0:Q_BLK] = zpad
    ka_s[Q_BLK + t:] = zpad
    va_s[0:Q_BLK] = zpad
    va_s[Q_BLK + t:] = zpad
    ka_s[Q_BLK:Q_BLK + t] = _rope(za_ref[:, 256:384], cos, sin).astype(BF16)
    va_s[Q_BLK:Q_BLK + t] = za_ref[:, 384:512].astype(BF16)
    bq = _head_rms(za_ref[:, 512:896], gq_ref[...], _head_ones(B_DIM))
    qb_s[...] = (_rope(bq, cos, sin) * QK_SCALE).astype(BF16)
    bk = _head_rms(za_ref[:, 896:1024], gk_ref[...], _head_ones(KV_DIM))
    kb_s[0:PAST_LEN] = cbk_ref[0].T.astype(BF16)
    kb_s[PAST_LEN:] = _rope(bk, cos, sin).astype(BF16)
    vb_s[0:PAST_LEN] = cbv_ref[0].T.astype(BF16)
    vb_s[PAST_LEN:] = za_ref[:, 1024:1152].astype(BF16)

    kca = cak_ref[0].T.astype(BF16)
    vca = cav_ref[0].T.astype(BF16)

    def a_block(n, carry):
        r0 = pl.multiple_of(_index32(n) * Q_BLK, Q_BLK)
        q = qa_s[pl.ds(r0, Q_BLK), :]
        kl = ka_s[pl.ds(r0, 3 * Q_BLK), :]
        vl = va_s[pl.ds(r0, 3 * Q_BLK), :]
        qpos = r0 + lax.broadcasted_iota(jnp.int32, (Q_BLK, 3 * Q_BLK), 0)
        kpos = r0 - Q_BLK + lax.broadcasted_iota(jnp.int32, (Q_BLK, 3 * Q_BLK), 1)
        mask = (jnp.abs(kpos - qpos) <= WINDOW) & (kpos >= 0) & (kpos < t)
        score_fns, finish_fns = [], []
        for h in range(A_HEADS):
            kv = h // (A_HEADS // A_KV_HEADS)
            hs = slice(kv * 64, (kv + 1) * 64)

            def scores(h=h, hs=hs):
                qh = q[:, h * 64:(h + 1) * 64]
                return [jnp.where(mask, _bdot_nt(qh, kl[:, hs]), NEG_INF), _bdot_nt(qh, kca[:, hs])]

            score_fns.append(scores)
            finish_fns.append(lambda sc, h=h, hs=hs: _softmax_pv(sc, [vl[:, hs], vca[:, hs]], sink_ref[layer, h]))
        o_ref[pl.ds(r0, Q_BLK), 0:A_DIM] = jnp.concatenate(_attend_heads(score_fns, finish_fns), axis=1)
        return carry

    lax.fori_loop(0, t // Q_BLK, a_block, 0)

    def b_block(n, carry):
        r0 = pl.multiple_of(_index32(n) * B_QROWS, B_QROWS)
        q = qb_s[pl.ds(r0, B_QROWS), :]
        score_fns, finish_fns = [], []
        for h in range(B_HEADS):
            kv = h // (B_HEADS // B_KV_HEADS)
            hs = slice(kv * 64, (kv + 1) * 64)
            score_fns.append(lambda h=h, hs=hs: [_bdot_nt(q[:, h * 64:(h + 1) * 64], kb_s[:, hs])])
            finish_fns.append(lambda sc, hs=hs: _softmax_pv(sc, [vb_s[:, hs]]))
        o_ref[pl.ds(r0, B_QROWS), A_DIM:ATT_OUT] = jnp.concatenate(_attend_heads(score_fns, finish_fns), axis=1)
        return carry

    lax.fori_loop(0, t // B_QROWS, b_block, 0)


def _attn_sample_call(za, sink, caches, cos, sin, gq, gk, layer):
    nb = za.shape[0] // DEC_SEQ
    t = DEC_SEQ
    cache_spec = pl.BlockSpec((1, None, KV_DIM, PAST_LEN), lambda b: (b, layer, 0, 0))
    return pl.pallas_call(
        functools.partial(_attn_sample_body, layer=layer),
        out_shape=jax.ShapeDtypeStruct((nb * t, ATT_OUT), F32),
        grid=(nb,),
        in_specs=[
            pl.BlockSpec(memory_space=pltpu.SMEM),
            pl.BlockSpec((t, ATT_COLS), lambda b: (b, 0)),
            cache_spec, cache_spec, cache_spec, cache_spec,
            pl.BlockSpec((t, 128), lambda b: (0, 0)),
            pl.BlockSpec((t, 128), lambda b: (0, 0)),
            _layer_spec((1, B_DIM), layer),
            _layer_spec((1, KV_DIM), layer),
        ],
        out_specs=pl.BlockSpec((t, ATT_OUT), lambda b: (b, 0)),
        scratch_shapes=[
            pltpu.VMEM((t, A_DIM), BF16),
            pltpu.VMEM((t + 2 * Q_BLK, KV_DIM), BF16),
            pltpu.VMEM((t + 2 * Q_BLK, KV_DIM), BF16),
            pltpu.VMEM((t, B_DIM), BF16),
            pltpu.VMEM((t + PAST_LEN, KV_DIM), BF16),
            pltpu.VMEM((t + PAST_LEN, KV_DIM), BF16),
        ],
        compiler_params=pltpu.CompilerParams(vmem_limit_bytes=VMEM_LIMIT),
        name="attn_sample",
    )(sink, za, *caches, cos, sin, gq, gk)


def _rwkv_body(*refs, t, nbatch, has_init, want_state, has_prev):
    refs = list(refs)
    zc_ref, cv_ref, wup_ref, aup_ref, gup_ref = refs[:5]
    pos = 5
    st0_ref = prev_ref = stout_ref = None
    if has_init:
        st0_ref = refs[pos]
        pos += 1
    if has_prev:
        prev_ref = refs[pos]
        pos += 1
    oc_ref = refs[pos]
    pos += 1
    if want_state:
        stout_ref = refs[pos]
        pos += 1
    aa_s, ld_s, b_s, kt_s, bon_s, y_s, st_s = refs[pos:]

    nchunks = t // CHUNK
    ones = _head_ones(C_DIM)
    k_k = cv_ref[0:1]
    k_a = cv_ref[1:2]
    r_k = cv_ref[2:3]
    ln_w = cv_ref[3:4]
    ln_b = cv_ref[4:5]
    w0 = (cv_ref[5:6], cv_ref[6:7])
    a0 = (cv_ref[7:8], cv_ref[8:9])
    wup = (wup_ref[0].astype(BF16), wup_ref[1].astype(BF16))
    aup = (aup_ref[0].astype(BF16), aup_ref[1].astype(BF16))

    for rb in range(nbatch * t // RWKV_ROWS):
        rows = slice(rb * RWKV_ROWS, (rb + 1) * RWKV_ROWS)
        zr = zc_ref[rows, 0:384]
        zk = zc_ref[rows, 384:768]
        zv = zc_ref[rows, 768:1152]
        tw = jnp.tanh(zc_ref[rows, 1152:1216])
        xa = zc_ref[rows, 1216:1280]
        kkr = zk * k_k
        kk = kkr * lax.rsqrt(_head_sum(kkr * kkr, ones) + 1e-12)
        aa_s[rows] = -kk
        kt_sum = None
        for d in range(2):
            wl = w0[d] + _bdot(tw, wup[d])
            ld_s[d, rows] = (-DECAY_RATE) * _sigmoid(wl)
            a = _sigmoid(a0[d] + _bdot(xa, aup[d]))
            kt = zk * (1.0 + (a - 1.0) * k_a)
            kt_s[d, rows] = kt
            b_s[d, rows] = kk * a
            kt_sum = kt if kt_sum is None else kt_sum + kt
        bon_s[rows] = _head_sum(zr * kt_sum * r_k, ones) * zv
        y_s[rows] = jnp.zeros((RWKV_ROWS, C_DIM), F32)

    ir = lax.broadcasted_iota(jnp.int32, (CHUNK, 128), 0)
    ic = lax.broadcasted_iota(jnp.int32, (CHUNK, 128), 1) & 63
    eye = jnp.where(ir == ic, 1.0, 0.0).astype(F32)
    pair_masks = []
    for j in range(CHUNK.bit_length() - 1):
        same_big = lax.shift_right_logical(ir, jnp.int32(j + 1)) == lax.shift_right_logical(ic, jnp.int32(j + 1))
        same_small = lax.shift_right_logical(ir, jnp.int32(j)) == lax.shift_right_logical(ic, jnp.int32(j))
        pair_masks.append(same_big & jnp.logical_not(same_small))
    incl = (ic <= ir, ic >= ir)
    strict = (ic < ir, ic > ir)
    sr = lax.broadcasted_iota(jnp.int32, (CHUNK, CHUNK), 0)
    sc = lax.broadcasted_iota(jnp.int32, (CHUNK, CHUNK), 1)
    tri = tuple(jnp.where(m, 1.0, 0.0).astype(BF16) for m in (sc <= sr, sc >= sr))
    br = lax.broadcasted_iota(jnp.int32, (128, 128), 0)
    bc = lax.broadcasted_iota(jnp.int32, (128, 128), 1)
    diag_blocks = (br < 64) == (bc < 64)
    left_half = lax.broadcasted_iota(jnp.int32, (CHUNK, 128), 1) < 64

    def blockdiag(x2):
        x2 = x2.astype(BF16)
        return jnp.where(diag_blocks, jnp.concatenate([x2, x2], axis=0), jnp.zeros((128, 128), BF16))

    npairs = C_HEADS // 2
    for bi in range(nbatch):
        for d in range(2):
            for p in range(npairs):
                if has_init:
                    st_s[bi, d, p] = jnp.concatenate([st0_ref[bi, d, 2 * p], st0_ref[bi, d, 2 * p + 1]], axis=1)
                else:
                    st_s[bi, d, p] = jnp.zeros((HEAD_DIM, 128), F32)

    def chunk(c, carry):
        c = _index32(c)
        items = []
        for bi, d in [(bi, d) for bi in range(nbatch) for d in range(2)]:
            r0 = bi * t + ((nchunks - 1 - c) if d == 1 else c) * CHUNK
            rows = pl.ds(pl.multiple_of(r0, CHUNK), CHUNK)
            ld = ld_s[d, rows]
            cinc = _split_dot_left(tri[d], ld)
            cexc = cinc - ld
            ctot = cinc[0:1] if d == 1 else cinc[CHUNK - 1:CHUNK]
            e_ninc = jnp.exp(-cinc)
            e_rem = jnp.exp(ctot - cinc)
            e_tot = jnp.exp(ctot)
            bb = b_s[d, rows]
            kt = kt_s[d, rows]
            at = (aa_s[rows] * jnp.exp(cexc)).astype(BF16)
            rt = (zc_ref[rows, 0:384] * jnp.exp(cinc)).astype(BF16)
            bt = (bb * e_ninc).astype(BF16)
            ktt = (kt * e_ninc).astype(BF16)
            bh = (bb * e_rem).astype(BF16)
            kh = (kt * e_rem).astype(BF16)
            vm = zc_ref[rows, 768:1152].astype(BF16)
            for p in range(npairs):
                ps = slice(p * 128, (p + 1) * 128)
                items.append(dict(bi=bi, d=d, p=p, rows=rows, e_tot=e_tot[:, ps], at=at[:, ps], rt=rt[:, ps],
                                  bt=bt[:, ps], kt=ktt[:, ps], bh=bh[:, ps], kh=kh[:, ps], vm=vm[:, ps]))
        for it in items:
            it["ar"] = jnp.concatenate([it["at"], it["rt"]], axis=0)
            it["vmd"] = blockdiag(it["vm"])
        for it in items:
            it["sb"] = _bdot_nt(it["ar"], blockdiag(it["bt"]))
        for it in items:
            it["sk"] = _bdot_nt(it["ar"], blockdiag(it["kt"]))
        for it in items:
            d = it["d"]
            it["l"] = jnp.where(strict[d], it["sb"][0:CHUNK], 0.0)
            it["mak"] = jnp.where(strict[d], it["sk"][0:CHUNK], 0.0).astype(BF16)
            it["nr"] = jnp.concatenate([jnp.where(incl[d], it["sb"][CHUNK:], 0.0).astype(BF16),
                                        jnp.where(incl[d], it["sk"][CHUNK:], 0.0).astype(BF16)], axis=1)
            it["tm"] = eye + jnp.where(pair_masks[0], it["l"], 0.0)
        for pm in pair_masks[1:]:
            for it in items:
                it["t1"] = _bdot(it["tm"], blockdiag(jnp.where(pm, it["l"], 0.0)))
            for it in items:
                it["tm"] = it["tm"] + _bdot(it["t1"], blockdiag(it["tm"]))
        for it in items:
            it["mv"] = _bdot(it["mak"], it["vmd"])
        for it in items:
            it["tau"] = _bdot(it["tm"], jnp.concatenate([blockdiag(it["at"]), blockdiag(it["mv"])], axis=1))
        for it in items:
            it["s0"] = st_s[it["bi"], it["d"], it["p"]]
            it["x"] = _bdot_nt(jnp.concatenate([it["tau"][:, 0:128].astype(BF16), it["rt"]], axis=0),
                               blockdiag(it["s0"]))
        for it in items:
            it["u"] = (it["x"][0:CHUNK] + it["tau"][:, 128:256]).astype(BF16)
        for it in items:
            it["y"] = it["x"][CHUNK:] + _bdot(it["nr"], jnp.concatenate([blockdiag(it["u"]), it["vmd"]], axis=0))
        for it in items:
            full = _bdot_tn(jnp.concatenate([it["u"], it["vm"]], axis=0),
                            jnp.concatenate([it["bh"], it["kh"]], axis=0))
            st_s[it["bi"], it["d"], it["p"]] = (it["s0"] * it["e_tot"]
                                                + jnp.where(left_half, full[0:64], full[64:128]))
        for i in range(0, len(items), npairs):
            rows = items[i]["rows"]
            y_s[rows] = y_s[rows] + jnp.concatenate([it["y"] for it in items[i:i + npairs]], axis=1)
        return carry

    lax.fori_loop(0, nchunks, chunk, 0)
    if want_state:
        for bi in range(nbatch):
            if has_prev:
                stout_ref[bi, 0] = prev_ref[bi]
            for d in range(2):
                for p in range(npairs):
                    s2 = st_s[bi, d, p]
                    for q in range(2):
                        if has_prev:
                            stout_ref[bi, 1, d, 2 * p + q] = s2[:, q * 64:(q + 1) * 64]
                        else:
                            stout_ref[bi, d, 2 * p + q] = s2[:, q * 64:(q + 1) * 64]

    gup = gup_ref[...].astype(BF16)
    for rb in range(nbatch * t // RWKV_ROWS):
        rows = slice(rb * RWKV_ROWS, (rb + 1) * RWKV_ROWS)
        y = y_s[rows]
        mu = _head_sum(y, ones) * (1.0 / HEAD_DIM)
        yc = y - mu
        var = _head_sum(yc * yc, ones) * (1.0 / HEAD_DIM)
        yn = yc * lax.rsqrt(var + GN_EPS)
        g = _bdot(_sigmoid(zc_ref[rows, 1280:1408]), gup)
        oc_ref[rows] = (yn * ln_w + ln_b + bon_s[rows]) * g


def _rwkv_call(zc, rvec, wup, aup, gup, layer, t, init_state, want_state, prev_state):
    nb = zc.shape[0] // t
    nbatch = RWKV_BATCH
    rows = nbatch * t
    has_init = init_state is not None
    has_prev = prev_state is not None
    st_shape = (2, C_HEADS, HEAD_DIM, HEAD_DIM)
    in_specs = [
        pl.BlockSpec((rows, RWKV_COLS), lambda b: (b, 0), pipeline_mode=pl.Buffered(1 if nb == nbatch else 2)),
        _layer_spec((16, C_DIM), layer),
        _layer_spec((2, W_RANK, C_DIM), layer),
        _layer_spec((2, A_RANK, C_DIM), layer),
        _layer_spec((G_RANK, C_DIM), layer),
    ]
    args = [zc, rvec, wup, aup, gup]
    if has_init:
        in_specs.append(pl.BlockSpec((nbatch, None) + st_shape, lambda b: (b, layer, 0, 0, 0, 0)))
        args.append(init_state)
    if has_prev:
        in_specs.append(pl.BlockSpec((nbatch,) + st_shape, lambda b: (b, 0, 0, 0, 0)))
        args.append(prev_state)
    out_shape = [jax.ShapeDtypeStruct((nb * t, C_DIM), F32)]
    out_specs = [pl.BlockSpec((rows, C_DIM), lambda b: (b, 0))]
    if want_state and has_prev:
        out_shape.append(jax.ShapeDtypeStruct((nb, DEPTH) + st_shape, F32))
        out_specs.append(pl.BlockSpec((nbatch, DEPTH) + st_shape, lambda b: (b, 0, 0, 0, 0, 0)))
    elif want_state:
        out_shape.append(jax.ShapeDtypeStruct((nb,) + st_shape, F32))
        out_specs.append(pl.BlockSpec((nbatch,) + st_shape, lambda b: (b, 0, 0, 0, 0)))
    return pl.pallas_call(
        functools.partial(_rwkv_body, t=t, nbatch=nbatch, has_init=has_init, want_state=want_state,
                          has_prev=has_prev),
        out_shape=tuple(out_shape),
        grid=(nb // nbatch,),
        in_specs=in_specs,
        out_specs=tuple(out_specs),
        scratch_shapes=[
            pltpu.VMEM((rows, C_DIM), F32),
            pltpu.VMEM((2, rows, C_DIM), F32),
            pltpu.VMEM((2, rows, C_DIM), F32),
            pltpu.VMEM((2, rows, C_DIM), F32),
            pltpu.VMEM((rows, C_DIM), F32),
            pltpu.VMEM((rows, C_DIM), F32),
            pltpu.VMEM((nbatch, 2, C_HEADS // 2, HEAD_DIM, 128), F32),
        ],
        compiler_params=pltpu.CompilerParams(vmem_limit_bytes=VMEM_LIMIT),
        name="rwkv_prompt" if want_state else "rwkv_sample",
    )(*args)


def _post_body(oa_ref, oc_ref, x_ref, mod_ref, g_ref, wout_ref, wgu_ref, wdn_ref, xo_ref, *, rows_per_mod):
    mod = _mod_row(mod_ref, rows_per_mod)
    gate1 = mod[:, 2 * D_MODEL:3 * D_MODEL]
    shift2 = mod[:, 3 * D_MODEL:4 * D_MODEL]
    scale2 = mod[:, 4 * D_MODEL:5 * D_MODEL]
    gate2 = mod[:, 5 * D_MODEL:6 * D_MODEL]
    o = (jnp.dot(oa_ref[...].astype(BF16), wout_ref[0:ATT_OUT], preferred_element_type=F32)
         + jnp.dot(oc_ref[...].astype(BF16), wout_ref[ATT_OUT:MIX_DIM], preferred_element_type=F32))
    x1 = x_ref[...] + gate1 * _rms(o, g_ref[1:2])
    h2 = (_rms(x1, g_ref[2:3]) * (1.0 + scale2) + shift2).astype(BF16)
    acc = None
    for j in range(FF_DIM // FF_CHUNK):
        lo = j * FF_CHUNK
        g = jnp.dot(h2, wgu_ref[:, lo:lo + FF_CHUNK], preferred_element_type=F32)
        u = jnp.dot(h2, wgu_ref[:, FF_DIM + lo:FF_DIM + lo + FF_CHUNK], preferred_element_type=F32)
        act = (g * _sigmoid(g) * u).astype(BF16)
        part = jnp.dot(act, wdn_ref[lo:lo + FF_CHUNK, :], preferred_element_type=F32)
        acc = part if acc is None else acc + part
    xo_ref[...] = x1 + gate2 * _rms(acc, g_ref[3:4])


def _post_call(oatt, oc, x, mods, gvec, wout_b, wgu_b, wdn_b, layer, rows_per_mod):
    n = x.shape[0]
    return pl.pallas_call(
        functools.partial(_post_body, rows_per_mod=rows_per_mod),
        out_shape=jax.ShapeDtypeStruct((n, D_MODEL), F32),
        grid=(n // ROW_TILE,),
        in_specs=[
            pl.BlockSpec((ROW_TILE, ATT_OUT), lambda i: (i, 0)),
            pl.BlockSpec((ROW_TILE, C_DIM), lambda i: (i, 0)),
            pl.BlockSpec((ROW_TILE, D_MODEL), lambda i: (i, 0)),
            _layer_spec((MOD_ROWS, MOD_COLS), layer),
            _layer_spec((8, D_MODEL), layer),
            _layer_spec((MIX_DIM, D_MODEL), layer),
            _layer_spec((D_MODEL, 2 * FF_DIM), layer),
            _layer_spec((FF_DIM, D_MODEL), layer),
        ],
        out_specs=pl.BlockSpec((ROW_TILE, D_MODEL), lambda i: (i, 0)),
        compiler_params=pltpu.CompilerParams(vmem_limit_bytes=VMEM_LIMIT),
        name="post_ffn",
    )(oatt, oc, x, mods, gvec, wout_b, wgu_b, wdn_b)


def _rope_tables(t):
    pos = jnp.arange(t)
    row = (pos // GRID_W).astype(F32)
    col = (pos % GRID_W).astype(F32)
    freqs = ROPE_THETA ** (-jnp.arange(ROPE_PAIRS_AXIS, dtype=F32) / ROPE_PAIRS_AXIS)
    ang = jnp.concatenate([row[:, None] * freqs, col[:, None] * freqs], axis=-1)
    cos = jnp.tile(jnp.cos(ang), (1, 4))
    sin = jnp.sin(ang)
    sin = jnp.tile(jnp.concatenate([-sin, sin], axis=-1), (1, 2))
    return cos, sin


def _to_time_minor(cache):
    b, l, t = cache.shape[:3]
    return jnp.transpose(cache, (0, 1, 3, 4, 2)).reshape(b, l, KV_DIM, t)


def _from_time_minor(slab):
    b, l, _, t = slab.shape
    return jnp.transpose(slab.reshape(b, l, 2, HEAD_DIM, t), (0, 1, 4, 2, 3))


def kernel(x_prompt, x_sample, cache_a_k, cache_a_v, cache_b_k, cache_b_v, state_c, c, c_ctx, w_mod, b_mod, norm_mix_pre, norm_mix_post, norm_ffn_pre, norm_ffn_post, w_in, w_out, a_sink, b_q_norm, b_k_norm, c_w0, c_w_up, c_a0, c_a_up, c_g_up, c_k_k, c_k_a, c_r_k, c_ln_w, c_ln_b, w_gu, w_down):
    cvec = jnp.zeros((MOD_ROWS, D_MODEL), F32).at[0].set(c_ctx).at[1:1 + DEC_BATCH].set(c)
    mods = _mod_call(cvec, w_mod, b_mod)
    cos, sin = _rope_tables(DEC_SEQ)

    w_in_b = w_in.astype(BF16)
    wout_b = w_out.astype(BF16)
    wgu_b = w_gu.astype(BF16)
    wdn_b = w_down.astype(BF16)
    gvec = jnp.zeros((DEPTH, 8, D_MODEL), F32)
    gvec = gvec.at[:, 0].set(norm_mix_pre).at[:, 1].set(norm_mix_post)
    gvec = gvec.at[:, 2].set(norm_ffn_pre).at[:, 3].set(norm_ffn_post)
    rvec = jnp.zeros((DEPTH, 16, C_DIM), F32)
    rvec = rvec.at[:, 0].set(c_k_k).at[:, 1].set(c_k_a).at[:, 2].set(c_r_k.reshape(DEPTH, C_DIM))
    rvec = rvec.at[:, 3].set(c_ln_w).at[:, 4].set(c_ln_b)
    rvec = rvec.at[:, 5:7].set(c_w0).at[:, 7:9].set(c_a0)
    gq = jnp.tile(b_q_norm, (1, B_HEADS))[:, None, :]
    gk = jnp.tile(b_k_norm, (1, B_KV_HEADS))[:, None, :]
    caches = tuple(_to_time_minor(t) for t in (cache_a_k, cache_a_v, cache_b_k, cache_b_v))

    xp = x_prompt.reshape(BATCH * SEQ, D_MODEL)
    xs = x_sample.reshape(DEC_BATCH * DEC_SEQ, D_MODEL)
    kv_prev = st_prev = None
    for l in range(DEPTH):
        za, zc = _proj_in_call(xp, mods, gvec, w_in_b, l, None)
        oatt, *kv_prev = _attn_prompt_call(za, a_sink, gq, gk, l, kv_prev)
        if l == 0:
            (kv_prev,) = kv_prev
        oc, st_prev = _rwkv_call(zc, rvec, c_w_up, c_a_up, c_g_up, l, SEQ, None, True, st_prev)
        xp = _post_call(oatt, oc, xp, mods, gvec, wout_b, wgu_b, wdn_b, l, None)

        za, zc = _proj_in_call(xs, mods, gvec, w_in_b, l, DEC_SEQ)
        oatt = _attn_sample_call(za, a_sink, caches, cos, sin, gq, gk, l)
        (oc,) = _rwkv_call(zc, rvec, c_w_up, c_a_up, c_g_up, l, DEC_SEQ, state_c, False, None)
        xs = _post_call(oatt, oc, xs, mods, gvec, wout_b, wgu_b, wdn_b, l, DEC_SEQ)

    new_caches = tuple(_from_time_minor(slab) for slab in kv_prev)
    return (xp.reshape(BATCH, SEQ, D_MODEL), xs.reshape(DEC_BATCH, DEC_SEQ, D_MODEL), *new_caches, st_prev)
```

```python
import functools
import math

import jax
import jax.numpy as jnp
from jax import lax
from jax.experimental import pallas as pl
from jax.experimental.pallas import tpu as pltpu

D_MODEL = 1024
BATCH = 32
SEQ = 256
DEPTH = 2
DEC_BATCH = 2
DEC_SEQ = 1024
PAST_LEN = 256
GRID_W = 64
HEAD_DIM = 64
A_HEADS = 4
A_KV_HEADS = 2
B_HEADS = 6
B_KV_HEADS = 2
C_HEADS = 6
A_DIM = A_HEADS * HEAD_DIM
B_DIM = B_HEADS * HEAD_DIM
C_DIM = C_HEADS * HEAD_DIM
MIX_DIM = A_DIM + B_DIM + C_DIM
WINDOW = 128
Q_BLK = 128
W_RANK = 64
A_RANK = 64
G_RANK = 128
FF_DIM = -(-8 * D_MODEL // (3 * 256)) * 256
ROPE_THETA = 10000.0
ROPE_PAIRS_AXIS = HEAD_DIM // 4
NORM_EPS = 1e-6
GN_EPS = 64e-5
NEG_INF = -1e30

KV_DIM = 2 * HEAD_DIM
ATT_COLS = A_DIM + 2 * KV_DIM + B_DIM + 2 * KV_DIM
RWKV_COLS = 3 * C_DIM + W_RANK + A_RANK + G_RANK
IN_COLS = ATT_COLS + RWKV_COLS
ATT_OUT = A_DIM + B_DIM
MOD_COLS = 6 * D_MODEL
MOD_ROWS = 8
N_CACHE = 4
CHUNK = 64
QK_SCALE = HEAD_DIM ** -0.5
DECAY_RATE = math.exp(-0.5)
ROW_TILE = 512
RWKV_ROWS = 256
RWKV_BATCH = 2
B_QROWS = 256
MXU_TILE = 256
FF_CHUNKS = ((0, 4 * MXU_TILE), (4 * MXU_TILE, 8 * MXU_TILE), (8 * MXU_TILE, FF_DIM))
assert FF_DIM % MXU_TILE == 0
VMEM_LIMIT = 56 * 1024 * 1024

F32 = jnp.float32
BF16 = jnp.bfloat16

assert DEPTH == 2


def _bdot(a, b):
    return jnp.dot(a.astype(BF16), b.astype(BF16), preferred_element_type=F32)


def _bdot_nt(a, b):
    return lax.dot_general(a.astype(BF16), b.astype(BF16), (((1,), (1,)), ((), ())), preferred_element_type=F32)


def _bdot_tn(a, b):
    return lax.dot_general(a.astype(BF16), b.astype(BF16), (((0,), (0,)), ((), ())), preferred_element_type=F32)


def _split(x):
    hi = x.astype(BF16)
    lo = (x - hi.astype(F32)).astype(BF16)
    return hi, lo


def _split_dot_left(m, x):
    hi, lo = _split(x)
    return jnp.dot(m, hi, preferred_element_type=F32) + jnp.dot(m, lo, preferred_element_type=F32)


def _head_sum(x, ones):
    return jnp.dot(x.astype(BF16), ones, preferred_element_type=F32)


def _sigmoid(x):
    return 0.5 * jnp.tanh(0.5 * x) + 0.5


def _head_ones(n):
    r = lax.shift_right_logical(lax.broadcasted_iota(jnp.int32, (n, n), 0), jnp.int32(6))
    c = lax.shift_right_logical(lax.broadcasted_iota(jnp.int32, (n, n), 1), jnp.int32(6))
    return jnp.where(r == c, 1.0, 0.0).astype(BF16)


def _rms(x, g):
    ms = jnp.mean(x * x, axis=-1, keepdims=True)
    return x * lax.rsqrt(ms + NORM_EPS) * g


def _head_rms(x, g, ones):
    ms = _head_sum(x * x, ones) * (1.0 / HEAD_DIM)
    return x * lax.rsqrt(ms + NORM_EPS) * g


def _rope(x, cos, sin):
    t = x.shape[0]
    lane = lax.broadcasted_iota(jnp.int32, (t, 128), 1)
    first = (lane & 32) == 0
    outs = []
    for i in range(x.shape[1] // 128):
        xs = x[:, i * 128:(i + 1) * 128]
        swapped = jnp.where(first, pltpu.roll(xs, 96, axis=1), pltpu.roll(xs, 32, axis=1))
        outs.append(xs * cos + swapped * sin)
    return outs[0] if len(outs) == 1 else jnp.concatenate(outs, axis=1)


def _softmax_pv(scores, values, sink=None):
    m = scores[0].max(axis=-1, keepdims=True)
    for s in scores[1:]:
        m = jnp.maximum(m, s.max(axis=-1, keepdims=True))
    if sink is not None:
        m = jnp.maximum(m, sink)
    denom = None
    acc = None
    for s, v in zip(scores, values):
        p = jnp.exp(s - m)
        d = p.sum(axis=-1, keepdims=True)
        o = jnp.dot(p.astype(BF16), v, preferred_element_type=F32)
        denom = d if denom is None else denom + d
        acc = o if acc is None else acc + o
    if sink is not None:
        denom = denom + jnp.exp(sink - m)
    return acc / denom


def _softmax_pv_t(scores_t, values_t, sink=None):
    m = scores_t[0].max(axis=0, keepdims=True)
    for s in scores_t[1:]:
        m = jnp.maximum(m, s.max(axis=0, keepdims=True))
    if sink is not None:
        m = jnp.maximum(m, sink)
    denom = None
    acc = None
    for s, vt in zip(scores_t, values_t):
        p = jnp.exp(s - m)
        d = p.sum(axis=0, keepdims=True)
        o = jnp.dot(vt, p.astype(BF16), preferred_element_type=F32)
        denom = d if denom is None else denom + d
        acc = o if acc is None else acc + o
    if sink is not None:
        denom = denom + jnp.exp(sink - m)
    return acc / denom


def _attend_heads(score_fns, finish_fns):
    outs = []
    nxt = score_fns[0]()
    for i in range(len(score_fns)):
        cur = nxt
        if i + 1 < len(score_fns):
            nxt = score_fns[i + 1]()
        outs.append(finish_fns[i](cur))
    return outs


def _index32(i):
    return jnp.asarray(i, jnp.int32)


def _layer_spec(shape, layer):
    zeros = (0,) * len(shape)
    return pl.BlockSpec((None,) + shape, lambda *_: (layer,) + zeros, pipeline_mode=pl.Buffered(1))


def _mod_row(mod_ref, rows_per_mod):
    if rows_per_mod is None:
        return mod_ref[0:1]
    row = 1 + pl.program_id(0) // (rows_per_mod // ROW_TILE)
    return mod_ref[pl.ds(row, 1), :]


def _mod_body(c_ref, w_ref, b_ref, o_ref):
    cv = c_ref[...]
    s = cv * _sigmoid(cv)
    o_ref[0] = _bdot(s, w_ref[0]) + b_ref[0]


def _mod_call(cvec, w_mod, b_mod):
    bn = MOD_COLS // 4
    return pl.pallas_call(
        _mod_body,
        out_shape=jax.ShapeDtypeStruct((DEPTH, MOD_ROWS, MOD_COLS), F32),
        grid=(DEPTH, MOD_COLS // bn),
        in_specs=[
            pl.BlockSpec((MOD_ROWS, D_MODEL), lambda l, j: (0, 0)),
            pl.BlockSpec((1, D_MODEL, bn), lambda l, j: (l, 0, j)),
            pl.BlockSpec((1, 1, bn), lambda l, j: (l, 0, j)),
        ],
        out_specs=pl.BlockSpec((1, MOD_ROWS, bn), lambda l, j: (l, 0, j)),
        compiler_params=pltpu.CompilerParams(vmem_limit_bytes=VMEM_LIMIT),
        name="mod_vectors",
    )(cvec, w_mod, b_mod.reshape(DEPTH, 1, MOD_COLS))


def _proj_in_body(x_ref, mod_ref, g_ref, w_ref, za_ref, zc_ref, *, rows_per_mod):
    mod = _mod_row(mod_ref, rows_per_mod)
    shift1 = mod[:, 0:D_MODEL]
    scale1 = mod[:, D_MODEL:2 * D_MODEL]
    h = (_rms(x_ref[...], g_ref[0:1]) * (1.0 + scale1) + shift1).astype(BF16)
    z = jnp.dot(h, w_ref[...], preferred_element_type=F32)
    za_ref[...] = z[:, 0:ATT_COLS]
    zc_ref[...] = z[:, ATT_COLS:IN_COLS]


def _proj_in_call(x, mods, gvec, w_in_b, layer, rows_per_mod):
    n = x.shape[0]
    return pl.pallas_call(
        functools.partial(_proj_in_body, rows_per_mod=rows_per_mod),
        out_shape=(jax.ShapeDtypeStruct((n, ATT_COLS), F32), jax.ShapeDtypeStruct((n, RWKV_COLS), F32)),
        grid=(n // ROW_TILE,),
        in_specs=[
            pl.BlockSpec((ROW_TILE, D_MODEL), lambda i: (i, 0)),
            _layer_spec((MOD_ROWS, MOD_COLS), layer),
            _layer_spec((8, D_MODEL), layer),
            _layer_spec((D_MODEL, IN_COLS), layer),
        ],
        out_specs=(pl.BlockSpec((ROW_TILE, ATT_COLS), lambda i: (i, 0)),
                   pl.BlockSpec((ROW_TILE, RWKV_COLS), lambda i: (i, 0))),
        compiler_params=pltpu.CompilerParams(vmem_limit_bytes=VMEM_LIMIT),
        name="proj_in",
    )(x, mods, gvec, w_in_b)


def _attn_prompt_body(*refs, layer):
    sink_ref, za_ref, gq_ref, gk_ref = refs[:4]
    if layer == 0:
        o_ref, kv_ref = refs[4:]
    else:
        prev_ref, o_ref = refs[4:6]
        cache_refs = refs[6:]
    aq = (za_ref[:, 0:A_DIM] * QK_SCALE).astype(BF16)
    ak = za_ref[:, 256:384]
    bq = _head_rms(za_ref[:, 512:896], gq_ref[...], _head_ones(B_DIM))
    bk = _head_rms(za_ref[:, 896:1024], gk_ref[...], _head_ones(KV_DIM))
    bqb = (bq * QK_SCALE).astype(BF16)
    akb = ak.astype(BF16)
    bkb = bk.astype(BF16)
    pieces_t = (ak.T, za_ref[:, 384:512].T, bk.T, za_ref[:, 1024:1152].T)
    avt = pieces_t[1].astype(BF16)
    bvt = pieces_t[3].astype(BF16)

    score_fns, finish_fns = [], []
    for h in range(A_HEADS):
        kv = h // (A_HEADS // A_KV_HEADS)
        hs = slice(kv * 64, (kv + 1) * 64)
        score_fns.append(lambda h=h, hs=hs: [_bdot_nt(akb[:, hs], aq[:, h * 64:(h + 1) * 64])])
        finish_fns.append(lambda sc, h=h, hs=hs: _softmax_pv_t(sc, [avt[hs]], sink_ref[layer, h]))
    for h in range(B_HEADS):
        kv = h // (B_HEADS // B_KV_HEADS)
        hs = slice(kv * 64, (kv + 1) * 64)
        score_fns.append(lambda h=h, hs=hs: [_bdot_nt(bkb[:, hs], bqb[:, h * 64:(h + 1) * 64])])
        finish_fns.append(lambda sc, hs=hs: _softmax_pv_t(sc, [bvt[hs]]))
    o_ref[...] = jnp.concatenate(_attend_heads(score_fns, finish_fns), axis=0).T

    for j, piece_t in enumerate(pieces_t):
        if layer == 0:
            kv_ref[0, j] = piece_t
        else:
            cache_refs[j][0, 0] = prev_ref[0, j]
            cache_refs[j][0, 1] = piece_t


def _attn_prompt_call(za, sink, gq, gk, layer, prev_kv):
    nb = za.shape[0] // SEQ
    in_specs = [
        pl.BlockSpec(memory_space=pltpu.SMEM),
        pl.BlockSpec((SEQ, ATT_COLS), lambda b: (b, 0)),
        _layer_spec((1, B_DIM), layer),
        _layer_spec((1, KV_DIM), layer),
    ]
    args = [sink, za, gq, gk]
    out_shape = [jax.ShapeDtypeStruct((nb * SEQ, ATT_OUT), F32)]
    out_specs = [pl.BlockSpec((SEQ, ATT_OUT), lambda b: (b, 0))]
    slab_spec = pl.BlockSpec((1, N_CACHE, KV_DIM, SEQ), lambda b: (b, 0, 0, 0))
    if layer == 0:
        out_shape.append(jax.ShapeDtypeStruct((nb, N_CACHE, KV_DIM, SEQ), F32))
        out_specs.append(slab_spec)
    else:
        in_specs.append(slab_spec)
        args.append(prev_kv)
        for _ in range(N_CACHE):
            out_shape.append(jax.ShapeDtypeStruct((nb, DEPTH, KV_DIM, SEQ), F32))
            out_specs.append(pl.BlockSpec((1, DEPTH, KV_DIM, SEQ), lambda b: (b, 0, 0, 0)))
    return pl.pallas_call(
        functools.partial(_attn_prompt_body, layer=layer),
        out_shape=tuple(out_shape),
        grid=(nb,),
        in_specs=in_specs,
        out_specs=tuple(out_specs),
        compiler_params=pltpu.CompilerParams(vmem_limit_bytes=VMEM_LIMIT),
        name="attn_prompt",
    )(*args)


def _attn_sample_body(sink_ref, za_ref, cak_ref, cav_ref, cbk_ref, cbv_ref, cos_ref, sin_ref, gq_ref, gk_ref,
                      o_ref, qa_s, ka_s, va_s, qb_s, kb_s, vb_s, *, layer):
    t = DEC_SEQ
    cos = cos_ref[...]
    sin = sin_ref[...]
    qa_s[...] = (_rope(za_ref[:, 0:A_DIM], cos, sin) * QK_SCALE).astype(BF16)
    zpad = jnp.zeros((Q_BLK, KV_DIM), BF16)
    ka_s[0:Q_BLK] = zpad
    ka_s[Q_BLK + t:] = zpad
    va_s[0:Q_BLK] = zpad
    va_s[Q_BLK + t:] = zpad
    ka_s[Q_BLK:Q_BLK + t] = _rope(za_ref[:, 256:384], cos, sin).astype(BF16)
    va_s[Q_BLK:Q_BLK + t] = za_ref[:, 384:512].astype(BF16)
    bq = _head_rms(za_ref[:, 512:896], gq_ref[...], _head_ones(B_DIM))
    qb_s[...] = (_rope(bq, cos, sin) * QK_SCALE).astype(BF16)
    bk = _head_rms(za_ref[:, 896:1024], gk_ref[...], _head_ones(KV_DIM))
    kb_s[0:PAST_LEN] = cbk_ref[0].T.astype(BF16)
    kb_s[PAST_LEN:] = _rope(bk, cos, sin).astype(BF16)
    vb_s[0:PAST_LEN] = cbv_ref[0].T.astype(BF16)
    vb_s[PAST_LEN:] = za_ref[:, 1024:1152].astype(BF16)

    kca = cak_ref[0].T.astype(BF16)
    vca = cav_ref[0].T.astype(BF16)

    def a_block(n, carry):
        r0 = pl.multiple_of(_index32(n) * Q_BLK, Q_BLK)
        q = qa_s[pl.ds(r0, Q_BLK), :]
        kl = ka_s[pl.ds(r0, 3 * Q_BLK), :]
        vl = va_s[pl.ds(r0, 3 * Q_BLK), :]
        qpos = r0 + lax.broadcasted_iota(jnp.int32, (Q_BLK, 3 * Q_BLK), 0)
        kpos = r0 - Q_BLK + lax.broadcasted_iota(jnp.int32, (Q_BLK, 3 * Q_BLK), 1)
        mask = (jnp.abs(kpos - qpos) <= WINDOW) & (kpos >= 0) & (kpos < t)
        score_fns, finish_fns = [], []
        for h in range(A_HEADS):
            kv = h // (A_HEADS // A_KV_HEADS)
            hs = slice(kv * 64, (kv + 1) * 64)

            def scores(h=h, hs=hs):
                qh = q[:, h * 64:(h + 1) * 64]
                return [jnp.where(mask, _bdot_nt(qh, kl[:, hs]), NEG_INF), _bdot_nt(qh, kca[:, hs])]

            score_fns.append(scores)
            finish_fns.append(lambda sc, h=h, hs=hs: _softmax_pv(sc, [vl[:, hs], vca[:, hs]], sink_ref[layer, h]))
        o_ref[pl.ds(r0, Q_BLK), 0:A_DIM] = jnp.concatenate(_attend_heads(score_fns, finish_fns), axis=1)
        return carry

    lax.fori_loop(0, t // Q_BLK, a_block, 0)

    def b_block(n, carry):
        r0 = pl.multiple_of(_index32(n) * B_QROWS, B_QROWS)
        q = qb_s[pl.ds(r0, B_QROWS), :]
        score_fns, finish_fns = [], []
        for h in range(B_HEADS):
            kv = h // (B_HEADS // B_KV_HEADS)
            hs = slice(kv * 64, (kv + 1) * 64)
            score_fns.append(lambda h=h, hs=hs: [_bdot_nt(q[:, h * 64:(h + 1) * 64], kb_s[:, hs])])
            finish_fns.append(lambda sc, hs=hs: _softmax_pv(sc, [vb_s[:, hs]]))
        o_ref[pl.ds(r0, B_QROWS), A_DIM:ATT_OUT] = jnp.concatenate(_attend_heads(score_fns, finish_fns), axis=1)
        return carry

    lax.fori_loop(0, t // B_QROWS, b_block, 0)


def _attn_sample_call(za, sink, caches, cos, sin, gq, gk, layer):
    nb = za.shape[0] // DEC_SEQ
    t = DEC_SEQ
    cache_spec = pl.BlockSpec((1, None, KV_DIM, PAST_LEN), lambda b: (b, layer, 0, 0))
    return pl.pallas_call(
        functools.partial(_attn_sample_body, layer=layer),
        out_shape=jax.ShapeDtypeStruct((nb * t, ATT_OUT), F32),
        grid=(nb,),
        in_specs=[
            pl.BlockSpec(memory_space=pltpu.SMEM),
            pl.BlockSpec((t, ATT_COLS), lambda b: (b, 0)),
            cache_spec, cache_spec, cache_spec, cache_spec,
            pl.BlockSpec((t, 128), lambda b: (0, 0)),
            pl.BlockSpec((t, 128), lambda b: (0, 0)),
            _layer_spec((1, B_DIM), layer),
            _layer_spec((1, KV_DIM), layer),
        ],
        out_specs=pl.BlockSpec((t, ATT_OUT), lambda b: (b, 0)),
        scratch_shapes=[
            pltpu.VMEM((t, A_DIM), BF16),
            pltpu.VMEM((t + 2 * Q_BLK, KV_DIM), BF16),
            pltpu.VMEM((t + 2 * Q_BLK, KV_DIM), BF16),
            pltpu.VMEM((t, B_DIM), BF16),
            pltpu.VMEM((t + PAST_LEN, KV_DIM), BF16),
            pltpu.VMEM((t + PAST_LEN, KV_DIM), BF16),
        ],
        compiler_params=pltpu.CompilerParams(vmem_limit_bytes=VMEM_LIMIT),
        name="attn_sample",
    )(sink, za, *caches, cos, sin, gq, gk)


def _rwkv_body(*refs, t, nbatch, has_init, want_state, has_prev):
    refs = list(refs)
    zc_ref, cv_ref, wup_ref, aup_ref, gup_ref = refs[:5]
    pos = 5
    st0_ref = prev_ref = stout_ref = None
    if has_init:
        st0_ref = refs[pos]
        pos += 1
    if has_prev:
        prev_ref = refs[pos]
        pos += 1
    oc_ref = refs[pos]
    pos += 1
    if want_state:
        stout_ref = refs[pos]
        pos += 1
    aa_s, ld_s, b_s, kt_s, bon_s, y_s, st_s = refs[pos:]

    nchunks = t // CHUNK
    ones = _head_ones(C_DIM)
    k_k = cv_ref[0:1]
    k_a = cv_ref[1:2]
    r_k = cv_ref[2:3]
    ln_w = cv_ref[3:4]
    ln_b = cv_ref[4:5]
    w0 = (cv_ref[5:6], cv_ref[6:7])
    a0 = (cv_ref[7:8], cv_ref[8:9])
    wup = (wup_ref[0].astype(BF16), wup_ref[1].astype(BF16))
    aup = (aup_ref[0].astype(BF16), aup_ref[1].astype(BF16))

    for rb in range(nbatch * t // RWKV_ROWS):
        rows = slice(rb * RWKV_ROWS, (rb + 1) * RWKV_ROWS)
        zr = zc_ref[rows, 0:384]
        zk = zc_ref[rows, 384:768]
        zv = zc_ref[rows, 768:1152]
        tw = jnp.tanh(zc_ref[rows, 1152:1216])
        xa = zc_ref[rows, 1216:1280]
        kkr = zk * k_k
        kk = kkr * lax.rsqrt(_head_sum(kkr * kkr, ones) + 1e-12)
        aa_s[rows] = -kk
        kt_sum = None
        for d in range(2):
            wl = w0[d] + _bdot(tw, wup[d])
            ld_s[d, rows] = (-DECAY_RATE) * _sigmoid(wl)
            a = _sigmoid(a0[d] + _bdot(xa, aup[d]))
            kt = zk * (1.0 + (a - 1.0) * k_a)
            kt_s[d, rows] = kt
            b_s[d, rows] = kk * a
            kt_sum = kt if kt_sum is None else kt_sum + kt
        bon_s[rows] = _head_sum(zr * kt_sum * r_k, ones) * zv
        y_s[rows] = jnp.zeros((RWKV_ROWS, C_DIM), F32)

    ir = lax.broadcasted_iota(jnp.int32, (CHUNK, 128), 0)
    ic = lax.broadcasted_iota(jnp.int32, (CHUNK, 128), 1) & 63
    eye = jnp.where(ir == ic, 1.0, 0.0).astype(F32)
    pair_masks = []
    for j in range(CHUNK.bit_length() - 1):
        same_big = lax.shift_right_logical(ir, jnp.int32(j + 1)) == lax.shift_right_logical(ic, jnp.int32(j + 1))
        same_small = lax.shift_right_logical(ir, jnp.int32(j)) == lax.shift_right_logical(ic, jnp.int32(j))
        pair_masks.append(same_big & jnp.logical_not(same_small))
    incl = (ic <= ir, ic >= ir)
    strict = (ic < ir, ic > ir)
    sr = lax.broadcasted_iota(jnp.int32, (CHUNK, CHUNK), 0)
    sc = lax.broadcasted_iota(jnp.int32, (CHUNK, CHUNK), 1)
    tri = tuple(jnp.where(m, 1.0, 0.0).astype(BF16) for m in (sc <= sr, sc >= sr))
    br = lax.broadcasted_iota(jnp.int32, (128, 128), 0)
    bc = lax.broadcasted_iota(jnp.int32, (128, 128), 1)
    diag_blocks = (br < 64) == (bc < 64)
    left_half = lax.broadcasted_iota(jnp.int32, (CHUNK, 128), 1) < 64

    def blockdiag(x2):
        x2 = x2.astype(BF16)
        return jnp.where(diag_blocks, jnp.concatenate([x2, x2], axis=0), jnp.zeros((128, 128), BF16))

    npairs = C_HEADS // 2
    for bi in range(nbatch):
        for d in range(2):
            for p in range(npairs):
                if has_init:
                    st_s[bi, d, p] = jnp.concatenate([st0_ref[bi, d, 2 * p], st0_ref[bi, d, 2 * p + 1]], axis=1)
                else:
                    st_s[bi, d, p] = jnp.zeros((HEAD_DIM, 128), F32)

    def chunk(c, carry):
        c = _index32(c)
        items = []
        for bi, d in [(bi, d) for bi in range(nbatch) for d in range(2)]:
            r0 = bi * t + ((nchunks - 1 - c) if d == 1 else c) * CHUNK
            rows = pl.ds(pl.multiple_of(r0, CHUNK), CHUNK)
            ld = ld_s[d, rows]
            cinc = _split_dot_left(tri[d], ld)
            cexc = cinc - ld
            ctot = cinc[0:1] if d == 1 else cinc[CHUNK - 1:CHUNK]
            e_ninc = jnp.exp(-cinc)
            e_rem = jnp.exp(ctot - cinc)
            e_tot = jnp.exp(ctot)
            bb = b_s[d, rows]
            kt = kt_s[d, rows]
            at = (aa_s[rows] * jnp.exp(cexc)).astype(BF16)
            rt = (zc_ref[rows, 0:384] * jnp.exp(cinc)).astype(BF16)
            bt = (bb * e_ninc).astype(BF16)
            ktt = (kt * e_ninc).astype(BF16)
            bh = (bb * e_rem).astype(BF16)
            kh = (kt * e_rem).astype(BF16)
            vm = zc_ref[rows, 768:1152].astype(BF16)
            for p in range(npairs):
                ps = slice(p * 128, (p + 1) * 128)
                items.append(dict(bi=bi, d=d, p=p, rows=rows, e_tot=e_tot[:, ps], at=at[:, ps], rt=rt[:, ps],
                                  bt=bt[:, ps], kt=ktt[:, ps], bh=bh[:, ps], kh=kh[:, ps], vm=vm[:, ps]))
        for it in items:
            it["ar"] = jnp.concatenate([it["at"], it["rt"]], axis=0)
            it["vmd"] = blockdiag(it["vm"])
        for it in items:
            it["sb"] = _bdot_nt(it["ar"], blockdiag(it["bt"]))
        for it in items:
            it["sk"] = _bdot_nt(it["ar"], blockdiag(it["kt"]))
        for it in items:
            d = it["d"]
            it["l"] = jnp.where(strict[d], it["sb"][0:CHUNK], 0.0)
            it["mak"] = jnp.where(strict[d], it["sk"][0:CHUNK], 0.0).astype(BF16)
            it["nr"] = jnp.concatenate([jnp.where(incl[d], it["sb"][CHUNK:], 0.0).astype(BF16),
                                        jnp.where(incl[d], it["sk"][CHUNK:], 0.0).astype(BF16)], axis=1)
            it["tm"] = eye + jnp.where(pair_masks[0], it["l"], 0.0)
        for pm in pair_masks[1:]:
            for it in items:
                it["t1"] = _bdot(it["tm"], blockdiag(jnp.where(pm, it["l"], 0.0)))
            for it in items:
                it["tm"] = it["tm"] + _bdot(it["t1"], blockdiag(it["tm"]))
        for it in items:
            it["mv"] = _bdot(it["mak"], it["vmd"])
        for it in items:
            it["tau"] = _bdot(it["tm"], jnp.concatenate([blockdiag(it["at"]), blockdiag(it["mv"])], axis=1))
        for it in items:
            it["s0"] = st_s[it["bi"], it["d"], it["p"]]
            it["x"] = _bdot_nt(jnp.concatenate([it["tau"][:, 0:128].astype(BF16), it["rt"]], axis=0),
                               blockdiag(it["s0"]))
        for it in items:
            it["u"] = (it["x"][0:CHUNK] + it["tau"][:, 128:256]).astype(BF16)
        for it in items:
            it["y"] = it["x"][CHUNK:] + _bdot(it["nr"], jnp.concatenate([blockdiag(it["u"]), it["vmd"]], axis=0))
        for it in items:
            full = _bdot_tn(jnp.concatenate([it["u"], it["vm"]], axis=0),
                            jnp.concatenate([it["bh"], it["kh"]], axis=0))
            st_s[it["bi"], it["d"], it["p"]] = (it["s0"] * it["e_tot"]
                                                + jnp.where(left_half, full[0:64], full[64:128]))
        for i in range(0, len(items), npairs):
            rows = items[i]["rows"]
            y_s[rows] = y_s[rows] + jnp.concatenate([it["y"] for it in items[i:i + npairs]], axis=1)
        return carry

    lax.fori_loop(0, nchunks, chunk, 0)
    if want_state:
        for bi in range(nbatch):
            if has_prev:
                stout_ref[bi, 0] = prev_ref[bi]
            for d in range(2):
                for p in range(npairs):
                    s2 = st_s[bi, d, p]
                    for q in range(2):
                        if has_prev:
                            stout_ref[bi, 1, d, 2 * p + q] = s2[:, q * 64:(q + 1) * 64]
                        else:
                            stout_ref[bi, d, 2 * p + q] = s2[:, q * 64:(q + 1) * 64]

    gup = gup_ref[...].astype(BF16)
    for rb in range(nbatch * t // RWKV_ROWS):
        rows = slice(rb * RWKV_ROWS, (rb + 1) * RWKV_ROWS)
        y = y_s[rows]
        mu = _head_sum(y, ones) * (1.0 / HEAD_DIM)
        yc = y - mu
        var = _head_sum(yc * yc, ones) * (1.0 / HEAD_DIM)
        yn = yc * lax.rsqrt(var + GN_EPS)
        g = _bdot(_sigmoid(zc_ref[rows, 1280:1408]), gup)
        oc_ref[rows] = (yn * ln_w + ln_b + bon_s[rows]) * g


def _rwkv_call(zc, rvec, wup, aup, gup, layer, t, init_state, want_state, prev_state):
    nb = zc.shape[0] // t
    nbatch = RWKV_BATCH
    rows = nbatch * t
    has_init = init_state is not None
    has_prev = prev_state is not None
    st_shape = (2, C_HEADS, HEAD_DIM, HEAD_DIM)
    in_specs = [
        pl.BlockSpec((rows, RWKV_COLS), lambda b: (b, 0), pipeline_mode=pl.Buffered(1 if nb == nbatch else 2)),
        _layer_spec((16, C_DIM), layer),
        _layer_spec((2, W_RANK, C_DIM), layer),
        _layer_spec((2, A_RANK, C_DIM), layer),
        _layer_spec((G_RANK, C_DIM), layer),
    ]
    args = [zc, rvec, wup, aup, gup]
    if has_init:
        in_specs.append(pl.BlockSpec((nbatch, None) + st_shape, lambda b: (b, layer, 0, 0, 0, 0)))
        args.append(init_state)
    if has_prev:
        in_specs.append(pl.BlockSpec((nbatch,) + st_shape, lambda b: (b, 0, 0, 0, 0)))
        args.append(prev_state)
    out_shape = [jax.ShapeDtypeStruct((nb * t, C_DIM), F32)]
    out_specs = [pl.BlockSpec((rows, C_DIM), lambda b: (b, 0))]
    if want_state and has_prev:
        out_shape.append(jax.ShapeDtypeStruct((nb, DEPTH) + st_shape, F32))
        out_specs.append(pl.BlockSpec((nbatch, DEPTH) + st_shape, lambda b: (b, 0, 0, 0, 0, 0)))
    elif want_state:
        out_shape.append(jax.ShapeDtypeStruct((nb,) + st_shape, F32))
        out_specs.append(pl.BlockSpec((nbatch,) + st_shape, lambda b: (b, 0, 0, 0, 0)))
    return pl.pallas_call(
        functools.partial(_rwkv_body, t=t, nbatch=nbatch, has_init=has_init, want_state=want_state,
                          has_prev=has_prev),
        out_shape=tuple(out_shape),
        grid=(nb // nbatch,),
        in_specs=in_specs,
        out_specs=tuple(out_specs),
        scratch_shapes=[
            pltpu.VMEM((rows, C_DIM), F32),
            pltpu.VMEM((2, rows, C_DIM), F32),
            pltpu.VMEM((2, rows, C_DIM), F32),
            pltpu.VMEM((2, rows, C_DIM), F32),
            pltpu.VMEM((rows, C_DIM), F32),
            pltpu.VMEM((rows, C_DIM), F32),
            pltpu.VMEM((nbatch, 2, C_HEADS // 2, HEAD_DIM, 128), F32),
        ],
        compiler_params=pltpu.CompilerParams(vmem_limit_bytes=VMEM_LIMIT),
        name="rwkv_prompt" if want_state else "rwkv_sample",
    )(*args)


def _post_body(oa_ref, oc_ref, x_ref, mod_ref, g_ref, wout_ref, wgu_ref, wdn_ref, xo_ref, *, rows_per_mod):
    mod = _mod_row(mod_ref, rows_per_mod)
    gate1 = mod[:, 2 * D_MODEL:3 * D_MODEL]
    shift2 = mod[:, 3 * D_MODEL:4 * D_MODEL]
    scale2 = mod[:, 4 * D_MODEL:5 * D_MODEL]
    gate2 = mod[:, 5 * D_MODEL:6 * D_MODEL]
    o_cat = jnp.concatenate([oa_ref[...].astype(BF16), oc_ref[...].astype(BF16)], axis=1)
    o = jnp.dot(o_cat, wout_ref[...], preferred_element_type=F32)
    x1 = x_ref[...] + gate1 * _rms(o, g_ref[1:2])
    h2 = (_rms(x1, g_ref[2:3]) * (1.0 + scale2) + shift2).astype(BF16)
    acc = None
    for lo, hi in FF_CHUNKS:
        g = jnp.dot(h2, wgu_ref[:, lo:hi], preferred_element_type=F32)
        u = jnp.dot(h2, wgu_ref[:, FF_DIM + lo:FF_DIM + hi], preferred_element_type=F32)
        act = (g * _sigmoid(g) * u).astype(BF16)
        part = jnp.dot(act, wdn_ref[lo:hi, :], preferred_element_type=F32)
        acc = part if acc is None else acc + part
    xo_ref[...] = x1 + gate2 * _rms(acc, g_ref[3:4])


def _post_call(oatt, oc, x, mods, gvec, wout_b, wgu_b, wdn_b, layer, rows_per_mod):
    n = x.shape[0]
    return pl.pallas_call(
        functools.partial(_post_body, rows_per_mod=rows_per_mod),
        out_shape=jax.ShapeDtypeStruct((n, D_MODEL), F32),
        grid=(n // ROW_TILE,),
        in_specs=[
            pl.BlockSpec((ROW_TILE, ATT_OUT), lambda i: (i, 0)),
            pl.BlockSpec((ROW_TILE, C_DIM), lambda i: (i, 0)),
            pl.BlockSpec((ROW_TILE, D_MODEL), lambda i: (i, 0)),
            _layer_spec((MOD_ROWS, MOD_COLS), layer),
            _layer_spec((8, D_MODEL), layer),
            _layer_spec((MIX_DIM, D_MODEL), layer),
            _layer_spec((D_MODEL, 2 * FF_DIM), layer),
            _layer_spec((FF_DIM, D_MODEL), layer),
        ],
        out_specs=pl.BlockSpec((ROW_TILE, D_MODEL), lambda i: (i, 0)),
        compiler_params=pltpu.CompilerParams(vmem_limit_bytes=VMEM_LIMIT),
        name="post_ffn",
    )(oatt, oc, x, mods, gvec, wout_b, wgu_b, wdn_b)


def _rope_tables(t):
    pos = jnp.arange(t)
    row = (pos // GRID_W).astype(F32)
    col = (pos % GRID_W).astype(F32)
    freqs = ROPE_THETA ** (-jnp.arange(ROPE_PAIRS_AXIS, dtype=F32) / ROPE_PAIRS_AXIS)
    ang = jnp.concatenate([row[:, None] * freqs, col[:, None] * freqs], axis=-1)
    cos = jnp.tile(jnp.cos(ang), (1, 4))
    sin = jnp.sin(ang)
    sin = jnp.tile(jnp.concatenate([-sin, sin], axis=-1), (1, 2))
    return cos, sin


def _to_time_minor(cache):
    b, l, t = cache.shape[:3]
    return jnp.transpose(cache, (0, 1, 3, 4, 2)).reshape(b, l, KV_DIM, t)


def _from_time_minor(slab):
    b, l, _, t = slab.shape
    return jnp.transpose(slab.reshape(b, l, 2, HEAD_DIM, t), (0, 1, 4, 2, 3))


def kernel(x_prompt, x_sample, cache_a_k, cache_a_v, cache_b_k, cache_b_v, state_c, c, c_ctx, w_mod, b_mod, norm_mix_pre, norm_mix_post, norm_ffn_pre, norm_ffn_post, w_in, w_out, a_sink, b_q_norm, b_k_norm, c_w0, c_w_up, c_a0, c_a_up, c_g_up, c_k_k, c_k_a, c_r_k, c_ln_w, c_ln_b, w_gu, w_down):
    cvec = jnp.zeros((MOD_ROWS, D_MODEL), F32).at[0].set(c_ctx).at[1:1 + DEC_BATCH].set(c)
    mods = _mod_call(cvec, w_mod, b_mod)
    cos, sin = _rope_tables(DEC_SEQ)

    w_in_b = w_in.astype(BF16)
    wout_b = w_out.astype(BF16)
    wgu_b = w_gu.astype(BF16)
    wdn_b = w_down.astype(BF16)
    gvec = jnp.zeros((DEPTH, 8, D_MODEL), F32)
    gvec = gvec.at[:, 0].set(norm_mix_pre).at[:, 1].set(norm_mix_post)
    gvec = gvec.at[:, 2].set(norm_ffn_pre).at[:, 3].set(norm_ffn_post)
    rvec = jnp.zeros((DEPTH, 16, C_DIM), F32)
    rvec = rvec.at[:, 0].set(c_k_k).at[:, 1].set(c_k_a).at[:, 2].set(c_r_k.reshape(DEPTH, C_DIM))
    rvec = rvec.at[:, 3].set(c_ln_w).at[:, 4].set(c_ln_b)
    rvec = rvec.at[:, 5:7].set(c_w0).at[:, 7:9].set(c_a0)
    gq = jnp.tile(b_q_norm, (1, B_HEADS))[:, None, :]
    gk = jnp.tile(b_k_norm, (1, B_KV_HEADS))[:, None, :]
    caches = tuple(_to_time_minor(t) for t in (cache_a_k, cache_a_v, cache_b_k, cache_b_v))

    xp = x_prompt.reshape(BATCH * SEQ, D_MODEL)
    xs = x_sample.reshape(DEC_BATCH * DEC_SEQ, D_MODEL)
    kv_prev = st_prev = None
    for l in range(DEPTH):
        za, zc = _proj_in_call(xp, mods, gvec, w_in_b, l, None)
        oatt, *kv_prev = _attn_prompt_call(za, a_sink, gq, gk, l, kv_prev)
        if l == 0:
            (kv_prev,) = kv_prev
        oc, st_prev = _rwkv_call(zc, rvec, c_w_up, c_a_up, c_g_up, l, SEQ, None, True, st_prev)
        xp = _post_call(oatt, oc, xp, mods, gvec, wout_b, wgu_b, wdn_b, l, None)

        za, zc = _proj_in_call(xs, mods, gvec, w_in_b, l, DEC_SEQ)
        oatt = _attn_sample_call(za, a_sink, caches, cos, sin, gq, gk, l)
        (oc,) = _rwkv_call(zc, rvec, c_w_up, c_a_up, c_g_up, l, DEC_SEQ, state_c, False, None)
        xs = _post_call(oatt, oc, xs, mods, gvec, wout_b, wgu_b, wdn_b, l, DEC_SEQ)

    new_caches = tuple(_from_time_minor(slab) for slab in kv_prev)
    return (xp.reshape(BATCH, SEQ, D_MODEL), xs.reshape(DEC_BATCH, DEC_SEQ, D_MODEL), *new_caches, st_prev)
```

```python
import functools
import math

import jax
import jax.numpy as jnp
from jax import lax
from jax.experimental import pallas as pl
from jax.experimental.pallas import tpu as pltpu

D_MODEL = 1024
BATCH = 32
SEQ = 256
DEPTH = 2
DEC_BATCH = 2
DEC_SEQ = 1024
PAST_LEN = 256
GRID_W = 64
HEAD_DIM = 64
A_HEADS = 4
A_KV_HEADS = 2
B_HEADS = 6
B_KV_HEADS = 2
C_HEADS = 6
A_DIM = A_HEADS * HEAD_DIM
B_DIM = B_HEADS * HEAD_DIM
C_DIM = C_HEADS * HEAD_DIM
MIX_DIM = A_DIM + B_DIM + C_DIM
WINDOW = 128
Q_BLK = 128
W_RANK = 64
A_RANK = 64
G_RANK = 128
FF_DIM = -(-8 * D_MODEL // (3 * 256)) * 256
ROPE_THETA = 10000.0
ROPE_PAIRS_AXIS = HEAD_DIM // 4
NORM_EPS = 1e-6
GN_EPS = 64e-5
NEG_INF = -1e30

KV_DIM = 2 * HEAD_DIM
ATT_COLS = A_DIM + 2 * KV_DIM + B_DIM + 2 * KV_DIM
RWKV_COLS = 3 * C_DIM + W_RANK + A_RANK + G_RANK
IN_COLS = ATT_COLS + RWKV_COLS
ATT_OUT = A_DIM + B_DIM
MOD_COLS = 6 * D_MODEL
MOD_ROWS = 8
N_CACHE = 4
CHUNK = 64
QK_SCALE = HEAD_DIM ** -0.5
DECAY_RATE = math.exp(-0.5)
ROW_TILE = 512
RWKV_ROWS = 256
RWKV_BATCH = 2
B_QROWS = 256
MXU_TILE = 256
FF_CHUNKS = ((0, 4 * MXU_TILE), (4 * MXU_TILE, 8 * MXU_TILE), (8 * MXU_TILE, FF_DIM))
assert FF_DIM % MXU_TILE == 0
VMEM_LIMIT = 56 * 1024 * 1024

F32 = jnp.float32
BF16 = jnp.bfloat16

assert DEPTH == 2


def _bdot(a, b):
    return jnp.dot(a.astype(BF16), b.astype(BF16), preferred_element_type=F32)


def _bdot_nt(a, b):
    return lax.dot_general(a.astype(BF16), b.astype(BF16), (((1,), (1,)), ((), ())), preferred_element_type=F32)


def _bdot_tn(a, b):
    return lax.dot_general(a.astype(BF16), b.astype(BF16), (((0,), (0,)), ((), ())), preferred_element_type=F32)


def _split(x):
    hi = x.astype(BF16)
    lo = (x - hi.astype(F32)).astype(BF16)
    return hi, lo


def _split_dot_left(m, x):
    hi, lo = _split(x)
    return jnp.dot(m, hi, preferred_element_type=F32) + jnp.dot(m, lo, preferred_element_type=F32)


def _head_sum(x, ones):
    return jnp.dot(x.astype(BF16), ones, preferred_element_type=F32)


def _sigmoid(x):
    return 0.5 * jnp.tanh(0.5 * x) + 0.5


def _head_ones(n):
    r = lax.shift_right_logical(lax.broadcasted_iota(jnp.int32, (n, n), 0), jnp.int32(6))
    c = lax.shift_right_logical(lax.broadcasted_iota(jnp.int32, (n, n), 1), jnp.int32(6))
    return jnp.where(r == c, 1.0, 0.0).astype(BF16)


def _rms(x, g):
    ms = jnp.mean(x * x, axis=-1, keepdims=True)
    return x * lax.rsqrt(ms + NORM_EPS) * g


def _head_rms(x, g, ones):
    ms = _head_sum(x * x, ones) * (1.0 / HEAD_DIM)
    return x * lax.rsqrt(ms + NORM_EPS) * g


def _rope(x, cos, sin):
    t = x.shape[0]
    lane = lax.broadcasted_iota(jnp.int32, (t, 128), 1)
    first = (lane & 32) == 0
    outs = []
    for i in range(x.shape[1] // 128):
        xs = x[:, i * 128:(i + 1) * 128]
        swapped = jnp.where(first, pltpu.roll(xs, 96, axis=1), pltpu.roll(xs, 32, axis=1))
        outs.append(xs * cos + swapped * sin)
    return outs[0] if len(outs) == 1 else jnp.concatenate(outs, axis=1)


def _softmax_pv(scores, values, sink=None):
    m = scores[0].max(axis=-1, keepdims=True)
    for s in scores[1:]:
        m = jnp.maximum(m, s.max(axis=-1, keepdims=True))
    if sink is not None:
        m = jnp.maximum(m, sink)
    denom = None
    acc = None
    for s, v in zip(scores, values):
        p = jnp.exp(s - m)
        d = p.sum(axis=-1, keepdims=True)
        o = jnp.dot(p.astype(BF16), v, preferred_element_type=F32)
        denom = d if denom is None else denom + d
        acc = o if acc is None else acc + o
    if sink is not None:
        denom = denom + jnp.exp(sink - m)
    return acc / denom


def _softmax_pv_t(scores_t, values_t, sink=None):
    m = scores_t[0].max(axis=0, keepdims=True)
    for s in scores_t[1:]:
        m = jnp.maximum(m, s.max(axis=0, keepdims=True))
    if sink is not None:
        m = jnp.maximum(m, sink)
    denom = None
    acc = None
    for s, vt in zip(scores_t, values_t):
        p = jnp.exp(s - m)
        d = p.sum(axis=0, keepdims=True)
        o = jnp.dot(vt, p.astype(BF16), preferred_element_type=F32)
        denom = d if denom is None else denom + d
        acc = o if acc is None else acc + o
    if sink is not None:
        denom = denom + jnp.exp(sink - m)
    return acc / denom


def _attend_heads(score_fns, finish_fns, ahead=1):
    n = len(score_fns)
    pending = [fn() for fn in score_fns[:ahead]]
    outs = []
    for i in range(n):
        if i + ahead < n:
            pending.append(score_fns[i + ahead]())
        outs.append(finish_fns[i](pending.pop(0)))
    return outs


def _index32(i):
    return jnp.asarray(i, jnp.int32)


def _layer_spec(shape, layer):
    zeros = (0,) * len(shape)
    return pl.BlockSpec((None,) + shape, lambda *_: (layer,) + zeros, pipeline_mode=pl.Buffered(1))


def _mod_row(mod_ref, rows_per_mod):
    if rows_per_mod is None:
        return mod_ref[0:1]
    row = 1 + pl.program_id(0) // (rows_per_mod // ROW_TILE)
    return mod_ref[pl.ds(row, 1), :]


def _mod_body(c_ref, w_ref, b_ref, o_ref):
    cv = c_ref[...]
    s = cv * _sigmoid(cv)
    o_ref[0] = _bdot(s, w_ref[0]) + b_ref[0]


def _mod_call(cvec, w_mod, b_mod):
    bn = MOD_COLS // 4
    return pl.pallas_call(
        _mod_body,
        out_shape=jax.ShapeDtypeStruct((DEPTH, MOD_ROWS, MOD_COLS), F32),
        grid=(DEPTH, MOD_COLS // bn),
        in_specs=[
            pl.BlockSpec((MOD_ROWS, D_MODEL), lambda l, j: (0, 0)),
            pl.BlockSpec((1, D_MODEL, bn), lambda l, j: (l, 0, j)),
            pl.BlockSpec((1, 1, bn), lambda l, j: (l, 0, j)),
        ],
        out_specs=pl.BlockSpec((1, MOD_ROWS, bn), lambda l, j: (l, 0, j)),
        compiler_params=pltpu.CompilerParams(vmem_limit_bytes=VMEM_LIMIT),
        name="mod_vectors",
    )(cvec, w_mod, b_mod.reshape(DEPTH, 1, MOD_COLS))


def _proj_in_body(x_ref, mod_ref, g_ref, w_ref, za_ref, zc_ref, *, rows_per_mod):
    mod = _mod_row(mod_ref, rows_per_mod)
    shift1 = mod[:, 0:D_MODEL]
    scale1 = mod[:, D_MODEL:2 * D_MODEL]
    h = (_rms(x_ref[...], g_ref[0:1]) * (1.0 + scale1) + shift1).astype(BF16)
    z = jnp.dot(h, w_ref[...], preferred_element_type=F32)
    za_ref[...] = z[:, 0:ATT_COLS]
    zc_ref[...] = z[:, ATT_COLS:IN_COLS]


def _proj_in_call(x, mods, gvec, w_in_b, layer, rows_per_mod):
    n = x.shape[0]
    return pl.pallas_call(
        functools.partial(_proj_in_body, rows_per_mod=rows_per_mod),
        out_shape=(jax.ShapeDtypeStruct((n, ATT_COLS), F32), jax.ShapeDtypeStruct((n, RWKV_COLS), F32)),
        grid=(n // ROW_TILE,),
        in_specs=[
            pl.BlockSpec((ROW_TILE, D_MODEL), lambda i: (i, 0)),
            _layer_spec((MOD_ROWS, MOD_COLS), layer),
            _layer_spec((8, D_MODEL), layer),
            _layer_spec((D_MODEL, IN_COLS), layer),
        ],
        out_specs=(pl.BlockSpec((ROW_TILE, ATT_COLS), lambda i: (i, 0)),
                   pl.BlockSpec((ROW_TILE, RWKV_COLS), lambda i: (i, 0))),
        compiler_params=pltpu.CompilerParams(vmem_limit_bytes=VMEM_LIMIT),
        name="proj_in",
    )(x, mods, gvec, w_in_b)


def _attn_prompt_body(*refs, layer):
    sink_ref, za_ref, gq_ref, gk_ref = refs[:4]
    if layer == 0:
        o_ref, kv_ref = refs[4:]
    else:
        prev_ref, o_ref = refs[4:6]
        cache_refs = refs[6:]
    aq = (za_ref[:, 0:A_DIM] * QK_SCALE).astype(BF16)
    ak = za_ref[:, 256:384]
    bq = _head_rms(za_ref[:, 512:896], gq_ref[...], _head_ones(B_DIM))
    bk = _head_rms(za_ref[:, 896:1024], gk_ref[...], _head_ones(KV_DIM))
    bqb = (bq * QK_SCALE).astype(BF16)
    akb = ak.astype(BF16)
    bkb = bk.astype(BF16)
    pieces_t = (ak.T, za_ref[:, 384:512].T, bk.T, za_ref[:, 1024:1152].T)
    avt = pieces_t[1].astype(BF16)
    bvt = pieces_t[3].astype(BF16)

    score_fns, finish_fns = [], []
    for h in range(A_HEADS):
        kv = h // (A_HEADS // A_KV_HEADS)
        hs = slice(kv * 64, (kv + 1) * 64)
        score_fns.append(lambda h=h, hs=hs: [_bdot_nt(akb[:, hs], aq[:, h * 64:(h + 1) * 64])])
        finish_fns.append(lambda sc, h=h, hs=hs: _softmax_pv_t(sc, [avt[hs]], sink_ref[layer, h]))
    for h in range(B_HEADS):
        kv = h // (B_HEADS // B_KV_HEADS)
        hs = slice(kv * 64, (kv + 1) * 64)
        score_fns.append(lambda h=h, hs=hs: [_bdot_nt(bkb[:, hs], bqb[:, h * 64:(h + 1) * 64])])
        finish_fns.append(lambda sc, hs=hs: _softmax_pv_t(sc, [bvt[hs]]))
    o_ref[...] = jnp.concatenate(_attend_heads(score_fns, finish_fns, ahead=len(score_fns)), axis=0).T

    for j, piece_t in enumerate(pieces_t):
        if layer == 0:
            kv_ref[0, j] = piece_t
        else:
            cache_refs[j][0, 0] = prev_ref[0, j]
            cache_refs[j][0, 1] = piece_t


def _attn_prompt_call(za, sink, gq, gk, layer, prev_kv):
    nb = za.shape[0] // SEQ
    in_specs = [
        pl.BlockSpec(memory_space=pltpu.SMEM),
        pl.BlockSpec((SEQ, ATT_COLS), lambda b: (b, 0)),
        _layer_spec((1, B_DIM), layer),
        _layer_spec((1, KV_DIM), layer),
    ]
    args = [sink, za, gq, gk]
    out_shape = [jax.ShapeDtypeStruct((nb * SEQ, ATT_OUT), F32)]
    out_specs = [pl.BlockSpec((SEQ, ATT_OUT), lambda b: (b, 0))]
    slab_spec = pl.BlockSpec((1, N_CACHE, KV_DIM, SEQ), lambda b: (b, 0, 0, 0))
    if layer == 0:
        out_shape.append(jax.ShapeDtypeStruct((nb, N_CACHE, KV_DIM, SEQ), F32))
        out_specs.append(slab_spec)
    else:
        in_specs.append(slab_spec)
        args.append(prev_kv)
        for _ in range(N_CACHE):
            out_shape.append(jax.ShapeDtypeStruct((nb, DEPTH, KV_DIM, SEQ), F32))
            out_specs.append(pl.BlockSpec((1, DEPTH, KV_DIM, SEQ), lambda b: (b, 0, 0, 0)))
    return pl.pallas_call(
        functools.partial(_attn_prompt_body, layer=layer),
        out_shape=tuple(out_shape),
        grid=(nb,),
        in_specs=in_specs,
        out_specs=tuple(out_specs),
        compiler_params=pltpu.CompilerParams(vmem_limit_bytes=VMEM_LIMIT),
        name="attn_prompt",
    )(*args)


def _attn_sample_body(sink_ref, za_ref, cak_ref, cav_ref, cbk_ref, cbv_ref, cos_ref, sin_ref, gq_ref, gk_ref,
                      o_ref, qa_s, ka_s, va_s, qb_s, kb_s, vb_s, *, layer):
    t = DEC_SEQ
    cos = cos_ref[...]
    sin = sin_ref[...]
    qa_s[...] = (_rope(za_ref[:, 0:A_DIM], cos, sin) * QK_SCALE).astype(BF16)
    zpad = jnp.zeros((Q_BLK, KV_DIM), BF16)
    ka_s[0:Q_BLK] = zpad
    ka_s[Q_BLK + t:] = zpad
    va_s[0:Q_BLK] = zpad
    va_s[Q_BLK + t:] = zpad
    ka_s[Q_BLK:Q_BLK + t] = _rope(za_ref[:, 256:384], cos, sin).astype(BF16)
    va_s[Q_BLK:Q_BLK + t] = za_ref[:, 384:512].astype(BF16)
    bq = _head_rms(za_ref[:, 512:896], gq_ref[...], _head_ones(B_DIM))
    qb_s[...] = (_rope(bq, cos, sin) * QK_SCALE).astype(BF16)
    bk = _head_rms(za_ref[:, 896:1024], gk_ref[...], _head_ones(KV_DIM))
    kb_s[0:PAST_LEN] = cbk_ref[0].T.astype(BF16)
    kb_s[PAST_LEN:] = _rope(bk, cos, sin).astype(BF16)
    vb_s[0:PAST_LEN] = cbv_ref[0].T.astype(BF16)
    vb_s[PAST_LEN:] = za_ref[:, 1024:1152].astype(BF16)

    kca = cak_ref[0].T.astype(BF16)
    vca = cav_ref[0].T.astype(BF16)

    ga = A_HEADS // A_KV_HEADS
    gb = B_HEADS // B_KV_HEADS
    qpos = lax.broadcasted_iota(jnp.int32, (ga * Q_BLK, 3 * Q_BLK), 0) & (Q_BLK - 1)
    kpos = lax.broadcasted_iota(jnp.int32, (ga * Q_BLK, 3 * Q_BLK), 1) - Q_BLK
    near = jnp.abs(kpos - qpos) <= WINDOW
    head_of_row = lax.shift_right_logical(lax.broadcasted_iota(jnp.int32, (ga * Q_BLK, 1), 0),
                                          jnp.int32(Q_BLK.bit_length() - 1))
    sinks = []
    for kv in range(A_KV_HEADS):
        col = jnp.full((ga * Q_BLK, 1), sink_ref[layer, kv * ga], F32)
        for g in range(1, ga):
            col = jnp.where(head_of_row == g, sink_ref[layer, kv * ga + g], col)
        sinks.append(col)

    def stack_heads(q, kv, group):
        return jnp.concatenate([q[:, (kv * group + g) * 64:(kv * group + g + 1) * 64] for g in range(group)], axis=0)

    def unstack_heads(outs, group, rows):
        return jnp.concatenate([o[g * rows:(g + 1) * rows] for o in outs for g in range(group)], axis=1)

    def a_block(n, carry):
        r0 = pl.multiple_of(_index32(n) * Q_BLK, Q_BLK)
        q = qa_s[pl.ds(r0, Q_BLK), :]
        kl = ka_s[pl.ds(r0, 3 * Q_BLK), :]
        vl = va_s[pl.ds(r0, 3 * Q_BLK), :]
        kabs = kpos + r0
        mask = near & (kabs >= 0) & (kabs < t)
        score_fns, finish_fns = [], []
        for kv in range(A_KV_HEADS):
            hs = slice(kv * 64, (kv + 1) * 64)

            def scores(kv=kv, hs=hs):
                qs = stack_heads(q, kv, ga)
                return [jnp.where(mask, _bdot_nt(qs, kl[:, hs]), NEG_INF), _bdot_nt(qs, kca[:, hs])]

            score_fns.append(scores)
            finish_fns.append(lambda sc, kv=kv, hs=hs: _softmax_pv(sc, [vl[:, hs], vca[:, hs]], sinks[kv]))
        outs = _attend_heads(score_fns, finish_fns, ahead=A_KV_HEADS)
        o_ref[pl.ds(r0, Q_BLK), 0:A_DIM] = unstack_heads(outs, ga, Q_BLK)
        return carry

    lax.fori_loop(0, t // Q_BLK, a_block, 0)

    def b_block(n, carry):
        r0 = pl.multiple_of(_index32(n) * B_QROWS, B_QROWS)
        q = qb_s[pl.ds(r0, B_QROWS), :]
        score_fns, finish_fns = [], []
        for kv in range(B_KV_HEADS):
            hs = slice(kv * 64, (kv + 1) * 64)
            score_fns.append(lambda kv=kv, hs=hs: [_bdot_nt(stack_heads(q, kv, gb), kb_s[:, hs])])
            finish_fns.append(lambda sc, hs=hs: _softmax_pv(sc, [vb_s[:, hs]]))
        outs = _attend_heads(score_fns, finish_fns)
        o_ref[pl.ds(r0, B_QROWS), A_DIM:ATT_OUT] = unstack_heads(outs, gb, B_QROWS)
        return carry

    lax.fori_loop(0, t // B_QROWS, b_block, 0)


def _attn_sample_call(za, sink, caches, cos, sin, gq, gk, layer):
    nb = za.shape[0] // DEC_SEQ
    t = DEC_SEQ
    cache_spec = pl.BlockSpec((1, None, KV_DIM, PAST_LEN), lambda b: (b, layer, 0, 0))
    return pl.pallas_call(
        functools.partial(_attn_sample_body, layer=layer),
        out_shape=jax.ShapeDtypeStruct((nb * t, ATT_OUT), F32),
        grid=(nb,),
        in_specs=[
            pl.BlockSpec(memory_space=pltpu.SMEM),
            pl.BlockSpec((t, ATT_COLS), lambda b: (b, 0)),
            cache_spec, cache_spec, cache_spec, cache_spec,
            pl.BlockSpec((t, 128), lambda b: (0, 0)),
            pl.BlockSpec((t, 128), lambda b: (0, 0)),
            _layer_spec((1, B_DIM), layer),
            _layer_spec((1, KV_DIM), layer),
        ],
        out_specs=pl.BlockSpec((t, ATT_OUT), lambda b: (b, 0)),
        scratch_shapes=[
            pltpu.VMEM((t, A_DIM), BF16),
            pltpu.VMEM((t + 2 * Q_BLK, KV_DIM), BF16),
            pltpu.VMEM((t + 2 * Q_BLK, KV_DIM), BF16),
            pltpu.VMEM((t, B_DIM), BF16),
            pltpu.VMEM((t + PAST_LEN, KV_DIM), BF16),
            pltpu.VMEM((t + PAST_LEN, KV_DIM), BF16),
        ],
        compiler_params=pltpu.CompilerParams(vmem_limit_bytes=VMEM_LIMIT),
        name="attn_sample",
    )(sink, za, *caches, cos, sin, gq, gk)


def _rwkv_body(*refs, t, nbatch, has_init, want_state, has_prev):
    refs = list(refs)
    zc_ref, cv_ref, wup_ref, aup_ref, gup_ref = refs[:5]
    pos = 5
    st0_ref = prev_ref = stout_ref = None
    if has_init:
        st0_ref = refs[pos]
        pos += 1
    if has_prev:
        prev_ref = refs[pos]
        pos += 1
    oc_ref = refs[pos]
    pos += 1
    if want_state:
        stout_ref = refs[pos]
        pos += 1
    aa_s, ld_s, b_s, kt_s, bon_s, y_s, st_s = refs[pos:]

    nchunks = t // CHUNK
    ones = _head_ones(C_DIM)
    k_k = cv_ref[0:1]
    k_a = cv_ref[1:2]
    r_k = cv_ref[2:3]
    ln_w = cv_ref[3:4]
    ln_b = cv_ref[4:5]
    w0 = (cv_ref[5:6], cv_ref[6:7])
    a0 = (cv_ref[7:8], cv_ref[8:9])
    wup = (wup_ref[0].astype(BF16), wup_ref[1].astype(BF16))
    aup = (aup_ref[0].astype(BF16), aup_ref[1].astype(BF16))

    for rb in range(nbatch * t // RWKV_ROWS):
        rows = slice(rb * RWKV_ROWS, (rb + 1) * RWKV_ROWS)
        zr = zc_ref[rows, 0:384]
        zk = zc_ref[rows, 384:768]
        zv = zc_ref[rows, 768:1152]
        tw = jnp.tanh(zc_ref[rows, 1152:1216])
        xa = zc_ref[rows, 1216:1280]
        kkr = zk * k_k
        kk = kkr * lax.rsqrt(_head_sum(kkr * kkr, ones) + 1e-12)
        aa_s[rows] = -kk
        kt_sum = None
        for d in range(2):
            wl = w0[d] + _bdot(tw, wup[d])
            ld_s[d, rows] = (-DECAY_RATE) * _sigmoid(wl)
            a = _sigmoid(a0[d] + _bdot(xa, aup[d]))
            kt = zk * (1.0 + (a - 1.0) * k_a)
            kt_s[d, rows] = kt
            b_s[d, rows] = kk * a
            kt_sum = kt if kt_sum is None else kt_sum + kt
        bon_s[rows] = _head_sum(zr * kt_sum * r_k, ones) * zv
        y_s[rows] = jnp.zeros((RWKV_ROWS, C_DIM), F32)

    ir = lax.broadcasted_iota(jnp.int32, (CHUNK, 128), 0)
    ic = lax.broadcasted_iota(jnp.int32, (CHUNK, 128), 1) & 63
    eye = jnp.where(ir == ic, 1.0, 0.0).astype(F32)
    pair_masks = []
    for j in range(CHUNK.bit_length() - 1):
        same_big = lax.shift_right_logical(ir, jnp.int32(j + 1)) == lax.shift_right_logical(ic, jnp.int32(j + 1))
        same_small = lax.shift_right_logical(ir, jnp.int32(j)) == lax.shift_right_logical(ic, jnp.int32(j))
        pair_masks.append(same_big & jnp.logical_not(same_small))
    incl = (ic <= ir, ic >= ir)
    strict = (ic < ir, ic > ir)
    sr = lax.broadcasted_iota(jnp.int32, (CHUNK, CHUNK), 0)
    sc = lax.broadcasted_iota(jnp.int32, (CHUNK, CHUNK), 1)
    tri = tuple(jnp.where(m, 1.0, 0.0).astype(BF16) for m in (sc <= sr, sc >= sr))
    br = lax.broadcasted_iota(jnp.int32, (128, 128), 0)
    bc = lax.broadcasted_iota(jnp.int32, (128, 128), 1)
    diag_blocks = (br < 64) == (bc < 64)
    left_half = lax.broadcasted_iota(jnp.int32, (CHUNK, 128), 1) < 64

    def blockdiag(x2):
        x2 = x2.astype(BF16)
        return jnp.where(diag_blocks, jnp.concatenate([x2, x2], axis=0), jnp.zeros((128, 128), BF16))

    npairs = C_HEADS // 2
    for bi in range(nbatch):
        for d in range(2):
            for p in range(npairs):
                if has_init:
                    st_s[bi, d, p] = jnp.concatenate([st0_ref[bi, d, 2 * p], st0_ref[bi, d, 2 * p + 1]], axis=1)
                else:
                    st_s[bi, d, p] = jnp.zeros((HEAD_DIM, 128), F32)

    def chunk(c, carry):
        c = _index32(c)
        items = []
        for bi, d in [(bi, d) for bi in range(nbatch) for d in range(2)]:
            r0 = bi * t + ((nchunks - 1 - c) if d == 1 else c) * CHUNK
            rows = pl.ds(pl.multiple_of(r0, CHUNK), CHUNK)
            ld = ld_s[d, rows]
            cinc = _split_dot_left(tri[d], ld)
            cexc = cinc - ld
            ctot = cinc[0:1] if d == 1 else cinc[CHUNK - 1:CHUNK]
            e_ninc = jnp.exp(-cinc)
            e_rem = jnp.exp(ctot - cinc)
            e_tot = jnp.exp(ctot)
            bb = b_s[d, rows]
            kt = kt_s[d, rows]
            at = (aa_s[rows] * jnp.exp(cexc)).astype(BF16)
            rt = (zc_ref[rows, 0:384] * jnp.exp(cinc)).astype(BF16)
            bt = (bb * e_ninc).astype(BF16)
            ktt = (kt * e_ninc).astype(BF16)
            bh = (bb * e_rem).astype(BF16)
            kh = (kt * e_rem).astype(BF16)
            vm = zc_ref[rows, 768:1152].astype(BF16)
            for p in range(npairs):
                ps = slice(p * 128, (p + 1) * 128)
                items.append(dict(bi=bi, d=d, p=p, rows=rows, e_tot=e_tot[:, ps], at=at[:, ps], rt=rt[:, ps],
                                  bt=bt[:, ps], kt=ktt[:, ps], bh=bh[:, ps], kh=kh[:, ps], vm=vm[:, ps]))
        for it in items:
            it["ar"] = jnp.concatenate([it["at"], it["rt"]], axis=0)
            it["vmd"] = blockdiag(it["vm"])
        for it in items:
            it["sb"] = _bdot_nt(it["ar"], blockdiag(it["bt"]))
        for it in items:
            it["sk"] = _bdot_nt(it["ar"], blockdiag(it["kt"]))
        for it in items:
            d = it["d"]
            it["l"] = jnp.where(strict[d], it["sb"][0:CHUNK], 0.0)
            it["mak"] = jnp.where(strict[d], it["sk"][0:CHUNK], 0.0).astype(BF16)
            it["nr"] = jnp.concatenate([jnp.where(incl[d], it["sb"][CHUNK:], 0.0).astype(BF16),
                                        jnp.where(incl[d], it["sk"][CHUNK:], 0.0).astype(BF16)], axis=1)
            it["tm"] = eye + jnp.where(pair_masks[0], it["l"], 0.0)
        for pm in pair_masks[1:]:
            for it in items:
                it["t1"] = _bdot(it["tm"], blockdiag(jnp.where(pm, it["l"], 0.0)))
            for it in items:
                it["tm"] = it["tm"] + _bdot(it["t1"], blockdiag(it["tm"]))
        for it in items:
            it["mv"] = _bdot(it["mak"], it["vmd"])
        for it in items:
            it["tau"] = _bdot(it["tm"], jnp.concatenate([blockdiag(it["at"]), blockdiag(it["mv"])], axis=1))
        for it in items:
            it["s0"] = st_s[it["bi"], it["d"], it["p"]]
            it["x"] = _bdot_nt(jnp.concatenate([it["tau"][:, 0:128].astype(BF16), it["rt"]], axis=0),
                               blockdiag(it["s0"]))
        for it in items:
            it["u"] = (it["x"][0:CHUNK] + it["tau"][:, 128:256]).astype(BF16)
        for it in items:
            it["y"] = it["x"][CHUNK:] + _bdot(it["nr"], jnp.concatenate([blockdiag(it["u"]), it["vmd"]], axis=0))
        for it in items:
            full = _bdot_tn(jnp.concatenate([it["u"], it["vm"]], axis=0),
                            jnp.concatenate([it["bh"], it["kh"]], axis=0))
            st_s[it["bi"], it["d"], it["p"]] = (it["s0"] * it["e_tot"]
                                                + jnp.where(left_half, full[0:64], full[64:128]))
        for i in range(0, len(items), npairs):
            rows = items[i]["rows"]
            y_s[rows] = y_s[rows] + jnp.concatenate([it["y"] for it in items[i:i + npairs]], axis=1)
        return carry

    lax.fori_loop(0, nchunks, chunk, 0)
    if want_state:
        for bi in range(nbatch):
            if has_prev:
                stout_ref[bi, 0] = prev_ref[bi]
            for d in range(2):
                for p in range(npairs):
                    s2 = st_s[bi, d, p]
                    for q in range(2):
                        if has_prev:
                            stout_ref[bi, 1, d, 2 * p + q] = s2[:, q * 64:(q + 1) * 64]
                        else:
                            stout_ref[bi, d, 2 * p + q] = s2[:, q * 64:(q + 1) * 64]

    gup = gup_ref[...].astype(BF16)
    for rb in range(nbatch * t // RWKV_ROWS):
        rows = slice(rb * RWKV_ROWS, (rb + 1) * RWKV_ROWS)
        y = y_s[rows]
        mu = _head_sum(y, ones) * (1.0 / HEAD_DIM)
        yc = y - mu
        var = _head_sum(yc * yc, ones) * (1.0 / HEAD_DIM)
        yn = yc * lax.rsqrt(var + GN_EPS)
        g = _bdot(_sigmoid(zc_ref[rows, 1280:1408]), gup)
        oc_ref[rows] = (yn * ln_w + ln_b + bon_s[rows]) * g


def _rwkv_call(zc, rvec, wup, aup, gup, layer, t, init_state, want_state, prev_state):
    nb = zc.shape[0] // t
    nbatch = RWKV_BATCH
    rows = nbatch * t
    has_init = init_state is not None
    has_prev = prev_state is not None
    st_shape = (2, C_HEADS, HEAD_DIM, HEAD_DIM)
    in_specs = [
        pl.BlockSpec((rows, RWKV_COLS), lambda b: (b, 0), pipeline_mode=pl.Buffered(1 if nb == nbatch else 2)),
        _layer_spec((16, C_DIM), layer),
        _layer_spec((2, W_RANK, C_DIM), layer),
        _layer_spec((2, A_RANK, C_DIM), layer),
        _layer_spec((G_RANK, C_DIM), layer),
    ]
    args = [zc, rvec, wup, aup, gup]
    if has_init:
        in_specs.append(pl.BlockSpec((nbatch, None) + st_shape, lambda b: (b, layer, 0, 0, 0, 0)))
        args.append(init_state)
    if has_prev:
        in_specs.append(pl.BlockSpec((nbatch,) + st_shape, lambda b: (b, 0, 0, 0, 0)))
        args.append(prev_state)
    out_shape = [jax.ShapeDtypeStruct((nb * t, C_DIM), F32)]
    out_specs = [pl.BlockSpec((rows, C_DIM), lambda b: (b, 0))]
    if want_state and has_prev:
        out_shape.append(jax.ShapeDtypeStruct((nb, DEPTH) + st_shape, F32))
        out_specs.append(pl.BlockSpec((nbatch, DEPTH) + st_shape, lambda b: (b, 0, 0, 0, 0, 0)))
    elif want_state:
        out_shape.append(jax.ShapeDtypeStruct((nb,) + st_shape, F32))
        out_specs.append(pl.BlockSpec((nbatch,) + st_shape, lambda b: (b, 0, 0, 0, 0)))
    return pl.pallas_call(
        functools.partial(_rwkv_body, t=t, nbatch=nbatch, has_init=has_init, want_state=want_state,
                          has_prev=has_prev),
        out_shape=tuple(out_shape),
        grid=(nb // nbatch,),
        in_specs=in_specs,
        out_specs=tuple(out_specs),
        scratch_shapes=[
            pltpu.VMEM((rows, C_DIM), F32),
            pltpu.VMEM((2, rows, C_DIM), F32),
            pltpu.VMEM((2, rows, C_DIM), F32),
            pltpu.VMEM((2, rows, C_DIM), F32),
            pltpu.VMEM((rows, C_DIM), F32),
            pltpu.VMEM((rows, C_DIM), F32),
            pltpu.VMEM((nbatch, 2, C_HEADS // 2, HEAD_DIM, 128), F32),
        ],
        compiler_params=pltpu.CompilerParams(vmem_limit_bytes=VMEM_LIMIT),
        name="rwkv_prompt" if want_state else "rwkv_sample",
    )(*args)


def _post_body(oa_ref, oc_ref, x_ref, mod_ref, g_ref, wout_ref, wgu_ref, wdn_ref, xo_ref, *, rows_per_mod):
    mod = _mod_row(mod_ref, rows_per_mod)
    gate1 = mod[:, 2 * D_MODEL:3 * D_MODEL]
    shift2 = mod[:, 3 * D_MODEL:4 * D_MODEL]
    scale2 = mod[:, 4 * D_MODEL:5 * D_MODEL]
    gate2 = mod[:, 5 * D_MODEL:6 * D_MODEL]
    o_cat = jnp.concatenate([oa_ref[...].astype(BF16), oc_ref[...].astype(BF16)], axis=1)
    o = jnp.dot(o_cat, wout_ref[...], preferred_element_type=F32)
    x1 = x_ref[...] + gate1 * _rms(o, g_ref[1:2])
    h2 = (_rms(x1, g_ref[2:3]) * (1.0 + scale2) + shift2).astype(BF16)
    acc = None
    for lo, hi in FF_CHUNKS:
        g = jnp.dot(h2, wgu_ref[:, lo:hi], preferred_element_type=F32)
        u = jnp.dot(h2, wgu_ref[:, FF_DIM + lo:FF_DIM + hi], preferred_element_type=F32)
        act = (g * _sigmoid(g) * u).astype(BF16)
        part = jnp.dot(act, wdn_ref[lo:hi, :], preferred_element_type=F32)
        acc = part if acc is None else acc + part
    xo_ref[...] = x1 + gate2 * _rms(acc, g_ref[3:4])


def _post_call(oatt, oc, x, mods, gvec, wout_b, wgu_b, wdn_b, layer, rows_per_mod):
    n = x.shape[0]
    return pl.pallas_call(
        functools.partial(_post_body, rows_per_mod=rows_per_mod),
        out_shape=jax.ShapeDtypeStruct((n, D_MODEL), F32),
        grid=(n // ROW_TILE,),
        in_specs=[
            pl.BlockSpec((ROW_TILE, ATT_OUT), lambda i: (i, 0)),
            pl.BlockSpec((ROW_TILE, C_DIM), lambda i: (i, 0)),
            pl.BlockSpec((ROW_TILE, D_MODEL), lambda i: (i, 0)),
            _layer_spec((MOD_ROWS, MOD_COLS), layer),
            _layer_spec((8, D_MODEL), layer),
            _layer_spec((MIX_DIM, D_MODEL), layer),
            _layer_spec((D_MODEL, 2 * FF_DIM), layer),
            _layer_spec((FF_DIM, D_MODEL), layer),
        ],
        out_specs=pl.BlockSpec((ROW_TILE, D_MODEL), lambda i: (i, 0)),
        compiler_params=pltpu.CompilerParams(vmem_limit_bytes=VMEM_LIMIT),
        name="post_ffn",
    )(oatt, oc, x, mods, gvec, wout_b, wgu_b, wdn_b)


def _rope_tables(t):
    pos = jnp.arange(t)
    row = (pos // GRID_W).astype(F32)
    col = (pos % GRID_W).astype(F32)
    freqs = ROPE_THETA ** (-jnp.arange(ROPE_PAIRS_AXIS, dtype=F32) / ROPE_PAIRS_AXIS)
    ang = jnp.concatenate([row[:, None] * freqs, col[:, None] * freqs], axis=-1)
    cos = jnp.tile(jnp.cos(ang), (1, 4))
    sin = jnp.sin(ang)
    sin = jnp.tile(jnp.concatenate([-sin, sin], axis=-1), (1, 2))
    return cos, sin


def _to_time_minor(cache):
    b, l, t = cache.shape[:3]
    return jnp.transpose(cache, (0, 1, 3, 4, 2)).reshape(b, l, KV_DIM, t)


def _from_time_minor(slab):
    b, l, _, t = slab.shape
    return jnp.transpose(slab.reshape(b, l, 2, HEAD_DIM, t), (0, 1, 4, 2, 3))


def kernel(x_prompt, x_sample, cache_a_k, cache_a_v, cache_b_k, cache_b_v, state_c, c, c_ctx, w_mod, b_mod, norm_mix_pre, norm_mix_post, norm_ffn_pre, norm_ffn_post, w_in, w_out, a_sink, b_q_norm, b_k_norm, c_w0, c_w_up, c_a0, c_a_up, c_g_up, c_k_k, c_k_a, c_r_k, c_ln_w, c_ln_b, w_gu, w_down):
    cvec = jnp.zeros((MOD_ROWS, D_MODEL), F32).at[0].set(c_ctx).at[1:1 + DEC_BATCH].set(c)
    mods = _mod_call(cvec, w_mod, b_mod)
    cos, sin = _rope_tables(DEC_SEQ)

    w_in_b = w_in.astype(BF16)
    wout_b = w_out.astype(BF16)
    wgu_b = w_gu.astype(BF16)
    wdn_b = w_down.astype(BF16)
    gvec = jnp.zeros((DEPTH, 8, D_MODEL), F32)
    gvec = gvec.at[:, 0].set(norm_mix_pre).at[:, 1].set(norm_mix_post)
    gvec = gvec.at[:, 2].set(norm_ffn_pre).at[:, 3].set(norm_ffn_post)
    rvec = jnp.zeros((DEPTH, 16, C_DIM), F32)
    rvec = rvec.at[:, 0].set(c_k_k).at[:, 1].set(c_k_a).at[:, 2].set(c_r_k.reshape(DEPTH, C_DIM))
    rvec = rvec.at[:, 3].set(c_ln_w).at[:, 4].set(c_ln_b)
    rvec = rvec.at[:, 5:7].set(c_w0).at[:, 7:9].set(c_a0)
    gq = jnp.tile(b_q_norm, (1, B_HEADS))[:, None, :]
    gk = jnp.tile(b_k_norm, (1, B_KV_HEADS))[:, None, :]
    caches = tuple(_to_time_minor(t) for t in (cache_a_k, cache_a_v, cache_b_k, cache_b_v))

    xp = x_prompt.reshape(BATCH * SEQ, D_MODEL)
    xs = x_sample.reshape(DEC_BATCH * DEC_SEQ, D_MODEL)
    kv_prev = st_prev = None
    for l in range(DEPTH):
        za, zc = _proj_in_call(xp, mods, gvec, w_in_b, l, None)
        oatt, *kv_prev = _attn_prompt_call(za, a_sink, gq, gk, l, kv_prev)
        if l == 0:
            (kv_prev,) = kv_prev
        oc, st_prev = _rwkv_call(zc, rvec, c_w_up, c_a_up, c_g_up, l, SEQ, None, True, st_prev)
        xp = _post_call(oatt, oc, xp, mods, gvec, wout_b, wgu_b, wdn_b, l, None)

        za, zc = _proj_in_call(xs, mods, gvec, w_in_b, l, DEC_SEQ)
        oatt = _attn_sample_call(za, a_sink, caches, cos, sin, gq, gk, l)
        (oc,) = _rwkv_call(zc, rvec, c_w_up, c_a_up, c_g_up, l, DEC_SEQ, state_c, False, None)
        xs = _post_call(oatt, oc, xs, mods, gvec, wout_b, wgu_b, wdn_b, l, DEC_SEQ)

    new_caches = tuple(_from_time_minor(slab) for slab in kv_prev)
    return (xp.reshape(BATCH, SEQ, D_MODEL), xs.reshape(DEC_BATCH, DEC_SEQ, D_MODEL), *new_caches, st_prev)
```

```python
import functools
import math

import jax
import jax.numpy as jnp
from jax import lax
from jax.experimental import pallas as pl
from jax.experimental.pallas import tpu as pltpu

D_MODEL = 1024
BATCH = 32
SEQ = 256
DEPTH = 2
DEC_BATCH = 2
DEC_SEQ = 1024
PAST_LEN = 256
GRID_W = 64
HEAD_DIM = 64
A_HEADS = 4
A_KV_HEADS = 2
B_HEADS = 6
B_KV_HEADS = 2
C_HEADS = 6
A_DIM = A_HEADS * HEAD_DIM
B_DIM = B_HEADS * HEAD_DIM
C_DIM = C_HEADS * HEAD_DIM
MIX_DIM = A_DIM + B_DIM + C_DIM
WINDOW = 128
Q_BLK = 128
W_RANK = 64
A_RANK = 64
G_RANK = 128
FF_DIM = -(-8 * D_MODEL // (3 * 256)) * 256
ROPE_THETA = 10000.0
ROPE_PAIRS_AXIS = HEAD_DIM // 4
NORM_EPS = 1e-6
GN_EPS = 64e-5
NEG_INF = -1e30

KV_DIM = 2 * HEAD_DIM
ATT_COLS = A_DIM + 2 * KV_DIM + B_DIM + 2 * KV_DIM
RWKV_COLS = 3 * C_DIM + W_RANK + A_RANK + G_RANK
IN_COLS = ATT_COLS + RWKV_COLS
ATT_OUT = A_DIM + B_DIM
MOD_COLS = 6 * D_MODEL
MOD_ROWS = 8
N_CACHE = 4
CHUNK = 64
QK_SCALE = HEAD_DIM ** -0.5
DECAY_RATE = math.exp(-0.5)
ROW_TILE = 512
RWKV_ROWS = 256
RWKV_BATCH = 4
B_QROWS = 256
MXU_TILE = 256
FF_CHUNKS = ((0, 4 * MXU_TILE), (4 * MXU_TILE, 8 * MXU_TILE), (8 * MXU_TILE, FF_DIM))
assert FF_DIM % MXU_TILE == 0
VMEM_LIMIT = 56 * 1024 * 1024

F32 = jnp.float32
BF16 = jnp.bfloat16

assert DEPTH == 2


def _bdot(a, b):
    return jnp.dot(a.astype(BF16), b.astype(BF16), preferred_element_type=F32)


def _bdot_nt(a, b):
    return lax.dot_general(a.astype(BF16), b.astype(BF16), (((1,), (1,)), ((), ())), preferred_element_type=F32)


def _bdot_tn(a, b):
    return lax.dot_general(a.astype(BF16), b.astype(BF16), (((0,), (0,)), ((), ())), preferred_element_type=F32)


def _split(x):
    hi = x.astype(BF16)
    lo = (x - hi.astype(F32)).astype(BF16)
    return hi, lo


def _split_dot_left(m, x):
    hi, lo = _split(x)
    return jnp.dot(m, hi, preferred_element_type=F32) + jnp.dot(m, lo, preferred_element_type=F32)


def _head_sum(x, ones):
    return jnp.dot(x.astype(BF16), ones, preferred_element_type=F32)


def _sigmoid(x):
    return 0.5 * jnp.tanh(0.5 * x) + 0.5


def _head_ones(n):
    r = lax.shift_right_logical(lax.broadcasted_iota(jnp.int32, (n, n), 0), jnp.int32(6))
    c = lax.shift_right_logical(lax.broadcasted_iota(jnp.int32, (n, n), 1), jnp.int32(6))
    return jnp.where(r == c, 1.0, 0.0).astype(BF16)


def _rms(x, g):
    ms = jnp.mean(x * x, axis=-1, keepdims=True)
    return x * lax.rsqrt(ms + NORM_EPS) * g


def _head_rms(x, g, ones):
    ms = _head_sum(x * x, ones) * (1.0 / HEAD_DIM)
    return x * lax.rsqrt(ms + NORM_EPS) * g


def _rope(x, cos, sin):
    t = x.shape[0]
    lane = lax.broadcasted_iota(jnp.int32, (t, 128), 1)
    first = (lane & 32) == 0
    outs = []
    for i in range(x.shape[1] // 128):
        xs = x[:, i * 128:(i + 1) * 128]
        swapped = jnp.where(first, pltpu.roll(xs, 96, axis=1), pltpu.roll(xs, 32, axis=1))
        outs.append(xs * cos + swapped * sin)
    return outs[0] if len(outs) == 1 else jnp.concatenate(outs, axis=1)


def _softmax_pv(scores, values, sink=None):
    m = scores[0].max(axis=-1, keepdims=True)
    for s in scores[1:]:
        m = jnp.maximum(m, s.max(axis=-1, keepdims=True))
    if sink is not None:
        m = jnp.maximum(m, sink)
    denom = None
    acc = None
    for s, v in zip(scores, values):
        p = jnp.exp(s - m)
        d = p.sum(axis=-1, keepdims=True)
        o = jnp.dot(p.astype(BF16), v, preferred_element_type=F32)
        denom = d if denom is None else denom + d
        acc = o if acc is None else acc + o
    if sink is not None:
        denom = denom + jnp.exp(sink - m)
    return acc / denom


def _softmax_pv_t(scores_t, values_t, sink=None):
    m = scores_t[0].max(axis=0, keepdims=True)
    for s in scores_t[1:]:
        m = jnp.maximum(m, s.max(axis=0, keepdims=True))
    if sink is not None:
        m = jnp.maximum(m, sink)
    denom = None
    acc = None
    for s, vt in zip(scores_t, values_t):
        p = jnp.exp(s - m)
        d = p.sum(axis=0, keepdims=True)
        o = jnp.dot(vt, p.astype(BF16), preferred_element_type=F32)
        denom = d if denom is None else denom + d
        acc = o if acc is None else acc + o
    if sink is not None:
        denom = denom + jnp.exp(sink - m)
    return acc / denom


def _attend_heads(score_fns, finish_fns, ahead=1):
    n = len(score_fns)
    pending = [fn() for fn in score_fns[:ahead]]
    outs = []
    for i in range(n):
        if i + ahead < n:
            pending.append(score_fns[i + ahead]())
        outs.append(finish_fns[i](pending.pop(0)))
    return outs


def _index32(i):
    return jnp.asarray(i, jnp.int32)


def _layer_spec(shape, layer):
    zeros = (0,) * len(shape)
    return pl.BlockSpec((None,) + shape, lambda *_: (layer,) + zeros, pipeline_mode=pl.Buffered(1))


def _mod_row(mod_ref, rows_per_mod):
    if rows_per_mod is None:
        return mod_ref[0:1]
    row = 1 + pl.program_id(0) // (rows_per_mod // ROW_TILE)
    return mod_ref[pl.ds(row, 1), :]


def _mod_body(c_ref, w_ref, b_ref, o_ref):
    cv = c_ref[...]
    s = cv * _sigmoid(cv)
    o_ref[0] = _bdot(s, w_ref[0]) + b_ref[0]


def _mod_call(cvec, w_mod, b_mod):
    bn = MOD_COLS // 4
    return pl.pallas_call(
        _mod_body,
        out_shape=jax.ShapeDtypeStruct((DEPTH, MOD_ROWS, MOD_COLS), F32),
        grid=(DEPTH, MOD_COLS // bn),
        in_specs=[
            pl.BlockSpec((MOD_ROWS, D_MODEL), lambda l, j: (0, 0)),
            pl.BlockSpec((1, D_MODEL, bn), lambda l, j: (l, 0, j)),
            pl.BlockSpec((1, 1, bn), lambda l, j: (l, 0, j)),
        ],
        out_specs=pl.BlockSpec((1, MOD_ROWS, bn), lambda l, j: (l, 0, j)),
        compiler_params=pltpu.CompilerParams(vmem_limit_bytes=VMEM_LIMIT),
        name="mod_vectors",
    )(cvec, w_mod, b_mod.reshape(DEPTH, 1, MOD_COLS))


def _proj_in_body(x_ref, mod_ref, g_ref, w_ref, za_ref, zc_ref, *, rows_per_mod):
    mod = _mod_row(mod_ref, rows_per_mod)
    shift1 = mod[:, 0:D_MODEL]
    scale1 = mod[:, D_MODEL:2 * D_MODEL]
    h = (_rms(x_ref[...], g_ref[0:1]) * (1.0 + scale1) + shift1).astype(BF16)
    z = jnp.dot(h, w_ref[...], preferred_element_type=F32)
    za_ref[...] = z[:, 0:ATT_COLS]
    zc_ref[...] = z[:, ATT_COLS:IN_COLS]


def _proj_in_call(x, mods, gvec, w_in_b, layer, rows_per_mod):
    n = x.shape[0]
    return pl.pallas_call(
        functools.partial(_proj_in_body, rows_per_mod=rows_per_mod),
        out_shape=(jax.ShapeDtypeStruct((n, ATT_COLS), F32), jax.ShapeDtypeStruct((n, RWKV_COLS), F32)),
        grid=(n // ROW_TILE,),
        in_specs=[
            pl.BlockSpec((ROW_TILE, D_MODEL), lambda i: (i, 0)),
            _layer_spec((MOD_ROWS, MOD_COLS), layer),
            _layer_spec((8, D_MODEL), layer),
            _layer_spec((D_MODEL, IN_COLS), layer),
        ],
        out_specs=(pl.BlockSpec((ROW_TILE, ATT_COLS), lambda i: (i, 0)),
                   pl.BlockSpec((ROW_TILE, RWKV_COLS), lambda i: (i, 0))),
        compiler_params=pltpu.CompilerParams(vmem_limit_bytes=VMEM_LIMIT),
        name="proj_in",
    )(x, mods, gvec, w_in_b)


def _attn_prompt_body(*refs, layer):
    sink_ref, za_ref, gq_ref, gk_ref = refs[:4]
    if layer == 0:
        o_ref, kv_ref = refs[4:]
    else:
        prev_ref, o_ref = refs[4:6]
        cache_refs = refs[6:]
    aq = (za_ref[:, 0:A_DIM] * QK_SCALE).astype(BF16)
    ak = za_ref[:, 256:384]
    bq = _head_rms(za_ref[:, 512:896], gq_ref[...], _head_ones(B_DIM))
    bk = _head_rms(za_ref[:, 896:1024], gk_ref[...], _head_ones(KV_DIM))
    bqb = (bq * QK_SCALE).astype(BF16)
    akb = ak.astype(BF16)
    bkb = bk.astype(BF16)
    pieces_t = (ak.T, za_ref[:, 384:512].T, bk.T, za_ref[:, 1024:1152].T)
    avt = pieces_t[1].astype(BF16)
    bvt = pieces_t[3].astype(BF16)

    score_fns, finish_fns = [], []
    for h in range(A_HEADS):
        kv = h // (A_HEADS // A_KV_HEADS)
        hs = slice(kv * 64, (kv + 1) * 64)
        score_fns.append(lambda h=h, hs=hs: [_bdot_nt(akb[:, hs], aq[:, h * 64:(h + 1) * 64])])
        finish_fns.append(lambda sc, h=h, hs=hs: _softmax_pv_t(sc, [avt[hs]], sink_ref[layer, h]))
    for h in range(B_HEADS):
        kv = h // (B_HEADS // B_KV_HEADS)
        hs = slice(kv * 64, (kv + 1) * 64)
        score_fns.append(lambda h=h, hs=hs: [_bdot_nt(bkb[:, hs], bqb[:, h * 64:(h + 1) * 64])])
        finish_fns.append(lambda sc, hs=hs: _softmax_pv_t(sc, [bvt[hs]]))
    o_ref[...] = jnp.concatenate(_attend_heads(score_fns, finish_fns, ahead=len(score_fns)), axis=0).T

    for j, piece_t in enumerate(pieces_t):
        if layer == 0:
            kv_ref[0, j] = piece_t
        else:
            cache_refs[j][0, 0] = prev_ref[0, j]
            cache_refs[j][0, 1] = piece_t


def _attn_prompt_call(za, sink, gq, gk, layer, prev_kv):
    nb = za.shape[0] // SEQ
    in_specs = [
        pl.BlockSpec(memory_space=pltpu.SMEM),
        pl.BlockSpec((SEQ, ATT_COLS), lambda b: (b, 0)),
        _layer_spec((1, B_DIM), layer),
        _layer_spec((1, KV_DIM), layer),
    ]
    args = [sink, za, gq, gk]
    out_shape = [jax.ShapeDtypeStruct((nb * SEQ, ATT_OUT), F32)]
    out_specs = [pl.BlockSpec((SEQ, ATT_OUT), lambda b: (b, 0))]
    slab_spec = pl.BlockSpec((1, N_CACHE, KV_DIM, SEQ), lambda b: (b, 0, 0, 0))
    if layer == 0:
        out_shape.append(jax.ShapeDtypeStruct((nb, N_CACHE, KV_DIM, SEQ), F32))
        out_specs.append(slab_spec)
    else:
        in_specs.append(slab_spec)
        args.append(prev_kv)
        for _ in range(N_CACHE):
            out_shape.append(jax.ShapeDtypeStruct((nb, DEPTH, KV_DIM, SEQ), F32))
            out_specs.append(pl.BlockSpec((1, DEPTH, KV_DIM, SEQ), lambda b: (b, 0, 0, 0)))
    return pl.pallas_call(
        functools.partial(_attn_prompt_body, layer=layer),
        out_shape=tuple(out_shape),
        grid=(nb,),
        in_specs=in_specs,
        out_specs=tuple(out_specs),
        compiler_params=pltpu.CompilerParams(vmem_limit_bytes=VMEM_LIMIT),
        name="attn_prompt",
    )(*args)


def _attn_sample_body(sink_ref, za_ref, cak_ref, cav_ref, cbk_ref, cbv_ref, cos_ref, sin_ref, gq_ref, gk_ref,
                      o_ref, qa_s, ka_s, va_s, qb_s, kb_s, vb_s, *, layer):
    t = DEC_SEQ
    cos = cos_ref[...]
    sin = sin_ref[...]
    qa_s[...] = (_rope(za_ref[:, 0:A_DIM], cos, sin) * QK_SCALE).astype(BF16)
    zpad = jnp.zeros((Q_BLK, KV_DIM), BF16)
    ka_s[0:Q_BLK] = zpad
    ka_s[Q_BLK + t:] = zpad
    va_s[0:Q_BLK] = zpad
    va_s[Q_BLK + t:] = zpad
    ka_s[Q_BLK:Q_BLK + t] = _rope(za_ref[:, 256:384], cos, sin).astype(BF16)
    va_s[Q_BLK:Q_BLK + t] = za_ref[:, 384:512].astype(BF16)
    bq = _head_rms(za_ref[:, 512:896], gq_ref[...], _head_ones(B_DIM))
    qb_s[...] = (_rope(bq, cos, sin) * QK_SCALE).astype(BF16)
    bk = _head_rms(za_ref[:, 896:1024], gk_ref[...], _head_ones(KV_DIM))
    kb_s[0:PAST_LEN] = cbk_ref[0].T.astype(BF16)
    kb_s[PAST_LEN:] = _rope(bk, cos, sin).astype(BF16)
    vb_s[0:PAST_LEN] = cbv_ref[0].T.astype(BF16)
    vb_s[PAST_LEN:] = za_ref[:, 1024:1152].astype(BF16)

    kca = cak_ref[0].T.astype(BF16)
    vca = cav_ref[0].T.astype(BF16)

    ga = A_HEADS // A_KV_HEADS
    gb = B_HEADS // B_KV_HEADS
    qpos = lax.broadcasted_iota(jnp.int32, (ga * Q_BLK, 3 * Q_BLK), 0) & (Q_BLK - 1)
    kpos = lax.broadcasted_iota(jnp.int32, (ga * Q_BLK, 3 * Q_BLK), 1) - Q_BLK
    near = jnp.abs(kpos - qpos) <= WINDOW
    head_of_row = lax.shift_right_logical(lax.broadcasted_iota(jnp.int32, (ga * Q_BLK, 1), 0),
                                          jnp.int32(Q_BLK.bit_length() - 1))
    sinks = []
    for kv in range(A_KV_HEADS):
        col = jnp.full((ga * Q_BLK, 1), sink_ref[layer, kv * ga], F32)
        for g in range(1, ga):
            col = jnp.where(head_of_row == g, sink_ref[layer, kv * ga + g], col)
        sinks.append(col)

    def stack_heads(q, kv, group):
        return jnp.concatenate([q[:, (kv * group + g) * 64:(kv * group + g + 1) * 64] for g in range(group)], axis=0)

    def unstack_heads(outs, group, rows):
        return jnp.concatenate([o[g * rows:(g + 1) * rows] for o in outs for g in range(group)], axis=1)

    def a_block(n, carry):
        r0 = pl.multiple_of(_index32(n) * Q_BLK, Q_BLK)
        q = qa_s[pl.ds(r0, Q_BLK), :]
        kl = ka_s[pl.ds(r0, 3 * Q_BLK), :]
        vl = va_s[pl.ds(r0, 3 * Q_BLK), :]
        kabs = kpos + r0
        mask = near & (kabs >= 0) & (kabs < t)
        score_fns, finish_fns = [], []
        for kv in range(A_KV_HEADS):
            hs = slice(kv * 64, (kv + 1) * 64)

            def scores(kv=kv, hs=hs):
                qs = stack_heads(q, kv, ga)
                return [jnp.where(mask, _bdot_nt(qs, kl[:, hs]), NEG_INF), _bdot_nt(qs, kca[:, hs])]

            score_fns.append(scores)
            finish_fns.append(lambda sc, kv=kv, hs=hs: _softmax_pv(sc, [vl[:, hs], vca[:, hs]], sinks[kv]))
        outs = _attend_heads(score_fns, finish_fns, ahead=A_KV_HEADS)
        o_ref[pl.ds(r0, Q_BLK), 0:A_DIM] = unstack_heads(outs, ga, Q_BLK)
        return carry

    lax.fori_loop(0, t // Q_BLK, a_block, 0)

    def b_block(n, carry):
        r0 = pl.multiple_of(_index32(n) * B_QROWS, B_QROWS)
        q = qb_s[pl.ds(r0, B_QROWS), :]
        score_fns, finish_fns = [], []
        for kv in range(B_KV_HEADS):
            hs = slice(kv * 64, (kv + 1) * 64)
            score_fns.append(lambda kv=kv, hs=hs: [_bdot_nt(stack_heads(q, kv, gb), kb_s[:, hs])])
            finish_fns.append(lambda sc, hs=hs: _softmax_pv(sc, [vb_s[:, hs]]))
        outs = _attend_heads(score_fns, finish_fns)
        o_ref[pl.ds(r0, B_QROWS), A_DIM:ATT_OUT] = unstack_heads(outs, gb, B_QROWS)
        return carry

    lax.fori_loop(0, t // B_QROWS, b_block, 0)


def _attn_sample_call(za, sink, caches, cos, sin, gq, gk, layer):
    nb = za.shape[0] // DEC_SEQ
    t = DEC_SEQ
    cache_spec = pl.BlockSpec((1, None, KV_DIM, PAST_LEN), lambda b: (b, layer, 0, 0))
    return pl.pallas_call(
        functools.partial(_attn_sample_body, layer=layer),
        out_shape=jax.ShapeDtypeStruct((nb * t, ATT_OUT), F32),
        grid=(nb,),
        in_specs=[
            pl.BlockSpec(memory_space=pltpu.SMEM),
            pl.BlockSpec((t, ATT_COLS), lambda b: (b, 0)),
            cache_spec, cache_spec, cache_spec, cache_spec,
            pl.BlockSpec((t, 128), lambda b: (0, 0)),
            pl.BlockSpec((t, 128), lambda b: (0, 0)),
            _layer_spec((1, B_DIM), layer),
            _layer_spec((1, KV_DIM), layer),
        ],
        out_specs=pl.BlockSpec((t, ATT_OUT), lambda b: (b, 0)),
        scratch_shapes=[
            pltpu.VMEM((t, A_DIM), BF16),
            pltpu.VMEM((t + 2 * Q_BLK, KV_DIM), BF16),
            pltpu.VMEM((t + 2 * Q_BLK, KV_DIM), BF16),
            pltpu.VMEM((t, B_DIM), BF16),
            pltpu.VMEM((t + PAST_LEN, KV_DIM), BF16),
            pltpu.VMEM((t + PAST_LEN, KV_DIM), BF16),
        ],
        compiler_params=pltpu.CompilerParams(vmem_limit_bytes=VMEM_LIMIT),
        name="attn_sample",
    )(sink, za, *caches, cos, sin, gq, gk)


def _rwkv_body(*refs, t, nbatch, has_init, want_state, has_prev):
    refs = list(refs)
    zc_ref, cv_ref, wup_ref, aup_ref, gup_ref = refs[:5]
    pos = 5
    st0_ref = prev_ref = stout_ref = None
    if has_init:
        st0_ref = refs[pos]
        pos += 1
    if has_prev:
        prev_ref = refs[pos]
        pos += 1
    oc_ref = refs[pos]
    pos += 1
    if want_state:
        stout_ref = refs[pos]
        pos += 1
    aa_s, ld_s, b_s, kt_s, bon_s, y_s, st_s = refs[pos:]

    nchunks = t // CHUNK
    ones = _head_ones(C_DIM)
    k_k = cv_ref[0:1]
    k_a = cv_ref[1:2]
    r_k = cv_ref[2:3]
    ln_w = cv_ref[3:4]
    ln_b = cv_ref[4:5]
    w0h = (0.5 * cv_ref[5:6], 0.5 * cv_ref[6:7])
    a0h = (0.5 * cv_ref[7:8], 0.5 * cv_ref[8:9])
    wup_h = ((0.5 * wup_ref[0]).astype(BF16), (0.5 * wup_ref[1]).astype(BF16))
    aup_h = ((0.5 * aup_ref[0]).astype(BF16), (0.5 * aup_ref[1]).astype(BF16))
    c1 = 1.0 - 0.5 * k_a
    c2 = 0.5 * k_a
    half_rate = 0.5 * DECAY_RATE

    for rb in range(nbatch * t // RWKV_ROWS):
        rows = slice(rb * RWKV_ROWS, (rb + 1) * RWKV_ROWS)
        zr = zc_ref[rows, 0:384]
        zk = zc_ref[rows, 384:768]
        zv = zc_ref[rows, 768:1152]
        tw = jnp.tanh(zc_ref[rows, 1152:1216])
        xa = zc_ref[rows, 1216:1280]
        kkr = zk * k_k
        kk_half = kkr * (0.5 * lax.rsqrt(_head_sum(kkr * kkr, ones) + 1e-12))
        aa_s[rows] = -2.0 * kk_half
        kt_sum = None
        for d in range(2):
            th_w = jnp.tanh(w0h[d] + _bdot(tw, wup_h[d]))
            ld_s[d, rows] = th_w * (-half_rate) - half_rate
            th_a = jnp.tanh(a0h[d] + _bdot(xa, aup_h[d]))
            kt = zk * (c1 + c2 * th_a)
            kt_s[d, rows] = kt
            b_s[d, rows] = kk_half * th_a + kk_half
            kt_sum = kt if kt_sum is None else kt_sum + kt
        bon_s[rows] = _head_sum(zr * kt_sum * r_k, ones) * zv
        y_s[rows] = jnp.zeros((RWKV_ROWS, C_DIM), F32)

    ir = lax.broadcasted_iota(jnp.int32, (CHUNK, 128), 0)
    ic = lax.broadcasted_iota(jnp.int32, (CHUNK, 128), 1) & 63
    eye = jnp.where(ir == ic, 1.0, 0.0).astype(F32)
    incl = (ic <= ir, ic >= ir)
    strict = (ic < ir, ic > ir)
    sr = lax.broadcasted_iota(jnp.int32, (CHUNK, CHUNK), 0)
    sc = lax.broadcasted_iota(jnp.int32, (CHUNK, CHUNK), 1)
    tri = tuple(jnp.where(m, 1.0, 0.0).astype(BF16) for m in (sc <= sr, sc >= sr))
    br = lax.broadcasted_iota(jnp.int32, (128, 128), 0)
    bc = lax.broadcasted_iota(jnp.int32, (128, 128), 1)
    diag_blocks = (br < 64) == (bc < 64)
    left_half = lax.broadcasted_iota(jnp.int32, (CHUNK, 128), 1) < 64

    def level_mask(rows, cols, j):
        same_big = lax.shift_right_logical(rows, jnp.int32(j + 1)) == lax.shift_right_logical(cols, jnp.int32(j + 1))
        same_small = lax.shift_right_logical(rows, jnp.int32(j)) == lax.shift_right_logical(cols, jnp.int32(j))
        return same_big & jnp.logical_not(same_small)

    first_level = level_mask(ir, ic, 0)
    level_blocks = [diag_blocks & level_mask(br & 63, bc & 63, j) for j in range(1, CHUNK.bit_length() - 1)]

    def blockdiag(x2):
        x2 = x2.astype(BF16)
        return jnp.where(diag_blocks, jnp.concatenate([x2, x2], axis=0), jnp.zeros((128, 128), BF16))

    npairs = C_HEADS // 2
    for bi in range(nbatch):
        for d in range(2):
            for p in range(npairs):
                if has_init:
                    st_s[bi, d, p] = jnp.concatenate([st0_ref[bi, d, 2 * p], st0_ref[bi, d, 2 * p + 1]], axis=1)
                else:
                    st_s[bi, d, p] = jnp.zeros((HEAD_DIM, 128), F32)

    def chunk(c, carry):
        c = _index32(c)
        items = []
        for bi, d in [(bi, d) for bi in range(nbatch) for d in range(2)]:
            r0 = bi * t + ((nchunks - 1 - c) if d == 1 else c) * CHUNK
            rows = pl.ds(pl.multiple_of(r0, CHUNK), CHUNK)
            ld = ld_s[d, rows]
            cinc = _split_dot_left(tri[d], ld)
            cexc = cinc - ld
            ctot = cinc[0:1] if d == 1 else cinc[CHUNK - 1:CHUNK]
            e_ninc = jnp.exp(-cinc)
            e_rem = jnp.exp(ctot - cinc)
            e_tot = jnp.exp(ctot)
            bb = b_s[d, rows]
            kt = kt_s[d, rows]
            at = (aa_s[rows] * jnp.exp(cexc)).astype(BF16)
            rt = (zc_ref[rows, 0:384] * jnp.exp(cinc)).astype(BF16)
            bt = (bb * e_ninc).astype(BF16)
            ktt = (kt * e_ninc).astype(BF16)
            bh = (bb * e_rem).astype(BF16)
            kh = (kt * e_rem).astype(BF16)
            vm = zc_ref[rows, 768:1152].astype(BF16)
            for p in range(npairs):
                ps = slice(p * 128, (p + 1) * 128)
                items.append(dict(bi=bi, d=d, p=p, rows=rows, e_tot=e_tot[:, ps], at=at[:, ps], rt=rt[:, ps],
                                  bt=bt[:, ps], kt=ktt[:, ps], bh=bh[:, ps], kh=kh[:, ps], vm=vm[:, ps]))
        for it in items:
            it["ar"] = jnp.concatenate([it["at"], it["rt"]], axis=0)
            it["vmd"] = blockdiag(it["vm"])
        for it in items:
            sbk = _bdot_nt(it["ar"], jnp.concatenate([blockdiag(it["bt"]), blockdiag(it["kt"])], axis=0))
            it["sb"] = sbk[:, 0:128]
            it["sk"] = sbk[:, 128:256]
        for it in items:
            d = it["d"]
            it["l"] = jnp.where(strict[d], it["sb"][0:CHUNK], 0.0)
            it["mak"] = jnp.where(strict[d], it["sk"][0:CHUNK], 0.0).astype(BF16)
            it["nr"] = jnp.concatenate([jnp.where(incl[d], it["sb"][CHUNK:], 0.0).astype(BF16),
                                        jnp.where(incl[d], it["sk"][CHUNK:], 0.0).astype(BF16)], axis=1)
            it["tm"] = eye + jnp.where(first_level, it["l"], 0.0)
            lb = it["l"].astype(BF16)
            it["l2"] = jnp.concatenate([lb, lb], axis=0)
        for lvl in level_blocks:
            for it in items:
                it["t1"] = _bdot(it["tm"], jnp.where(lvl, it["l2"], jnp.zeros((128, 128), BF16)))
            for it in items:
                it["tm"] = it["tm"] + _bdot(it["t1"], blockdiag(it["tm"]))
        for it in items:
            it["mv"] = _bdot(it["mak"], it["vmd"])
        for it in items:
            it["tau"] = _bdot(it["tm"], jnp.concatenate([blockdiag(it["at"]), blockdiag(it["mv"])], axis=1))
        for it in items:
            it["s0"] = st_s[it["bi"], it["d"], it["p"]]
            it["x"] = _bdot_nt(jnp.concatenate([it["tau"][:, 0:128].astype(BF16), it["rt"]], axis=0),
                               blockdiag(it["s0"]))
        for it in items:
            it["u"] = (it["x"][0:CHUNK] + it["tau"][:, 128:256]).astype(BF16)
        for it in items:
            it["y"] = it["x"][CHUNK:] + _bdot(it["nr"], jnp.concatenate([blockdiag(it["u"]), it["vmd"]], axis=0))
        for it in items:
            full = _bdot_tn(jnp.concatenate([it["u"], it["vm"]], axis=0),
                            jnp.concatenate([it["bh"], it["kh"]], axis=0))
            st_s[it["bi"], it["d"], it["p"]] = (it["s0"] * it["e_tot"]
                                                + jnp.where(left_half, full[0:64], full[64:128]))
        for i in range(0, len(items), npairs):
            rows = items[i]["rows"]
            y_s[rows] = y_s[rows] + jnp.concatenate([it["y"] for it in items[i:i + npairs]], axis=1)
        return carry

    lax.fori_loop(0, nchunks, chunk, 0)
    if want_state:
        for bi in range(nbatch):
            if has_prev:
                stout_ref[bi, 0] = prev_ref[bi]
            for d in range(2):
                for p in range(npairs):
                    s2 = st_s[bi, d, p]
                    for q in range(2):
                        if has_prev:
                            stout_ref[bi, 1, d, 2 * p + q] = s2[:, q * 64:(q + 1) * 64]
                        else:
                            stout_ref[bi, d, 2 * p + q] = s2[:, q * 64:(q + 1) * 64]

    gup = gup_ref[...].astype(BF16)
    for rb in range(nbatch * t // RWKV_ROWS):
        rows = slice(rb * RWKV_ROWS, (rb + 1) * RWKV_ROWS)
        y = y_s[rows]
        mu = _head_sum(y, ones) * (1.0 / HEAD_DIM)
        yc = y - mu
        var = _head_sum(yc * yc, ones) * (1.0 / HEAD_DIM)
        yn = yc * lax.rsqrt(var + GN_EPS)
        g = _bdot(_sigmoid(zc_ref[rows, 1280:1408]), gup)
        oc_ref[rows] = (yn * ln_w + ln_b + bon_s[rows]) * g


def _rwkv_call(zc, rvec, wup, aup, gup, layer, t, init_state, want_state, prev_state):
    nb = zc.shape[0] // t
    nbatch = min(RWKV_BATCH, nb)
    rows = nbatch * t
    has_init = init_state is not None
    has_prev = prev_state is not None
    st_shape = (2, C_HEADS, HEAD_DIM, HEAD_DIM)
    in_specs = [
        pl.BlockSpec((rows, RWKV_COLS), lambda b: (b, 0), pipeline_mode=pl.Buffered(1 if nb == nbatch else 2)),
        _layer_spec((16, C_DIM), layer),
        _layer_spec((2, W_RANK, C_DIM), layer),
        _layer_spec((2, A_RANK, C_DIM), layer),
        _layer_spec((G_RANK, C_DIM), layer),
    ]
    args = [zc, rvec, wup, aup, gup]
    if has_init:
        in_specs.append(pl.BlockSpec((nbatch, None) + st_shape, lambda b: (b, layer, 0, 0, 0, 0)))
        args.append(init_state)
    if has_prev:
        in_specs.append(pl.BlockSpec((nbatch,) + st_shape, lambda b: (b, 0, 0, 0, 0)))
        args.append(prev_state)
    out_shape = [jax.ShapeDtypeStruct((nb * t, C_DIM), F32)]
    out_specs = [pl.BlockSpec((rows, C_DIM), lambda b: (b, 0))]
    if want_state and has_prev:
        out_shape.append(jax.ShapeDtypeStruct((nb, DEPTH) + st_shape, F32))
        out_specs.append(pl.BlockSpec((nbatch, DEPTH) + st_shape, lambda b: (b, 0, 0, 0, 0, 0)))
    elif want_state:
        out_shape.append(jax.ShapeDtypeStruct((nb,) + st_shape, F32))
        out_specs.append(pl.BlockSpec((nbatch,) + st_shape, lambda b: (b, 0, 0, 0, 0)))
    return pl.pallas_call(
        functools.partial(_rwkv_body, t=t, nbatch=nbatch, has_init=has_init, want_state=want_state,
                          has_prev=has_prev),
        out_shape=tuple(out_shape),
        grid=(nb // nbatch,),
        in_specs=in_specs,
        out_specs=tuple(out_specs),
        scratch_shapes=[
            pltpu.VMEM((rows, C_DIM), F32),
            pltpu.VMEM((2, rows, C_DIM), F32),
            pltpu.VMEM((2, rows, C_DIM), F32),
            pltpu.VMEM((2, rows, C_DIM), F32),
            pltpu.VMEM((rows, C_DIM), F32),
            pltpu.VMEM((rows, C_DIM), F32),
            pltpu.VMEM((nbatch, 2, C_HEADS // 2, HEAD_DIM, 128), F32),
        ],
        compiler_params=pltpu.CompilerParams(vmem_limit_bytes=VMEM_LIMIT),
        name="rwkv_prompt" if want_state else "rwkv_sample",
    )(*args)


def _post_body(oa_ref, oc_ref, x_ref, mod_ref, g_ref, wout_ref, wgu_ref, wdn_ref, xo_ref, *, rows_per_mod):
    mod = _mod_row(mod_ref, rows_per_mod)
    gate1 = mod[:, 2 * D_MODEL:3 * D_MODEL]
    shift2 = mod[:, 3 * D_MODEL:4 * D_MODEL]
    scale2 = mod[:, 4 * D_MODEL:5 * D_MODEL]
    gate2 = mod[:, 5 * D_MODEL:6 * D_MODEL]
    o_cat = jnp.concatenate([oa_ref[...].astype(BF16), oc_ref[...].astype(BF16)], axis=1)
    o = jnp.dot(o_cat, wout_ref[...], preferred_element_type=F32)
    x1 = x_ref[...] + gate1 * _rms(o, g_ref[1:2])
    h2 = (_rms(x1, g_ref[2:3]) * (1.0 + scale2) + shift2).astype(BF16)
    acc = None
    for lo, hi in FF_CHUNKS:
        g = jnp.dot(h2, wgu_ref[:, lo:hi], preferred_element_type=F32)
        u = jnp.dot(h2, wgu_ref[:, FF_DIM + lo:FF_DIM + hi], preferred_element_type=F32)
        act = (g * _sigmoid(g) * u).astype(BF16)
        part = jnp.dot(act, wdn_ref[lo:hi, :], preferred_element_type=F32)
        acc = part if acc is None else acc + part
    xo_ref[...] = x1 + gate2 * _rms(acc, g_ref[3:4])


def _post_call(oatt, oc, x, mods, gvec, wout_b, wgu_b, wdn_b, layer, rows_per_mod):
    n = x.shape[0]
    return pl.pallas_call(
        functools.partial(_post_body, rows_per_mod=rows_per_mod),
        out_shape=jax.ShapeDtypeStruct((n, D_MODEL), F32),
        grid=(n // ROW_TILE,),
        in_specs=[
            pl.BlockSpec((ROW_TILE, ATT_OUT), lambda i: (i, 0)),
            pl.BlockSpec((ROW_TILE, C_DIM), lambda i: (i, 0)),
            pl.BlockSpec((ROW_TILE, D_MODEL), lambda i: (i, 0)),
            _layer_spec((MOD_ROWS, MOD_COLS), layer),
            _layer_spec((8, D_MODEL), layer),
            _layer_spec((MIX_DIM, D_MODEL), layer),
            _layer_spec((D_MODEL, 2 * FF_DIM), layer),
            _layer_spec((FF_DIM, D_MODEL), layer),
        ],
        out_specs=pl.BlockSpec((ROW_TILE, D_MODEL), lambda i: (i, 0)),
        compiler_params=pltpu.CompilerParams(vmem_limit_bytes=VMEM_LIMIT),
        name="post_ffn",
    )(oatt, oc, x, mods, gvec, wout_b, wgu_b, wdn_b)


def _rope_tables(t):
    pos = jnp.arange(t)
    row = (pos // GRID_W).astype(F32)
    col = (pos % GRID_W).astype(F32)
    freqs = ROPE_THETA ** (-jnp.arange(ROPE_PAIRS_AXIS, dtype=F32) / ROPE_PAIRS_AXIS)
    ang = jnp.concatenate([row[:, None] * freqs, col[:, None] * freqs], axis=-1)
    cos = jnp.tile(jnp.cos(ang), (1, 4))
    sin = jnp.sin(ang)
    sin = jnp.tile(jnp.concatenate([-sin, sin], axis=-1), (1, 2))
    return cos, sin


def _to_time_minor(cache):
    b, l, t = cache.shape[:3]
    return jnp.transpose(cache, (0, 1, 3, 4, 2)).reshape(b, l, KV_DIM, t)


def _from_time_minor(slab):
    b, l, _, t = slab.shape
    return jnp.transpose(slab.reshape(b, l, 2, HEAD_DIM, t), (0, 1, 4, 2, 3))


def kernel(x_prompt, x_sample, cache_a_k, cache_a_v, cache_b_k, cache_b_v, state_c, c, c_ctx, w_mod, b_mod, norm_mix_pre, norm_mix_post, norm_ffn_pre, norm_ffn_post, w_in, w_out, a_sink, b_q_norm, b_k_norm, c_w0, c_w_up, c_a0, c_a_up, c_g_up, c_k_k, c_k_a, c_r_k, c_ln_w, c_ln_b, w_gu, w_down):
    cvec = jnp.zeros((MOD_ROWS, D_MODEL), F32).at[0].set(c_ctx).at[1:1 + DEC_BATCH].set(c)
    mods = _mod_call(cvec, w_mod, b_mod)
    cos, sin = _rope_tables(DEC_SEQ)

    w_in_b = w_in.astype(BF16)
    wout_b = w_out.astype(BF16)
    wgu_b = w_gu.astype(BF16)
    wdn_b = w_down.astype(BF16)
    gvec = jnp.zeros((DEPTH, 8, D_MODEL), F32)
    gvec = gvec.at[:, 0].set(norm_mix_pre).at[:, 1].set(norm_mix_post)
    gvec = gvec.at[:, 2].set(norm_ffn_pre).at[:, 3].set(norm_ffn_post)
    rvec = jnp.zeros((DEPTH, 16, C_DIM), F32)
    rvec = rvec.at[:, 0].set(c_k_k).at[:, 1].set(c_k_a).at[:, 2].set(c_r_k.reshape(DEPTH, C_DIM))
    rvec = rvec.at[:, 3].set(c_ln_w).at[:, 4].set(c_ln_b)
    rvec = rvec.at[:, 5:7].set(c_w0).at[:, 7:9].set(c_a0)
    gq = jnp.tile(b_q_norm, (1, B_HEADS))[:, None, :]
    gk = jnp.tile(b_k_norm, (1, B_KV_HEADS))[:, None, :]
    caches = tuple(_to_time_minor(t) for t in (cache_a_k, cache_a_v, cache_b_k, cache_b_v))

    xp = x_prompt.reshape(BATCH * SEQ, D_MODEL)
    xs = x_sample.reshape(DEC_BATCH * DEC_SEQ, D_MODEL)
    kv_prev = st_prev = None
    for l in range(DEPTH):
        za, zc = _proj_in_call(xp, mods, gvec, w_in_b, l, None)
        oatt, *kv_prev = _attn_prompt_call(za, a_sink, gq, gk, l, kv_prev)
        if l == 0:
            (kv_prev,) = kv_prev
        oc, st_prev = _rwkv_call(zc, rvec, c_w_up, c_a_up, c_g_up, l, SEQ, None, True, st_prev)
        xp = _post_call(oatt, oc, xp, mods, gvec, wout_b, wgu_b, wdn_b, l, None)

        za, zc = _proj_in_call(xs, mods, gvec, w_in_b, l, DEC_SEQ)
        oatt = _attn_sample_call(za, a_sink, caches, cos, sin, gq, gk, l)
        (oc,) = _rwkv_call(zc, rvec, c_w_up, c_a_up, c_g_up, l, DEC_SEQ, state_c, False, None)
        xs = _post_call(oatt, oc, xs, mods, gvec, wout_b, wgu_b, wdn_b, l, DEC_SEQ)

    new_caches = tuple(_from_time_minor(slab) for slab in kv_prev)
    return (xp.reshape(BATCH, SEQ, D_MODEL), xs.reshape(DEC_BATCH, DEC_SEQ, D_MODEL), *new_caches, st_prev)
```

```python
import functools
import math

import jax
import jax.numpy as jnp
from jax import lax
from jax.experimental import pallas as pl
from jax.experimental.pallas import tpu as pltpu

D_MODEL = 1024
BATCH = 32
SEQ = 256
DEPTH = 2
DEC_BATCH = 2
DEC_SEQ = 1024
PAST_LEN = 256
GRID_W = 64
HEAD_DIM = 64
A_HEADS = 4
A_KV_HEADS = 2
B_HEADS = 6
B_KV_HEADS = 2
C_HEADS = 6
A_DIM = A_HEADS * HEAD_DIM
B_DIM = B_HEADS * HEAD_DIM
C_DIM = C_HEADS * HEAD_DIM
MIX_DIM = A_DIM + B_DIM + C_DIM
WINDOW = 128
Q_BLK = 128
W_RANK = 64
A_RANK = 64
G_RANK = 128
FF_DIM = -(-8 * D_MODEL // (3 * 256)) * 256
ROPE_THETA = 10000.0
ROPE_PAIRS_AXIS = HEAD_DIM // 4
NORM_EPS = 1e-6
GN_EPS = 64e-5
NEG_INF = -1e30

KV_DIM = 2 * HEAD_DIM
ATT_COLS = A_DIM + 2 * KV_DIM + B_DIM + 2 * KV_DIM
RWKV_COLS = 3 * C_DIM + W_RANK + A_RANK + G_RANK
IN_COLS = ATT_COLS + RWKV_COLS
ATT_OUT = A_DIM + B_DIM
MOD_COLS = 6 * D_MODEL
MOD_ROWS = 8
N_CACHE = 4
CHUNK = 64
QK_SCALE = HEAD_DIM ** -0.5
DECAY_RATE = math.exp(-0.5)
ROW_TILE = 512
ROW_SPLIT = 2
RWKV_ROWS = 256
RWKV_BATCH = 4
B_QROWS = 256
MXU_TILE = 256
FF_CHUNKS = ((0, 4 * MXU_TILE), (4 * MXU_TILE, 8 * MXU_TILE), (8 * MXU_TILE, FF_DIM))
assert FF_DIM % MXU_TILE == 0
VMEM_LIMIT = 56 * 1024 * 1024

F32 = jnp.float32
BF16 = jnp.bfloat16

assert DEPTH == 2


def _bdot(a, b):
    return jnp.dot(a.astype(BF16), b.astype(BF16), preferred_element_type=F32)


def _bdot_nt(a, b):
    return lax.dot_general(a.astype(BF16), b.astype(BF16), (((1,), (1,)), ((), ())), preferred_element_type=F32)


def _bdot_tn(a, b):
    return lax.dot_general(a.astype(BF16), b.astype(BF16), (((0,), (0,)), ((), ())), preferred_element_type=F32)


def _split(x):
    hi = x.astype(BF16)
    lo = (x - hi.astype(F32)).astype(BF16)
    return hi, lo


def _split_dot_left(m, x):
    hi, lo = _split(x)
    return jnp.dot(m, hi, preferred_element_type=F32) + jnp.dot(m, lo, preferred_element_type=F32)


def _head_sum(x, ones):
    return jnp.dot(x.astype(BF16), ones, preferred_element_type=F32)


def _sigmoid(x):
    return 0.5 * jnp.tanh(0.5 * x) + 0.5


def _head_ones(n):
    r = lax.shift_right_logical(lax.broadcasted_iota(jnp.int32, (n, n), 0), jnp.int32(6))
    c = lax.shift_right_logical(lax.broadcasted_iota(jnp.int32, (n, n), 1), jnp.int32(6))
    return jnp.where(r == c, 1.0, 0.0).astype(BF16)


def _rms(x, g):
    ms = jnp.mean(x * x, axis=-1, keepdims=True)
    return x * lax.rsqrt(ms + NORM_EPS) * g


def _head_rms(x, g, ones):
    ms = _head_sum(x * x, ones) * (1.0 / HEAD_DIM)
    return x * lax.rsqrt(ms + NORM_EPS) * g


def _rope(x, cos, sin):
    t = x.shape[0]
    lane = lax.broadcasted_iota(jnp.int32, (t, 128), 1)
    first = (lane & 32) == 0
    outs = []
    for i in range(x.shape[1] // 128):
        xs = x[:, i * 128:(i + 1) * 128]
        swapped = jnp.where(first, pltpu.roll(xs, 96, axis=1), pltpu.roll(xs, 32, axis=1))
        outs.append(xs * cos + swapped * sin)
    return outs[0] if len(outs) == 1 else jnp.concatenate(outs, axis=1)


def _softmax_pv(scores, values, sink=None):
    m = scores[0].max(axis=-1, keepdims=True)
    for s in scores[1:]:
        m = jnp.maximum(m, s.max(axis=-1, keepdims=True))
    if sink is not None:
        m = jnp.maximum(m, sink)
    denom = None
    acc = None
    for s, v in zip(scores, values):
        p = jnp.exp(s - m)
        d = p.sum(axis=-1, keepdims=True)
        o = jnp.dot(p.astype(BF16), v, preferred_element_type=F32)
        denom = d if denom is None else denom + d
        acc = o if acc is None else acc + o
    if sink is not None:
        denom = denom + jnp.exp(sink - m)
    return acc / denom


def _softmax_pv_t(scores_t, values_t, sink=None):
    m = scores_t[0].max(axis=0, keepdims=True)
    for s in scores_t[1:]:
        m = jnp.maximum(m, s.max(axis=0, keepdims=True))
    if sink is not None:
        m = jnp.maximum(m, sink)
    denom = None
    acc = None
    for s, vt in zip(scores_t, values_t):
        p = jnp.exp(s - m)
        d = p.sum(axis=0, keepdims=True)
        o = jnp.dot(vt, p.astype(BF16), preferred_element_type=F32)
        denom = d if denom is None else denom + d
        acc = o if acc is None else acc + o
    if sink is not None:
        denom = denom + jnp.exp(sink - m)
    return acc / denom


def _attend_heads(score_fns, finish_fns, ahead=1):
    n = len(score_fns)
    pending = [fn() for fn in score_fns[:ahead]]
    outs = []
    for i in range(n):
        if i + ahead < n:
            pending.append(score_fns[i + ahead]())
        outs.append(finish_fns[i](pending.pop(0)))
    return outs


def _index32(i):
    return jnp.asarray(i, jnp.int32)


def _layer_spec(shape, layer):
    zeros = (0,) * len(shape)
    return pl.BlockSpec((None,) + shape, lambda *_: (layer,) + zeros, pipeline_mode=pl.Buffered(1))


def _mod_row(mod_ref, rows_per_mod):
    if rows_per_mod is None:
        return mod_ref[0:1]
    row = 1 + pl.program_id(0) // (rows_per_mod // ROW_TILE)
    return mod_ref[pl.ds(row, 1), :]


def _mod_body(c_ref, w_ref, b_ref, o_ref):
    cv = c_ref[...]
    s = cv * _sigmoid(cv)
    o_ref[0] = _bdot(s, w_ref[0]) + b_ref[0]


def _mod_call(cvec, w_mod, b_mod):
    bn = MOD_COLS // 4
    return pl.pallas_call(
        _mod_body,
        out_shape=jax.ShapeDtypeStruct((DEPTH, MOD_ROWS, MOD_COLS), F32),
        grid=(DEPTH, MOD_COLS // bn),
        in_specs=[
            pl.BlockSpec((MOD_ROWS, D_MODEL), lambda l, j: (0, 0)),
            pl.BlockSpec((1, D_MODEL, bn), lambda l, j: (l, 0, j)),
            pl.BlockSpec((1, 1, bn), lambda l, j: (l, 0, j)),
        ],
        out_specs=pl.BlockSpec((1, MOD_ROWS, bn), lambda l, j: (l, 0, j)),
        compiler_params=pltpu.CompilerParams(vmem_limit_bytes=VMEM_LIMIT),
        name="mod_vectors",
    )(cvec, w_mod, b_mod.reshape(DEPTH, 1, MOD_COLS))


def _proj_in_body(x_ref, mod_ref, g_ref, w_ref, za_ref, zc_ref, *, rows_per_mod):
    mod = _mod_row(mod_ref, rows_per_mod)
    shift1 = mod[:, 0:D_MODEL]
    scale1 = mod[:, D_MODEL:2 * D_MODEL]
    sub = ROW_TILE // ROW_SPLIT
    for i in range(ROW_SPLIT):
        r = slice(i * sub, (i + 1) * sub)
        h = (_rms(x_ref[r, :], g_ref[0:1]) * (1.0 + scale1) + shift1).astype(BF16)
        z = jnp.dot(h, w_ref[...], preferred_element_type=F32)
        za_ref[r, :] = z[:, 0:ATT_COLS]
        zc_ref[r, :] = z[:, ATT_COLS:IN_COLS]


def _proj_in_call(x, mods, gvec, w_in_b, layer, rows_per_mod):
    n = x.shape[0]
    return pl.pallas_call(
        functools.partial(_proj_in_body, rows_per_mod=rows_per_mod),
        out_shape=(jax.ShapeDtypeStruct((n, ATT_COLS), F32), jax.ShapeDtypeStruct((n, RWKV_COLS), F32)),
        grid=(n // ROW_TILE,),
        in_specs=[
            pl.BlockSpec((ROW_TILE, D_MODEL), lambda i: (i, 0)),
            _layer_spec((MOD_ROWS, MOD_COLS), layer),
            _layer_spec((8, D_MODEL), layer),
            _layer_spec((D_MODEL, IN_COLS), layer),
        ],
        out_specs=(pl.BlockSpec((ROW_TILE, ATT_COLS), lambda i: (i, 0)),
                   pl.BlockSpec((ROW_TILE, RWKV_COLS), lambda i: (i, 0))),
        compiler_params=pltpu.CompilerParams(vmem_limit_bytes=VMEM_LIMIT),
        name="proj_in",
    )(x, mods, gvec, w_in_b)


def _attn_prompt_body(*refs, layer):
    sink_ref, za_ref, gq_ref, gk_ref = refs[:4]
    if layer == 0:
        o_ref, kv_ref = refs[4:]
    else:
        prev_ref, o_ref = refs[4:6]
        cache_refs = refs[6:]
    aq = (za_ref[:, 0:A_DIM] * QK_SCALE).astype(BF16)
    ak = za_ref[:, 256:384]
    bq = _head_rms(za_ref[:, 512:896], gq_ref[...], _head_ones(B_DIM))
    bk = _head_rms(za_ref[:, 896:1024], gk_ref[...], _head_ones(KV_DIM))
    bqb = (bq * QK_SCALE).astype(BF16)
    akb = ak.astype(BF16)
    bkb = bk.astype(BF16)
    pieces_t = (ak.T, za_ref[:, 384:512].T, bk.T, za_ref[:, 1024:1152].T)
    avt = pieces_t[1].astype(BF16)
    bvt = pieces_t[3].astype(BF16)

    score_fns, finish_fns = [], []
    for h in range(A_HEADS):
        kv = h // (A_HEADS // A_KV_HEADS)
        hs = slice(kv * 64, (kv + 1) * 64)
        score_fns.append(lambda h=h, hs=hs: [_bdot_nt(akb[:, hs], aq[:, h * 64:(h + 1) * 64])])
        finish_fns.append(lambda sc, h=h, hs=hs: _softmax_pv_t(sc, [avt[hs]], sink_ref[layer, h]))
    for h in range(B_HEADS):
        kv = h // (B_HEADS // B_KV_HEADS)
        hs = slice(kv * 64, (kv + 1) * 64)
        score_fns.append(lambda h=h, hs=hs: [_bdot_nt(bkb[:, hs], bqb[:, h * 64:(h + 1) * 64])])
        finish_fns.append(lambda sc, hs=hs: _softmax_pv_t(sc, [bvt[hs]]))
    o_ref[...] = jnp.concatenate(_attend_heads(score_fns, finish_fns, ahead=len(score_fns)), axis=0).T

    for j, piece_t in enumerate(pieces_t):
        if layer == 0:
            kv_ref[0, j] = piece_t
        else:
            cache_refs[j][0, 0] = prev_ref[0, j]
            cache_refs[j][0, 1] = piece_t


def _attn_prompt_call(za, sink, gq, gk, layer, prev_kv):
    nb = za.shape[0] // SEQ
    in_specs = [
        pl.BlockSpec(memory_space=pltpu.SMEM),
        pl.BlockSpec((SEQ, ATT_COLS), lambda b: (b, 0)),
        _layer_spec((1, B_DIM), layer),
        _layer_spec((1, KV_DIM), layer),
    ]
    args = [sink, za, gq, gk]
    out_shape = [jax.ShapeDtypeStruct((nb * SEQ, ATT_OUT), F32)]
    out_specs = [pl.BlockSpec((SEQ, ATT_OUT), lambda b: (b, 0))]
    slab_spec = pl.BlockSpec((1, N_CACHE, KV_DIM, SEQ), lambda b: (b, 0, 0, 0))
    if layer == 0:
        out_shape.append(jax.ShapeDtypeStruct((nb, N_CACHE, KV_DIM, SEQ), F32))
        out_specs.append(slab_spec)
    else:
        in_specs.append(slab_spec)
        args.append(prev_kv)
        for _ in range(N_CACHE):
            out_shape.append(jax.ShapeDtypeStruct((nb, DEPTH, KV_DIM, SEQ), F32))
            out_specs.append(pl.BlockSpec((1, DEPTH, KV_DIM, SEQ), lambda b: (b, 0, 0, 0)))
    return pl.pallas_call(
        functools.partial(_attn_prompt_body, layer=layer),
        out_shape=tuple(out_shape),
        grid=(nb,),
        in_specs=in_specs,
        out_specs=tuple(out_specs),
        compiler_params=pltpu.CompilerParams(vmem_limit_bytes=VMEM_LIMIT),
        name="attn_prompt",
    )(*args)


def _attn_sample_body(sink_ref, za_ref, cak_ref, cav_ref, cbk_ref, cbv_ref, cos_ref, sin_ref, gq_ref, gk_ref,
                      o_ref, qa_s, ka_s, va_s, qb_s, kb_s, vb_s, *, layer):
    t = DEC_SEQ
    cos = cos_ref[...]
    sin = sin_ref[...]
    qa_s[...] = (_rope(za_ref[:, 0:A_DIM], cos, sin) * QK_SCALE).astype(BF16)
    zpad = jnp.zeros((Q_BLK, KV_DIM), BF16)
    ka_s[0:Q_BLK] = zpad
    ka_s[Q_BLK + t:] = zpad
    va_s[0:Q_BLK] = zpad
    va_s[Q_BLK + t:] = zpad
    ka_s[Q_BLK:Q_BLK + t] = _rope(za_ref[:, 256:384], cos, sin).astype(BF16)
    va_s[Q_BLK:Q_BLK + t] = za_ref[:, 384:512].astype(BF16)
    bq = _head_rms(za_ref[:, 512:896], gq_ref[...], _head_ones(B_DIM))
    qb_s[...] = (_rope(bq, cos, sin) * QK_SCALE).astype(BF16)
    bk = _head_rms(za_ref[:, 896:1024], gk_ref[...], _head_ones(KV_DIM))
    kb_s[0:PAST_LEN] = cbk_ref[0].T.astype(BF16)
    kb_s[PAST_LEN:] = _rope(bk, cos, sin).astype(BF16)
    vb_s[0:PAST_LEN] = cbv_ref[0].T.astype(BF16)
    vb_s[PAST_LEN:] = za_ref[:, 1024:1152].astype(BF16)

    kca = cak_ref[0].T.astype(BF16)
    vca = cav_ref[0].T.astype(BF16)

    ga = A_HEADS // A_KV_HEADS
    gb = B_HEADS // B_KV_HEADS
    qpos = lax.broadcasted_iota(jnp.int32, (ga * Q_BLK, 3 * Q_BLK), 0) & (Q_BLK - 1)
    kpos = lax.broadcasted_iota(jnp.int32, (ga * Q_BLK, 3 * Q_BLK), 1) - Q_BLK
    near = jnp.abs(kpos - qpos) <= WINDOW
    head_of_row = lax.shift_right_logical(lax.broadcasted_iota(jnp.int32, (ga * Q_BLK, 1), 0),
                                          jnp.int32(Q_BLK.bit_length() - 1))
    sinks = []
    for kv in range(A_KV_HEADS):
        col = jnp.full((ga * Q_BLK, 1), sink_ref[layer, kv * ga], F32)
        for g in range(1, ga):
            col = jnp.where(head_of_row == g, sink_ref[layer, kv * ga + g], col)
        sinks.append(col)

    def stack_heads(q, kv, group):
        return jnp.concatenate([q[:, (kv * group + g) * 64:(kv * group + g + 1) * 64] for g in range(group)], axis=0)

    def unstack_heads(outs, group, rows):
        return jnp.concatenate([o[g * rows:(g + 1) * rows] for o in outs for g in range(group)], axis=1)

    def a_block(n, carry):
        r0 = pl.multiple_of(_index32(n) * Q_BLK, Q_BLK)
        q = qa_s[pl.ds(r0, Q_BLK), :]
        kl = ka_s[pl.ds(r0, 3 * Q_BLK), :]
        vl = va_s[pl.ds(r0, 3 * Q_BLK), :]
        kabs = kpos + r0
        mask = near & (kabs >= 0) & (kabs < t)
        score_fns, finish_fns = [], []
        for kv in range(A_KV_HEADS):
            hs = slice(kv * 64, (kv + 1) * 64)

            def scores(kv=kv, hs=hs):
                qs = stack_heads(q, kv, ga)
                return [jnp.where(mask, _bdot_nt(qs, kl[:, hs]), NEG_INF), _bdot_nt(qs, kca[:, hs])]

            score_fns.append(scores)
            finish_fns.append(lambda sc, kv=kv, hs=hs: _softmax_pv(sc, [vl[:, hs], vca[:, hs]], sinks[kv]))
        outs = _attend_heads(score_fns, finish_fns, ahead=A_KV_HEADS)
        o_ref[pl.ds(r0, Q_BLK), 0:A_DIM] = unstack_heads(outs, ga, Q_BLK)
        return carry

    lax.fori_loop(0, t // Q_BLK, a_block, 0)

    def b_block(n, carry):
        r0 = pl.multiple_of(_index32(n) * B_QROWS, B_QROWS)
        q = qb_s[pl.ds(r0, B_QROWS), :]
        score_fns, finish_fns = [], []
        for kv in range(B_KV_HEADS):
            hs = slice(kv * 64, (kv + 1) * 64)
            score_fns.append(lambda kv=kv, hs=hs: [_bdot_nt(stack_heads(q, kv, gb), kb_s[:, hs])])
            finish_fns.append(lambda sc, hs=hs: _softmax_pv(sc, [vb_s[:, hs]]))
        outs = _attend_heads(score_fns, finish_fns)
        o_ref[pl.ds(r0, B_QROWS), A_DIM:ATT_OUT] = unstack_heads(outs, gb, B_QROWS)
        return carry

    lax.fori_loop(0, t // B_QROWS, b_block, 0)


def _attn_sample_call(za, sink, caches, cos, sin, gq, gk, layer):
    nb = za.shape[0] // DEC_SEQ
    t = DEC_SEQ
    cache_spec = pl.BlockSpec((1, None, KV_DIM, PAST_LEN), lambda b: (b, layer, 0, 0))
    return pl.pallas_call(
        functools.partial(_attn_sample_body, layer=layer),
        out_shape=jax.ShapeDtypeStruct((nb * t, ATT_OUT), F32),
        grid=(nb,),
        in_specs=[
            pl.BlockSpec(memory_space=pltpu.SMEM),
            pl.BlockSpec((t, ATT_COLS), lambda b: (b, 0)),
            cache_spec, cache_spec, cache_spec, cache_spec,
            pl.BlockSpec((t, 128), lambda b: (0, 0)),
            pl.BlockSpec((t, 128), lambda b: (0, 0)),
            _layer_spec((1, B_DIM), layer),
            _layer_spec((1, KV_DIM), layer),
        ],
        out_specs=pl.BlockSpec((t, ATT_OUT), lambda b: (b, 0)),
        scratch_shapes=[
            pltpu.VMEM((t, A_DIM), BF16),
            pltpu.VMEM((t + 2 * Q_BLK, KV_DIM), BF16),
            pltpu.VMEM((t + 2 * Q_BLK, KV_DIM), BF16),
            pltpu.VMEM((t, B_DIM), BF16),
            pltpu.VMEM((t + PAST_LEN, KV_DIM), BF16),
            pltpu.VMEM((t + PAST_LEN, KV_DIM), BF16),
        ],
        compiler_params=pltpu.CompilerParams(vmem_limit_bytes=VMEM_LIMIT),
        name="attn_sample",
    )(sink, za, *caches, cos, sin, gq, gk)


def _rwkv_body(*refs, t, nbatch, has_init, want_state, has_prev):
    refs = list(refs)
    zc_ref, cv_ref, wup_ref, aup_ref, gup_ref = refs[:5]
    pos = 5
    st0_ref = prev_ref = stout_ref = None
    if has_init:
        st0_ref = refs[pos]
        pos += 1
    if has_prev:
        prev_ref = refs[pos]
        pos += 1
    oc_ref = refs[pos]
    pos += 1
    if want_state:
        stout_ref = refs[pos]
        pos += 1
    aa_s, ld_s, b_s, kt_s, bon_s, y_s, st_s = refs[pos:]

    nchunks = t // CHUNK
    ones = _head_ones(C_DIM)
    k_k = cv_ref[0:1]
    k_a = cv_ref[1:2]
    r_k = cv_ref[2:3]
    ln_w = cv_ref[3:4]
    ln_b = cv_ref[4:5]
    w0h = (0.5 * cv_ref[5:6], 0.5 * cv_ref[6:7])
    a0h = (0.5 * cv_ref[7:8], 0.5 * cv_ref[8:9])
    wup_h = ((0.5 * wup_ref[0]).astype(BF16), (0.5 * wup_ref[1]).astype(BF16))
    aup_h = ((0.5 * aup_ref[0]).astype(BF16), (0.5 * aup_ref[1]).astype(BF16))
    c1 = 1.0 - 0.5 * k_a
    c2 = 0.5 * k_a
    half_rate = 0.5 * DECAY_RATE

    for rb in range(nbatch * t // RWKV_ROWS):
        rows = slice(rb * RWKV_ROWS, (rb + 1) * RWKV_ROWS)
        zr = zc_ref[rows, 0:384]
        zk = zc_ref[rows, 384:768]
        zv = zc_ref[rows, 768:1152]
        tw = jnp.tanh(zc_ref[rows, 1152:1216])
        xa = zc_ref[rows, 1216:1280]
        kkr = zk * k_k
        kk_half = kkr * (0.5 * lax.rsqrt(_head_sum(kkr * kkr, ones) + 1e-12))
        aa_s[rows] = -2.0 * kk_half
        kt_sum = None
        for d in range(2):
            th_w = jnp.tanh(w0h[d] + _bdot(tw, wup_h[d]))
            ld_s[d, rows] = th_w * (-half_rate) - half_rate
            th_a = jnp.tanh(a0h[d] + _bdot(xa, aup_h[d]))
            kt = zk * (c1 + c2 * th_a)
            kt_s[d, rows] = kt
            b_s[d, rows] = kk_half * th_a + kk_half
            kt_sum = kt if kt_sum is None else kt_sum + kt
        bon_s[rows] = _head_sum(zr * kt_sum * r_k, ones) * zv
        y_s[rows] = jnp.zeros((RWKV_ROWS, C_DIM), F32)

    ir = lax.broadcasted_iota(jnp.int32, (CHUNK, 128), 0)
    ic = lax.broadcasted_iota(jnp.int32, (CHUNK, 128), 1) & 63
    eye = jnp.where(ir == ic, 1.0, 0.0).astype(F32)
    incl = (ic <= ir, ic >= ir)
    strict = (ic < ir, ic > ir)
    sr = lax.broadcasted_iota(jnp.int32, (CHUNK, CHUNK), 0)
    sc = lax.broadcasted_iota(jnp.int32, (CHUNK, CHUNK), 1)
    tri = tuple(jnp.where(m, 1.0, 0.0).astype(BF16) for m in (sc <= sr, sc >= sr))
    br = lax.broadcasted_iota(jnp.int32, (128, 128), 0)
    bc = lax.broadcasted_iota(jnp.int32, (128, 128), 1)
    diag_blocks = (br < 64) == (bc < 64)
    left_half = lax.broadcasted_iota(jnp.int32, (CHUNK, 128), 1) < 64

    def level_mask(rows, cols, j):
        same_big = lax.shift_right_logical(rows, jnp.int32(j + 1)) == lax.shift_right_logical(cols, jnp.int32(j + 1))
        same_small = lax.shift_right_logical(rows, jnp.int32(j)) == lax.shift_right_logical(cols, jnp.int32(j))
        return same_big & jnp.logical_not(same_small)

    first_level = level_mask(ir, ic, 0)
    level_blocks = [diag_blocks & level_mask(br & 63, bc & 63, j) for j in range(1, CHUNK.bit_length() - 1)]

    def blockdiag(x2):
        x2 = x2.astype(BF16)
        return jnp.where(diag_blocks, jnp.concatenate([x2, x2], axis=0), jnp.zeros((128, 128), BF16))

    npairs = C_HEADS // 2
    for bi in range(nbatch):
        for d in range(2):
            for p in range(npairs):
                if has_init:
                    st_s[bi, d, p] = jnp.concatenate([st0_ref[bi, d, 2 * p], st0_ref[bi, d, 2 * p + 1]], axis=1)
                else:
                    st_s[bi, d, p] = jnp.zeros((HEAD_DIM, 128), F32)

    def chunk(c, carry):
        c = _index32(c)
        items = []
        for bi, d in [(bi, d) for bi in range(nbatch) for d in range(2)]:
            r0 = bi * t + ((nchunks - 1 - c) if d == 1 else c) * CHUNK
            rows = pl.ds(pl.multiple_of(r0, CHUNK), CHUNK)
            ld = ld_s[d, rows]
            cinc = _split_dot_left(tri[d], ld)
            cexc = cinc - ld
            ctot = cinc[0:1] if d == 1 else cinc[CHUNK - 1:CHUNK]
            e_ninc = jnp.exp(-cinc)
            e_rem = jnp.exp(ctot - cinc)
            e_tot = jnp.exp(ctot)
            bb = b_s[d, rows]
            kt = kt_s[d, rows]
            at = (aa_s[rows] * jnp.exp(cexc)).astype(BF16)
            rt = (zc_ref[rows, 0:384] * jnp.exp(cinc)).astype(BF16)
            bt = (bb * e_ninc).astype(BF16)
            ktt = (kt * e_ninc).astype(BF16)
            bh = (bb * e_rem).astype(BF16)
            kh = (kt * e_rem).astype(BF16)
            vm = zc_ref[rows, 768:1152].astype(BF16)
            for p in range(npairs):
                ps = slice(p * 128, (p + 1) * 128)
                items.append(dict(bi=bi, d=d, p=p, rows=rows, e_tot=e_tot[:, ps], at=at[:, ps], rt=rt[:, ps],
                                  bt=bt[:, ps], kt=ktt[:, ps], bh=bh[:, ps], kh=kh[:, ps], vm=vm[:, ps]))
        for it in items:
            it["ar"] = jnp.concatenate([it["at"], it["rt"]], axis=0)
            it["vmd"] = blockdiag(it["vm"])
        for it in items:
            sbk = _bdot_nt(it["ar"], jnp.concatenate([blockdiag(it["bt"]), blockdiag(it["kt"])], axis=0))
            it["sb"] = sbk[:, 0:128]
            it["sk"] = sbk[:, 128:256]
        for it in items:
            d = it["d"]
            it["l"] = jnp.where(strict[d], it["sb"][0:CHUNK], 0.0)
            it["mak"] = jnp.where(strict[d], it["sk"][0:CHUNK], 0.0).astype(BF16)
            it["nr"] = jnp.concatenate([jnp.where(incl[d], it["sb"][CHUNK:], 0.0).astype(BF16),
                                        jnp.where(incl[d], it["sk"][CHUNK:], 0.0).astype(BF16)], axis=1)
            it["tm"] = eye + jnp.where(first_level, it["l"], 0.0)
            lb = it["l"].astype(BF16)
            it["l2"] = jnp.concatenate([lb, lb], axis=0)
        for lvl in level_blocks:
            for it in items:
                it["t1"] = _bdot(it["tm"], jnp.where(lvl, it["l2"], jnp.zeros((128, 128), BF16)))
            for it in items:
                it["tm"] = it["tm"] + _bdot(it["t1"], blockdiag(it["tm"]))
        for it in items:
            it["mv"] = _bdot(it["mak"], it["vmd"])
        for it in items:
            it["tau"] = _bdot(it["tm"], jnp.concatenate([blockdiag(it["at"]), blockdiag(it["mv"])], axis=1))
        for it in items:
            it["s0"] = st_s[it["bi"], it["d"], it["p"]]
            it["x"] = _bdot_nt(jnp.concatenate([it["tau"][:, 0:128].astype(BF16), it["rt"]], axis=0),
                               blockdiag(it["s0"]))
        for it in items:
            it["u"] = (it["x"][0:CHUNK] + it["tau"][:, 128:256]).astype(BF16)
        for it in items:
            it["y"] = it["x"][CHUNK:] + _bdot(it["nr"], jnp.concatenate([blockdiag(it["u"]), it["vmd"]], axis=0))
        for it in items:
            full = _bdot_tn(jnp.concatenate([it["u"], it["vm"]], axis=0),
                            jnp.concatenate([it["bh"], it["kh"]], axis=0))
            st_s[it["bi"], it["d"], it["p"]] = (it["s0"] * it["e_tot"]
                                                + jnp.where(left_half, full[0:64], full[64:128]))
        for i in range(0, len(items), npairs):
            rows = items[i]["rows"]
            y_s[rows] = y_s[rows] + jnp.concatenate([it["y"] for it in items[i:i + npairs]], axis=1)
        return carry

    lax.fori_loop(0, nchunks, chunk, 0)
    if want_state:
        for bi in range(nbatch):
            if has_prev:
                stout_ref[bi, 0] = prev_ref[bi]
            for d in range(2):
                for p in range(npairs):
                    s2 = st_s[bi, d, p]
                    for q in range(2):
                        if has_prev:
                            stout_ref[bi, 1, d, 2 * p + q] = s2[:, q * 64:(q + 1) * 64]
                        else:
                            stout_ref[bi, d, 2 * p + q] = s2[:, q * 64:(q + 1) * 64]

    gup = gup_ref[...].astype(BF16)
    for rb in range(nbatch * t // RWKV_ROWS):
        rows = slice(rb * RWKV_ROWS, (rb + 1) * RWKV_ROWS)
        y = y_s[rows]
        mu = _head_sum(y, ones) * (1.0 / HEAD_DIM)
        yc = y - mu
        var = _head_sum(yc * yc, ones) * (1.0 / HEAD_DIM)
        yn = yc * lax.rsqrt(var + GN_EPS)
        g = _bdot(_sigmoid(zc_ref[rows, 1280:1408]), gup)
        oc_ref[rows] = (yn * ln_w + ln_b + bon_s[rows]) * g


def _rwkv_call(zc, rvec, wup, aup, gup, layer, t, init_state, want_state, prev_state):
    nb = zc.shape[0] // t
    nbatch = min(RWKV_BATCH, nb)
    rows = nbatch * t
    has_init = init_state is not None
    has_prev = prev_state is not None
    st_shape = (2, C_HEADS, HEAD_DIM, HEAD_DIM)
    in_specs = [
        pl.BlockSpec((rows, RWKV_COLS), lambda b: (b, 0), pipeline_mode=pl.Buffered(1 if nb == nbatch else 2)),
        _layer_spec((16, C_DIM), layer),
        _layer_spec((2, W_RANK, C_DIM), layer),
        _layer_spec((2, A_RANK, C_DIM), layer),
        _layer_spec((G_RANK, C_DIM), layer),
    ]
    args = [zc, rvec, wup, aup, gup]
    if has_init:
        in_specs.append(pl.BlockSpec((nbatch, None) + st_shape, lambda b: (b, layer, 0, 0, 0, 0)))
        args.append(init_state)
    if has_prev:
        in_specs.append(pl.BlockSpec((nbatch,) + st_shape, lambda b: (b, 0, 0, 0, 0)))
        args.append(prev_state)
    out_shape = [jax.ShapeDtypeStruct((nb * t, C_DIM), F32)]
    out_specs = [pl.BlockSpec((rows, C_DIM), lambda b: (b, 0))]
    if want_state and has_prev:
        out_shape.append(jax.ShapeDtypeStruct((nb, DEPTH) + st_shape, F32))
        out_specs.append(pl.BlockSpec((nbatch, DEPTH) + st_shape, lambda b: (b, 0, 0, 0, 0, 0)))
    elif want_state:
        out_shape.append(jax.ShapeDtypeStruct((nb,) + st_shape, F32))
        out_specs.append(pl.BlockSpec((nbatch,) + st_shape, lambda b: (b, 0, 0, 0, 0)))
    return pl.pallas_call(
        functools.partial(_rwkv_body, t=t, nbatch=nbatch, has_init=has_init, want_state=want_state,
                          has_prev=has_prev),
        out_shape=tuple(out_shape),
        grid=(nb // nbatch,),
        in_specs=in_specs,
        out_specs=tuple(out_specs),
        scratch_shapes=[
            pltpu.VMEM((rows, C_DIM), F32),
            pltpu.VMEM((2, rows, C_DIM), F32),
            pltpu.VMEM((2, rows, C_DIM), F32),
            pltpu.VMEM((2, rows, C_DIM), F32),
            pltpu.VMEM((rows, C_DIM), F32),
            pltpu.VMEM((rows, C_DIM), F32),
            pltpu.VMEM((nbatch, 2, C_HEADS // 2, HEAD_DIM, 128), F32),
        ],
        compiler_params=pltpu.CompilerParams(vmem_limit_bytes=VMEM_LIMIT),
        name="rwkv_prompt" if want_state else "rwkv_sample",
    )(*args)


def _post_body(oa_ref, oc_ref, x_ref, mod_ref, g_ref, wout_ref, wgu_ref, wdn_ref, xo_ref, *, rows_per_mod):
    mod = _mod_row(mod_ref, rows_per_mod)
    gate1 = mod[:, 2 * D_MODEL:3 * D_MODEL]
    shift2 = mod[:, 3 * D_MODEL:4 * D_MODEL]
    scale2 = mod[:, 4 * D_MODEL:5 * D_MODEL]
    gate2 = mod[:, 5 * D_MODEL:6 * D_MODEL]
    sub = ROW_TILE // ROW_SPLIT
    parts = [slice(i * sub, (i + 1) * sub) for i in range(ROW_SPLIT)]
    o = [jnp.dot(jnp.concatenate([oa_ref[r, :].astype(BF16), oc_ref[r, :].astype(BF16)], axis=1), wout_ref[...],
                 preferred_element_type=F32) for r in parts]
    x1 = [x_ref[r, :] + gate1 * _rms(oi, g_ref[1:2]) for r, oi in zip(parts, o)]
    h2 = [(_rms(xi, g_ref[2:3]) * (1.0 + scale2) + shift2).astype(BF16) for xi in x1]

    def gate_up(step):
        (lo, hi), i = step
        g = jnp.dot(h2[i], wgu_ref[:, lo:hi], preferred_element_type=F32)
        u = jnp.dot(h2[i], wgu_ref[:, FF_DIM + lo:FF_DIM + hi], preferred_element_type=F32)
        return g, u

    steps = [(chunk, i) for chunk in FF_CHUNKS for i in range(ROW_SPLIT)]
    acc = [None] * ROW_SPLIT
    pending = gate_up(steps[0])
    for k, ((lo, hi), i) in enumerate(steps):
        g, u = pending
        if k + 1 < len(steps):
            pending = gate_up(steps[k + 1])
        act = (g * _sigmoid(g) * u).astype(BF16)
        part = jnp.dot(act, wdn_ref[lo:hi, :], preferred_element_type=F32)
        acc[i] = part if acc[i] is None else acc[i] + part
    for r, xi, ai in zip(parts, x1, acc):
        xo_ref[r, :] = xi + gate2 * _rms(ai, g_ref[3:4])


def _post_call(oatt, oc, x, mods, gvec, wout_b, wgu_b, wdn_b, layer, rows_per_mod):
    n = x.shape[0]
    return pl.pallas_call(
        functools.partial(_post_body, rows_per_mod=rows_per_mod),
        out_shape=jax.ShapeDtypeStruct((n, D_MODEL), F32),
        grid=(n // ROW_TILE,),
        in_specs=[
            pl.BlockSpec((ROW_TILE, ATT_OUT), lambda i: (i, 0)),
            pl.BlockSpec((ROW_TILE, C_DIM), lambda i: (i, 0)),
            pl.BlockSpec((ROW_TILE, D_MODEL), lambda i: (i, 0)),
            _layer_spec((MOD_ROWS, MOD_COLS), layer),
            _layer_spec((8, D_MODEL), layer),
            _layer_spec((MIX_DIM, D_MODEL), layer),
            _layer_spec((D_MODEL, 2 * FF_DIM), layer),
            _layer_spec((FF_DIM, D_MODEL), layer),
        ],
        out_specs=pl.BlockSpec((ROW_TILE, D_MODEL), lambda i: (i, 0)),
        compiler_params=pltpu.CompilerParams(vmem_limit_bytes=VMEM_LIMIT),
        name="post_ffn",
    )(oatt, oc, x, mods, gvec, wout_b, wgu_b, wdn_b)


def _rope_tables(t):
    pos = jnp.arange(t)
    row = (pos // GRID_W).astype(F32)
    col = (pos % GRID_W).astype(F32)
    freqs = ROPE_THETA ** (-jnp.arange(ROPE_PAIRS_AXIS, dtype=F32) / ROPE_PAIRS_AXIS)
    ang = jnp.concatenate([row[:, None] * freqs, col[:, None] * freqs], axis=-1)
    cos = jnp.tile(jnp.cos(ang), (1, 4))
    sin = jnp.sin(ang)
    sin = jnp.tile(jnp.concatenate([-sin, sin], axis=-1), (1, 2))
    return cos, sin


def _to_time_minor(cache):
    b, l, t = cache.shape[:3]
    return jnp.transpose(cache, (0, 1, 3, 4, 2)).reshape(b, l, KV_DIM, t)


def _from_time_minor(slab):
    b, l, _, t = slab.shape
    return jnp.transpose(slab.reshape(b, l, 2, HEAD_DIM, t), (0, 1, 4, 2, 3))


def kernel(x_prompt, x_sample, cache_a_k, cache_a_v, cache_b_k, cache_b_v, state_c, c, c_ctx, w_mod, b_mod, norm_mix_pre, norm_mix_post, norm_ffn_pre, norm_ffn_post, w_in, w_out, a_sink, b_q_norm, b_k_norm, c_w0, c_w_up, c_a0, c_a_up, c_g_up, c_k_k, c_k_a, c_r_k, c_ln_w, c_ln_b, w_gu, w_down):
    cvec = jnp.zeros((MOD_ROWS, D_MODEL), F32).at[0].set(c_ctx).at[1:1 + DEC_BATCH].set(c)
    mods = _mod_call(cvec, w_mod, b_mod)
    cos, sin = _rope_tables(DEC_SEQ)

    w_in_b = w_in.astype(BF16)
    wout_b = w_out.astype(BF16)
    wgu_b = w_gu.astype(BF16)
    wdn_b = w_down.astype(BF16)
    gvec = jnp.zeros((DEPTH, 8, D_MODEL), F32)
    gvec = gvec.at[:, 0].set(norm_mix_pre).at[:, 1].set(norm_mix_post)
    gvec = gvec.at[:, 2].set(norm_ffn_pre).at[:, 3].set(norm_ffn_post)
    rvec = jnp.zeros((DEPTH, 16, C_DIM), F32)
    rvec = rvec.at[:, 0].set(c_k_k).at[:, 1].set(c_k_a).at[:, 2].set(c_r_k.reshape(DEPTH, C_DIM))
    rvec = rvec.at[:, 3].set(c_ln_w).at[:, 4].set(c_ln_b)
    rvec = rvec.at[:, 5:7].set(c_w0).at[:, 7:9].set(c_a0)
    gq = jnp.tile(b_q_norm, (1, B_HEADS))[:, None, :]
    gk = jnp.tile(b_k_norm, (1, B_KV_HEADS))[:, None, :]
    caches = tuple(_to_time_minor(t) for t in (cache_a_k, cache_a_v, cache_b_k, cache_b_v))

    xp = x_prompt.reshape(BATCH * SEQ, D_MODEL)
    xs = x_sample.reshape(DEC_BATCH * DEC_SEQ, D_MODEL)
    kv_prev = st_prev = None
    for l in range(DEPTH):
        za, zc = _proj_in_call(xp, mods, gvec, w_in_b, l, None)
        oatt, *kv_prev = _attn_prompt_call(za, a_sink, gq, gk, l, kv_prev)
        if l == 0:
            (kv_prev,) = kv_prev
        oc, st_prev = _rwkv_call(zc, rvec, c_w_up, c_a_up, c_g_up, l, SEQ, None, True, st_prev)
        xp = _post_call(oatt, oc, xp, mods, gvec, wout_b, wgu_b, wdn_b, l, None)

        za, zc = _proj_in_call(xs, mods, gvec, w_in_b, l, DEC_SEQ)
        oatt = _attn_sample_call(za, a_sink, caches, cos, sin, gq, gk, l)
        (oc,) = _rwkv_call(zc, rvec, c_w_up, c_a_up, c_g_up, l, DEC_SEQ, state_c, False, None)
        xs = _post_call(oatt, oc, xs, mods, gvec, wout_b, wgu_b, wdn_b, l, DEC_SEQ)

    new_caches = tuple(_from_time_minor(slab) for slab in kv_prev)
    return (xp.reshape(BATCH, SEQ, D_MODEL), xs.reshape(DEC_BATCH, DEC_SEQ, D_MODEL), *new_caches, st_prev)
```

```python
import functools
import math

import jax
import jax.numpy as jnp
from jax import lax
from jax.experimental import pallas as pl
from jax.experimental.pallas import tpu as pltpu

D_MODEL = 1024
BATCH = 32
SEQ = 256
DEPTH = 2
DEC_BATCH = 2
DEC_SEQ = 1024
PAST_LEN = 256
GRID_W = 64
HEAD_DIM = 64
A_HEADS = 4
A_KV_HEADS = 2
B_HEADS = 6
B_KV_HEADS = 2
C_HEADS = 6
A_DIM = A_HEADS * HEAD_DIM
B_DIM = B_HEADS * HEAD_DIM
C_DIM = C_HEADS * HEAD_DIM
MIX_DIM = A_DIM + B_DIM + C_DIM
WINDOW = 128
Q_BLK = 128
W_RANK = 64
A_RANK = 64
G_RANK = 128
FF_DIM = -(-8 * D_MODEL // (3 * 256)) * 256
ROPE_THETA = 10000.0
ROPE_PAIRS_AXIS = HEAD_DIM // 4
NORM_EPS = 1e-6
GN_EPS = 64e-5
NEG_INF = -1e30

KV_DIM = 2 * HEAD_DIM
ATT_COLS = A_DIM + 2 * KV_DIM + B_DIM + 2 * KV_DIM
RWKV_COLS = 3 * C_DIM + W_RANK + A_RANK + G_RANK
IN_COLS = ATT_COLS + RWKV_COLS
ATT_OUT = A_DIM + B_DIM
MOD_COLS = 6 * D_MODEL
MOD_ROWS = 8
N_CACHE = 4
CHUNK = 64
QK_SCALE = HEAD_DIM ** -0.5
DECAY_RATE = math.exp(-0.5)
ROW_TILE = 512
ROW_SPLIT = 2
RWKV_ROWS = 256
RWKV_BATCH = 4
B_QROWS = 256
MXU_TILE = 256
FF_CHUNKS = ((0, 4 * MXU_TILE), (4 * MXU_TILE, 8 * MXU_TILE), (8 * MXU_TILE, FF_DIM))
assert FF_DIM % MXU_TILE == 0
VMEM_LIMIT = 56 * 1024 * 1024

F32 = jnp.float32
BF16 = jnp.bfloat16

assert DEPTH == 2


def _bdot(a, b):
    return jnp.dot(a.astype(BF16), b.astype(BF16), preferred_element_type=F32)


def _bdot_nt(a, b):
    return lax.dot_general(a.astype(BF16), b.astype(BF16), (((1,), (1,)), ((), ())), preferred_element_type=F32)


def _bdot_tn(a, b):
    return lax.dot_general(a.astype(BF16), b.astype(BF16), (((0,), (0,)), ((), ())), preferred_element_type=F32)


def _split(x):
    hi = x.astype(BF16)
    lo = (x - hi.astype(F32)).astype(BF16)
    return hi, lo


def _split_dot_left(m, x):
    hi, lo = _split(x)
    return jnp.dot(m, hi, preferred_element_type=F32) + jnp.dot(m, lo, preferred_element_type=F32)


def _head_sum(x, ones):
    return jnp.dot(x.astype(BF16), ones, preferred_element_type=F32)


def _sigmoid(x):
    return 0.5 * jnp.tanh(0.5 * x) + 0.5


def _head_ones(n):
    r = lax.shift_right_logical(lax.broadcasted_iota(jnp.int32, (n, n), 0), jnp.int32(6))
    c = lax.shift_right_logical(lax.broadcasted_iota(jnp.int32, (n, n), 1), jnp.int32(6))
    return jnp.where(r == c, 1.0, 0.0).astype(BF16)


def _rms(x, g):
    ms = jnp.mean(x * x, axis=-1, keepdims=True)
    return x * lax.rsqrt(ms + NORM_EPS) * g


def _head_rms(x, g, ones):
    ms = _head_sum(x * x, ones) * (1.0 / HEAD_DIM)
    return x * lax.rsqrt(ms + NORM_EPS) * g


def _rope(x, cos, sin):
    t = x.shape[0]
    lane = lax.broadcasted_iota(jnp.int32, (t, 128), 1)
    first = (lane & 32) == 0
    outs = []
    for i in range(x.shape[1] // 128):
        xs = x[:, i * 128:(i + 1) * 128]
        swapped = jnp.where(first, pltpu.roll(xs, 96, axis=1), pltpu.roll(xs, 32, axis=1))
        outs.append(xs * cos + swapped * sin)
    return outs[0] if len(outs) == 1 else jnp.concatenate(outs, axis=1)


def _softmax_pv(scores, values, sink=None):
    m = scores[0].max(axis=-1, keepdims=True)
    for s in scores[1:]:
        m = jnp.maximum(m, s.max(axis=-1, keepdims=True))
    if sink is not None:
        m = jnp.maximum(m, sink)
    denom = None
    acc = None
    for s, v in zip(scores, values):
        p = jnp.exp(s - m)
        d = p.sum(axis=-1, keepdims=True)
        o = jnp.dot(p.astype(BF16), v, preferred_element_type=F32)
        denom = d if denom is None else denom + d
        acc = o if acc is None else acc + o
    if sink is not None:
        denom = denom + jnp.exp(sink - m)
    return acc / denom


def _softmax_pv_t(scores_t, values_t, sink=None):
    m = scores_t[0].max(axis=0, keepdims=True)
    for s in scores_t[1:]:
        m = jnp.maximum(m, s.max(axis=0, keepdims=True))
    if sink is not None:
        m = jnp.maximum(m, sink)
    denom = None
    acc = None
    for s, vt in zip(scores_t, values_t):
        p = jnp.exp(s - m)
        d = p.sum(axis=0, keepdims=True)
        o = jnp.dot(vt, p.astype(BF16), preferred_element_type=F32)
        denom = d if denom is None else denom + d
        acc = o if acc is None else acc + o
    if sink is not None:
        denom = denom + jnp.exp(sink - m)
    return acc / denom


def _attend_heads(score_fns, finish_fns, ahead=1):
    n = len(score_fns)
    pending = [fn() for fn in score_fns[:ahead]]
    outs = []
    for i in range(n):
        if i + ahead < n:
            pending.append(score_fns[i + ahead]())
        outs.append(finish_fns[i](pending.pop(0)))
    return outs


def _index32(i):
    return jnp.asarray(i, jnp.int32)


def _layer_spec(shape, layer):
    zeros = (0,) * len(shape)
    return pl.BlockSpec((None,) + shape, lambda *_: (layer,) + zeros, pipeline_mode=pl.Buffered(1))


def _whole_spec(shape):
    zeros = (0,) * len(shape)
    return pl.BlockSpec(shape, lambda *_: zeros, pipeline_mode=pl.Buffered(1))


def _mod_row(mod_ref, rows_per_mod):
    if rows_per_mod is None:
        return mod_ref[0:1]
    row = 1 + pl.program_id(0) // (rows_per_mod // ROW_TILE)
    return mod_ref[pl.ds(row, 1), :]


def _mod_body(c_ref, w_ref, b_ref, o_ref):
    cv = c_ref[...]
    s = cv * _sigmoid(cv)
    o_ref[0] = _bdot(s, w_ref[0]) + b_ref[0]


def _mod_call(cvec, w_mod, b_mod):
    bn = MOD_COLS // 4
    return pl.pallas_call(
        _mod_body,
        out_shape=jax.ShapeDtypeStruct((DEPTH, MOD_ROWS, MOD_COLS), F32),
        grid=(DEPTH, MOD_COLS // bn),
        in_specs=[
            pl.BlockSpec((MOD_ROWS, D_MODEL), lambda l, j: (0, 0)),
            pl.BlockSpec((1, D_MODEL, bn), lambda l, j: (l, 0, j)),
            pl.BlockSpec((1, 1, bn), lambda l, j: (l, 0, j)),
        ],
        out_specs=pl.BlockSpec((1, MOD_ROWS, bn), lambda l, j: (l, 0, j)),
        compiler_params=pltpu.CompilerParams(vmem_limit_bytes=VMEM_LIMIT),
        name="mod_vectors",
    )(cvec, w_mod, b_mod.reshape(DEPTH, 1, MOD_COLS))


def _cast_blocks(srcs, dsts):
    for src, dst in zip(srcs, dsts):
        dst[...] = src[...].astype(BF16)


def _cast_specs(weights, layer, steps):
    in_specs = [pl.BlockSpec((None, w.shape[1] // steps, w.shape[2]), lambda i: (layer, i, 0)) for w in weights]
    out_shape = [jax.ShapeDtypeStruct(w.shape[1:], BF16) for w in weights]
    out_specs = [pl.BlockSpec((w.shape[1] // steps, w.shape[2]), lambda i: (i, 0)) for w in weights]
    return in_specs, out_shape, out_specs


def _proj_in_body(x_ref, mod_ref, g_ref, w_ref, *rest, rows_per_mod, ncast):
    cast_src = rest[:ncast]
    za_ref, zc_ref = rest[ncast:ncast + 2]
    cast_dst = rest[ncast + 2:2 * ncast + 2]
    wb_s = rest[2 * ncast + 2]
    _cast_blocks(cast_src, cast_dst)

    @pl.when(pl.program_id(0) == 0)
    def _():
        wb_s[...] = w_ref[...].astype(BF16)

    mod = _mod_row(mod_ref, rows_per_mod)
    shift1 = mod[:, 0:D_MODEL]
    scale1 = mod[:, D_MODEL:2 * D_MODEL]
    h = (_rms(x_ref[...], g_ref[0:1]) * (1.0 + scale1) + shift1).astype(BF16)
    z = jnp.dot(h, wb_s[...], preferred_element_type=F32)
    za_ref[...] = z[:, 0:ATT_COLS]
    zc_ref[...] = z[:, ATT_COLS:IN_COLS]


def _proj_in_call(x, mods, gvec, w_in, layer, rows_per_mod, cast_weights=()):
    n = x.shape[0]
    cast_in, cast_shape, cast_out = _cast_specs(cast_weights, layer, n // ROW_TILE)
    return pl.pallas_call(
        functools.partial(_proj_in_body, rows_per_mod=rows_per_mod, ncast=len(cast_weights)),
        out_shape=(jax.ShapeDtypeStruct((n, ATT_COLS), F32), jax.ShapeDtypeStruct((n, RWKV_COLS), F32), *cast_shape),
        grid=(n // ROW_TILE,),
        in_specs=[
            pl.BlockSpec((ROW_TILE, D_MODEL), lambda i: (i, 0)),
            _layer_spec((MOD_ROWS, MOD_COLS), layer),
            _layer_spec((8, D_MODEL), layer),
            _layer_spec((D_MODEL, IN_COLS), layer),
            *cast_in,
        ],
        out_specs=(pl.BlockSpec((ROW_TILE, ATT_COLS), lambda i: (i, 0)),
                   pl.BlockSpec((ROW_TILE, RWKV_COLS), lambda i: (i, 0)), *cast_out),
        scratch_shapes=[pltpu.VMEM((D_MODEL, IN_COLS), BF16)],
        compiler_params=pltpu.CompilerParams(dimension_semantics=("arbitrary",), vmem_limit_bytes=VMEM_LIMIT),
        name="proj_in",
    )(x, mods, gvec, w_in, *cast_weights)


def _attn_prompt_body(*refs, layer):
    sink_ref, za_ref, gq_ref, gk_ref = refs[:4]
    if layer == 0:
        o_ref, kv_ref = refs[4:]
    else:
        prev_ref, o_ref = refs[4:6]
        cache_refs = refs[6:]
    aq = (za_ref[:, 0:A_DIM] * QK_SCALE).astype(BF16)
    ak = za_ref[:, 256:384]
    bq = _head_rms(za_ref[:, 512:896], gq_ref[...], _head_ones(B_DIM))
    bk = _head_rms(za_ref[:, 896:1024], gk_ref[...], _head_ones(KV_DIM))
    bqb = (bq * QK_SCALE).astype(BF16)
    akb = ak.astype(BF16)
    bkb = bk.astype(BF16)
    pieces_t = (ak.T, za_ref[:, 384:512].T, bk.T, za_ref[:, 1024:1152].T)
    avt = pieces_t[1].astype(BF16)
    bvt = pieces_t[3].astype(BF16)

    score_fns, finish_fns = [], []
    for h in range(A_HEADS):
        kv = h // (A_HEADS // A_KV_HEADS)
        hs = slice(kv * 64, (kv + 1) * 64)
        score_fns.append(lambda h=h, hs=hs: [_bdot_nt(akb[:, hs], aq[:, h * 64:(h + 1) * 64])])
        finish_fns.append(lambda sc, h=h, hs=hs: _softmax_pv_t(sc, [avt[hs]], sink_ref[layer, h]))
    for h in range(B_HEADS):
        kv = h // (B_HEADS // B_KV_HEADS)
        hs = slice(kv * 64, (kv + 1) * 64)
        score_fns.append(lambda h=h, hs=hs: [_bdot_nt(bkb[:, hs], bqb[:, h * 64:(h + 1) * 64])])
        finish_fns.append(lambda sc, hs=hs: _softmax_pv_t(sc, [bvt[hs]]))
    o_ref[...] = jnp.concatenate(_attend_heads(score_fns, finish_fns, ahead=len(score_fns)), axis=0).T

    for j, piece_t in enumerate(pieces_t):
        if layer == 0:
            kv_ref[0, j] = piece_t
        else:
            cache_refs[j][0, 0] = prev_ref[0, j]
            cache_refs[j][0, 1] = piece_t


def _attn_prompt_call(za, sink, gq, gk, layer, prev_kv):
    nb = za.shape[0] // SEQ
    in_specs = [
        pl.BlockSpec(memory_space=pltpu.SMEM),
        pl.BlockSpec((SEQ, ATT_COLS), lambda b: (b, 0)),
        _layer_spec((1, B_DIM), layer),
        _layer_spec((1, KV_DIM), layer),
    ]
    args = [sink, za, gq, gk]
    out_shape = [jax.ShapeDtypeStruct((nb * SEQ, ATT_OUT), F32)]
    out_specs = [pl.BlockSpec((SEQ, ATT_OUT), lambda b: (b, 0))]
    slab_spec = pl.BlockSpec((1, N_CACHE, KV_DIM, SEQ), lambda b: (b, 0, 0, 0))
    if layer == 0:
        out_shape.append(jax.ShapeDtypeStruct((nb, N_CACHE, KV_DIM, SEQ), F32))
        out_specs.append(slab_spec)
    else:
        in_specs.append(slab_spec)
        args.append(prev_kv)
        for _ in range(N_CACHE):
            out_shape.append(jax.ShapeDtypeStruct((nb, DEPTH, KV_DIM, SEQ), F32))
            out_specs.append(pl.BlockSpec((1, DEPTH, KV_DIM, SEQ), lambda b: (b, 0, 0, 0)))
    return pl.pallas_call(
        functools.partial(_attn_prompt_body, layer=layer),
        out_shape=tuple(out_shape),
        grid=(nb,),
        in_specs=in_specs,
        out_specs=tuple(out_specs),
        compiler_params=pltpu.CompilerParams(vmem_limit_bytes=VMEM_LIMIT),
        name="attn_prompt",
    )(*args)


def _attn_sample_body(sink_ref, za_ref, cak_ref, cav_ref, cbk_ref, cbv_ref, cos_ref, sin_ref, gq_ref, gk_ref,
                      o_ref, qa_s, ka_s, va_s, qb_s, kb_s, vb_s, *, layer):
    t = DEC_SEQ
    cos = cos_ref[...]
    sin = sin_ref[...]
    qa_s[...] = (_rope(za_ref[:, 0:A_DIM], cos, sin) * QK_SCALE).astype(BF16)
    zpad = jnp.zeros((Q_BLK, KV_DIM), BF16)
    ka_s[0:Q_BLK] = zpad
    ka_s[Q_BLK + t:] = zpad
    va_s[0:Q_BLK] = zpad
    va_s[Q_BLK + t:] = zpad
    ka_s[Q_BLK:Q_BLK + t] = _rope(za_ref[:, 256:384], cos, sin).astype(BF16)
    va_s[Q_BLK:Q_BLK + t] = za_ref[:, 384:512].astype(BF16)
    bq = _head_rms(za_ref[:, 512:896], gq_ref[...], _head_ones(B_DIM))
    qb_s[...] = (_rope(bq, cos, sin) * QK_SCALE).astype(BF16)
    bk = _head_rms(za_ref[:, 896:1024], gk_ref[...], _head_ones(KV_DIM))
    kb_s[0:PAST_LEN] = cbk_ref[0].T.astype(BF16)
    kb_s[PAST_LEN:] = _rope(bk, cos, sin).astype(BF16)
    vb_s[0:PAST_LEN] = cbv_ref[0].T.astype(BF16)
    vb_s[PAST_LEN:] = za_ref[:, 1024:1152].astype(BF16)

    kca = cak_ref[0].T.astype(BF16)
    vca = cav_ref[0].T.astype(BF16)

    ga = A_HEADS // A_KV_HEADS
    gb = B_HEADS // B_KV_HEADS
    qpos = lax.broadcasted_iota(jnp.int32, (ga * Q_BLK, 3 * Q_BLK), 0) & (Q_BLK - 1)
    kpos = lax.broadcasted_iota(jnp.int32, (ga * Q_BLK, 3 * Q_BLK), 1) - Q_BLK
    near = jnp.abs(kpos - qpos) <= WINDOW
    head_of_row = lax.shift_right_logical(lax.broadcasted_iota(jnp.int32, (ga * Q_BLK, 1), 0),
                                          jnp.int32(Q_BLK.bit_length() - 1))
    sinks = []
    for kv in range(A_KV_HEADS):
        col = jnp.full((ga * Q_BLK, 1), sink_ref[layer, kv * ga], F32)
        for g in range(1, ga):
            col = jnp.where(head_of_row == g, sink_ref[layer, kv * ga + g], col)
        sinks.append(col)

    def stack_heads(q, kv, group):
        return jnp.concatenate([q[:, (kv * group + g) * 64:(kv * group + g + 1) * 64] for g in range(group)], axis=0)

    def unstack_heads(outs, group, rows):
        return jnp.concatenate([o[g * rows:(g + 1) * rows] for o in outs for g in range(group)], axis=1)

    def a_block(n, carry):
        r0 = pl.multiple_of(_index32(n) * Q_BLK, Q_BLK)
        q = qa_s[pl.ds(r0, Q_BLK), :]
        kl = ka_s[pl.ds(r0, 3 * Q_BLK), :]
        vl = va_s[pl.ds(r0, 3 * Q_BLK), :]
        kabs = kpos + r0
        mask = near & (kabs >= 0) & (kabs < t)
        score_fns, finish_fns = [], []
        for kv in range(A_KV_HEADS):
            hs = slice(kv * 64, (kv + 1) * 64)

            def scores(kv=kv, hs=hs):
                qs = stack_heads(q, kv, ga)
                return [jnp.where(mask, _bdot_nt(qs, kl[:, hs]), NEG_INF), _bdot_nt(qs, kca[:, hs])]

            score_fns.append(scores)
            finish_fns.append(lambda sc, kv=kv, hs=hs: _softmax_pv(sc, [vl[:, hs], vca[:, hs]], sinks[kv]))
        outs = _attend_heads(score_fns, finish_fns, ahead=A_KV_HEADS)
        o_ref[pl.ds(r0, Q_BLK), 0:A_DIM] = unstack_heads(outs, ga, Q_BLK)
        return carry

    lax.fori_loop(0, t // Q_BLK, a_block, 0)

    def b_block(n, carry):
        r0 = pl.multiple_of(_index32(n) * B_QROWS, B_QROWS)
        q = qb_s[pl.ds(r0, B_QROWS), :]
        score_fns, finish_fns = [], []
        for kv in range(B_KV_HEADS):
            hs = slice(kv * 64, (kv + 1) * 64)
            score_fns.append(lambda kv=kv, hs=hs: [_bdot_nt(stack_heads(q, kv, gb), kb_s[:, hs])])
            finish_fns.append(lambda sc, hs=hs: _softmax_pv(sc, [vb_s[:, hs]]))
        outs = _attend_heads(score_fns, finish_fns)
        o_ref[pl.ds(r0, B_QROWS), A_DIM:ATT_OUT] = unstack_heads(outs, gb, B_QROWS)
        return carry

    lax.fori_loop(0, t // B_QROWS, b_block, 0)


def _attn_sample_call(za, sink, caches, cos, sin, gq, gk, layer):
    nb = za.shape[0] // DEC_SEQ
    t = DEC_SEQ
    cache_spec = pl.BlockSpec((1, None, KV_DIM, PAST_LEN), lambda b: (b, layer, 0, 0))
    return pl.pallas_call(
        functools.partial(_attn_sample_body, layer=layer),
        out_shape=jax.ShapeDtypeStruct((nb * t, ATT_OUT), F32),
        grid=(nb,),
        in_specs=[
            pl.BlockSpec(memory_space=pltpu.SMEM),
            pl.BlockSpec((t, ATT_COLS), lambda b: (b, 0)),
            cache_spec, cache_spec, cache_spec, cache_spec,
            pl.BlockSpec((t, 128), lambda b: (0, 0)),
            pl.BlockSpec((t, 128), lambda b: (0, 0)),
            _layer_spec((1, B_DIM), layer),
            _layer_spec((1, KV_DIM), layer),
        ],
        out_specs=pl.BlockSpec((t, ATT_OUT), lambda b: (b, 0)),
        scratch_shapes=[
            pltpu.VMEM((t, A_DIM), BF16),
            pltpu.VMEM((t + 2 * Q_BLK, KV_DIM), BF16),
            pltpu.VMEM((t + 2 * Q_BLK, KV_DIM), BF16),
            pltpu.VMEM((t, B_DIM), BF16),
            pltpu.VMEM((t + PAST_LEN, KV_DIM), BF16),
            pltpu.VMEM((t + PAST_LEN, KV_DIM), BF16),
        ],
        compiler_params=pltpu.CompilerParams(vmem_limit_bytes=VMEM_LIMIT),
        name="attn_sample",
    )(sink, za, *caches, cos, sin, gq, gk)


def _rwkv_body(*refs, t, nbatch, has_init, want_state, has_prev, ncast):
    refs = list(refs)
    zc_ref, cv_ref, wup_ref, aup_ref, gup_ref = refs[:5]
    pos = 5
    st0_ref = prev_ref = stout_ref = None
    if has_init:
        st0_ref = refs[pos]
        pos += 1
    if has_prev:
        prev_ref = refs[pos]
        pos += 1
    cast_src = refs[pos:pos + ncast]
    pos += ncast
    oc_ref = refs[pos]
    pos += 1
    if want_state:
        stout_ref = refs[pos]
        pos += 1
    cast_dst = refs[pos:pos + ncast]
    pos += ncast
    aa_s, ld_s, b_s, kt_s, bon_s, y_s, st_s = refs[pos:]

    _cast_blocks(cast_src, cast_dst)

    nchunks = t // CHUNK
    ones = _head_ones(C_DIM)
    k_k = cv_ref[0:1]
    k_a = cv_ref[1:2]
    r_k = cv_ref[2:3]
    ln_w = cv_ref[3:4]
    ln_b = cv_ref[4:5]
    w0h = (0.5 * cv_ref[5:6], 0.5 * cv_ref[6:7])
    a0h = (0.5 * cv_ref[7:8], 0.5 * cv_ref[8:9])
    wup_h = ((0.5 * wup_ref[0]).astype(BF16), (0.5 * wup_ref[1]).astype(BF16))
    aup_h = ((0.5 * aup_ref[0]).astype(BF16), (0.5 * aup_ref[1]).astype(BF16))
    c1 = 1.0 - 0.5 * k_a
    c2 = 0.5 * k_a
    half_rate = 0.5 * DECAY_RATE

    for rb in range(nbatch * t // RWKV_ROWS):
        rows = slice(rb * RWKV_ROWS, (rb + 1) * RWKV_ROWS)
        zr = zc_ref[rows, 0:384]
        zk = zc_ref[rows, 384:768]
        zv = zc_ref[rows, 768:1152]
        tw = jnp.tanh(zc_ref[rows, 1152:1216])
        xa = zc_ref[rows, 1216:1280]
        kkr = zk * k_k
        kk_half = kkr * (0.5 * lax.rsqrt(_head_sum(kkr * kkr, ones) + 1e-12))
        aa_s[rows] = -2.0 * kk_half
        kt_sum = None
        for d in range(2):
            th_w = jnp.tanh(w0h[d] + _bdot(tw, wup_h[d]))
            ld_s[d, rows] = th_w * (-half_rate) - half_rate
            th_a = jnp.tanh(a0h[d] + _bdot(xa, aup_h[d]))
            kt = zk * (c1 + c2 * th_a)
            kt_s[d, rows] = kt
            b_s[d, rows] = kk_half * th_a + kk_half
            kt_sum = kt if kt_sum is None else kt_sum + kt
        bon_s[rows] = _head_sum(zr * kt_sum * r_k, ones) * zv
        y_s[rows] = jnp.zeros((RWKV_ROWS, C_DIM), F32)

    ir = lax.broadcasted_iota(jnp.int32, (CHUNK, 128), 0)
    ic = lax.broadcasted_iota(jnp.int32, (CHUNK, 128), 1) & 63
    eye = jnp.where(ir == ic, 1.0, 0.0).astype(F32)
    incl = (ic <= ir, ic >= ir)
    strict = (ic < ir, ic > ir)
    sr = lax.broadcasted_iota(jnp.int32, (CHUNK, CHUNK), 0)
    sc = lax.broadcasted_iota(jnp.int32, (CHUNK, CHUNK), 1)
    tri = tuple(jnp.where(m, 1.0, 0.0).astype(BF16) for m in (sc <= sr, sc >= sr))
    br = lax.broadcasted_iota(jnp.int32, (128, 128), 0)
    bc = lax.broadcasted_iota(jnp.int32, (128, 128), 1)
    diag_blocks = (br < 64) == (bc < 64)
    left_half = lax.broadcasted_iota(jnp.int32, (CHUNK, 128), 1) < 64

    def level_mask(rows, cols, j):
        same_big = lax.shift_right_logical(rows, jnp.int32(j + 1)) == lax.shift_right_logical(cols, jnp.int32(j + 1))
        same_small = lax.shift_right_logical(rows, jnp.int32(j)) == lax.shift_right_logical(cols, jnp.int32(j))
        return same_big & jnp.logical_not(same_small)

    first_level = level_mask(ir, ic, 0)
    level_blocks = [diag_blocks & level_mask(br & 63, bc & 63, j) for j in range(1, CHUNK.bit_length() - 1)]

    def blockdiag(x2):
        x2 = x2.astype(BF16)
        return jnp.where(diag_blocks, jnp.concatenate([x2, x2], axis=0), jnp.zeros((128, 128), BF16))

    npairs = C_HEADS // 2
    for bi in range(nbatch):
        for d in range(2):
            for p in range(npairs):
                if has_init:
                    st_s[bi, d, p] = jnp.concatenate([st0_ref[bi, d, 2 * p], st0_ref[bi, d, 2 * p + 1]], axis=1)
                else:
                    st_s[bi, d, p] = jnp.zeros((HEAD_DIM, 128), F32)

    def chunk(c, carry):
        c = _index32(c)
        items = []
        for bi, d in [(bi, d) for bi in range(nbatch) for d in range(2)]:
            r0 = bi * t + ((nchunks - 1 - c) if d == 1 else c) * CHUNK
            rows = pl.ds(pl.multiple_of(r0, CHUNK), CHUNK)
            ld = ld_s[d, rows]
            cinc = _split_dot_left(tri[d], ld)
            cexc = cinc - ld
            ctot = cinc[0:1] if d == 1 else cinc[CHUNK - 1:CHUNK]
            e_ninc = jnp.exp(-cinc)
            e_rem = jnp.exp(ctot - cinc)
            e_tot = jnp.exp(ctot)
            bb = b_s[d, rows]
            kt = kt_s[d, rows]
            at = (aa_s[rows] * jnp.exp(cexc)).astype(BF16)
            rt = (zc_ref[rows, 0:384] * jnp.exp(cinc)).astype(BF16)
            bt = (bb * e_ninc).astype(BF16)
            ktt = (kt * e_ninc).astype(BF16)
            bh = (bb * e_rem).astype(BF16)
            kh = (kt * e_rem).astype(BF16)
            vm = zc_ref[rows, 768:1152].astype(BF16)
            for p in range(npairs):
                ps = slice(p * 128, (p + 1) * 128)
                items.append(dict(bi=bi, d=d, p=p, rows=rows, e_tot=e_tot[:, ps], at=at[:, ps], rt=rt[:, ps],
                                  bt=bt[:, ps], kt=ktt[:, ps], bh=bh[:, ps], kh=kh[:, ps], vm=vm[:, ps]))
        for it in items:
            it["ar"] = jnp.concatenate([it["at"], it["rt"]], axis=0)
            it["vmd"] = blockdiag(it["vm"])
        for it in items:
            sbk = _bdot_nt(it["ar"], jnp.concatenate([blockdiag(it["bt"]), blockdiag(it["kt"])], axis=0))
            it["sb"] = sbk[:, 0:128]
            it["sk"] = sbk[:, 128:256]
        for it in items:
            d = it["d"]
            it["l"] = jnp.where(strict[d], it["sb"][0:CHUNK], 0.0)
            it["mak"] = jnp.where(strict[d], it["sk"][0:CHUNK], 0.0).astype(BF16)
            it["nr"] = jnp.concatenate([jnp.where(incl[d], it["sb"][CHUNK:], 0.0).astype(BF16),
                                        jnp.where(incl[d], it["sk"][CHUNK:], 0.0).astype(BF16)], axis=1)
            it["tm"] = eye + jnp.where(first_level, it["l"], 0.0)
            lb = it["l"].astype(BF16)
            it["l2"] = jnp.concatenate([lb, lb], axis=0)
        for lvl in level_blocks:
            for it in items:
                it["t1"] = _bdot(it["tm"], jnp.where(lvl, it["l2"], jnp.zeros((128, 128), BF16)))
            for it in items:
                it["tm"] = it["tm"] + _bdot(it["t1"], blockdiag(it["tm"]))
        for it in items:
            it["mv"] = _bdot(it["mak"], it["vmd"])
        for it in items:
            it["tau"] = _bdot(it["tm"], jnp.concatenate([blockdiag(it["at"]), blockdiag(it["mv"])], axis=1))
        for it in items:
            it["s0"] = st_s[it["bi"], it["d"], it["p"]]
            it["x"] = _bdot_nt(jnp.concatenate([it["tau"][:, 0:128].astype(BF16), it["rt"]], axis=0),
                               blockdiag(it["s0"]))
        for it in items:
            it["u"] = (it["x"][0:CHUNK] + it["tau"][:, 128:256]).astype(BF16)
        for it in items:
            it["y"] = it["x"][CHUNK:] + _bdot(it["nr"], jnp.concatenate([blockdiag(it["u"]), it["vmd"]], axis=0))
        for it in items:
            full = _bdot_tn(jnp.concatenate([it["u"], it["vm"]], axis=0),
                            jnp.concatenate([it["bh"], it["kh"]], axis=0))
            st_s[it["bi"], it["d"], it["p"]] = (it["s0"] * it["e_tot"]
                                                + jnp.where(left_half, full[0:64], full[64:128]))
        for i in range(0, len(items), npairs):
            rows = items[i]["rows"]
            y_s[rows] = y_s[rows] + jnp.concatenate([it["y"] for it in items[i:i + npairs]], axis=1)
        return carry

    lax.fori_loop(0, nchunks, chunk, 0)
    if want_state:
        for bi in range(nbatch):
            if has_prev:
                stout_ref[bi, 0] = prev_ref[bi]
            for d in range(2):
                for p in range(npairs):
                    s2 = st_s[bi, d, p]
                    for q in range(2):
                        if has_prev:
                            stout_ref[bi, 1, d, 2 * p + q] = s2[:, q * 64:(q + 1) * 64]
                        else:
                            stout_ref[bi, d, 2 * p + q] = s2[:, q * 64:(q + 1) * 64]

    gup = gup_ref[...].astype(BF16)
    for rb in range(nbatch * t // RWKV_ROWS):
        rows = slice(rb * RWKV_ROWS, (rb + 1) * RWKV_ROWS)
        y = y_s[rows]
        mu = _head_sum(y, ones) * (1.0 / HEAD_DIM)
        yc = y - mu
        var = _head_sum(yc * yc, ones) * (1.0 / HEAD_DIM)
        yn = yc * lax.rsqrt(var + GN_EPS)
        g = _bdot(_sigmoid(zc_ref[rows, 1280:1408]), gup)
        oc_ref[rows] = (yn * ln_w + ln_b + bon_s[rows]) * g


def _rwkv_call(zc, rvec, wup, aup, gup, layer, t, init_state, want_state, prev_state, cast_weights=()):
    nb = zc.shape[0] // t
    nbatch = min(RWKV_BATCH, nb)
    rows = nbatch * t
    has_init = init_state is not None
    has_prev = prev_state is not None
    st_shape = (2, C_HEADS, HEAD_DIM, HEAD_DIM)
    in_specs = [
        pl.BlockSpec((rows, RWKV_COLS), lambda b: (b, 0), pipeline_mode=pl.Buffered(1 if nb == nbatch else 2)),
        _layer_spec((16, C_DIM), layer),
        _layer_spec((2, W_RANK, C_DIM), layer),
        _layer_spec((2, A_RANK, C_DIM), layer),
        _layer_spec((G_RANK, C_DIM), layer),
    ]
    args = [zc, rvec, wup, aup, gup]
    if has_init:
        in_specs.append(pl.BlockSpec((nbatch, None) + st_shape, lambda b: (b, layer, 0, 0, 0, 0)))
        args.append(init_state)
    if has_prev:
        in_specs.append(pl.BlockSpec((nbatch,) + st_shape, lambda b: (b, 0, 0, 0, 0)))
        args.append(prev_state)
    steps = nb // nbatch
    cast_in, cast_shape, cast_out = _cast_specs(cast_weights, layer, steps)
    in_specs.extend(cast_in)
    args.extend(cast_weights)
    out_shape = [jax.ShapeDtypeStruct((nb * t, C_DIM), F32)]
    out_specs = [pl.BlockSpec((rows, C_DIM), lambda b: (b, 0))]
    if want_state and has_prev:
        out_shape.append(jax.ShapeDtypeStruct((nb, DEPTH) + st_shape, F32))
        out_specs.append(pl.BlockSpec((nbatch, DEPTH) + st_shape, lambda b: (b, 0, 0, 0, 0, 0)))
    elif want_state:
        out_shape.append(jax.ShapeDtypeStruct((nb,) + st_shape, F32))
        out_specs.append(pl.BlockSpec((nbatch,) + st_shape, lambda b: (b, 0, 0, 0, 0)))
    out_shape.extend(cast_shape)
    out_specs.extend(cast_out)
    return pl.pallas_call(
        functools.partial(_rwkv_body, t=t, nbatch=nbatch, has_init=has_init, want_state=want_state,
                          has_prev=has_prev, ncast=len(cast_weights)),
        out_shape=tuple(out_shape),
        grid=(steps,),
        in_specs=in_specs,
        out_specs=tuple(out_specs),
        scratch_shapes=[
            pltpu.VMEM((rows, C_DIM), F32),
            pltpu.VMEM((2, rows, C_DIM), F32),
            pltpu.VMEM((2, rows, C_DIM), F32),
            pltpu.VMEM((2, rows, C_DIM), F32),
            pltpu.VMEM((rows, C_DIM), F32),
            pltpu.VMEM((rows, C_DIM), F32),
            pltpu.VMEM((nbatch, 2, C_HEADS // 2, HEAD_DIM, 128), F32),
        ],
        compiler_params=pltpu.CompilerParams(vmem_limit_bytes=VMEM_LIMIT),
        name="rwkv_prompt" if want_state else "rwkv_sample",
    )(*args)


def _post_body(oa_ref, oc_ref, x_ref, mod_ref, g_ref, wout_ref, wgu_ref, wdn_ref, xo_ref, *, rows_per_mod):
    mod = _mod_row(mod_ref, rows_per_mod)
    gate1 = mod[:, 2 * D_MODEL:3 * D_MODEL]
    shift2 = mod[:, 3 * D_MODEL:4 * D_MODEL]
    scale2 = mod[:, 4 * D_MODEL:5 * D_MODEL]
    gate2 = mod[:, 5 * D_MODEL:6 * D_MODEL]
    sub = ROW_TILE // ROW_SPLIT
    parts = [slice(i * sub, (i + 1) * sub) for i in range(ROW_SPLIT)]
    o = [jnp.dot(jnp.concatenate([oa_ref[r, :].astype(BF16), oc_ref[r, :].astype(BF16)], axis=1), wout_ref[...],
                 preferred_element_type=F32) for r in parts]
    x1 = [x_ref[r, :] + gate1 * _rms(oi, g_ref[1:2]) for r, oi in zip(parts, o)]
    h2 = [(_rms(xi, g_ref[2:3]) * (1.0 + scale2) + shift2).astype(BF16) for xi in x1]

    def gate_up(step):
        (lo, hi), i = step
        g = jnp.dot(h2[i], wgu_ref[:, lo:hi], preferred_element_type=F32)
        u = jnp.dot(h2[i], wgu_ref[:, FF_DIM + lo:FF_DIM + hi], preferred_element_type=F32)
        return g, u

    steps = [(chunk, i) for chunk in FF_CHUNKS for i in range(ROW_SPLIT)]
    acc = [None] * ROW_SPLIT
    pending = gate_up(steps[0])
    for k, ((lo, hi), i) in enumerate(steps):
        g, u = pending
        if k + 1 < len(steps):
            pending = gate_up(steps[k + 1])
        act = (g * _sigmoid(g) * u).astype(BF16)
        part = jnp.dot(act, wdn_ref[lo:hi, :], preferred_element_type=F32)
        acc[i] = part if acc[i] is None else acc[i] + part
    for r, xi, ai in zip(parts, x1, acc):
        xo_ref[r, :] = xi + gate2 * _rms(ai, g_ref[3:4])


def _post_call(oatt, oc, x, mods, gvec, wout_b, wgu_b, wdn_b, layer, rows_per_mod):
    n = x.shape[0]
    return pl.pallas_call(
        functools.partial(_post_body, rows_per_mod=rows_per_mod),
        out_shape=jax.ShapeDtypeStruct((n, D_MODEL), F32),
        grid=(n // ROW_TILE,),
        in_specs=[
            pl.BlockSpec((ROW_TILE, ATT_OUT), lambda i: (i, 0)),
            pl.BlockSpec((ROW_TILE, C_DIM), lambda i: (i, 0)),
            pl.BlockSpec((ROW_TILE, D_MODEL), lambda i: (i, 0)),
            _layer_spec((MOD_ROWS, MOD_COLS), layer),
            _layer_spec((8, D_MODEL), layer),
            _whole_spec((MIX_DIM, D_MODEL)),
            _whole_spec((D_MODEL, 2 * FF_DIM)),
            _whole_spec((FF_DIM, D_MODEL)),
        ],
        out_specs=pl.BlockSpec((ROW_TILE, D_MODEL), lambda i: (i, 0)),
        compiler_params=pltpu.CompilerParams(vmem_limit_bytes=VMEM_LIMIT),
        name="post_ffn",
    )(oatt, oc, x, mods, gvec, wout_b, wgu_b, wdn_b)


def _rope_tables(t):
    pos = jnp.arange(t)
    row = (pos // GRID_W).astype(F32)
    col = (pos % GRID_W).astype(F32)
    freqs = ROPE_THETA ** (-jnp.arange(ROPE_PAIRS_AXIS, dtype=F32) / ROPE_PAIRS_AXIS)
    ang = jnp.concatenate([row[:, None] * freqs, col[:, None] * freqs], axis=-1)
    cos = jnp.tile(jnp.cos(ang), (1, 4))
    sin = jnp.sin(ang)
    sin = jnp.tile(jnp.concatenate([-sin, sin], axis=-1), (1, 2))
    return cos, sin


def _to_time_minor(cache):
    b, l, t = cache.shape[:3]
    return jnp.transpose(cache, (0, 1, 3, 4, 2)).reshape(b, l, KV_DIM, t)


def _from_time_minor(slab):
    b, l, _, t = slab.shape
    return jnp.transpose(slab.reshape(b, l, 2, HEAD_DIM, t), (0, 1, 4, 2, 3))


def kernel(x_prompt, x_sample, cache_a_k, cache_a_v, cache_b_k, cache_b_v, state_c, c, c_ctx, w_mod, b_mod, norm_mix_pre, norm_mix_post, norm_ffn_pre, norm_ffn_post, w_in, w_out, a_sink, b_q_norm, b_k_norm, c_w0, c_w_up, c_a0, c_a_up, c_g_up, c_k_k, c_k_a, c_r_k, c_ln_w, c_ln_b, w_gu, w_down):
    cvec = jnp.zeros((MOD_ROWS, D_MODEL), F32).at[0].set(c_ctx).at[1:1 + DEC_BATCH].set(c)
    mods = _mod_call(cvec, w_mod, b_mod)
    cos, sin = _rope_tables(DEC_SEQ)

    gvec = jnp.zeros((DEPTH, 8, D_MODEL), F32)
    gvec = gvec.at[:, 0].set(norm_mix_pre).at[:, 1].set(norm_mix_post)
    gvec = gvec.at[:, 2].set(norm_ffn_pre).at[:, 3].set(norm_ffn_post)
    rvec = jnp.zeros((DEPTH, 16, C_DIM), F32)
    rvec = rvec.at[:, 0].set(c_k_k).at[:, 1].set(c_k_a).at[:, 2].set(c_r_k.reshape(DEPTH, C_DIM))
    rvec = rvec.at[:, 3].set(c_ln_w).at[:, 4].set(c_ln_b)
    rvec = rvec.at[:, 5:7].set(c_w0).at[:, 7:9].set(c_a0)
    gq = jnp.tile(b_q_norm, (1, B_HEADS))[:, None, :]
    gk = jnp.tile(b_k_norm, (1, B_KV_HEADS))[:, None, :]
    caches = tuple(_to_time_minor(t) for t in (cache_a_k, cache_a_v, cache_b_k, cache_b_v))

    xp = x_prompt.reshape(BATCH * SEQ, D_MODEL)
    xs = x_sample.reshape(DEC_BATCH * DEC_SEQ, D_MODEL)
    kv_prev = st_prev = None
    for l in range(DEPTH):
        za, zc, wout_b, wdn_b = _proj_in_call(xp, mods, gvec, w_in, l, None, cast_weights=(w_out, w_down))
        oatt, *kv_prev = _attn_prompt_call(za, a_sink, gq, gk, l, kv_prev)
        if l == 0:
            (kv_prev,) = kv_prev
        oc, st_prev, wgu_b = _rwkv_call(zc, rvec, c_w_up, c_a_up, c_g_up, l, SEQ, None, True, st_prev,
                                        cast_weights=(w_gu,))
        xp = _post_call(oatt, oc, xp, mods, gvec, wout_b, wgu_b, wdn_b, l, None)

        za, zc = _proj_in_call(xs, mods, gvec, w_in, l, DEC_SEQ)
        oatt = _attn_sample_call(za, a_sink, caches, cos, sin, gq, gk, l)
        (oc,) = _rwkv_call(zc, rvec, c_w_up, c_a_up, c_g_up, l, DEC_SEQ, state_c, False, None)
        xs = _post_call(oatt, oc, xs, mods, gvec, wout_b, wgu_b, wdn_b, l, DEC_SEQ)

    new_caches = tuple(_from_time_minor(slab) for slab in kv_prev)
    return (xp.reshape(BATCH, SEQ, D_MODEL), xs.reshape(DEC_BATCH, DEC_SEQ, D_MODEL), *new_caches, st_prev)
```

```python
import functools
import math

import jax
import jax.numpy as jnp
from jax import lax
from jax.experimental import pallas as pl
from jax.experimental.pallas import tpu as pltpu

D_MODEL = 1024
BATCH = 32
SEQ = 256
DEPTH = 2
DEC_BATCH = 2
DEC_SEQ = 1024
PAST_LEN = 256
GRID_W = 64
HEAD_DIM = 64
A_HEADS = 4
A_KV_HEADS = 2
B_HEADS = 6
B_KV_HEADS = 2
C_HEADS = 6
A_DIM = A_HEADS * HEAD_DIM
B_DIM = B_HEADS * HEAD_DIM
C_DIM = C_HEADS * HEAD_DIM
MIX_DIM = A_DIM + B_DIM + C_DIM
WINDOW = 128
Q_BLK = 128
W_RANK = 64
A_RANK = 64
G_RANK = 128
FF_DIM = -(-8 * D_MODEL // (3 * 256)) * 256
ROPE_THETA = 10000.0
ROPE_PAIRS_AXIS = HEAD_DIM // 4
NORM_EPS = 1e-6
GN_EPS = 64e-5
NEG_INF = -1e30

KV_DIM = 2 * HEAD_DIM
ATT_COLS = A_DIM + 2 * KV_DIM + B_DIM + 2 * KV_DIM
RWKV_COLS = 3 * C_DIM + W_RANK + A_RANK + G_RANK
IN_COLS = ATT_COLS + RWKV_COLS
ATT_OUT = A_DIM + B_DIM
MOD_COLS = 6 * D_MODEL
MOD_ROWS = 8
N_CACHE = 4
CHUNK = 64
QK_SCALE = HEAD_DIM ** -0.5
DECAY_RATE = math.exp(-0.5)
ROW_TILE = 512
ROW_SPLIT = 2
RWKV_ROWS = 512
RWKV_BATCH = 4
B_QROWS = 256
MXU_TILE = 256
FF_CHUNKS = ((0, 4 * MXU_TILE), (4 * MXU_TILE, 8 * MXU_TILE), (8 * MXU_TILE, FF_DIM))
assert FF_DIM % MXU_TILE == 0
VMEM_LIMIT = 56 * 1024 * 1024

F32 = jnp.float32
BF16 = jnp.bfloat16

assert DEPTH == 2


def _bdot(a, b):
    return jnp.dot(a.astype(BF16), b.astype(BF16), preferred_element_type=F32)


def _bdot_nt(a, b):
    return lax.dot_general(a.astype(BF16), b.astype(BF16), (((1,), (1,)), ((), ())), preferred_element_type=F32)


def _bdot_tn(a, b):
    return lax.dot_general(a.astype(BF16), b.astype(BF16), (((0,), (0,)), ((), ())), preferred_element_type=F32)


def _split(x):
    hi = x.astype(BF16)
    lo = (x - hi.astype(F32)).astype(BF16)
    return hi, lo


def _split_dot_left(m, x):
    hi, lo = _split(x)
    return jnp.dot(m, hi, preferred_element_type=F32) + jnp.dot(m, lo, preferred_element_type=F32)


def _head_sum(x, ones):
    return jnp.dot(x.astype(BF16), ones, preferred_element_type=F32)


def _sigmoid(x):
    return 0.5 * jnp.tanh(0.5 * x) + 0.5


def _head_ones(n):
    r = lax.shift_right_logical(lax.broadcasted_iota(jnp.int32, (n, n), 0), jnp.int32(6))
    c = lax.shift_right_logical(lax.broadcasted_iota(jnp.int32, (n, n), 1), jnp.int32(6))
    return jnp.where(r == c, 1.0, 0.0).astype(BF16)


def _rms(x, g):
    ms = jnp.mean(x * x, axis=-1, keepdims=True)
    return x * lax.rsqrt(ms + NORM_EPS) * g


def _head_rms(x, g, ones):
    ms = _head_sum(x * x, ones) * (1.0 / HEAD_DIM)
    return x * lax.rsqrt(ms + NORM_EPS) * g


def _rope(x, cos, sin):
    t = x.shape[0]
    lane = lax.broadcasted_iota(jnp.int32, (t, 128), 1)
    first = (lane & 32) == 0
    outs = []
    for i in range(x.shape[1] // 128):
        xs = x[:, i * 128:(i + 1) * 128]
        swapped = jnp.where(first, pltpu.roll(xs, 96, axis=1), pltpu.roll(xs, 32, axis=1))
        outs.append(xs * cos + swapped * sin)
    return outs[0] if len(outs) == 1 else jnp.concatenate(outs, axis=1)


def _softmax_pv(scores, values, sink=None):
    m = scores[0].max(axis=-1, keepdims=True)
    for s in scores[1:]:
        m = jnp.maximum(m, s.max(axis=-1, keepdims=True))
    if sink is not None:
        m = jnp.maximum(m, sink)
    denom = None
    acc = None
    for s, v in zip(scores, values):
        p = jnp.exp(s - m)
        d = p.sum(axis=-1, keepdims=True)
        o = jnp.dot(p.astype(BF16), v, preferred_element_type=F32)
        denom = d if denom is None else denom + d
        acc = o if acc is None else acc + o
    if sink is not None:
        denom = denom + jnp.exp(sink - m)
    return acc / denom


def _softmax_pv_t(scores_t, values_t, sink=None):
    m = scores_t[0].max(axis=0, keepdims=True)
    for s in scores_t[1:]:
        m = jnp.maximum(m, s.max(axis=0, keepdims=True))
    if sink is not None:
        m = jnp.maximum(m, sink)
    denom = None
    acc = None
    for s, vt in zip(scores_t, values_t):
        p = jnp.exp(s - m)
        d = p.sum(axis=0, keepdims=True)
        o = jnp.dot(vt, p.astype(BF16), preferred_element_type=F32)
        denom = d if denom is None else denom + d
        acc = o if acc is None else acc + o
    if sink is not None:
        denom = denom + jnp.exp(sink - m)
    return acc / denom


def _attend_heads(score_fns, finish_fns, ahead=1):
    n = len(score_fns)
    pending = [fn() for fn in score_fns[:ahead]]
    outs = []
    for i in range(n):
        if i + ahead < n:
            pending.append(score_fns[i + ahead]())
        outs.append(finish_fns[i](pending.pop(0)))
    return outs


def _index32(i):
    return jnp.asarray(i, jnp.int32)


def _layer_spec(shape, layer):
    zeros = (0,) * len(shape)
    return pl.BlockSpec((None,) + shape, lambda *_: (layer,) + zeros, pipeline_mode=pl.Buffered(1))


def _whole_spec(shape):
    zeros = (0,) * len(shape)
    return pl.BlockSpec(shape, lambda *_: zeros, pipeline_mode=pl.Buffered(1))


def _mod_row(mod_ref, rows_per_mod):
    if rows_per_mod is None:
        return mod_ref[0:1]
    row = 1 + pl.program_id(0) // (rows_per_mod // ROW_TILE)
    return mod_ref[pl.ds(row, 1), :]


def _mod_body(c_ref, w_ref, b_ref, o_ref):
    cv = c_ref[...]
    s = cv * _sigmoid(cv)
    o_ref[0] = _bdot(s, w_ref[0]) + b_ref[0]


def _mod_call(cvec, w_mod, b_mod):
    bn = MOD_COLS // 4
    return pl.pallas_call(
        _mod_body,
        out_shape=jax.ShapeDtypeStruct((DEPTH, MOD_ROWS, MOD_COLS), F32),
        grid=(DEPTH, MOD_COLS // bn),
        in_specs=[
            pl.BlockSpec((MOD_ROWS, D_MODEL), lambda l, j: (0, 0)),
            pl.BlockSpec((1, D_MODEL, bn), lambda l, j: (l, 0, j)),
            pl.BlockSpec((1, 1, bn), lambda l, j: (l, 0, j)),
        ],
        out_specs=pl.BlockSpec((1, MOD_ROWS, bn), lambda l, j: (l, 0, j)),
        compiler_params=pltpu.CompilerParams(vmem_limit_bytes=VMEM_LIMIT),
        name="mod_vectors",
    )(cvec, w_mod, b_mod.reshape(DEPTH, 1, MOD_COLS))


def _cast_blocks(srcs, dsts):
    for src, dst in zip(srcs, dsts):
        dst[...] = src[...].astype(BF16)


def _cast_specs(weights, layer, steps):
    in_specs = [pl.BlockSpec((None, w.shape[1] // steps, w.shape[2]), lambda i: (layer, i, 0)) for w in weights]
    out_shape = [jax.ShapeDtypeStruct(w.shape[1:], BF16) for w in weights]
    out_specs = [pl.BlockSpec((w.shape[1] // steps, w.shape[2]), lambda i: (i, 0)) for w in weights]
    return in_specs, out_shape, out_specs


def _proj_in_body(x_ref, mod_ref, g_ref, w_ref, *rest, rows_per_mod, ncast):
    cast_src = rest[:ncast]
    za_ref, zc_ref = rest[ncast:ncast + 2]
    cast_dst = rest[ncast + 2:2 * ncast + 2]
    wb_s = rest[2 * ncast + 2]
    _cast_blocks(cast_src, cast_dst)

    @pl.when(pl.program_id(0) == 0)
    def _():
        wb_s[...] = w_ref[...].astype(BF16)

    mod = _mod_row(mod_ref, rows_per_mod)
    shift1 = mod[:, 0:D_MODEL]
    scale1 = mod[:, D_MODEL:2 * D_MODEL]
    h = (_rms(x_ref[...], g_ref[0:1]) * (1.0 + scale1) + shift1).astype(BF16)
    z = jnp.dot(h, wb_s[...], preferred_element_type=F32)
    za_ref[...] = z[:, 0:ATT_COLS]
    zc_ref[...] = z[:, ATT_COLS:IN_COLS]


def _proj_in_call(x, mods, gvec, w_in, layer, rows_per_mod, cast_weights=()):
    n = x.shape[0]
    cast_in, cast_shape, cast_out = _cast_specs(cast_weights, layer, n // ROW_TILE)
    return pl.pallas_call(
        functools.partial(_proj_in_body, rows_per_mod=rows_per_mod, ncast=len(cast_weights)),
        out_shape=(jax.ShapeDtypeStruct((n, ATT_COLS), F32), jax.ShapeDtypeStruct((n, RWKV_COLS), F32), *cast_shape),
        grid=(n // ROW_TILE,),
        in_specs=[
            pl.BlockSpec((ROW_TILE, D_MODEL), lambda i: (i, 0)),
            _layer_spec((MOD_ROWS, MOD_COLS), layer),
            _layer_spec((8, D_MODEL), layer),
            _layer_spec((D_MODEL, IN_COLS), layer),
            *cast_in,
        ],
        out_specs=(pl.BlockSpec((ROW_TILE, ATT_COLS), lambda i: (i, 0)),
                   pl.BlockSpec((ROW_TILE, RWKV_COLS), lambda i: (i, 0)), *cast_out),
        scratch_shapes=[pltpu.VMEM((D_MODEL, IN_COLS), BF16)],
        compiler_params=pltpu.CompilerParams(dimension_semantics=("arbitrary",), vmem_limit_bytes=VMEM_LIMIT),
        name="proj_in",
    )(x, mods, gvec, w_in, *cast_weights)


def _attn_prompt_body(*refs, layer):
    sink_ref, za_ref, gq_ref, gk_ref = refs[:4]
    if layer == 0:
        o_ref, kv_ref = refs[4:]
    else:
        prev_ref, o_ref = refs[4:6]
        cache_refs = refs[6:]
    aq = (za_ref[:, 0:A_DIM] * QK_SCALE).astype(BF16)
    ak = za_ref[:, 256:384]
    bq = _head_rms(za_ref[:, 512:896], gq_ref[...], _head_ones(B_DIM))
    bk = _head_rms(za_ref[:, 896:1024], gk_ref[...], _head_ones(KV_DIM))
    bqb = (bq * QK_SCALE).astype(BF16)
    akb = ak.astype(BF16)
    bkb = bk.astype(BF16)
    pieces_t = (ak.T, za_ref[:, 384:512].T, bk.T, za_ref[:, 1024:1152].T)
    avt = pieces_t[1].astype(BF16)
    bvt = pieces_t[3].astype(BF16)

    score_fns, finish_fns = [], []
    for h in range(A_HEADS):
        kv = h // (A_HEADS // A_KV_HEADS)
        hs = slice(kv * 64, (kv + 1) * 64)
        score_fns.append(lambda h=h, hs=hs: [_bdot_nt(akb[:, hs], aq[:, h * 64:(h + 1) * 64])])
        finish_fns.append(lambda sc, h=h, hs=hs: _softmax_pv_t(sc, [avt[hs]], sink_ref[layer, h]))
    for h in range(B_HEADS):
        kv = h // (B_HEADS // B_KV_HEADS)
        hs = slice(kv * 64, (kv + 1) * 64)
        score_fns.append(lambda h=h, hs=hs: [_bdot_nt(bkb[:, hs], bqb[:, h * 64:(h + 1) * 64])])
        finish_fns.append(lambda sc, hs=hs: _softmax_pv_t(sc, [bvt[hs]]))
    o_ref[...] = jnp.concatenate(_attend_heads(score_fns, finish_fns, ahead=len(score_fns)), axis=0).T

    for j, piece_t in enumerate(pieces_t):
        if layer == 0:
            kv_ref[0, j] = piece_t
        else:
            cache_refs[j][0, 0] = prev_ref[0, j]
            cache_refs[j][0, 1] = piece_t


def _attn_prompt_call(za, sink, gq, gk, layer, prev_kv):
    nb = za.shape[0] // SEQ
    in_specs = [
        pl.BlockSpec(memory_space=pltpu.SMEM),
        pl.BlockSpec((SEQ, ATT_COLS), lambda b: (b, 0)),
        _layer_spec((1, B_DIM), layer),
        _layer_spec((1, KV_DIM), layer),
    ]
    args = [sink, za, gq, gk]
    out_shape = [jax.ShapeDtypeStruct((nb * SEQ, ATT_OUT), F32)]
    out_specs = [pl.BlockSpec((SEQ, ATT_OUT), lambda b: (b, 0))]
    slab_spec = pl.BlockSpec((1, N_CACHE, KV_DIM, SEQ), lambda b: (b, 0, 0, 0))
    if layer == 0:
        out_shape.append(jax.ShapeDtypeStruct((nb, N_CACHE, KV_DIM, SEQ), F32))
        out_specs.append(slab_spec)
    else:
        in_specs.append(slab_spec)
        args.append(prev_kv)
        for _ in range(N_CACHE):
            out_shape.append(jax.ShapeDtypeStruct((nb, DEPTH, KV_DIM, SEQ), F32))
            out_specs.append(pl.BlockSpec((1, DEPTH, KV_DIM, SEQ), lambda b: (b, 0, 0, 0)))
    return pl.pallas_call(
        functools.partial(_attn_prompt_body, layer=layer),
        out_shape=tuple(out_shape),
        grid=(nb,),
        in_specs=in_specs,
        out_specs=tuple(out_specs),
        compiler_params=pltpu.CompilerParams(vmem_limit_bytes=VMEM_LIMIT),
        name="attn_prompt",
    )(*args)


def _attn_sample_body(sink_ref, za_ref, cak_ref, cav_ref, cbk_ref, cbv_ref, cos_ref, sin_ref, gq_ref, gk_ref,
                      o_ref, qa_s, ka_s, va_s, qb_s, kb_s, vb_s, *, layer):
    t = DEC_SEQ
    cos = cos_ref[...]
    sin = sin_ref[...]
    qa_s[...] = (_rope(za_ref[:, 0:A_DIM], cos, sin) * QK_SCALE).astype(BF16)
    zpad = jnp.zeros((Q_BLK, KV_DIM), BF16)
    ka_s[0:Q_BLK] = zpad
    ka_s[Q_BLK + t:] = zpad
    va_s[0:Q_BLK] = zpad
    va_s[Q_BLK + t:] = zpad
    ka_s[Q_BLK:Q_BLK + t] = _rope(za_ref[:, 256:384], cos, sin).astype(BF16)
    va_s[Q_BLK:Q_BLK + t] = za_ref[:, 384:512].astype(BF16)
    bq = _head_rms(za_ref[:, 512:896], gq_ref[...], _head_ones(B_DIM))
    qb_s[...] = (_rope(bq, cos, sin) * QK_SCALE).astype(BF16)
    bk = _head_rms(za_ref[:, 896:1024], gk_ref[...], _head_ones(KV_DIM))
    kb_s[0:PAST_LEN] = cbk_ref[0].T.astype(BF16)
    kb_s[PAST_LEN:] = _rope(bk, cos, sin).astype(BF16)
    vb_s[0:PAST_LEN] = cbv_ref[0].T.astype(BF16)
    vb_s[PAST_LEN:] = za_ref[:, 1024:1152].astype(BF16)

    kca = cak_ref[0].T.astype(BF16)
    vca = cav_ref[0].T.astype(BF16)

    ga = A_HEADS // A_KV_HEADS
    gb = B_HEADS // B_KV_HEADS
    qpos = lax.broadcasted_iota(jnp.int32, (ga * Q_BLK, 3 * Q_BLK), 0) & (Q_BLK - 1)
    kpos = lax.broadcasted_iota(jnp.int32, (ga * Q_BLK, 3 * Q_BLK), 1) - Q_BLK
    near = jnp.abs(kpos - qpos) <= WINDOW
    head_of_row = lax.shift_right_logical(lax.broadcasted_iota(jnp.int32, (ga * Q_BLK, 1), 0),
                                          jnp.int32(Q_BLK.bit_length() - 1))
    sinks = []
    for kv in range(A_KV_HEADS):
        col = jnp.full((ga * Q_BLK, 1), sink_ref[layer, kv * ga], F32)
        for g in range(1, ga):
            col = jnp.where(head_of_row == g, sink_ref[layer, kv * ga + g], col)
        sinks.append(col)

    def stack_heads(q, kv, group):
        return jnp.concatenate([q[:, (kv * group + g) * 64:(kv * group + g + 1) * 64] for g in range(group)], axis=0)

    def unstack_heads(outs, group, rows):
        return jnp.concatenate([o[g * rows:(g + 1) * rows] for o in outs for g in range(group)], axis=1)

    def a_block(n, carry):
        r0 = pl.multiple_of(_index32(n) * Q_BLK, Q_BLK)
        q = qa_s[pl.ds(r0, Q_BLK), :]
        kl = ka_s[pl.ds(r0, 3 * Q_BLK), :]
        vl = va_s[pl.ds(r0, 3 * Q_BLK), :]
        kabs = kpos + r0
        mask = near & (kabs >= 0) & (kabs < t)
        score_fns, finish_fns = [], []
        for kv in range(A_KV_HEADS):
            hs = slice(kv * 64, (kv + 1) * 64)

            def scores(kv=kv, hs=hs):
                qs = stack_heads(q, kv, ga)
                return [jnp.where(mask, _bdot_nt(qs, kl[:, hs]), NEG_INF), _bdot_nt(qs, kca[:, hs])]

            score_fns.append(scores)
            finish_fns.append(lambda sc, kv=kv, hs=hs: _softmax_pv(sc, [vl[:, hs], vca[:, hs]], sinks[kv]))
        outs = _attend_heads(score_fns, finish_fns, ahead=A_KV_HEADS)
        o_ref[pl.ds(r0, Q_BLK), 0:A_DIM] = unstack_heads(outs, ga, Q_BLK)
        return carry

    lax.fori_loop(0, t // Q_BLK, a_block, 0)

    def b_block(n, carry):
        r0 = pl.multiple_of(_index32(n) * B_QROWS, B_QROWS)
        q = qb_s[pl.ds(r0, B_QROWS), :]
        score_fns, finish_fns = [], []
        for kv in range(B_KV_HEADS):
            hs = slice(kv * 64, (kv + 1) * 64)
            score_fns.append(lambda kv=kv, hs=hs: [_bdot_nt(stack_heads(q, kv, gb), kb_s[:, hs])])
            finish_fns.append(lambda sc, hs=hs: _softmax_pv(sc, [vb_s[:, hs]]))
        outs = _attend_heads(score_fns, finish_fns)
        o_ref[pl.ds(r0, B_QROWS), A_DIM:ATT_OUT] = unstack_heads(outs, gb, B_QROWS)
        return carry

    lax.fori_loop(0, t // B_QROWS, b_block, 0)


def _attn_sample_call(za, sink, caches, cos, sin, gq, gk, layer):
    nb = za.shape[0] // DEC_SEQ
    t = DEC_SEQ
    cache_spec = pl.BlockSpec((1, None, KV_DIM, PAST_LEN), lambda b: (b, layer, 0, 0))
    return pl.pallas_call(
        functools.partial(_attn_sample_body, layer=layer),
        out_shape=jax.ShapeDtypeStruct((nb * t, ATT_OUT), F32),
        grid=(nb,),
        in_specs=[
            pl.BlockSpec(memory_space=pltpu.SMEM),
            pl.BlockSpec((t, ATT_COLS), lambda b: (b, 0)),
            cache_spec, cache_spec, cache_spec, cache_spec,
            pl.BlockSpec((t, 128), lambda b: (0, 0)),
            pl.BlockSpec((t, 128), lambda b: (0, 0)),
            _layer_spec((1, B_DIM), layer),
            _layer_spec((1, KV_DIM), layer),
        ],
        out_specs=pl.BlockSpec((t, ATT_OUT), lambda b: (b, 0)),
        scratch_shapes=[
            pltpu.VMEM((t, A_DIM), BF16),
            pltpu.VMEM((t + 2 * Q_BLK, KV_DIM), BF16),
            pltpu.VMEM((t + 2 * Q_BLK, KV_DIM), BF16),
            pltpu.VMEM((t, B_DIM), BF16),
            pltpu.VMEM((t + PAST_LEN, KV_DIM), BF16),
            pltpu.VMEM((t + PAST_LEN, KV_DIM), BF16),
        ],
        compiler_params=pltpu.CompilerParams(vmem_limit_bytes=VMEM_LIMIT),
        name="attn_sample",
    )(sink, za, *caches, cos, sin, gq, gk)


def _rwkv_body(*refs, t, nbatch, has_init, want_state, has_prev, ncast):
    refs = list(refs)
    zc_ref, cv_ref, wup_ref, aup_ref, gup_ref = refs[:5]
    pos = 5
    st0_ref = prev_ref = stout_ref = None
    if has_init:
        st0_ref = refs[pos]
        pos += 1
    if has_prev:
        prev_ref = refs[pos]
        pos += 1
    cast_src = refs[pos:pos + ncast]
    pos += ncast
    oc_ref = refs[pos]
    pos += 1
    if want_state:
        stout_ref = refs[pos]
        pos += 1
    cast_dst = refs[pos:pos + ncast]
    pos += ncast
    aa_s, ld_s, b_s, kt_s, bon_s, y_s, st_s = refs[pos:]

    _cast_blocks(cast_src, cast_dst)

    nchunks = t // CHUNK
    ones = _head_ones(C_DIM)
    k_k = cv_ref[0:1]
    k_a = cv_ref[1:2]
    r_k = cv_ref[2:3]
    ln_w = cv_ref[3:4]
    ln_b = cv_ref[4:5]
    w0h = (0.5 * cv_ref[5:6], 0.5 * cv_ref[6:7])
    a0h = (0.5 * cv_ref[7:8], 0.5 * cv_ref[8:9])
    wup_h = ((0.5 * wup_ref[0]).astype(BF16), (0.5 * wup_ref[1]).astype(BF16))
    aup_h = ((0.5 * aup_ref[0]).astype(BF16), (0.5 * aup_ref[1]).astype(BF16))
    c1 = 1.0 - 0.5 * k_a
    c2 = 0.5 * k_a
    half_rate = 0.5 * DECAY_RATE

    for rb in range(nbatch * t // RWKV_ROWS):
        rows = slice(rb * RWKV_ROWS, (rb + 1) * RWKV_ROWS)
        zr = zc_ref[rows, 0:384]
        zk = zc_ref[rows, 384:768]
        zv = zc_ref[rows, 768:1152]
        tw = jnp.tanh(zc_ref[rows, 1152:1216])
        xa = zc_ref[rows, 1216:1280]
        kkr = zk * k_k
        kk_half = kkr * (0.5 * lax.rsqrt(_head_sum(kkr * kkr, ones) + 1e-12))
        aa_s[rows] = -2.0 * kk_half
        kt_sum = None
        for d in range(2):
            th_w = jnp.tanh(w0h[d] + _bdot(tw, wup_h[d]))
            ld_s[d, rows] = th_w * (-half_rate) - half_rate
            th_a = jnp.tanh(a0h[d] + _bdot(xa, aup_h[d]))
            kt = zk * (c1 + c2 * th_a)
            kt_s[d, rows] = kt
            b_s[d, rows] = kk_half * th_a + kk_half
            kt_sum = kt if kt_sum is None else kt_sum + kt
        bon_s[rows] = _head_sum(zr * kt_sum * r_k, ones) * zv
        y_s[rows] = jnp.zeros((RWKV_ROWS, C_DIM), F32)

    ir = lax.broadcasted_iota(jnp.int32, (CHUNK, 128), 0)
    ic = lax.broadcasted_iota(jnp.int32, (CHUNK, 128), 1) & 63
    eye = jnp.where(ir == ic, 1.0, 0.0).astype(F32)
    incl = (ic <= ir, ic >= ir)
    strict = (ic < ir, ic > ir)
    sr = lax.broadcasted_iota(jnp.int32, (CHUNK, CHUNK), 0)
    sc = lax.broadcasted_iota(jnp.int32, (CHUNK, CHUNK), 1)
    tri = tuple(jnp.where(m, 1.0, 0.0).astype(BF16) for m in (sc <= sr, sc >= sr))
    br = lax.broadcasted_iota(jnp.int32, (128, 128), 0)
    bc = lax.broadcasted_iota(jnp.int32, (128, 128), 1)
    diag_blocks = (br < 64) == (bc < 64)
    left_half = lax.broadcasted_iota(jnp.int32, (CHUNK, 128), 1) < 64

    def level_mask(rows, cols, j):
        same_big = lax.shift_right_logical(rows, jnp.int32(j + 1)) == lax.shift_right_logical(cols, jnp.int32(j + 1))
        same_small = lax.shift_right_logical(rows, jnp.int32(j)) == lax.shift_right_logical(cols, jnp.int32(j))
        return same_big & jnp.logical_not(same_small)

    first_level = level_mask(ir, ic, 0)
    level_blocks = [diag_blocks & level_mask(br & 63, bc & 63, j) for j in range(1, CHUNK.bit_length() - 1)]

    def blockdiag(x2):
        x2 = x2.astype(BF16)
        return jnp.where(diag_blocks, jnp.concatenate([x2, x2], axis=0), jnp.zeros((128, 128), BF16))

    npairs = C_HEADS // 2
    for bi in range(nbatch):
        for d in range(2):
            for p in range(npairs):
                if has_init:
                    st_s[bi, d, p] = jnp.concatenate([st0_ref[bi, d, 2 * p], st0_ref[bi, d, 2 * p + 1]], axis=1)
                else:
                    st_s[bi, d, p] = jnp.zeros((HEAD_DIM, 128), F32)

    def chunk(c, carry):
        c = _index32(c)
        items = []
        for bi, d in [(bi, d) for bi in range(nbatch) for d in range(2)]:
            r0 = bi * t + ((nchunks - 1 - c) if d == 1 else c) * CHUNK
            rows = pl.ds(pl.multiple_of(r0, CHUNK), CHUNK)
            ld = ld_s[d, rows]
            cinc = _split_dot_left(tri[d], ld)
            cexc = cinc - ld
            ctot = cinc[0:1] if d == 1 else cinc[CHUNK - 1:CHUNK]
            e_ninc = jnp.exp(-cinc)
            e_rem = jnp.exp(ctot - cinc)
            e_tot = jnp.exp(ctot)
            bb = b_s[d, rows]
            kt = kt_s[d, rows]
            at = (aa_s[rows] * jnp.exp(cexc)).astype(BF16)
            rt = (zc_ref[rows, 0:384] * jnp.exp(cinc)).astype(BF16)
            bt = (bb * e_ninc).astype(BF16)
            ktt = (kt * e_ninc).astype(BF16)
            bh = (bb * e_rem).astype(BF16)
            kh = (kt * e_rem).astype(BF16)
            vm = zc_ref[rows, 768:1152].astype(BF16)
            for p in range(npairs):
                ps = slice(p * 128, (p + 1) * 128)
                items.append(dict(bi=bi, d=d, p=p, rows=rows, e_tot=e_tot[:, ps], at=at[:, ps], rt=rt[:, ps],
                                  bt=bt[:, ps], kt=ktt[:, ps], bh=bh[:, ps], kh=kh[:, ps], vm=vm[:, ps]))
        for it in items:
            it["ar"] = jnp.concatenate([it["at"], it["rt"]], axis=0)
            it["vmd"] = blockdiag(it["vm"])
        for it in items:
            sbk = _bdot_nt(it["ar"], jnp.concatenate([blockdiag(it["bt"]), blockdiag(it["kt"])], axis=0))
            it["sb"] = sbk[:, 0:128]
            it["sk"] = sbk[:, 128:256]
        for it in items:
            d = it["d"]
            it["l"] = jnp.where(strict[d], it["sb"][0:CHUNK], 0.0)
            it["mak"] = jnp.where(strict[d], it["sk"][0:CHUNK], 0.0).astype(BF16)
            it["nr"] = jnp.concatenate([jnp.where(incl[d], it["sb"][CHUNK:], 0.0).astype(BF16),
                                        jnp.where(incl[d], it["sk"][CHUNK:], 0.0).astype(BF16)], axis=1)
            it["tm"] = eye + jnp.where(first_level, it["l"], 0.0)
            lb = it["l"].astype(BF16)
            it["l2"] = jnp.concatenate([lb, lb], axis=0)
        for lvl in level_blocks:
            for it in items:
                it["t1"] = _bdot(it["tm"], jnp.where(lvl, it["l2"], jnp.zeros((128, 128), BF16)))
            for it in items:
                it["tm"] = it["tm"] + _bdot(it["t1"], blockdiag(it["tm"]))
        for it in items:
            it["mv"] = _bdot(it["mak"], it["vmd"])
        for it in items:
            it["tau"] = _bdot(it["tm"], jnp.concatenate([blockdiag(it["at"]), blockdiag(it["mv"])], axis=1))
        for it in items:
            it["s0"] = st_s[it["bi"], it["d"], it["p"]]
            it["x"] = _bdot_nt(jnp.concatenate([it["tau"][:, 0:128].astype(BF16), it["rt"]], axis=0),
                               blockdiag(it["s0"]))
        for it in items:
            it["u"] = (it["x"][0:CHUNK] + it["tau"][:, 128:256]).astype(BF16)
        for it in items:
            it["y"] = it["x"][CHUNK:] + _bdot(it["nr"], jnp.concatenate([blockdiag(it["u"]), it["vmd"]], axis=0))
        for it in items:
            full = _bdot_tn(jnp.concatenate([it["u"], it["vm"]], axis=0),
                            jnp.concatenate([it["bh"], it["kh"]], axis=0))
            st_s[it["bi"], it["d"], it["p"]] = (it["s0"] * it["e_tot"]
                                                + jnp.where(left_half, full[0:64], full[64:128]))
        for i in range(0, len(items), npairs):
            rows = items[i]["rows"]
            y_s[rows] = y_s[rows] + jnp.concatenate([it["y"] for it in items[i:i + npairs]], axis=1)
        return carry

    lax.fori_loop(0, nchunks, chunk, 0)
    if want_state:
        for bi in range(nbatch):
            if has_prev:
                stout_ref[bi, 0] = prev_ref[bi]
            for d in range(2):
                for p in range(npairs):
                    s2 = st_s[bi, d, p]
                    for q in range(2):
                        if has_prev:
                            stout_ref[bi, 1, d, 2 * p + q] = s2[:, q * 64:(q + 1) * 64]
                        else:
                            stout_ref[bi, d, 2 * p + q] = s2[:, q * 64:(q + 1) * 64]

    gup = gup_ref[...].astype(BF16)
    for rb in range(nbatch * t // RWKV_ROWS):
        rows = slice(rb * RWKV_ROWS, (rb + 1) * RWKV_ROWS)
        y = y_s[rows]
        mu = _head_sum(y, ones) * (1.0 / HEAD_DIM)
        yc = y - mu
        var = _head_sum(yc * yc, ones) * (1.0 / HEAD_DIM)
        yn = yc * lax.rsqrt(var + GN_EPS)
        g = _bdot(_sigmoid(zc_ref[rows, 1280:1408]), gup)
        oc_ref[rows] = (yn * ln_w + ln_b + bon_s[rows]) * g


def _rwkv_call(zc, rvec, wup, aup, gup, layer, t, init_state, want_state, prev_state, cast_weights=()):
    nb = zc.shape[0] // t
    nbatch = min(RWKV_BATCH, nb)
    rows = nbatch * t
    has_init = init_state is not None
    has_prev = prev_state is not None
    st_shape = (2, C_HEADS, HEAD_DIM, HEAD_DIM)
    in_specs = [
        pl.BlockSpec((rows, RWKV_COLS), lambda b: (b, 0), pipeline_mode=pl.Buffered(1 if nb == nbatch else 2)),
        _layer_spec((16, C_DIM), layer),
        _layer_spec((2, W_RANK, C_DIM), layer),
        _layer_spec((2, A_RANK, C_DIM), layer),
        _layer_spec((G_RANK, C_DIM), layer),
    ]
    args = [zc, rvec, wup, aup, gup]
    if has_init:
        in_specs.append(pl.BlockSpec((nbatch, None) + st_shape, lambda b: (b, layer, 0, 0, 0, 0)))
        args.append(init_state)
    if has_prev:
        in_specs.append(pl.BlockSpec((nbatch,) + st_shape, lambda b: (b, 0, 0, 0, 0)))
        args.append(prev_state)
    steps = nb // nbatch
    cast_in, cast_shape, cast_out = _cast_specs(cast_weights, layer, steps)
    in_specs.extend(cast_in)
    args.extend(cast_weights)
    out_shape = [jax.ShapeDtypeStruct((nb * t, C_DIM), F32)]
    out_specs = [pl.BlockSpec((rows, C_DIM), lambda b: (b, 0))]
    if want_state and has_prev:
        out_shape.append(jax.ShapeDtypeStruct((nb, DEPTH) + st_shape, F32))
        out_specs.append(pl.BlockSpec((nbatch, DEPTH) + st_shape, lambda b: (b, 0, 0, 0, 0, 0)))
    elif want_state:
        out_shape.append(jax.ShapeDtypeStruct((nb,) + st_shape, F32))
        out_specs.append(pl.BlockSpec((nbatch,) + st_shape, lambda b: (b, 0, 0, 0, 0)))
    out_shape.extend(cast_shape)
    out_specs.extend(cast_out)
    return pl.pallas_call(
        functools.partial(_rwkv_body, t=t, nbatch=nbatch, has_init=has_init, want_state=want_state,
                          has_prev=has_prev, ncast=len(cast_weights)),
        out_shape=tuple(out_shape),
        grid=(steps,),
        in_specs=in_specs,
        out_specs=tuple(out_specs),
        scratch_shapes=[
            pltpu.VMEM((rows, C_DIM), F32),
            pltpu.VMEM((2, rows, C_DIM), F32),
            pltpu.VMEM((2, rows, C_DIM), F32),
            pltpu.VMEM((2, rows, C_DIM), F32),
            pltpu.VMEM((rows, C_DIM), F32),
            pltpu.VMEM((rows, C_DIM), F32),
            pltpu.VMEM((nbatch, 2, C_HEADS // 2, HEAD_DIM, 128), F32),
        ],
        compiler_params=pltpu.CompilerParams(vmem_limit_bytes=VMEM_LIMIT),
        name="rwkv_prompt" if want_state else "rwkv_sample",
    )(*args)


def _post_body(oa_ref, oc_ref, x_ref, mod_ref, g_ref, wout_ref, wgu_ref, wdn_ref, xo_ref, *, rows_per_mod):
    mod = _mod_row(mod_ref, rows_per_mod)
    gate1 = mod[:, 2 * D_MODEL:3 * D_MODEL]
    shift2 = mod[:, 3 * D_MODEL:4 * D_MODEL]
    scale2 = mod[:, 4 * D_MODEL:5 * D_MODEL]
    gate2 = mod[:, 5 * D_MODEL:6 * D_MODEL]
    sub = ROW_TILE // ROW_SPLIT
    parts = [slice(i * sub, (i + 1) * sub) for i in range(ROW_SPLIT)]
    o = [jnp.dot(jnp.concatenate([oa_ref[r, :].astype(BF16), oc_ref[r, :].astype(BF16)], axis=1), wout_ref[...],
                 preferred_element_type=F32) for r in parts]
    x1 = [x_ref[r, :] + gate1 * _rms(oi, g_ref[1:2]) for r, oi in zip(parts, o)]
    h2 = [(_rms(xi, g_ref[2:3]) * (1.0 + scale2) + shift2).astype(BF16) for xi in x1]

    def gate_up(step):
        (lo, hi), i = step
        g = jnp.dot(h2[i], wgu_ref[:, lo:hi], preferred_element_type=F32)
        u = jnp.dot(h2[i], wgu_ref[:, FF_DIM + lo:FF_DIM + hi], preferred_element_type=F32)
        return g, u

    steps = [(chunk, i) for chunk in FF_CHUNKS for i in range(ROW_SPLIT)]
    acc = [None] * ROW_SPLIT
    pending = gate_up(steps[0])
    for k, ((lo, hi), i) in enumerate(steps):
        g, u = pending
        if k + 1 < len(steps):
            pending = gate_up(steps[k + 1])
        act = (g * _sigmoid(g) * u).astype(BF16)
        part = jnp.dot(act, wdn_ref[lo:hi, :], preferred_element_type=F32)
        acc[i] = part if acc[i] is None else acc[i] + part
    for r, xi, ai in zip(parts, x1, acc):
        xo_ref[r, :] = xi + gate2 * _rms(ai, g_ref[3:4])


def _post_call(oatt, oc, x, mods, gvec, wout_b, wgu_b, wdn_b, layer, rows_per_mod):
    n = x.shape[0]
    return pl.pallas_call(
        functools.partial(_post_body, rows_per_mod=rows_per_mod),
        out_shape=jax.ShapeDtypeStruct((n, D_MODEL), F32),
        grid=(n // ROW_TILE,),
        in_specs=[
            pl.BlockSpec((ROW_TILE, ATT_OUT), lambda i: (i, 0)),
            pl.BlockSpec((ROW_TILE, C_DIM), lambda i: (i, 0)),
            pl.BlockSpec((ROW_TILE, D_MODEL), lambda i: (i, 0)),
            _layer_spec((MOD_ROWS, MOD_COLS), layer),
            _layer_spec((8, D_MODEL), layer),
            _whole_spec((MIX_DIM, D_MODEL)),
            _whole_spec((D_MODEL, 2 * FF_DIM)),
            _whole_spec((FF_DIM, D_MODEL)),
        ],
        out_specs=pl.BlockSpec((ROW_TILE, D_MODEL), lambda i: (i, 0)),
        compiler_params=pltpu.CompilerParams(vmem_limit_bytes=VMEM_LIMIT),
        name="post_ffn",
    )(oatt, oc, x, mods, gvec, wout_b, wgu_b, wdn_b)


def _rope_tables(t):
    pos = jnp.arange(t)
    row = (pos // GRID_W).astype(F32)
    col = (pos % GRID_W).astype(F32)
    freqs = ROPE_THETA ** (-jnp.arange(ROPE_PAIRS_AXIS, dtype=F32) / ROPE_PAIRS_AXIS)
    ang = jnp.concatenate([row[:, None] * freqs, col[:, None] * freqs], axis=-1)
    cos = jnp.tile(jnp.cos(ang), (1, 4))
    sin = jnp.sin(ang)
    sin = jnp.tile(jnp.concatenate([-sin, sin], axis=-1), (1, 2))
    return cos, sin


def _to_time_minor(cache):
    b, l, t = cache.shape[:3]
    return jnp.transpose(cache, (0, 1, 3, 4, 2)).reshape(b, l, KV_DIM, t)


def _from_time_minor(slab):
    b, l, _, t = slab.shape
    return jnp.transpose(slab.reshape(b, l, 2, HEAD_DIM, t), (0, 1, 4, 2, 3))


def kernel(x_prompt, x_sample, cache_a_k, cache_a_v, cache_b_k, cache_b_v, state_c, c, c_ctx, w_mod, b_mod, norm_mix_pre, norm_mix_post, norm_ffn_pre, norm_ffn_post, w_in, w_out, a_sink, b_q_norm, b_k_norm, c_w0, c_w_up, c_a0, c_a_up, c_g_up, c_k_k, c_k_a, c_r_k, c_ln_w, c_ln_b, w_gu, w_down):
    cvec = jnp.zeros((MOD_ROWS, D_MODEL), F32).at[0].set(c_ctx).at[1:1 + DEC_BATCH].set(c)
    mods = _mod_call(cvec, w_mod, b_mod)
    cos, sin = _rope_tables(DEC_SEQ)

    gvec = jnp.zeros((DEPTH, 8, D_MODEL), F32)
    gvec = gvec.at[:, 0].set(norm_mix_pre).at[:, 1].set(norm_mix_post)
    gvec = gvec.at[:, 2].set(norm_ffn_pre).at[:, 3].set(norm_ffn_post)
    rvec = jnp.zeros((DEPTH, 16, C_DIM), F32)
    rvec = rvec.at[:, 0].set(c_k_k).at[:, 1].set(c_k_a).at[:, 2].set(c_r_k.reshape(DEPTH, C_DIM))
    rvec = rvec.at[:, 3].set(c_ln_w).at[:, 4].set(c_ln_b)
    rvec = rvec.at[:, 5:7].set(c_w0).at[:, 7:9].set(c_a0)
    gq = jnp.tile(b_q_norm, (1, B_HEADS))[:, None, :]
    gk = jnp.tile(b_k_norm, (1, B_KV_HEADS))[:, None, :]
    caches = tuple(_to_time_minor(t) for t in (cache_a_k, cache_a_v, cache_b_k, cache_b_v))

    xp = x_prompt.reshape(BATCH * SEQ, D_MODEL)
    xs = x_sample.reshape(DEC_BATCH * DEC_SEQ, D_MODEL)
    kv_prev = st_prev = None
    for l in range(DEPTH):
        za, zc, wout_b, wdn_b = _proj_in_call(xp, mods, gvec, w_in, l, None, cast_weights=(w_out, w_down))
        oatt, *kv_prev = _attn_prompt_call(za, a_sink, gq, gk, l, kv_prev)
        if l == 0:
            (kv_prev,) = kv_prev
        oc, st_prev, wgu_b = _rwkv_call(zc, rvec, c_w_up, c_a_up, c_g_up, l, SEQ, None, True, st_prev,
                                        cast_weights=(w_gu,))
        xp = _post_call(oatt, oc, xp, mods, gvec, wout_b, wgu_b, wdn_b, l, None)

        za, zc = _proj_in_call(xs, mods, gvec, w_in, l, DEC_SEQ)
        oatt = _attn_sample_call(za, a_sink, caches, cos, sin, gq, gk, l)
        (oc,) = _rwkv_call(zc, rvec, c_w_up, c_a_up, c_g_up, l, DEC_SEQ, state_c, False, None)
        xs = _post_call(oatt, oc, xs, mods, gvec, wout_b, wgu_b, wdn_b, l, DEC_SEQ)

    new_caches = tuple(_from_time_minor(slab) for slab in kv_prev)
    return (xp.reshape(BATCH, SEQ, D_MODEL), xs.reshape(DEC_BATCH, DEC_SEQ, D_MODEL), *new_caches, st_prev)
```

```python
import functools
import math

import jax
import jax.numpy as jnp
from jax import lax
from jax.experimental import pallas as pl
from jax.experimental.pallas import tpu as pltpu

D_MODEL = 1024
BATCH = 32
SEQ = 256
DEPTH = 2
DEC_BATCH = 2
DEC_SEQ = 1024
PAST_LEN = 256
GRID_W = 64
HEAD_DIM = 64
A_HEADS = 4
A_KV_HEADS = 2
B_HEADS = 6
B_KV_HEADS = 2
C_HEADS = 6
A_DIM = A_HEADS * HEAD_DIM
B_DIM = B_HEADS * HEAD_DIM
C_DIM = C_HEADS * HEAD_DIM
MIX_DIM = A_DIM + B_DIM + C_DIM
WINDOW = 128
Q_BLK = 128
W_RANK = 64
A_RANK = 64
G_RANK = 128
FF_DIM = -(-8 * D_MODEL // (3 * 256)) * 256
ROPE_THETA = 10000.0
ROPE_PAIRS_AXIS = HEAD_DIM // 4
NORM_EPS = 1e-6
GN_EPS = 64e-5
NEG_INF = -1e30

KV_DIM = 2 * HEAD_DIM
ATT_COLS = A_DIM + 2 * KV_DIM + B_DIM + 2 * KV_DIM
RWKV_COLS = 3 * C_DIM + W_RANK + A_RANK + G_RANK
IN_COLS = ATT_COLS + RWKV_COLS
ATT_OUT = A_DIM + B_DIM
MOD_COLS = 6 * D_MODEL
MOD_ROWS = 8
N_CACHE = 4
CHUNK = 64
QK_SCALE = HEAD_DIM ** -0.5
DECAY_RATE = math.exp(-0.5)
ROW_TILE = 512
ROW_SPLIT = 2
RWKV_ROWS = 512
RWKV_BATCH = 4
B_QROWS = 256
MXU_TILE = 256
FF_CHUNKS = ((0, 4 * MXU_TILE), (4 * MXU_TILE, 8 * MXU_TILE), (8 * MXU_TILE, FF_DIM))
assert FF_DIM % MXU_TILE == 0
VMEM_LIMIT = 56 * 1024 * 1024

F32 = jnp.float32
BF16 = jnp.bfloat16

assert DEPTH == 2


def _bdot(a, b):
    return jnp.dot(a.astype(BF16), b.astype(BF16), preferred_element_type=F32)


def _bdot_nt(a, b):
    return lax.dot_general(a.astype(BF16), b.astype(BF16), (((1,), (1,)), ((), ())), preferred_element_type=F32)


def _bdot_tn(a, b):
    return lax.dot_general(a.astype(BF16), b.astype(BF16), (((0,), (0,)), ((), ())), preferred_element_type=F32)


def _split(x):
    hi = x.astype(BF16)
    lo = (x - hi.astype(F32)).astype(BF16)
    return hi, lo


def _split_dot_left(m, x):
    hi, lo = _split(x)
    return jnp.dot(m, hi, preferred_element_type=F32) + jnp.dot(m, lo, preferred_element_type=F32)


def _head_sum(x, ones):
    return jnp.dot(x.astype(BF16), ones, preferred_element_type=F32)


def _sigmoid(x):
    return 0.5 * jnp.tanh(0.5 * x) + 0.5


def _head_ones(n):
    r = lax.shift_right_logical(lax.broadcasted_iota(jnp.int32, (n, n), 0), jnp.int32(6))
    c = lax.shift_right_logical(lax.broadcasted_iota(jnp.int32, (n, n), 1), jnp.int32(6))
    return jnp.where(r == c, 1.0, 0.0).astype(BF16)


def _rms(x, g):
    ms = jnp.mean(x * x, axis=-1, keepdims=True)
    return x * lax.rsqrt(ms + NORM_EPS) * g


def _head_rms(x, g, ones):
    ms = _head_sum(x * x, ones) * (1.0 / HEAD_DIM)
    return x * lax.rsqrt(ms + NORM_EPS) * g


def _rope(x, cos, sin):
    t = x.shape[0]
    lane = lax.broadcasted_iota(jnp.int32, (t, 128), 1)
    first = (lane & 32) == 0
    outs = []
    for i in range(x.shape[1] // 128):
        xs = x[:, i * 128:(i + 1) * 128]
        swapped = jnp.where(first, pltpu.roll(xs, 96, axis=1), pltpu.roll(xs, 32, axis=1))
        outs.append(xs * cos + swapped * sin)
    return outs[0] if len(outs) == 1 else jnp.concatenate(outs, axis=1)


def _softmax_pv(scores, values, sink=None):
    m = scores[0].max(axis=-1, keepdims=True)
    for s in scores[1:]:
        m = jnp.maximum(m, s.max(axis=-1, keepdims=True))
    if sink is not None:
        m = jnp.maximum(m, sink)
    denom = None
    acc = None
    for s, v in zip(scores, values):
        p = jnp.exp(s - m)
        d = p.sum(axis=-1, keepdims=True)
        o = jnp.dot(p.astype(BF16), v, preferred_element_type=F32)
        denom = d if denom is None else denom + d
        acc = o if acc is None else acc + o
    if sink is not None:
        denom = denom + jnp.exp(sink - m)
    return acc / denom


def _softmax_pv_t(scores_t, values_t, sink=None):
    m = scores_t[0].max(axis=0, keepdims=True)
    for s in scores_t[1:]:
        m = jnp.maximum(m, s.max(axis=0, keepdims=True))
    if sink is not None:
        m = jnp.maximum(m, sink)
    denom = None
    acc = None
    for s, vt in zip(scores_t, values_t):
        p = jnp.exp(s - m)
        d = p.sum(axis=0, keepdims=True)
        o = jnp.dot(vt, p.astype(BF16), preferred_element_type=F32)
        denom = d if denom is None else denom + d
        acc = o if acc is None else acc + o
    if sink is not None:
        denom = denom + jnp.exp(sink - m)
    return acc / denom


def _attend_heads(score_fns, finish_fns, ahead=1):
    n = len(score_fns)
    pending = [fn() for fn in score_fns[:ahead]]
    outs = []
    for i in range(n):
        if i + ahead < n:
            pending.append(score_fns[i + ahead]())
        outs.append(finish_fns[i](pending.pop(0)))
    return outs


def _index32(i):
    return jnp.asarray(i, jnp.int32)


def _layer_spec(shape, layer):
    zeros = (0,) * len(shape)
    return pl.BlockSpec((None,) + shape, lambda *_: (layer,) + zeros, pipeline_mode=pl.Buffered(1))


def _whole_spec(shape):
    zeros = (0,) * len(shape)
    return pl.BlockSpec(shape, lambda *_: zeros, pipeline_mode=pl.Buffered(1))


def _mod_body(c_ref, w_ref, b_ref, o_ref):
    cv = c_ref[...]
    s = cv * _sigmoid(cv)
    o_ref[0] = _bdot(s, w_ref[0]) + b_ref[0]


def _mod_call(cvec, w_mod, b_mod):
    bn = MOD_COLS // 4
    return pl.pallas_call(
        _mod_body,
        out_shape=jax.ShapeDtypeStruct((DEPTH, MOD_ROWS, MOD_COLS), F32),
        grid=(DEPTH, MOD_COLS // bn),
        in_specs=[
            pl.BlockSpec((MOD_ROWS, D_MODEL), lambda l, j: (0, 0)),
            pl.BlockSpec((1, D_MODEL, bn), lambda l, j: (l, 0, j)),
            pl.BlockSpec((1, 1, bn), lambda l, j: (l, 0, j)),
        ],
        out_specs=pl.BlockSpec((1, MOD_ROWS, bn), lambda l, j: (l, 0, j)),
        compiler_params=pltpu.CompilerParams(vmem_limit_bytes=VMEM_LIMIT),
        name="mod_vectors",
    )(cvec, w_mod, b_mod.reshape(DEPTH, 1, MOD_COLS))


def _cast_blocks(srcs, dsts):
    for src, dst in zip(srcs, dsts):
        dst[...] = src[...].astype(BF16)


def _cast_specs(weights, layer, steps):
    def block(i):
        return jnp.minimum(i, steps - 1)

    in_specs = [pl.BlockSpec((None, w.shape[1] // steps, w.shape[2]), lambda i: (layer, block(i), 0)) for w in weights]
    out_shape = [jax.ShapeDtypeStruct(w.shape[1:], BF16) for w in weights]
    out_specs = [pl.BlockSpec((w.shape[1] // steps, w.shape[2]), lambda i: (block(i), 0)) for w in weights]
    return in_specs, out_shape, out_specs


def _two_group_specs(cols, n_first):
    first = pl.BlockSpec((ROW_TILE, cols), lambda i: (jnp.minimum(i, n_first - 1), 0))
    second = pl.BlockSpec((ROW_TILE, cols), lambda i: (jnp.maximum(i - n_first, 0), 0))
    return first, second


def _latent_mod_row(mod_ref, step, n_first):
    row = 1 + (step - n_first) // (DEC_SEQ // ROW_TILE)
    return mod_ref[pl.ds(row, 1), :]


def _proj_in_body(xp_ref, xs_ref, mod_ref, g_ref, w_ref, *rest, n_first, ncast):
    cast_src = rest[:ncast]
    zap_ref, zcp_ref, zas_ref, zcs_ref = rest[ncast:ncast + 4]
    cast_dst = rest[ncast + 4:2 * ncast + 4]
    wb_s = rest[2 * ncast + 4]
    step = pl.program_id(0)

    @pl.when(step == 0)
    def _():
        wb_s[...] = w_ref[...].astype(BF16)

    def project(x_ref, mod, za_ref, zc_ref):
        shift1 = mod[:, 0:D_MODEL]
        scale1 = mod[:, D_MODEL:2 * D_MODEL]
        h = (_rms(x_ref[...], g_ref[0:1]) * (1.0 + scale1) + shift1).astype(BF16)
        z = jnp.dot(h, wb_s[...], preferred_element_type=F32)
        za_ref[...] = z[:, 0:ATT_COLS]
        zc_ref[...] = z[:, ATT_COLS:IN_COLS]

    @pl.when(step < n_first)
    def _():
        _cast_blocks(cast_src, cast_dst)
        project(xp_ref, mod_ref[0:1], zap_ref, zcp_ref)

    @pl.when(step >= n_first)
    def _():
        project(xs_ref, _latent_mod_row(mod_ref, step, n_first), zas_ref, zcs_ref)


def _proj_in_call(xp, xs, mods, gvec, w_in, layer, cast_weights):
    n_first = xp.shape[0] // ROW_TILE
    n_second = xs.shape[0] // ROW_TILE
    cast_in, cast_shape, cast_out = _cast_specs(cast_weights, layer, n_first)
    x_first, x_second = _two_group_specs(D_MODEL, n_first)
    za_first, za_second = _two_group_specs(ATT_COLS, n_first)
    zc_first, zc_second = _two_group_specs(RWKV_COLS, n_first)
    return pl.pallas_call(
        functools.partial(_proj_in_body, n_first=n_first, ncast=len(cast_weights)),
        out_shape=(jax.ShapeDtypeStruct((xp.shape[0], ATT_COLS), F32), jax.ShapeDtypeStruct((xp.shape[0], RWKV_COLS), F32),
                   jax.ShapeDtypeStruct((xs.shape[0], ATT_COLS), F32), jax.ShapeDtypeStruct((xs.shape[0], RWKV_COLS), F32),
                   *cast_shape),
        grid=(n_first + n_second,),
        in_specs=[
            x_first, x_second,
            _layer_spec((MOD_ROWS, MOD_COLS), layer),
            _layer_spec((8, D_MODEL), layer),
            _layer_spec((D_MODEL, IN_COLS), layer),
            *cast_in,
        ],
        out_specs=(za_first, zc_first, za_second, zc_second, *cast_out),
        scratch_shapes=[pltpu.VMEM((D_MODEL, IN_COLS), BF16)],
        compiler_params=pltpu.CompilerParams(dimension_semantics=("arbitrary",), vmem_limit_bytes=VMEM_LIMIT),
        name="proj_in",
    )(xp, xs, mods, gvec, w_in, *cast_weights)


def _attn_prompt_body(*refs, layer):
    sink_ref, za_ref, gq_ref, gk_ref = refs[:4]
    if layer == 0:
        o_ref, kv_ref = refs[4:]
    else:
        prev_ref, o_ref = refs[4:6]
        cache_refs = refs[6:]
    aq = (za_ref[:, 0:A_DIM] * QK_SCALE).astype(BF16)
    ak = za_ref[:, 256:384]
    bq = _head_rms(za_ref[:, 512:896], gq_ref[...], _head_ones(B_DIM))
    bk = _head_rms(za_ref[:, 896:1024], gk_ref[...], _head_ones(KV_DIM))
    bqb = (bq * QK_SCALE).astype(BF16)
    akb = ak.astype(BF16)
    bkb = bk.astype(BF16)
    pieces_t = (ak.T, za_ref[:, 384:512].T, bk.T, za_ref[:, 1024:1152].T)
    avt = pieces_t[1].astype(BF16)
    bvt = pieces_t[3].astype(BF16)

    score_fns, finish_fns = [], []
    for h in range(A_HEADS):
        kv = h // (A_HEADS // A_KV_HEADS)
        hs = slice(kv * 64, (kv + 1) * 64)
        score_fns.append(lambda h=h, hs=hs: [_bdot_nt(akb[:, hs], aq[:, h * 64:(h + 1) * 64])])
        finish_fns.append(lambda sc, h=h, hs=hs: _softmax_pv_t(sc, [avt[hs]], sink_ref[layer, h]))
    for h in range(B_HEADS):
        kv = h // (B_HEADS // B_KV_HEADS)
        hs = slice(kv * 64, (kv + 1) * 64)
        score_fns.append(lambda h=h, hs=hs: [_bdot_nt(bkb[:, hs], bqb[:, h * 64:(h + 1) * 64])])
        finish_fns.append(lambda sc, hs=hs: _softmax_pv_t(sc, [bvt[hs]]))
    o_ref[...] = jnp.concatenate(_attend_heads(score_fns, finish_fns, ahead=len(score_fns)), axis=0).T

    for j, piece_t in enumerate(pieces_t):
        if layer == 0:
            kv_ref[0, j] = piece_t
        else:
            cache_refs[j][0, 0] = prev_ref[0, j]
            cache_refs[j][0, 1] = piece_t


def _attn_prompt_call(za, sink, gq, gk, layer, prev_kv):
    nb = za.shape[0] // SEQ
    in_specs = [
        pl.BlockSpec(memory_space=pltpu.SMEM),
        pl.BlockSpec((SEQ, ATT_COLS), lambda b: (b, 0)),
        _layer_spec((1, B_DIM), layer),
        _layer_spec((1, KV_DIM), layer),
    ]
    args = [sink, za, gq, gk]
    out_shape = [jax.ShapeDtypeStruct((nb * SEQ, ATT_OUT), F32)]
    out_specs = [pl.BlockSpec((SEQ, ATT_OUT), lambda b: (b, 0))]
    slab_spec = pl.BlockSpec((1, N_CACHE, KV_DIM, SEQ), lambda b: (b, 0, 0, 0))
    if layer == 0:
        out_shape.append(jax.ShapeDtypeStruct((nb, N_CACHE, KV_DIM, SEQ), F32))
        out_specs.append(slab_spec)
    else:
        in_specs.append(slab_spec)
        args.append(prev_kv)
        for _ in range(N_CACHE):
            out_shape.append(jax.ShapeDtypeStruct((nb, DEPTH, KV_DIM, SEQ), F32))
            out_specs.append(pl.BlockSpec((1, DEPTH, KV_DIM, SEQ), lambda b: (b, 0, 0, 0)))
    return pl.pallas_call(
        functools.partial(_attn_prompt_body, layer=layer),
        out_shape=tuple(out_shape),
        grid=(nb,),
        in_specs=in_specs,
        out_specs=tuple(out_specs),
        compiler_params=pltpu.CompilerParams(vmem_limit_bytes=VMEM_LIMIT),
        name="attn_prompt",
    )(*args)


def _attn_sample_body(sink_ref, za_ref, cak_ref, cav_ref, cbk_ref, cbv_ref, cos_ref, sin_ref, gq_ref, gk_ref,
                      o_ref, qa_s, ka_s, va_s, qb_s, kb_s, vb_s, *, layer):
    t = DEC_SEQ
    cos = cos_ref[...]
    sin = sin_ref[...]
    qa_s[...] = (_rope(za_ref[:, 0:A_DIM], cos, sin) * QK_SCALE).astype(BF16)
    zpad = jnp.zeros((Q_BLK, KV_DIM), BF16)
    ka_s[0:Q_BLK] = zpad
    ka_s[Q_BLK + t:] = zpad
    va_s[0:Q_BLK] = zpad
    va_s[Q_BLK + t:] = zpad
    ka_s[Q_BLK:Q_BLK + t] = _rope(za_ref[:, 256:384], cos, sin).astype(BF16)
    va_s[Q_BLK:Q_BLK + t] = za_ref[:, 384:512].astype(BF16)
    bq = _head_rms(za_ref[:, 512:896], gq_ref[...], _head_ones(B_DIM))
    qb_s[...] = (_rope(bq, cos, sin) * QK_SCALE).astype(BF16)
    bk = _head_rms(za_ref[:, 896:1024], gk_ref[...], _head_ones(KV_DIM))
    kb_s[0:PAST_LEN] = cbk_ref[0].T.astype(BF16)
    kb_s[PAST_LEN:] = _rope(bk, cos, sin).astype(BF16)
    vb_s[0:PAST_LEN] = cbv_ref[0].T.astype(BF16)
    vb_s[PAST_LEN:] = za_ref[:, 1024:1152].astype(BF16)

    kca = cak_ref[0].T.astype(BF16)
    vca = cav_ref[0].T.astype(BF16)

    ga = A_HEADS // A_KV_HEADS
    gb = B_HEADS // B_KV_HEADS
    qpos = lax.broadcasted_iota(jnp.int32, (ga * Q_BLK, 3 * Q_BLK), 0) & (Q_BLK - 1)
    kpos = lax.broadcasted_iota(jnp.int32, (ga * Q_BLK, 3 * Q_BLK), 1) - Q_BLK
    near = jnp.abs(kpos - qpos) <= WINDOW
    head_of_row = lax.shift_right_logical(lax.broadcasted_iota(jnp.int32, (ga * Q_BLK, 1), 0),
                                          jnp.int32(Q_BLK.bit_length() - 1))
    sinks = []
    for kv in range(A_KV_HEADS):
        col = jnp.full((ga * Q_BLK, 1), sink_ref[layer, kv * ga], F32)
        for g in range(1, ga):
            col = jnp.where(head_of_row == g, sink_ref[layer, kv * ga + g], col)
        sinks.append(col)

    def stack_heads(q, kv, group):
        return jnp.concatenate([q[:, (kv * group + g) * 64:(kv * group + g + 1) * 64] for g in range(group)], axis=0)

    def unstack_heads(outs, group, rows):
        return jnp.concatenate([o[g * rows:(g + 1) * rows] for o in outs for g in range(group)], axis=1)

    def a_block(n, carry):
        r0 = pl.multiple_of(_index32(n) * Q_BLK, Q_BLK)
        q = qa_s[pl.ds(r0, Q_BLK), :]
        kl = ka_s[pl.ds(r0, 3 * Q_BLK), :]
        vl = va_s[pl.ds(r0, 3 * Q_BLK), :]
        kabs = kpos + r0
        mask = near & (kabs >= 0) & (kabs < t)
        score_fns, finish_fns = [], []
        for kv in range(A_KV_HEADS):
            hs = slice(kv * 64, (kv + 1) * 64)

            def scores(kv=kv, hs=hs):
                qs = stack_heads(q, kv, ga)
                return [jnp.where(mask, _bdot_nt(qs, kl[:, hs]), NEG_INF), _bdot_nt(qs, kca[:, hs])]

            score_fns.append(scores)
            finish_fns.append(lambda sc, kv=kv, hs=hs: _softmax_pv(sc, [vl[:, hs], vca[:, hs]], sinks[kv]))
        outs = _attend_heads(score_fns, finish_fns, ahead=A_KV_HEADS)
        o_ref[pl.ds(r0, Q_BLK), 0:A_DIM] = unstack_heads(outs, ga, Q_BLK)
        return carry

    lax.fori_loop(0, t // Q_BLK, a_block, 0)

    def b_block(n, carry):
        r0 = pl.multiple_of(_index32(n) * B_QROWS, B_QROWS)
        q = qb_s[pl.ds(r0, B_QROWS), :]
        score_fns, finish_fns = [], []
        for kv in range(B_KV_HEADS):
            hs = slice(kv * 64, (kv + 1) * 64)
            score_fns.append(lambda kv=kv, hs=hs: [_bdot_nt(stack_heads(q, kv, gb), kb_s[:, hs])])
            finish_fns.append(lambda sc, hs=hs: _softmax_pv(sc, [vb_s[:, hs]]))
        outs = _attend_heads(score_fns, finish_fns)
        o_ref[pl.ds(r0, B_QROWS), A_DIM:ATT_OUT] = unstack_heads(outs, gb, B_QROWS)
        return carry

    lax.fori_loop(0, t // B_QROWS, b_block, 0)


def _attn_sample_call(za, sink, caches, cos, sin, gq, gk, layer):
    nb = za.shape[0] // DEC_SEQ
    t = DEC_SEQ
    cache_spec = pl.BlockSpec((1, None, KV_DIM, PAST_LEN), lambda b: (b, layer, 0, 0))
    return pl.pallas_call(
        functools.partial(_attn_sample_body, layer=layer),
        out_shape=jax.ShapeDtypeStruct((nb * t, ATT_OUT), F32),
        grid=(nb,),
        in_specs=[
            pl.BlockSpec(memory_space=pltpu.SMEM),
            pl.BlockSpec((t, ATT_COLS), lambda b: (b, 0)),
            cache_spec, cache_spec, cache_spec, cache_spec,
            pl.BlockSpec((t, 128), lambda b: (0, 0)),
            pl.BlockSpec((t, 128), lambda b: (0, 0)),
            _layer_spec((1, B_DIM), layer),
            _layer_spec((1, KV_DIM), layer),
        ],
        out_specs=pl.BlockSpec((t, ATT_OUT), lambda b: (b, 0)),
        scratch_shapes=[
            pltpu.VMEM((t, A_DIM), BF16),
            pltpu.VMEM((t + 2 * Q_BLK, KV_DIM), BF16),
            pltpu.VMEM((t + 2 * Q_BLK, KV_DIM), BF16),
            pltpu.VMEM((t, B_DIM), BF16),
            pltpu.VMEM((t + PAST_LEN, KV_DIM), BF16),
            pltpu.VMEM((t + PAST_LEN, KV_DIM), BF16),
        ],
        compiler_params=pltpu.CompilerParams(vmem_limit_bytes=VMEM_LIMIT),
        name="attn_sample",
    )(sink, za, *caches, cos, sin, gq, gk)


def _rwkv_body(*refs, t, nbatch, has_init, want_state, has_prev, ncast):
    refs = list(refs)
    zc_ref, cv_ref, wup_ref, aup_ref, gup_ref = refs[:5]
    pos = 5
    st0_ref = prev_ref = stout_ref = None
    if has_init:
        st0_ref = refs[pos]
        pos += 1
    if has_prev:
        prev_ref = refs[pos]
        pos += 1
    cast_src = refs[pos:pos + ncast]
    pos += ncast
    oc_ref = refs[pos]
    pos += 1
    if want_state:
        stout_ref = refs[pos]
        pos += 1
    cast_dst = refs[pos:pos + ncast]
    pos += ncast
    aa_s, ld_s, b_s, kt_s, bon_s, y_s, st_s = refs[pos:]

    _cast_blocks(cast_src, cast_dst)

    nchunks = t // CHUNK
    ones = _head_ones(C_DIM)
    k_k = cv_ref[0:1]
    k_a = cv_ref[1:2]
    r_k = cv_ref[2:3]
    ln_w = cv_ref[3:4]
    ln_b = cv_ref[4:5]
    w0h = (0.5 * cv_ref[5:6], 0.5 * cv_ref[6:7])
    a0h = (0.5 * cv_ref[7:8], 0.5 * cv_ref[8:9])
    wup_h = ((0.5 * wup_ref[0]).astype(BF16), (0.5 * wup_ref[1]).astype(BF16))
    aup_h = ((0.5 * aup_ref[0]).astype(BF16), (0.5 * aup_ref[1]).astype(BF16))
    c1 = 1.0 - 0.5 * k_a
    c2 = 0.5 * k_a
    half_rate = 0.5 * DECAY_RATE

    for rb in range(nbatch * t // RWKV_ROWS):
        rows = slice(rb * RWKV_ROWS, (rb + 1) * RWKV_ROWS)
        zr = zc_ref[rows, 0:384]
        zk = zc_ref[rows, 384:768]
        zv = zc_ref[rows, 768:1152]
        tw = jnp.tanh(zc_ref[rows, 1152:1216])
        xa = zc_ref[rows, 1216:1280]
        kkr = zk * k_k
        kk_half = kkr * (0.5 * lax.rsqrt(_head_sum(kkr * kkr, ones) + 1e-12))
        aa_s[rows] = -2.0 * kk_half
        kt_sum = None
        for d in range(2):
            th_w = jnp.tanh(w0h[d] + _bdot(tw, wup_h[d]))
            ld_s[d, rows] = th_w * (-half_rate) - half_rate
            th_a = jnp.tanh(a0h[d] + _bdot(xa, aup_h[d]))
            kt = zk * (c1 + c2 * th_a)
            kt_s[d, rows] = kt
            b_s[d, rows] = kk_half * th_a + kk_half
            kt_sum = kt if kt_sum is None else kt_sum + kt
        bon_s[rows] = _head_sum(zr * kt_sum * r_k, ones) * zv
        y_s[rows] = jnp.zeros((RWKV_ROWS, C_DIM), F32)

    ir = lax.broadcasted_iota(jnp.int32, (CHUNK, 128), 0)
    ic = lax.broadcasted_iota(jnp.int32, (CHUNK, 128), 1) & 63
    eye = jnp.where(ir == ic, 1.0, 0.0).astype(F32)
    incl = (ic <= ir, ic >= ir)
    strict = (ic < ir, ic > ir)
    sr = lax.broadcasted_iota(jnp.int32, (CHUNK, CHUNK), 0)
    sc = lax.broadcasted_iota(jnp.int32, (CHUNK, CHUNK), 1)
    tri = tuple(jnp.where(m, 1.0, 0.0).astype(BF16) for m in (sc <= sr, sc >= sr))
    br = lax.broadcasted_iota(jnp.int32, (128, 128), 0)
    bc = lax.broadcasted_iota(jnp.int32, (128, 128), 1)
    diag_blocks = (br < 64) == (bc < 64)
    left_half = lax.broadcasted_iota(jnp.int32, (CHUNK, 128), 1) < 64

    def level_mask(rows, cols, j):
        same_big = lax.shift_right_logical(rows, jnp.int32(j + 1)) == lax.shift_right_logical(cols, jnp.int32(j + 1))
        same_small = lax.shift_right_logical(rows, jnp.int32(j)) == lax.shift_right_logical(cols, jnp.int32(j))
        return same_big & jnp.logical_not(same_small)

    first_level = level_mask(ir, ic, 0)
    level_blocks = [diag_blocks & level_mask(br & 63, bc & 63, j) for j in range(1, CHUNK.bit_length() - 1)]

    def blockdiag(x2):
        x2 = x2.astype(BF16)
        return jnp.where(diag_blocks, jnp.concatenate([x2, x2], axis=0), jnp.zeros((128, 128), BF16))

    npairs = C_HEADS // 2
    for bi in range(nbatch):
        for d in range(2):
            for p in range(npairs):
                if has_init:
                    st_s[bi, d, p] = jnp.concatenate([st0_ref[bi, d, 2 * p], st0_ref[bi, d, 2 * p + 1]], axis=1)
                else:
                    st_s[bi, d, p] = jnp.zeros((HEAD_DIM, 128), F32)

    def chunk(c, carry):
        c = _index32(c)
        items = []
        for bi, d in [(bi, d) for bi in range(nbatch) for d in range(2)]:
            r0 = bi * t + ((nchunks - 1 - c) if d == 1 else c) * CHUNK
            rows = pl.ds(pl.multiple_of(r0, CHUNK), CHUNK)
            ld = ld_s[d, rows]
            cinc = _split_dot_left(tri[d], ld)
            cexc = cinc - ld
            ctot = cinc[0:1] if d == 1 else cinc[CHUNK - 1:CHUNK]
            e_ninc = jnp.exp(-cinc)
            e_rem = jnp.exp(ctot - cinc)
            e_tot = jnp.exp(ctot)
            bb = b_s[d, rows]
            kt = kt_s[d, rows]
            at = (aa_s[rows] * jnp.exp(cexc)).astype(BF16)
            rt = (zc_ref[rows, 0:384] * jnp.exp(cinc)).astype(BF16)
            bt = (bb * e_ninc).astype(BF16)
            ktt = (kt * e_ninc).astype(BF16)
            bh = (bb * e_rem).astype(BF16)
            kh = (kt * e_rem).astype(BF16)
            vm = zc_ref[rows, 768:1152].astype(BF16)
            for p in range(npairs):
                ps = slice(p * 128, (p + 1) * 128)
                items.append(dict(bi=bi, d=d, p=p, rows=rows, e_tot=e_tot[:, ps], at=at[:, ps], rt=rt[:, ps],
                                  bt=bt[:, ps], kt=ktt[:, ps], bh=bh[:, ps], kh=kh[:, ps], vm=vm[:, ps]))
        for it in items:
            it["ar"] = jnp.concatenate([it["at"], it["rt"]], axis=0)
            it["vmd"] = blockdiag(it["vm"])
        for it in items:
            sbk = _bdot_nt(it["ar"], jnp.concatenate([blockdiag(it["bt"]), blockdiag(it["kt"])], axis=0))
            it["sb"] = sbk[:, 0:128]
            it["sk"] = sbk[:, 128:256]
        for it in items:
            d = it["d"]
            it["l"] = jnp.where(strict[d], it["sb"][0:CHUNK], 0.0)
            it["mak"] = jnp.where(strict[d], it["sk"][0:CHUNK], 0.0).astype(BF16)
            it["nr"] = jnp.concatenate([jnp.where(incl[d], it["sb"][CHUNK:], 0.0).astype(BF16),
                                        jnp.where(incl[d], it["sk"][CHUNK:], 0.0).astype(BF16)], axis=1)
            it["tm"] = eye + jnp.where(first_level, it["l"], 0.0)
            lb = it["l"].astype(BF16)
            it["l2"] = jnp.concatenate([lb, lb], axis=0)
        for lvl in level_blocks:
            for it in items:
                it["t1"] = _bdot(it["tm"], jnp.where(lvl, it["l2"], jnp.zeros((128, 128), BF16)))
            for it in items:
                it["tm"] = it["tm"] + _bdot(it["t1"], blockdiag(it["tm"]))
        for it in items:
            it["mv"] = _bdot(it["mak"], it["vmd"])
        for it in items:
            it["tau"] = _bdot(it["tm"], jnp.concatenate([blockdiag(it["at"]), blockdiag(it["mv"])], axis=1))
        for it in items:
            it["s0"] = st_s[it["bi"], it["d"], it["p"]]
            it["x"] = _bdot_nt(jnp.concatenate([it["tau"][:, 0:128].astype(BF16), it["rt"]], axis=0),
                               blockdiag(it["s0"]))
        for it in items:
            it["u"] = (it["x"][0:CHUNK] + it["tau"][:, 128:256]).astype(BF16)
        for it in items:
            it["y"] = it["x"][CHUNK:] + _bdot(it["nr"], jnp.concatenate([blockdiag(it["u"]), it["vmd"]], axis=0))
        for it in items:
            full = _bdot_tn(jnp.concatenate([it["u"], it["vm"]], axis=0),
                            jnp.concatenate([it["bh"], it["kh"]], axis=0))
            st_s[it["bi"], it["d"], it["p"]] = (it["s0"] * it["e_tot"]
                                                + jnp.where(left_half, full[0:64], full[64:128]))
        for i in range(0, len(items), npairs):
            rows = items[i]["rows"]
            y_s[rows] = y_s[rows] + jnp.concatenate([it["y"] for it in items[i:i + npairs]], axis=1)
        return carry

    lax.fori_loop(0, nchunks, chunk, 0)
    if want_state:
        for bi in range(nbatch):
            if has_prev:
                stout_ref[bi, 0] = prev_ref[bi]
            for d in range(2):
                for p in range(npairs):
                    s2 = st_s[bi, d, p]
                    for q in range(2):
                        if has_prev:
                            stout_ref[bi, 1, d, 2 * p + q] = s2[:, q * 64:(q + 1) * 64]
                        else:
                            stout_ref[bi, d, 2 * p + q] = s2[:, q * 64:(q + 1) * 64]

    gup = gup_ref[...].astype(BF16)
    for rb in range(nbatch * t // RWKV_ROWS):
        rows = slice(rb * RWKV_ROWS, (rb + 1) * RWKV_ROWS)
        y = y_s[rows]
        mu = _head_sum(y, ones) * (1.0 / HEAD_DIM)
        yc = y - mu
        var = _head_sum(yc * yc, ones) * (1.0 / HEAD_DIM)
        yn = yc * lax.rsqrt(var + GN_EPS)
        g = _bdot(_sigmoid(zc_ref[rows, 1280:1408]), gup)
        oc_ref[rows] = (yn * ln_w + ln_b + bon_s[rows]) * g


def _rwkv_call(zc, rvec, wup, aup, gup, layer, t, init_state, want_state, prev_state, cast_weights=()):
    nb = zc.shape[0] // t
    nbatch = min(RWKV_BATCH, nb)
    rows = nbatch * t
    has_init = init_state is not None
    has_prev = prev_state is not None
    st_shape = (2, C_HEADS, HEAD_DIM, HEAD_DIM)
    in_specs = [
        pl.BlockSpec((rows, RWKV_COLS), lambda b: (b, 0), pipeline_mode=pl.Buffered(1 if nb == nbatch else 2)),
        _layer_spec((16, C_DIM), layer),
        _layer_spec((2, W_RANK, C_DIM), layer),
        _layer_spec((2, A_RANK, C_DIM), layer),
        _layer_spec((G_RANK, C_DIM), layer),
    ]
    args = [zc, rvec, wup, aup, gup]
    if has_init:
        in_specs.append(pl.BlockSpec((nbatch, None) + st_shape, lambda b: (b, layer, 0, 0, 0, 0)))
        args.append(init_state)
    if has_prev:
        in_specs.append(pl.BlockSpec((nbatch,) + st_shape, lambda b: (b, 0, 0, 0, 0)))
        args.append(prev_state)
    steps = nb // nbatch
    cast_in, cast_shape, cast_out = _cast_specs(cast_weights, layer, steps)
    in_specs.extend(cast_in)
    args.extend(cast_weights)
    out_shape = [jax.ShapeDtypeStruct((nb * t, C_DIM), F32)]
    out_specs = [pl.BlockSpec((rows, C_DIM), lambda b: (b, 0))]
    if want_state and has_prev:
        out_shape.append(jax.ShapeDtypeStruct((nb, DEPTH) + st_shape, F32))
        out_specs.append(pl.BlockSpec((nbatch, DEPTH) + st_shape, lambda b: (b, 0, 0, 0, 0, 0)))
    elif want_state:
        out_shape.append(jax.ShapeDtypeStruct((nb,) + st_shape, F32))
        out_specs.append(pl.BlockSpec((nbatch,) + st_shape, lambda b: (b, 0, 0, 0, 0)))
    out_shape.extend(cast_shape)
    out_specs.extend(cast_out)
    return pl.pallas_call(
        functools.partial(_rwkv_body, t=t, nbatch=nbatch, has_init=has_init, want_state=want_state,
                          has_prev=has_prev, ncast=len(cast_weights)),
        out_shape=tuple(out_shape),
        grid=(steps,),
        in_specs=in_specs,
        out_specs=tuple(out_specs),
        scratch_shapes=[
            pltpu.VMEM((rows, C_DIM), F32),
            pltpu.VMEM((2, rows, C_DIM), F32),
            pltpu.VMEM((2, rows, C_DIM), F32),
            pltpu.VMEM((2, rows, C_DIM), F32),
            pltpu.VMEM((rows, C_DIM), F32),
            pltpu.VMEM((rows, C_DIM), F32),
            pltpu.VMEM((nbatch, 2, C_HEADS // 2, HEAD_DIM, 128), F32),
        ],
        compiler_params=pltpu.CompilerParams(vmem_limit_bytes=VMEM_LIMIT),
        name="rwkv_prompt" if want_state else "rwkv_sample",
    )(*args)


def _post_tile(oa_ref, oc_ref, x_ref, mod, g_ref, wout_ref, wgu_ref, wdn_ref, xo_ref):
    gate1 = mod[:, 2 * D_MODEL:3 * D_MODEL]
    shift2 = mod[:, 3 * D_MODEL:4 * D_MODEL]
    scale2 = mod[:, 4 * D_MODEL:5 * D_MODEL]
    gate2 = mod[:, 5 * D_MODEL:6 * D_MODEL]
    sub = ROW_TILE // ROW_SPLIT
    parts = [slice(i * sub, (i + 1) * sub) for i in range(ROW_SPLIT)]
    o = [jnp.dot(jnp.concatenate([oa_ref[r, :].astype(BF16), oc_ref[r, :].astype(BF16)], axis=1), wout_ref[...],
                 preferred_element_type=F32) for r in parts]
    x1 = [x_ref[r, :] + gate1 * _rms(oi, g_ref[1:2]) for r, oi in zip(parts, o)]
    h2 = [(_rms(xi, g_ref[2:3]) * (1.0 + scale2) + shift2).astype(BF16) for xi in x1]

    def gate_up(step):
        (lo, hi), i = step
        g = jnp.dot(h2[i], wgu_ref[:, lo:hi], preferred_element_type=F32)
        u = jnp.dot(h2[i], wgu_ref[:, FF_DIM + lo:FF_DIM + hi], preferred_element_type=F32)
        return g, u

    steps = [(chunk, i) for chunk in FF_CHUNKS for i in range(ROW_SPLIT)]
    acc = [None] * ROW_SPLIT
    pending = gate_up(steps[0])
    for k, ((lo, hi), i) in enumerate(steps):
        g, u = pending
        if k + 1 < len(steps):
            pending = gate_up(steps[k + 1])
        act = (g * _sigmoid(g) * u).astype(BF16)
        part = jnp.dot(act, wdn_ref[lo:hi, :], preferred_element_type=F32)
        acc[i] = part if acc[i] is None else acc[i] + part
    for r, xi, ai in zip(parts, x1, acc):
        xo_ref[r, :] = xi + gate2 * _rms(ai, g_ref[3:4])


def _post_body(oap_ref, oas_ref, ocp_ref, ocs_ref, xp_ref, xs_ref, mod_ref, g_ref, wout_ref, wgu_ref, wdn_ref,
               xop_ref, xos_ref, *, n_first):
    step = pl.program_id(0)

    @pl.when(step < n_first)
    def _():
        _post_tile(oap_ref, ocp_ref, xp_ref, mod_ref[0:1], g_ref, wout_ref, wgu_ref, wdn_ref, xop_ref)

    @pl.when(step >= n_first)
    def _():
        _post_tile(oas_ref, ocs_ref, xs_ref, _latent_mod_row(mod_ref, step, n_first), g_ref, wout_ref, wgu_ref,
                   wdn_ref, xos_ref)


def _post_call(oatt_p, oatt_s, oc_p, oc_s, xp, xs, mods, gvec, wout_b, wgu_b, wdn_b, layer):
    n_first = xp.shape[0] // ROW_TILE
    n_second = xs.shape[0] // ROW_TILE
    x_first, x_second = _two_group_specs(D_MODEL, n_first)
    return pl.pallas_call(
        functools.partial(_post_body, n_first=n_first),
        out_shape=(jax.ShapeDtypeStruct(xp.shape, F32), jax.ShapeDtypeStruct(xs.shape, F32)),
        grid=(n_first + n_second,),
        in_specs=[
            *_two_group_specs(ATT_OUT, n_first),
            *_two_group_specs(C_DIM, n_first),
            x_first, x_second,
            _layer_spec((MOD_ROWS, MOD_COLS), layer),
            _layer_spec((8, D_MODEL), layer),
            _whole_spec((MIX_DIM, D_MODEL)),
            _whole_spec((D_MODEL, 2 * FF_DIM)),
            _whole_spec((FF_DIM, D_MODEL)),
        ],
        out_specs=(x_first, x_second),
        compiler_params=pltpu.CompilerParams(dimension_semantics=("arbitrary",), vmem_limit_bytes=VMEM_LIMIT),
        name="post_ffn",
    )(oatt_p, oatt_s, oc_p, oc_s, xp, xs, mods, gvec, wout_b, wgu_b, wdn_b)


def _rope_tables(t):
    pos = jnp.arange(t)
    row = (pos // GRID_W).astype(F32)
    col = (pos % GRID_W).astype(F32)
    freqs = ROPE_THETA ** (-jnp.arange(ROPE_PAIRS_AXIS, dtype=F32) / ROPE_PAIRS_AXIS)
    ang = jnp.concatenate([row[:, None] * freqs, col[:, None] * freqs], axis=-1)
    cos = jnp.tile(jnp.cos(ang), (1, 4))
    sin = jnp.sin(ang)
    sin = jnp.tile(jnp.concatenate([-sin, sin], axis=-1), (1, 2))
    return cos, sin


def _to_time_minor(cache):
    b, l, t = cache.shape[:3]
    return jnp.transpose(cache, (0, 1, 3, 4, 2)).reshape(b, l, KV_DIM, t)


def _from_time_minor(slab):
    b, l, _, t = slab.shape
    return jnp.transpose(slab.reshape(b, l, 2, HEAD_DIM, t), (0, 1, 4, 2, 3))


def kernel(x_prompt, x_sample, cache_a_k, cache_a_v, cache_b_k, cache_b_v, state_c, c, c_ctx, w_mod, b_mod, norm_mix_pre, norm_mix_post, norm_ffn_pre, norm_ffn_post, w_in, w_out, a_sink, b_q_norm, b_k_norm, c_w0, c_w_up, c_a0, c_a_up, c_g_up, c_k_k, c_k_a, c_r_k, c_ln_w, c_ln_b, w_gu, w_down):
    cvec = jnp.zeros((MOD_ROWS, D_MODEL), F32).at[0].set(c_ctx).at[1:1 + DEC_BATCH].set(c)
    mods = _mod_call(cvec, w_mod, b_mod)
    cos, sin = _rope_tables(DEC_SEQ)

    gvec = jnp.zeros((DEPTH, 8, D_MODEL), F32)
    gvec = gvec.at[:, 0].set(norm_mix_pre).at[:, 1].set(norm_mix_post)
    gvec = gvec.at[:, 2].set(norm_ffn_pre).at[:, 3].set(norm_ffn_post)
    rvec = jnp.zeros((DEPTH, 16, C_DIM), F32)
    rvec = rvec.at[:, 0].set(c_k_k).at[:, 1].set(c_k_a).at[:, 2].set(c_r_k.reshape(DEPTH, C_DIM))
    rvec = rvec.at[:, 3].set(c_ln_w).at[:, 4].set(c_ln_b)
    rvec = rvec.at[:, 5:7].set(c_w0).at[:, 7:9].set(c_a0)
    gq = jnp.tile(b_q_norm, (1, B_HEADS))[:, None, :]
    gk = jnp.tile(b_k_norm, (1, B_KV_HEADS))[:, None, :]
    caches = tuple(_to_time_minor(t) for t in (cache_a_k, cache_a_v, cache_b_k, cache_b_v))

    xp = x_prompt.reshape(BATCH * SEQ, D_MODEL)
    xs = x_sample.reshape(DEC_BATCH * DEC_SEQ, D_MODEL)
    kv_prev = st_prev = None
    for l in range(DEPTH):
        za_p, zc_p, za_s, zc_s, wout_b, wdn_b = _proj_in_call(xp, xs, mods, gvec, w_in, l, (w_out, w_down))
        oatt_p, *kv_prev = _attn_prompt_call(za_p, a_sink, gq, gk, l, kv_prev)
        if l == 0:
            (kv_prev,) = kv_prev
        oc_p, st_prev, wgu_b = _rwkv_call(zc_p, rvec, c_w_up, c_a_up, c_g_up, l, SEQ, None, True, st_prev,
                                          cast_weights=(w_gu,))
        oatt_s = _attn_sample_call(za_s, a_sink, caches, cos, sin, gq, gk, l)
        (oc_s,) = _rwkv_call(zc_s, rvec, c_w_up, c_a_up, c_g_up, l, DEC_SEQ, state_c, False, None)
        xp, xs = _post_call(oatt_p, oatt_s, oc_p, oc_s, xp, xs, mods, gvec, wout_b, wgu_b, wdn_b, l)

    new_caches = tuple(_from_time_minor(slab) for slab in kv_prev)
    return (xp.reshape(BATCH, SEQ, D_MODEL), xs.reshape(DEC_BATCH, DEC_SEQ, D_MODEL), *new_caches, st_prev)
```

```python
import functools
import math

import jax
import jax.numpy as jnp
from jax import lax
from jax.experimental import pallas as pl
from jax.experimental.pallas import tpu as pltpu

D_MODEL = 1024
BATCH = 32
SEQ = 256
DEPTH = 2
DEC_BATCH = 2
DEC_SEQ = 1024
PAST_LEN = 256
GRID_W = 64
HEAD_DIM = 64
A_HEADS = 4
A_KV_HEADS = 2
B_HEADS = 6
B_KV_HEADS = 2
C_HEADS = 6
A_DIM = A_HEADS * HEAD_DIM
B_DIM = B_HEADS * HEAD_DIM
C_DIM = C_HEADS * HEAD_DIM
MIX_DIM = A_DIM + B_DIM + C_DIM
WINDOW = 128
Q_BLK = 128
W_RANK = 64
A_RANK = 64
G_RANK = 128
FF_DIM = -(-8 * D_MODEL // (3 * 256)) * 256
ROPE_THETA = 10000.0
ROPE_PAIRS_AXIS = HEAD_DIM // 4
NORM_EPS = 1e-6
GN_EPS = 64e-5
NEG_INF = -1e30

KV_DIM = 2 * HEAD_DIM
ATT_COLS = A_DIM + 2 * KV_DIM + B_DIM + 2 * KV_DIM
RWKV_COLS = 3 * C_DIM + W_RANK + A_RANK + G_RANK
IN_COLS = ATT_COLS + RWKV_COLS
ATT_OUT = A_DIM + B_DIM
MOD_COLS = 6 * D_MODEL
MOD_ROWS = 8
N_CACHE = 4
CHUNK = 64
QK_SCALE = HEAD_DIM ** -0.5
DECAY_RATE = math.exp(-0.5)
ROW_TILE = 512
ROW_SPLIT = 2
RWKV_ROWS = 512
RWKV_BATCH = 4
B_QROWS = 256
ATTN_BATCH = 2
MXU_TILE = 256
FF_CHUNKS = ((0, 4 * MXU_TILE), (4 * MXU_TILE, 8 * MXU_TILE), (8 * MXU_TILE, FF_DIM))
assert FF_DIM % MXU_TILE == 0
VMEM_LIMIT = 56 * 1024 * 1024

F32 = jnp.float32
BF16 = jnp.bfloat16

assert DEPTH == 2


def _bdot(a, b):
    return jnp.dot(a.astype(BF16), b.astype(BF16), preferred_element_type=F32)


def _bdot_nt(a, b):
    return lax.dot_general(a.astype(BF16), b.astype(BF16), (((1,), (1,)), ((), ())), preferred_element_type=F32)


def _bdot_tn(a, b):
    return lax.dot_general(a.astype(BF16), b.astype(BF16), (((0,), (0,)), ((), ())), preferred_element_type=F32)


def _split(x):
    hi = x.astype(BF16)
    lo = (x - hi.astype(F32)).astype(BF16)
    return hi, lo


def _split_dot_left(m, x):
    hi, lo = _split(x)
    return jnp.dot(m, hi, preferred_element_type=F32) + jnp.dot(m, lo, preferred_element_type=F32)


def _head_sum(x, ones):
    return jnp.dot(x.astype(BF16), ones, preferred_element_type=F32)


def _sigmoid(x):
    return 0.5 * jnp.tanh(0.5 * x) + 0.5


def _head_ones(n):
    r = lax.shift_right_logical(lax.broadcasted_iota(jnp.int32, (n, n), 0), jnp.int32(6))
    c = lax.shift_right_logical(lax.broadcasted_iota(jnp.int32, (n, n), 1), jnp.int32(6))
    return jnp.where(r == c, 1.0, 0.0).astype(BF16)


def _rms(x, g):
    ms = jnp.mean(x * x, axis=-1, keepdims=True)
    return x * lax.rsqrt(ms + NORM_EPS) * g


def _head_rms(x, g, ones):
    ms = _head_sum(x * x, ones) * (1.0 / HEAD_DIM)
    return x * lax.rsqrt(ms + NORM_EPS) * g


def _rope(x, cos, sin):
    t = x.shape[0]
    lane = lax.broadcasted_iota(jnp.int32, (t, 128), 1)
    first = (lane & 32) == 0
    outs = []
    for i in range(x.shape[1] // 128):
        xs = x[:, i * 128:(i + 1) * 128]
        swapped = jnp.where(first, pltpu.roll(xs, 96, axis=1), pltpu.roll(xs, 32, axis=1))
        outs.append(xs * cos + swapped * sin)
    return outs[0] if len(outs) == 1 else jnp.concatenate(outs, axis=1)


def _softmax_pv(scores, values, sink=None):
    m = scores[0].max(axis=-1, keepdims=True)
    for s in scores[1:]:
        m = jnp.maximum(m, s.max(axis=-1, keepdims=True))
    if sink is not None:
        m = jnp.maximum(m, sink)
    denom = None
    acc = None
    for s, v in zip(scores, values):
        p = jnp.exp(s - m)
        d = p.sum(axis=-1, keepdims=True)
        o = jnp.dot(p.astype(BF16), v, preferred_element_type=F32)
        denom = d if denom is None else denom + d
        acc = o if acc is None else acc + o
    if sink is not None:
        denom = denom + jnp.exp(sink - m)
    return acc / denom


def _softmax_pv_t(scores_t, values_t, sink=None):
    m = scores_t[0].max(axis=0, keepdims=True)
    for s in scores_t[1:]:
        m = jnp.maximum(m, s.max(axis=0, keepdims=True))
    if sink is not None:
        m = jnp.maximum(m, sink)
    denom = None
    acc = None
    for s, vt in zip(scores_t, values_t):
        p = jnp.exp(s - m)
        d = p.sum(axis=0, keepdims=True)
        o = jnp.dot(vt, p.astype(BF16), preferred_element_type=F32)
        denom = d if denom is None else denom + d
        acc = o if acc is None else acc + o
    if sink is not None:
        denom = denom + jnp.exp(sink - m)
    return acc / denom


def _attend_heads(score_fns, finish_fns, ahead=1):
    n = len(score_fns)
    pending = [fn() for fn in score_fns[:ahead]]
    outs = []
    for i in range(n):
        if i + ahead < n:
            pending.append(score_fns[i + ahead]())
        outs.append(finish_fns[i](pending.pop(0)))
    return outs


def _index32(i):
    return jnp.asarray(i, jnp.int32)


def _layer_spec(shape, layer):
    zeros = (0,) * len(shape)
    return pl.BlockSpec((None,) + shape, lambda *_: (layer,) + zeros, pipeline_mode=pl.Buffered(1))


def _whole_spec(shape):
    zeros = (0,) * len(shape)
    return pl.BlockSpec(shape, lambda *_: zeros, pipeline_mode=pl.Buffered(1))


def _mod_body(c_ref, w_ref, b_ref, o_ref):
    cv = c_ref[...]
    s = cv * _sigmoid(cv)
    o_ref[0] = _bdot(s, w_ref[0]) + b_ref[0]


def _mod_call(cvec, w_mod, b_mod):
    bn = MOD_COLS // 4
    return pl.pallas_call(
        _mod_body,
        out_shape=jax.ShapeDtypeStruct((DEPTH, MOD_ROWS, MOD_COLS), F32),
        grid=(DEPTH, MOD_COLS // bn),
        in_specs=[
            pl.BlockSpec((MOD_ROWS, D_MODEL), lambda l, j: (0, 0)),
            pl.BlockSpec((1, D_MODEL, bn), lambda l, j: (l, 0, j)),
            pl.BlockSpec((1, 1, bn), lambda l, j: (l, 0, j)),
        ],
        out_specs=pl.BlockSpec((1, MOD_ROWS, bn), lambda l, j: (l, 0, j)),
        compiler_params=pltpu.CompilerParams(vmem_limit_bytes=VMEM_LIMIT),
        name="mod_vectors",
    )(cvec, w_mod, b_mod.reshape(DEPTH, 1, MOD_COLS))


def _cast_blocks(srcs, dsts):
    for src, dst in zip(srcs, dsts):
        dst[...] = src[...].astype(BF16)


def _cast_specs(weights, layer, steps):
    def block(i):
        return jnp.minimum(i, steps - 1)

    in_specs = [pl.BlockSpec((None, w.shape[1] // steps, w.shape[2]), lambda i: (layer, block(i), 0)) for w in weights]
    out_shape = [jax.ShapeDtypeStruct(w.shape[1:], BF16) for w in weights]
    out_specs = [pl.BlockSpec((w.shape[1] // steps, w.shape[2]), lambda i: (block(i), 0)) for w in weights]
    return in_specs, out_shape, out_specs


def _two_group_specs(cols, n_first):
    first = pl.BlockSpec((ROW_TILE, cols), lambda i: (jnp.minimum(i, n_first - 1), 0))
    second = pl.BlockSpec((ROW_TILE, cols), lambda i: (jnp.maximum(i - n_first, 0), 0))
    return first, second


def _latent_mod_row(mod_ref, step, n_first):
    row = 1 + (step - n_first) // (DEC_SEQ // ROW_TILE)
    return mod_ref[pl.ds(row, 1), :]


def _proj_in_body(xp_ref, xs_ref, mod_ref, g_ref, w_ref, *rest, n_first, ncast):
    cast_src = rest[:ncast]
    zap_ref, zcp_ref, zas_ref, zcs_ref = rest[ncast:ncast + 4]
    cast_dst = rest[ncast + 4:2 * ncast + 4]
    wb_s = rest[2 * ncast + 4]
    step = pl.program_id(0)

    @pl.when(step == 0)
    def _():
        wb_s[...] = w_ref[...].astype(BF16)

    def project(x_ref, mod, za_ref, zc_ref):
        shift1 = mod[:, 0:D_MODEL]
        scale1 = mod[:, D_MODEL:2 * D_MODEL]
        h = (_rms(x_ref[...], g_ref[0:1]) * (1.0 + scale1) + shift1).astype(BF16)
        z = jnp.dot(h, wb_s[...], preferred_element_type=F32)
        za_ref[...] = z[:, 0:ATT_COLS]
        zc_ref[...] = z[:, ATT_COLS:IN_COLS]

    @pl.when(step < n_first)
    def _():
        _cast_blocks(cast_src, cast_dst)
        project(xp_ref, mod_ref[0:1], zap_ref, zcp_ref)

    @pl.when(step >= n_first)
    def _():
        project(xs_ref, _latent_mod_row(mod_ref, step, n_first), zas_ref, zcs_ref)


def _proj_in_call(xp, xs, mods, gvec, w_in, layer, cast_weights):
    n_first = xp.shape[0] // ROW_TILE
    n_second = xs.shape[0] // ROW_TILE
    cast_in, cast_shape, cast_out = _cast_specs(cast_weights, layer, n_first)
    x_first, x_second = _two_group_specs(D_MODEL, n_first)
    za_first, za_second = _two_group_specs(ATT_COLS, n_first)
    zc_first, zc_second = _two_group_specs(RWKV_COLS, n_first)
    return pl.pallas_call(
        functools.partial(_proj_in_body, n_first=n_first, ncast=len(cast_weights)),
        out_shape=(jax.ShapeDtypeStruct((xp.shape[0], ATT_COLS), F32), jax.ShapeDtypeStruct((xp.shape[0], RWKV_COLS), F32),
                   jax.ShapeDtypeStruct((xs.shape[0], ATT_COLS), F32), jax.ShapeDtypeStruct((xs.shape[0], RWKV_COLS), F32),
                   *cast_shape),
        grid=(n_first + n_second,),
        in_specs=[
            x_first, x_second,
            _layer_spec((MOD_ROWS, MOD_COLS), layer),
            _layer_spec((8, D_MODEL), layer),
            _layer_spec((D_MODEL, IN_COLS), layer),
            *cast_in,
        ],
        out_specs=(za_first, zc_first, za_second, zc_second, *cast_out),
        scratch_shapes=[pltpu.VMEM((D_MODEL, IN_COLS), BF16)],
        compiler_params=pltpu.CompilerParams(dimension_semantics=("arbitrary",), vmem_limit_bytes=VMEM_LIMIT),
        name="proj_in",
    )(xp, xs, mods, gvec, w_in, *cast_weights)


def _attn_prompt_body(*refs, layer, nseq):
    sink_ref, za_ref, gq_ref, gk_ref = refs[:4]
    if layer == 0:
        o_ref, kv_ref = refs[4:]
    else:
        prev_ref, o_ref = refs[4:6]
        cache_refs = refs[6:]
    ones_q = _head_ones(B_DIM)
    ones_k = _head_ones(KV_DIM)
    score_fns, finish_fns = [], []
    for si in range(nseq):
        rows = slice(si * SEQ, (si + 1) * SEQ)
        aq = (za_ref[rows, 0:A_DIM] * QK_SCALE).astype(BF16)
        ak = za_ref[rows, 256:384]
        bq = _head_rms(za_ref[rows, 512:896], gq_ref[...], ones_q)
        bk = _head_rms(za_ref[rows, 896:1024], gk_ref[...], ones_k)
        bqb = (bq * QK_SCALE).astype(BF16)
        akb = ak.astype(BF16)
        bkb = bk.astype(BF16)
        pieces_t = (ak.T, za_ref[rows, 384:512].T, bk.T, za_ref[rows, 1024:1152].T)
        avt = pieces_t[1].astype(BF16)
        bvt = pieces_t[3].astype(BF16)
        for j, piece_t in enumerate(pieces_t):
            if layer == 0:
                kv_ref[si, j] = piece_t
            else:
                cache_refs[j][si, 0] = prev_ref[si, j]
                cache_refs[j][si, 1] = piece_t

        for h in range(A_HEADS):
            kv = h // (A_HEADS // A_KV_HEADS)
            hs = slice(kv * 64, (kv + 1) * 64)
            score_fns.append(lambda h=h, hs=hs, akb=akb, aq=aq: [_bdot_nt(akb[:, hs], aq[:, h * 64:(h + 1) * 64])])
            finish_fns.append(lambda sc, h=h, hs=hs, avt=avt: _softmax_pv_t(sc, [avt[hs]], sink_ref[layer, h]))
        for h in range(B_HEADS):
            kv = h // (B_HEADS // B_KV_HEADS)
            hs = slice(kv * 64, (kv + 1) * 64)
            score_fns.append(lambda h=h, hs=hs, bkb=bkb, bqb=bqb: [_bdot_nt(bkb[:, hs], bqb[:, h * 64:(h + 1) * 64])])
            finish_fns.append(lambda sc, hs=hs, bvt=bvt: _softmax_pv_t(sc, [bvt[hs]]))
    outs = _attend_heads(score_fns, finish_fns, ahead=len(score_fns))
    heads = A_HEADS + B_HEADS
    for si in range(nseq):
        o_ref[si * SEQ:(si + 1) * SEQ, :] = jnp.concatenate(outs[si * heads:(si + 1) * heads], axis=0).T


def _attn_prompt_call(za, sink, gq, gk, layer, prev_kv):
    nb = za.shape[0] // SEQ
    nseq = ATTN_BATCH
    in_specs = [
        pl.BlockSpec(memory_space=pltpu.SMEM),
        pl.BlockSpec((nseq * SEQ, ATT_COLS), lambda b: (b, 0)),
        _layer_spec((1, B_DIM), layer),
        _layer_spec((1, KV_DIM), layer),
    ]
    args = [sink, za, gq, gk]
    out_shape = [jax.ShapeDtypeStruct((nb * SEQ, ATT_OUT), F32)]
    out_specs = [pl.BlockSpec((nseq * SEQ, ATT_OUT), lambda b: (b, 0))]
    slab_spec = pl.BlockSpec((nseq, N_CACHE, KV_DIM, SEQ), lambda b: (b, 0, 0, 0))
    if layer == 0:
        out_shape.append(jax.ShapeDtypeStruct((nb, N_CACHE, KV_DIM, SEQ), F32))
        out_specs.append(slab_spec)
    else:
        in_specs.append(slab_spec)
        args.append(prev_kv)
        for _ in range(N_CACHE):
            out_shape.append(jax.ShapeDtypeStruct((nb, DEPTH, KV_DIM, SEQ), F32))
            out_specs.append(pl.BlockSpec((nseq, DEPTH, KV_DIM, SEQ), lambda b: (b, 0, 0, 0)))
    return pl.pallas_call(
        functools.partial(_attn_prompt_body, layer=layer, nseq=nseq),
        out_shape=tuple(out_shape),
        grid=(nb // nseq,),
        in_specs=in_specs,
        out_specs=tuple(out_specs),
        compiler_params=pltpu.CompilerParams(vmem_limit_bytes=VMEM_LIMIT),
        name="attn_prompt",
    )(*args)


def _attn_sample_body(sink_ref, za_ref, cak_ref, cav_ref, cbk_ref, cbv_ref, cos_ref, sin_ref, gq_ref, gk_ref,
                      o_ref, qa_s, ka_s, va_s, qb_s, kb_s, vb_s, *, layer):
    t = DEC_SEQ
    cos = cos_ref[...]
    sin = sin_ref[...]
    qa_s[...] = (_rope(za_ref[:, 0:A_DIM], cos, sin) * QK_SCALE).astype(BF16)
    zpad = jnp.zeros((Q_BLK, KV_DIM), BF16)
    ka_s[0:Q_BLK] = zpad
    ka_s[Q_BLK + t:] = zpad
    va_s[0:Q_BLK] = zpad
    va_s[Q_BLK + t:] = zpad
    ka_s[Q_BLK:Q_BLK + t] = _rope(za_ref[:, 256:384], cos, sin).astype(BF16)
    va_s[Q_BLK:Q_BLK + t] = za_ref[:, 384:512].astype(BF16)
    bq = _head_rms(za_ref[:, 512:896], gq_ref[...], _head_ones(B_DIM))
    qb_s[...] = (_rope(bq, cos, sin) * QK_SCALE).astype(BF16)
    bk = _head_rms(za_ref[:, 896:1024], gk_ref[...], _head_ones(KV_DIM))
    kb_s[0:PAST_LEN] = cbk_ref[0].T.astype(BF16)
    kb_s[PAST_LEN:] = _rope(bk, cos, sin).astype(BF16)
    vb_s[0:PAST_LEN] = cbv_ref[0].T.astype(BF16)
    vb_s[PAST_LEN:] = za_ref[:, 1024:1152].astype(BF16)

    kca = cak_ref[0].T.astype(BF16)
    vca = cav_ref[0].T.astype(BF16)

    ga = A_HEADS // A_KV_HEADS
    gb = B_HEADS // B_KV_HEADS
    qpos = lax.broadcasted_iota(jnp.int32, (ga * Q_BLK, 3 * Q_BLK), 0) & (Q_BLK - 1)
    kpos = lax.broadcasted_iota(jnp.int32, (ga * Q_BLK, 3 * Q_BLK), 1) - Q_BLK
    near = jnp.abs(kpos - qpos) <= WINDOW
    head_of_row = lax.shift_right_logical(lax.broadcasted_iota(jnp.int32, (ga * Q_BLK, 1), 0),
                                          jnp.int32(Q_BLK.bit_length() - 1))
    sinks = []
    for kv in range(A_KV_HEADS):
        col = jnp.full((ga * Q_BLK, 1), sink_ref[layer, kv * ga], F32)
        for g in range(1, ga):
            col = jnp.where(head_of_row == g, sink_ref[layer, kv * ga + g], col)
        sinks.append(col)

    def stack_heads(q, kv, group):
        return jnp.concatenate([q[:, (kv * group + g) * 64:(kv * group + g + 1) * 64] for g in range(group)], axis=0)

    def unstack_heads(outs, group, rows):
        return jnp.concatenate([o[g * rows:(g + 1) * rows] for o in outs for g in range(group)], axis=1)

    def a_block(n, carry):
        r0 = pl.multiple_of(_index32(n) * Q_BLK, Q_BLK)
        q = qa_s[pl.ds(r0, Q_BLK), :]
        kl = ka_s[pl.ds(r0, 3 * Q_BLK), :]
        vl = va_s[pl.ds(r0, 3 * Q_BLK), :]
        kabs = kpos + r0
        mask = near & (kabs >= 0) & (kabs < t)
        score_fns, finish_fns = [], []
        for kv in range(A_KV_HEADS):
            hs = slice(kv * 64, (kv + 1) * 64)

            def scores(kv=kv, hs=hs):
                qs = stack_heads(q, kv, ga)
                return [jnp.where(mask, _bdot_nt(qs, kl[:, hs]), NEG_INF), _bdot_nt(qs, kca[:, hs])]

            score_fns.append(scores)
            finish_fns.append(lambda sc, kv=kv, hs=hs: _softmax_pv(sc, [vl[:, hs], vca[:, hs]], sinks[kv]))
        outs = _attend_heads(score_fns, finish_fns, ahead=A_KV_HEADS)
        o_ref[pl.ds(r0, Q_BLK), 0:A_DIM] = unstack_heads(outs, ga, Q_BLK)
        return carry

    lax.fori_loop(0, t // Q_BLK, a_block, 0)

    def b_block(n, carry):
        r0 = pl.multiple_of(_index32(n) * B_QROWS, B_QROWS)
        q = qb_s[pl.ds(r0, B_QROWS), :]
        score_fns, finish_fns = [], []
        for kv in range(B_KV_HEADS):
            hs = slice(kv * 64, (kv + 1) * 64)
            score_fns.append(lambda kv=kv, hs=hs: [_bdot_nt(stack_heads(q, kv, gb), kb_s[:, hs])])
            finish_fns.append(lambda sc, hs=hs: _softmax_pv(sc, [vb_s[:, hs]]))
        outs = _attend_heads(score_fns, finish_fns)
        o_ref[pl.ds(r0, B_QROWS), A_DIM:ATT_OUT] = unstack_heads(outs, gb, B_QROWS)
        return carry

    lax.fori_loop(0, t // B_QROWS, b_block, 0)


def _attn_sample_call(za, sink, caches, cos, sin, gq, gk, layer):
    nb = za.shape[0] // DEC_SEQ
    t = DEC_SEQ
    cache_spec = pl.BlockSpec((1, None, KV_DIM, PAST_LEN), lambda b: (b, layer, 0, 0))
    return pl.pallas_call(
        functools.partial(_attn_sample_body, layer=layer),
        out_shape=jax.ShapeDtypeStruct((nb * t, ATT_OUT), F32),
        grid=(nb,),
        in_specs=[
            pl.BlockSpec(memory_space=pltpu.SMEM),
            pl.BlockSpec((t, ATT_COLS), lambda b: (b, 0)),
            cache_spec, cache_spec, cache_spec, cache_spec,
            pl.BlockSpec((t, 128), lambda b: (0, 0)),
            pl.BlockSpec((t, 128), lambda b: (0, 0)),
            _layer_spec((1, B_DIM), layer),
            _layer_spec((1, KV_DIM), layer),
        ],
        out_specs=pl.BlockSpec((t, ATT_OUT), lambda b: (b, 0)),
        scratch_shapes=[
            pltpu.VMEM((t, A_DIM), BF16),
            pltpu.VMEM((t + 2 * Q_BLK, KV_DIM), BF16),
            pltpu.VMEM((t + 2 * Q_BLK, KV_DIM), BF16),
            pltpu.VMEM((t, B_DIM), BF16),
            pltpu.VMEM((t + PAST_LEN, KV_DIM), BF16),
            pltpu.VMEM((t + PAST_LEN, KV_DIM), BF16),
        ],
        compiler_params=pltpu.CompilerParams(vmem_limit_bytes=VMEM_LIMIT),
        name="attn_sample",
    )(sink, za, *caches, cos, sin, gq, gk)


def _rwkv_body(*refs, t, nbatch, has_init, want_state, has_prev, ncast):
    refs = list(refs)
    zc_ref, cv_ref, wup_ref, aup_ref, gup_ref = refs[:5]
    pos = 5
    st0_ref = prev_ref = stout_ref = None
    if has_init:
        st0_ref = refs[pos]
        pos += 1
    if has_prev:
        prev_ref = refs[pos]
        pos += 1
    cast_src = refs[pos:pos + ncast]
    pos += ncast
    oc_ref = refs[pos]
    pos += 1
    if want_state:
        stout_ref = refs[pos]
        pos += 1
    cast_dst = refs[pos:pos + ncast]
    pos += ncast
    aa_s, ld_s, b_s, kt_s, bon_s, y_s, st_s = refs[pos:]

    _cast_blocks(cast_src, cast_dst)

    nchunks = t // CHUNK
    ones = _head_ones(C_DIM)
    k_k = cv_ref[0:1]
    k_a = cv_ref[1:2]
    r_k = cv_ref[2:3]
    ln_w = cv_ref[3:4]
    ln_b = cv_ref[4:5]
    w0h = (0.5 * cv_ref[5:6], 0.5 * cv_ref[6:7])
    a0h = (0.5 * cv_ref[7:8], 0.5 * cv_ref[8:9])
    wup_h = ((0.5 * wup_ref[0]).astype(BF16), (0.5 * wup_ref[1]).astype(BF16))
    aup_h = ((0.5 * aup_ref[0]).astype(BF16), (0.5 * aup_ref[1]).astype(BF16))
    c1 = 1.0 - 0.5 * k_a
    c2 = 0.5 * k_a
    half_rate = 0.5 * DECAY_RATE

    for rb in range(nbatch * t // RWKV_ROWS):
        rows = slice(rb * RWKV_ROWS, (rb + 1) * RWKV_ROWS)
        zr = zc_ref[rows, 0:384]
        zk = zc_ref[rows, 384:768]
        zv = zc_ref[rows, 768:1152]
        tw = jnp.tanh(zc_ref[rows, 1152:1216])
        xa = zc_ref[rows, 1216:1280]
        kkr = zk * k_k
        kk_half = kkr * (0.5 * lax.rsqrt(_head_sum(kkr * kkr, ones) + 1e-12))
        aa_s[rows] = -2.0 * kk_half
        kt_sum = None
        for d in range(2):
            th_w = jnp.tanh(w0h[d] + _bdot(tw, wup_h[d]))
            ld_s[d, rows] = th_w * (-half_rate) - half_rate
            th_a = jnp.tanh(a0h[d] + _bdot(xa, aup_h[d]))
            kt = zk * (c1 + c2 * th_a)
            kt_s[d, rows] = kt
            b_s[d, rows] = kk_half * th_a + kk_half
            kt_sum = kt if kt_sum is None else kt_sum + kt
        bon_s[rows] = _head_sum(zr * kt_sum * r_k, ones) * zv
        y_s[rows] = jnp.zeros((RWKV_ROWS, C_DIM), F32)

    ir = lax.broadcasted_iota(jnp.int32, (CHUNK, 128), 0)
    ic = lax.broadcasted_iota(jnp.int32, (CHUNK, 128), 1) & 63
    eye = jnp.where(ir == ic, 1.0, 0.0).astype(F32)
    incl = (ic <= ir, ic >= ir)
    strict = (ic < ir, ic > ir)
    sr = lax.broadcasted_iota(jnp.int32, (CHUNK, CHUNK), 0)
    sc = lax.broadcasted_iota(jnp.int32, (CHUNK, CHUNK), 1)
    tri = tuple(jnp.where(m, 1.0, 0.0).astype(BF16) for m in (sc <= sr, sc >= sr))
    br = lax.broadcasted_iota(jnp.int32, (128, 128), 0)
    bc = lax.broadcasted_iota(jnp.int32, (128, 128), 1)
    diag_blocks = (br < 64) == (bc < 64)
    left_half = lax.broadcasted_iota(jnp.int32, (CHUNK, 128), 1) < 64

    def level_mask(rows, cols, j):
        same_big = lax.shift_right_logical(rows, jnp.int32(j + 1)) == lax.shift_right_logical(cols, jnp.int32(j + 1))
        same_small = lax.shift_right_logical(rows, jnp.int32(j)) == lax.shift_right_logical(cols, jnp.int32(j))
        return same_big & jnp.logical_not(same_small)

    first_level = level_mask(ir, ic, 0)
    level_blocks = [diag_blocks & level_mask(br & 63, bc & 63, j) for j in range(1, CHUNK.bit_length() - 1)]

    def blockdiag(x2):
        x2 = x2.astype(BF16)
        return jnp.where(diag_blocks, jnp.concatenate([x2, x2], axis=0), jnp.zeros((128, 128), BF16))

    npairs = C_HEADS // 2
    for bi in range(nbatch):
        for d in range(2):
            for p in range(npairs):
                if has_init:
                    st_s[bi, d, p] = jnp.concatenate([st0_ref[bi, d, 2 * p], st0_ref[bi, d, 2 * p + 1]], axis=1)
                else:
                    st_s[bi, d, p] = jnp.zeros((HEAD_DIM, 128), F32)

    def chunk(c, carry):
        c = _index32(c)
        items = []
        for bi, d in [(bi, d) for bi in range(nbatch) for d in range(2)]:
            r0 = bi * t + ((nchunks - 1 - c) if d == 1 else c) * CHUNK
            rows = pl.ds(pl.multiple_of(r0, CHUNK), CHUNK)
            ld = ld_s[d, rows]
            cinc = _split_dot_left(tri[d], ld)
            cexc = cinc - ld
            ctot = cinc[0:1] if d == 1 else cinc[CHUNK - 1:CHUNK]
            e_ninc = jnp.exp(-cinc)
            e_rem = jnp.exp(ctot - cinc)
            e_tot = jnp.exp(ctot)
            bb = b_s[d, rows]
            kt = kt_s[d, rows]
            at = (aa_s[rows] * jnp.exp(cexc)).astype(BF16)
            rt = (zc_ref[rows, 0:384] * jnp.exp(cinc)).astype(BF16)
            bt = (bb * e_ninc).astype(BF16)
            ktt = (kt * e_ninc).astype(BF16)
            bh = (bb * e_rem).astype(BF16)
            kh = (kt * e_rem).astype(BF16)
            vm = zc_ref[rows, 768:1152].astype(BF16)
            for p in range(npairs):
                ps = slice(p * 128, (p + 1) * 128)
                items.append(dict(bi=bi, d=d, p=p, rows=rows, e_tot=e_tot[:, ps], at=at[:, ps], rt=rt[:, ps],
                                  bt=bt[:, ps], kt=ktt[:, ps], bh=bh[:, ps], kh=kh[:, ps], vm=vm[:, ps]))
        for it in items:
            it["ar"] = jnp.concatenate([it["at"], it["rt"]], axis=0)
            it["vmd"] = blockdiag(it["vm"])
        for it in items:
            sbk = _bdot_nt(it["ar"], jnp.concatenate([blockdiag(it["bt"]), blockdiag(it["kt"])], axis=0))
            it["sb"] = sbk[:, 0:128]
            it["sk"] = sbk[:, 128:256]
        for it in items:
            d = it["d"]
            it["l"] = jnp.where(strict[d], it["sb"][0:CHUNK], 0.0)
            it["mak"] = jnp.where(strict[d], it["sk"][0:CHUNK], 0.0).astype(BF16)
            it["nr"] = jnp.concatenate([jnp.where(incl[d], it["sb"][CHUNK:], 0.0).astype(BF16),
                                        jnp.where(incl[d], it["sk"][CHUNK:], 0.0).astype(BF16)], axis=1)
            it["tm"] = eye + jnp.where(first_level, it["l"], 0.0)
            lb = it["l"].astype(BF16)
            it["l2"] = jnp.concatenate([lb, lb], axis=0)
        for lvl in level_blocks:
            for it in items:
                it["t1"] = _bdot(it["tm"], jnp.where(lvl, it["l2"], jnp.zeros((128, 128), BF16)))
            for it in items:
                it["tm"] = it["tm"] + _bdot(it["t1"], blockdiag(it["tm"]))
        for it in items:
            it["mv"] = _bdot(it["mak"], it["vmd"])
        for it in items:
            it["tau"] = _bdot(it["tm"], jnp.concatenate([blockdiag(it["at"]), blockdiag(it["mv"])], axis=1))
        for it in items:
            it["s0"] = st_s[it["bi"], it["d"], it["p"]]
            it["x"] = _bdot_nt(jnp.concatenate([it["tau"][:, 0:128].astype(BF16), it["rt"]], axis=0),
                               blockdiag(it["s0"]))
        for it in items:
            it["u"] = (it["x"][0:CHUNK] + it["tau"][:, 128:256]).astype(BF16)
        for it in items:
            it["y"] = it["x"][CHUNK:] + _bdot(it["nr"], jnp.concatenate([blockdiag(it["u"]), it["vmd"]], axis=0))
        for it in items:
            full = _bdot_tn(jnp.concatenate([it["u"], it["vm"]], axis=0),
                            jnp.concatenate([it["bh"], it["kh"]], axis=0))
            st_s[it["bi"], it["d"], it["p"]] = (it["s0"] * it["e_tot"]
                                                + jnp.where(left_half, full[0:64], full[64:128]))
        for i in range(0, len(items), npairs):
            rows = items[i]["rows"]
            y_s[rows] = y_s[rows] + jnp.concatenate([it["y"] for it in items[i:i + npairs]], axis=1)
        return carry

    lax.fori_loop(0, nchunks, chunk, 0)
    if want_state:
        for bi in range(nbatch):
            if has_prev:
                stout_ref[bi, 0] = prev_ref[bi]
            for d in range(2):
                for p in range(npairs):
                    s2 = st_s[bi, d, p]
                    for q in range(2):
                        if has_prev:
                            stout_ref[bi, 1, d, 2 * p + q] = s2[:, q * 64:(q + 1) * 64]
                        else:
                            stout_ref[bi, d, 2 * p + q] = s2[:, q * 64:(q + 1) * 64]

    gup = gup_ref[...].astype(BF16)
    for rb in range(nbatch * t // RWKV_ROWS):
        rows = slice(rb * RWKV_ROWS, (rb + 1) * RWKV_ROWS)
        y = y_s[rows]
        mu = _head_sum(y, ones) * (1.0 / HEAD_DIM)
        yc = y - mu
        var = _head_sum(yc * yc, ones) * (1.0 / HEAD_DIM)
        yn = yc * lax.rsqrt(var + GN_EPS)
        g = _bdot(_sigmoid(zc_ref[rows, 1280:1408]), gup)
        oc_ref[rows] = (yn * ln_w + ln_b + bon_s[rows]) * g


def _rwkv_call(zc, rvec, wup, aup, gup, layer, t, init_state, want_state, prev_state, cast_weights=()):
    nb = zc.shape[0] // t
    nbatch = min(RWKV_BATCH, nb)
    rows = nbatch * t
    has_init = init_state is not None
    has_prev = prev_state is not None
    st_shape = (2, C_HEADS, HEAD_DIM, HEAD_DIM)
    in_specs = [
        pl.BlockSpec((rows, RWKV_COLS), lambda b: (b, 0), pipeline_mode=pl.Buffered(1 if nb == nbatch else 2)),
        _layer_spec((16, C_DIM), layer),
        _layer_spec((2, W_RANK, C_DIM), layer),
        _layer_spec((2, A_RANK, C_DIM), layer),
        _layer_spec((G_RANK, C_DIM), layer),
    ]
    args = [zc, rvec, wup, aup, gup]
    if has_init:
        in_specs.append(pl.BlockSpec((nbatch, None) + st_shape, lambda b: (b, layer, 0, 0, 0, 0)))
        args.append(init_state)
    if has_prev:
        in_specs.append(pl.BlockSpec((nbatch,) + st_shape, lambda b: (b, 0, 0, 0, 0)))
        args.append(prev_state)
    steps = nb // nbatch
    cast_in, cast_shape, cast_out = _cast_specs(cast_weights, layer, steps)
    in_specs.extend(cast_in)
    args.extend(cast_weights)
    out_shape = [jax.ShapeDtypeStruct((nb * t, C_DIM), F32)]
    out_specs = [pl.BlockSpec((rows, C_DIM), lambda b: (b, 0))]
    if want_state and has_prev:
        out_shape.append(jax.ShapeDtypeStruct((nb, DEPTH) + st_shape, F32))
        out_specs.append(pl.BlockSpec((nbatch, DEPTH) + st_shape, lambda b: (b, 0, 0, 0, 0, 0)))
    elif want_state:
        out_shape.append(jax.ShapeDtypeStruct((nb,) + st_shape, F32))
        out_specs.append(pl.BlockSpec((nbatch,) + st_shape, lambda b: (b, 0, 0, 0, 0)))
    out_shape.extend(cast_shape)
    out_specs.extend(cast_out)
    return pl.pallas_call(
        functools.partial(_rwkv_body, t=t, nbatch=nbatch, has_init=has_init, want_state=want_state,
                          has_prev=has_prev, ncast=len(cast_weights)),
        out_shape=tuple(out_shape),
        grid=(steps,),
        in_specs=in_specs,
        out_specs=tuple(out_specs),
        scratch_shapes=[
            pltpu.VMEM((rows, C_DIM), F32),
            pltpu.VMEM((2, rows, C_DIM), F32),
            pltpu.VMEM((2, rows, C_DIM), F32),
            pltpu.VMEM((2, rows, C_DIM), F32),
            pltpu.VMEM((rows, C_DIM), F32),
            pltpu.VMEM((rows, C_DIM), F32),
            pltpu.VMEM((nbatch, 2, C_HEADS // 2, HEAD_DIM, 128), F32),
        ],
        compiler_params=pltpu.CompilerParams(vmem_limit_bytes=VMEM_LIMIT),
        name="rwkv_prompt" if want_state else "rwkv_sample",
    )(*args)


def _post_tile(oa_ref, oc_ref, x_ref, mod, g_ref, wout_ref, wgu_ref, wdn_ref, xo_ref):
    gate1 = mod[:, 2 * D_MODEL:3 * D_MODEL]
    shift2 = mod[:, 3 * D_MODEL:4 * D_MODEL]
    scale2 = mod[:, 4 * D_MODEL:5 * D_MODEL]
    gate2 = mod[:, 5 * D_MODEL:6 * D_MODEL]
    sub = ROW_TILE // ROW_SPLIT
    parts = [slice(i * sub, (i + 1) * sub) for i in range(ROW_SPLIT)]
    o = [jnp.dot(jnp.concatenate([oa_ref[r, :].astype(BF16), oc_ref[r, :].astype(BF16)], axis=1), wout_ref[...],
                 preferred_element_type=F32) for r in parts]
    x1 = [x_ref[r, :] + gate1 * _rms(oi, g_ref[1:2]) for r, oi in zip(parts, o)]
    h2 = [(_rms(xi, g_ref[2:3]) * (1.0 + scale2) + shift2).astype(BF16) for xi in x1]

    def gate_up(step):
        (lo, hi), i = step
        g = jnp.dot(h2[i], wgu_ref[:, lo:hi], preferred_element_type=F32)
        u = jnp.dot(h2[i], wgu_ref[:, FF_DIM + lo:FF_DIM + hi], preferred_element_type=F32)
        return g, u

    steps = [(chunk, i) for chunk in FF_CHUNKS for i in range(ROW_SPLIT)]
    acc = [None] * ROW_SPLIT
    pending = gate_up(steps[0])
    for k, ((lo, hi), i) in enumerate(steps):
        g, u = pending
        if k + 1 < len(steps):
            pending = gate_up(steps[k + 1])
        act = (g * _sigmoid(g) * u).astype(BF16)
        part = jnp.dot(act, wdn_ref[lo:hi, :], preferred_element_type=F32)
        acc[i] = part if acc[i] is None else acc[i] + part
    for r, xi, ai in zip(parts, x1, acc):
        xo_ref[r, :] = xi + gate2 * _rms(ai, g_ref[3:4])


def _post_body(oap_ref, oas_ref, ocp_ref, ocs_ref, xp_ref, xs_ref, mod_ref, g_ref, wout_ref, wgu_ref, wdn_ref,
               xop_ref, xos_ref, *, n_first):
    step = pl.program_id(0)

    @pl.when(step < n_first)
    def _():
        _post_tile(oap_ref, ocp_ref, xp_ref, mod_ref[0:1], g_ref, wout_ref, wgu_ref, wdn_ref, xop_ref)

    @pl.when(step >= n_first)
    def _():
        _post_tile(oas_ref, ocs_ref, xs_ref, _latent_mod_row(mod_ref, step, n_first), g_ref, wout_ref, wgu_ref,
                   wdn_ref, xos_ref)


def _post_call(oatt_p, oatt_s, oc_p, oc_s, xp, xs, mods, gvec, wout_b, wgu_b, wdn_b, layer):
    n_first = xp.shape[0] // ROW_TILE
    n_second = xs.shape[0] // ROW_TILE
    x_first, x_second = _two_group_specs(D_MODEL, n_first)
    return pl.pallas_call(
        functools.partial(_post_body, n_first=n_first),
        out_shape=(jax.ShapeDtypeStruct(xp.shape, F32), jax.ShapeDtypeStruct(xs.shape, F32)),
        grid=(n_first + n_second,),
        in_specs=[
            *_two_group_specs(ATT_OUT, n_first),
            *_two_group_specs(C_DIM, n_first),
            x_first, x_second,
            _layer_spec((MOD_ROWS, MOD_COLS), layer),
            _layer_spec((8, D_MODEL), layer),
            _whole_spec((MIX_DIM, D_MODEL)),
            _whole_spec((D_MODEL, 2 * FF_DIM)),
            _whole_spec((FF_DIM, D_MODEL)),
        ],
        out_specs=(x_first, x_second),
        compiler_params=pltpu.CompilerParams(dimension_semantics=("arbitrary",), vmem_limit_bytes=VMEM_LIMIT),
        name="post_ffn",
    )(oatt_p, oatt_s, oc_p, oc_s, xp, xs, mods, gvec, wout_b, wgu_b, wdn_b)


def _rope_tables(t):
    pos = jnp.arange(t)
    row = (pos // GRID_W).astype(F32)
    col = (pos % GRID_W).astype(F32)
    freqs = ROPE_THETA ** (-jnp.arange(ROPE_PAIRS_AXIS, dtype=F32) / ROPE_PAIRS_AXIS)
    ang = jnp.concatenate([row[:, None] * freqs, col[:, None] * freqs], axis=-1)
    cos = jnp.tile(jnp.cos(ang), (1, 4))
    sin = jnp.sin(ang)
    sin = jnp.tile(jnp.concatenate([-sin, sin], axis=-1), (1, 2))
    return cos, sin


def _to_time_minor(cache):
    b, l, t = cache.shape[:3]
    return jnp.transpose(cache, (0, 1, 3, 4, 2)).reshape(b, l, KV_DIM, t)


def _from_time_minor(slab):
    b, l, _, t = slab.shape
    return jnp.transpose(slab.reshape(b, l, 2, HEAD_DIM, t), (0, 1, 4, 2, 3))


def kernel(x_prompt, x_sample, cache_a_k, cache_a_v, cache_b_k, cache_b_v, state_c, c, c_ctx, w_mod, b_mod, norm_mix_pre, norm_mix_post, norm_ffn_pre, norm_ffn_post, w_in, w_out, a_sink, b_q_norm, b_k_norm, c_w0, c_w_up, c_a0, c_a_up, c_g_up, c_k_k, c_k_a, c_r_k, c_ln_w, c_ln_b, w_gu, w_down):
    cvec = jnp.zeros((MOD_ROWS, D_MODEL), F32).at[0].set(c_ctx).at[1:1 + DEC_BATCH].set(c)
    mods = _mod_call(cvec, w_mod, b_mod)
    cos, sin = _rope_tables(DEC_SEQ)

    gvec = jnp.zeros((DEPTH, 8, D_MODEL), F32)
    gvec = gvec.at[:, 0].set(norm_mix_pre).at[:, 1].set(norm_mix_post)
    gvec = gvec.at[:, 2].set(norm_ffn_pre).at[:, 3].set(norm_ffn_post)
    rvec = jnp.zeros((DEPTH, 16, C_DIM), F32)
    rvec = rvec.at[:, 0].set(c_k_k).at[:, 1].set(c_k_a).at[:, 2].set(c_r_k.reshape(DEPTH, C_DIM))
    rvec = rvec.at[:, 3].set(c_ln_w).at[:, 4].set(c_ln_b)
    rvec = rvec.at[:, 5:7].set(c_w0).at[:, 7:9].set(c_a0)
    gq = jnp.tile(b_q_norm, (1, B_HEADS))[:, None, :]
    gk = jnp.tile(b_k_norm, (1, B_KV_HEADS))[:, None, :]
    caches = tuple(_to_time_minor(t) for t in (cache_a_k, cache_a_v, cache_b_k, cache_b_v))

    xp = x_prompt.reshape(BATCH * SEQ, D_MODEL)
    xs = x_sample.reshape(DEC_BATCH * DEC_SEQ, D_MODEL)
    kv_prev = st_prev = None
    for l in range(DEPTH):
        za_p, zc_p, za_s, zc_s, wout_b, wdn_b = _proj_in_call(xp, xs, mods, gvec, w_in, l, (w_out, w_down))
        oatt_p, *kv_prev = _attn_prompt_call(za_p, a_sink, gq, gk, l, kv_prev)
        if l == 0:
            (kv_prev,) = kv_prev
        oc_p, st_prev, wgu_b = _rwkv_call(zc_p, rvec, c_w_up, c_a_up, c_g_up, l, SEQ, None, True, st_prev,
                                          cast_weights=(w_gu,))
        oatt_s = _attn_sample_call(za_s, a_sink, caches, cos, sin, gq, gk, l)
        (oc_s,) = _rwkv_call(zc_s, rvec, c_w_up, c_a_up, c_g_up, l, DEC_SEQ, state_c, False, None)
        xp, xs = _post_call(oatt_p, oatt_s, oc_p, oc_s, xp, xs, mods, gvec, wout_b, wgu_b, wdn_b, l)

    new_caches = tuple(_from_time_minor(slab) for slab in kv_prev)
    return (xp.reshape(BATCH, SEQ, D_MODEL), xs.reshape(DEC_BATCH, DEC_SEQ, D_MODEL), *new_caches, st_prev)
```

```python
import functools
import math

import jax
import jax.numpy as jnp
from jax import lax
from jax.experimental import pallas as pl
from jax.experimental.pallas import tpu as pltpu

D_MODEL = 1024
BATCH = 32
SEQ = 256
DEPTH = 2
DEC_BATCH = 2
DEC_SEQ = 1024
PAST_LEN = 256
GRID_W = 64
HEAD_DIM = 64
A_HEADS = 4
A_KV_HEADS = 2
B_HEADS = 6
B_KV_HEADS = 2
C_HEADS = 6
A_DIM = A_HEADS * HEAD_DIM
B_DIM = B_HEADS * HEAD_DIM
C_DIM = C_HEADS * HEAD_DIM
MIX_DIM = A_DIM + B_DIM + C_DIM
WINDOW = 128
Q_BLK = 128
W_RANK = 64
A_RANK = 64
G_RANK = 128
FF_DIM = -(-8 * D_MODEL // (3 * 256)) * 256
ROPE_THETA = 10000.0
ROPE_PAIRS_AXIS = HEAD_DIM // 4
NORM_EPS = 1e-6
GN_EPS = 64e-5
NEG_INF = -1e30

LANES = 128
KV_DIM = 2 * HEAD_DIM
assert KV_DIM == LANES


def _col_ranges(widths):
    edges = [0]
    for w in widths:
        edges.append(edges[-1] + w)
    return tuple(slice(lo, hi) for lo, hi in zip(edges[:-1], edges[1:])), edges[-1]


(A_Q, A_K, A_V, B_Q, B_K, B_V), ATT_COLS = _col_ranges((A_DIM, KV_DIM, KV_DIM, B_DIM, KV_DIM, KV_DIM))
(C_R, C_K, C_V, C_W, C_A, C_G), RWKV_COLS = _col_ranges((C_DIM, C_DIM, C_DIM, W_RANK, A_RANK, G_RANK))
IN_COLS = ATT_COLS + RWKV_COLS
ATT_OUT = A_DIM + B_DIM
MOD_COLS = 6 * D_MODEL
MOD_ROWS = 8
N_CACHE = 4
CHUNK = 64
assert CHUNK == HEAD_DIM
QK_SCALE = HEAD_DIM ** -0.5
DECAY_RATE = math.exp(-0.5)
ROW_TILE = 512
ROW_SPLIT = 2
RWKV_ROWS = 512
RWKV_BATCH = 4
B_QROWS = 256
ATTN_BATCH = 2
MXU_TILE = 256
FF_CHUNKS = ((0, 4 * MXU_TILE), (4 * MXU_TILE, 8 * MXU_TILE), (8 * MXU_TILE, FF_DIM))
assert FF_DIM % MXU_TILE == 0
VMEM_LIMIT = 56 * 1024 * 1024

F32 = jnp.float32
BF16 = jnp.bfloat16

assert DEPTH == 2


def _bdot(a, b):
    return jnp.dot(a.astype(BF16), b.astype(BF16), preferred_element_type=F32)


def _bdot_nt(a, b):
    return lax.dot_general(a.astype(BF16), b.astype(BF16), (((1,), (1,)), ((), ())), preferred_element_type=F32)


def _bdot_tn(a, b):
    return lax.dot_general(a.astype(BF16), b.astype(BF16), (((0,), (0,)), ((), ())), preferred_element_type=F32)


def _split(x):
    hi = x.astype(BF16)
    lo = (x - hi.astype(F32)).astype(BF16)
    return hi, lo


def _split_dot_left(m, x):
    hi, lo = _split(x)
    return jnp.dot(m, hi, preferred_element_type=F32) + jnp.dot(m, lo, preferred_element_type=F32)


def _head_sum(x, ones):
    return jnp.dot(x.astype(BF16), ones, preferred_element_type=F32)


def _sigmoid(x):
    return 0.5 * jnp.tanh(0.5 * x) + 0.5


def _head_ones(n):
    r = lax.shift_right_logical(lax.broadcasted_iota(jnp.int32, (n, n), 0), jnp.int32(HEAD_DIM.bit_length() - 1))
    c = lax.shift_right_logical(lax.broadcasted_iota(jnp.int32, (n, n), 1), jnp.int32(HEAD_DIM.bit_length() - 1))
    return jnp.where(r == c, 1.0, 0.0).astype(BF16)


def _rms(x, g):
    ms = jnp.mean(x * x, axis=-1, keepdims=True)
    return x * lax.rsqrt(ms + NORM_EPS) * g


def _head_rms(x, g, ones):
    ms = _head_sum(x * x, ones) * (1.0 / HEAD_DIM)
    return x * lax.rsqrt(ms + NORM_EPS) * g


def _rope(x, cos, sin):
    t = x.shape[0]
    lane = lax.broadcasted_iota(jnp.int32, (t, LANES), 1)
    first = (lane & (HEAD_DIM // 2)) == 0
    outs = []
    for i in range(x.shape[1] // LANES):
        xs = x[:, i * LANES:(i + 1) * LANES]
        swapped = jnp.where(first, pltpu.roll(xs, LANES - HEAD_DIM // 2, axis=1), pltpu.roll(xs, HEAD_DIM // 2, axis=1))
        outs.append(xs * cos + swapped * sin)
    return outs[0] if len(outs) == 1 else jnp.concatenate(outs, axis=1)


def _softmax_pv(scores, values, sink=None):
    m = scores[0].max(axis=-1, keepdims=True)
    for s in scores[1:]:
        m = jnp.maximum(m, s.max(axis=-1, keepdims=True))
    if sink is not None:
        m = jnp.maximum(m, sink)
    denom = None
    acc = None
    for s, v in zip(scores, values):
        p = jnp.exp(s - m)
        d = p.sum(axis=-1, keepdims=True)
        o = jnp.dot(p.astype(BF16), v, preferred_element_type=F32)
        denom = d if denom is None else denom + d
        acc = o if acc is None else acc + o
    if sink is not None:
        denom = denom + jnp.exp(sink - m)
    return acc / denom


def _softmax_pv_t(scores_t, values_t, sink=None):
    m = scores_t[0].max(axis=0, keepdims=True)
    for s in scores_t[1:]:
        m = jnp.maximum(m, s.max(axis=0, keepdims=True))
    if sink is not None:
        m = jnp.maximum(m, sink)
    denom = None
    acc = None
    for s, vt in zip(scores_t, values_t):
        p = jnp.exp(s - m)
        d = p.sum(axis=0, keepdims=True)
        o = jnp.dot(vt, p.astype(BF16), preferred_element_type=F32)
        denom = d if denom is None else denom + d
        acc = o if acc is None else acc + o
    if sink is not None:
        denom = denom + jnp.exp(sink - m)
    return acc / denom


def _attend_heads(score_fns, finish_fns, ahead=1):
    n = len(score_fns)
    pending = [fn() for fn in score_fns[:ahead]]
    outs = []
    for i in range(n):
        if i + ahead < n:
            pending.append(score_fns[i + ahead]())
        outs.append(finish_fns[i](pending.pop(0)))
    return outs


def _index32(i):
    return jnp.asarray(i, jnp.int32)


def _layer_spec(shape, layer):
    zeros = (0,) * len(shape)
    return pl.BlockSpec((None,) + shape, lambda *_: (layer,) + zeros, pipeline_mode=pl.Buffered(1))


def _whole_spec(shape):
    zeros = (0,) * len(shape)
    return pl.BlockSpec(shape, lambda *_: zeros, pipeline_mode=pl.Buffered(1))


def _mod_body(c_ref, w_ref, b_ref, o_ref):
    cv = c_ref[...]
    s = cv * _sigmoid(cv)
    o_ref[0] = _bdot(s, w_ref[0]) + b_ref[0]


def _mod_call(cvec, w_mod, b_mod):
    bn = MOD_COLS // 4
    return pl.pallas_call(
        _mod_body,
        out_shape=jax.ShapeDtypeStruct((DEPTH, MOD_ROWS, MOD_COLS), F32),
        grid=(DEPTH, MOD_COLS // bn),
        in_specs=[
            pl.BlockSpec((MOD_ROWS, D_MODEL), lambda l, j: (0, 0)),
            pl.BlockSpec((1, D_MODEL, bn), lambda l, j: (l, 0, j)),
            pl.BlockSpec((1, 1, bn), lambda l, j: (l, 0, j)),
        ],
        out_specs=pl.BlockSpec((1, MOD_ROWS, bn), lambda l, j: (l, 0, j)),
        compiler_params=pltpu.CompilerParams(vmem_limit_bytes=VMEM_LIMIT),
        name="mod_vectors",
    )(cvec, w_mod, b_mod.reshape(DEPTH, 1, MOD_COLS))


def _cast_blocks(srcs, dsts):
    for src, dst in zip(srcs, dsts):
        dst[...] = src[...].astype(BF16)


def _cast_specs(weights, layer, steps):
    def block(i):
        return jnp.minimum(i, steps - 1)

    in_specs = [pl.BlockSpec((None, w.shape[1] // steps, w.shape[2]), lambda i: (layer, block(i), 0)) for w in weights]
    out_shape = [jax.ShapeDtypeStruct(w.shape[1:], BF16) for w in weights]
    out_specs = [pl.BlockSpec((w.shape[1] // steps, w.shape[2]), lambda i: (block(i), 0)) for w in weights]
    return in_specs, out_shape, out_specs


def _two_group_specs(cols, n_first):
    first = pl.BlockSpec((ROW_TILE, cols), lambda i: (jnp.minimum(i, n_first - 1), 0))
    second = pl.BlockSpec((ROW_TILE, cols), lambda i: (jnp.maximum(i - n_first, 0), 0))
    return first, second


def _latent_mod_row(mod_ref, step, n_first):
    row = 1 + (step - n_first) // (DEC_SEQ // ROW_TILE)
    return mod_ref[pl.ds(row, 1), :]


def _proj_in_body(xp_ref, xs_ref, mod_ref, g_ref, w_ref, *rest, n_first, ncast):
    cast_src = rest[:ncast]
    zap_ref, zcp_ref, zas_ref, zcs_ref = rest[ncast:ncast + 4]
    cast_dst = rest[ncast + 4:2 * ncast + 4]
    wb_s = rest[2 * ncast + 4]
    step = pl.program_id(0)

    @pl.when(step == 0)
    def _():
        wb_s[...] = w_ref[...].astype(BF16)

    def project(x_ref, mod, za_ref, zc_ref):
        shift1 = mod[:, 0:D_MODEL]
        scale1 = mod[:, D_MODEL:2 * D_MODEL]
        h = (_rms(x_ref[...], g_ref[0:1]) * (1.0 + scale1) + shift1).astype(BF16)
        z = jnp.dot(h, wb_s[...], preferred_element_type=F32)
        za_ref[...] = z[:, 0:ATT_COLS]
        zc_ref[...] = z[:, ATT_COLS:IN_COLS]

    @pl.when(step < n_first)
    def _():
        _cast_blocks(cast_src, cast_dst)
        project(xp_ref, mod_ref[0:1], zap_ref, zcp_ref)

    @pl.when(step >= n_first)
    def _():
        project(xs_ref, _latent_mod_row(mod_ref, step, n_first), zas_ref, zcs_ref)


def _proj_in_call(xp, xs, mods, gvec, w_in, layer, cast_weights):
    n_first = xp.shape[0] // ROW_TILE
    n_second = xs.shape[0] // ROW_TILE
    cast_in, cast_shape, cast_out = _cast_specs(cast_weights, layer, n_first)
    x_first, x_second = _two_group_specs(D_MODEL, n_first)
    za_first, za_second = _two_group_specs(ATT_COLS, n_first)
    zc_first, zc_second = _two_group_specs(RWKV_COLS, n_first)
    return pl.pallas_call(
        functools.partial(_proj_in_body, n_first=n_first, ncast=len(cast_weights)),
        out_shape=(jax.ShapeDtypeStruct((xp.shape[0], ATT_COLS), F32), jax.ShapeDtypeStruct((xp.shape[0], RWKV_COLS), F32),
                   jax.ShapeDtypeStruct((xs.shape[0], ATT_COLS), F32), jax.ShapeDtypeStruct((xs.shape[0], RWKV_COLS), F32),
                   *cast_shape),
        grid=(n_first + n_second,),
        in_specs=[
            x_first, x_second,
            _layer_spec((MOD_ROWS, MOD_COLS), layer),
            _layer_spec((8, D_MODEL), layer),
            _layer_spec((D_MODEL, IN_COLS), layer),
            *cast_in,
        ],
        out_specs=(za_first, zc_first, za_second, zc_second, *cast_out),
        scratch_shapes=[pltpu.VMEM((D_MODEL, IN_COLS), BF16)],
        compiler_params=pltpu.CompilerParams(dimension_semantics=("arbitrary",), vmem_limit_bytes=VMEM_LIMIT),
        name="proj_in",
    )(xp, xs, mods, gvec, w_in, *cast_weights)


def _attn_prompt_body(*refs, layer, nseq):
    sink_ref, za_ref, gq_ref, gk_ref = refs[:4]
    if layer == 0:
        o_ref, kv_ref = refs[4:]
    else:
        prev_ref, o_ref = refs[4:6]
        cache_refs = refs[6:]
    ones_q = _head_ones(B_DIM)
    ones_k = _head_ones(KV_DIM)
    score_fns, finish_fns = [], []
    for si in range(nseq):
        rows = slice(si * SEQ, (si + 1) * SEQ)
        aq = (za_ref[rows, A_Q] * QK_SCALE).astype(BF16)
        ak = za_ref[rows, A_K]
        bq = _head_rms(za_ref[rows, B_Q], gq_ref[...], ones_q)
        bk = _head_rms(za_ref[rows, B_K], gk_ref[...], ones_k)
        bqb = (bq * QK_SCALE).astype(BF16)
        akb = ak.astype(BF16)
        bkb = bk.astype(BF16)
        pieces_t = (ak.T, za_ref[rows, A_V].T, bk.T, za_ref[rows, B_V].T)
        avt = pieces_t[1].astype(BF16)
        bvt = pieces_t[3].astype(BF16)
        for j, piece_t in enumerate(pieces_t):
            if layer == 0:
                kv_ref[si, j] = piece_t
            else:
                cache_refs[j][si, 0] = prev_ref[si, j]
                cache_refs[j][si, 1] = piece_t

        for h in range(A_HEADS):
            kv = h // (A_HEADS // A_KV_HEADS)
            hs = slice(kv * HEAD_DIM, (kv + 1) * HEAD_DIM)
            score_fns.append(lambda h=h, hs=hs, akb=akb, aq=aq: [_bdot_nt(akb[:, hs], aq[:, h * HEAD_DIM:(h + 1) * HEAD_DIM])])
            finish_fns.append(lambda sc, h=h, hs=hs, avt=avt: _softmax_pv_t(sc, [avt[hs]], sink_ref[layer, h]))
        for h in range(B_HEADS):
            kv = h // (B_HEADS // B_KV_HEADS)
            hs = slice(kv * HEAD_DIM, (kv + 1) * HEAD_DIM)
            score_fns.append(lambda h=h, hs=hs, bkb=bkb, bqb=bqb: [_bdot_nt(bkb[:, hs], bqb[:, h * HEAD_DIM:(h + 1) * HEAD_DIM])])
            finish_fns.append(lambda sc, hs=hs, bvt=bvt: _softmax_pv_t(sc, [bvt[hs]]))
    outs = _attend_heads(score_fns, finish_fns, ahead=len(score_fns))
    heads = A_HEADS + B_HEADS
    for si in range(nseq):
        o_ref[si * SEQ:(si + 1) * SEQ, :] = jnp.concatenate(outs[si * heads:(si + 1) * heads], axis=0).T


def _attn_prompt_call(za, sink, gq, gk, layer, prev_kv):
    nb = za.shape[0] // SEQ
    nseq = ATTN_BATCH
    in_specs = [
        pl.BlockSpec(memory_space=pltpu.SMEM),
        pl.BlockSpec((nseq * SEQ, ATT_COLS), lambda b: (b, 0)),
        _layer_spec((1, B_DIM), layer),
        _layer_spec((1, KV_DIM), layer),
    ]
    args = [sink, za, gq, gk]
    out_shape = [jax.ShapeDtypeStruct((nb * SEQ, ATT_OUT), F32)]
    out_specs = [pl.BlockSpec((nseq * SEQ, ATT_OUT), lambda b: (b, 0))]
    slab_spec = pl.BlockSpec((nseq, N_CACHE, KV_DIM, SEQ), lambda b: (b, 0, 0, 0))
    if layer == 0:
        out_shape.append(jax.ShapeDtypeStruct((nb, N_CACHE, KV_DIM, SEQ), F32))
        out_specs.append(slab_spec)
    else:
        in_specs.append(slab_spec)
        args.append(prev_kv)
        for _ in range(N_CACHE):
            out_shape.append(jax.ShapeDtypeStruct((nb, DEPTH, KV_DIM, SEQ), F32))
            out_specs.append(pl.BlockSpec((nseq, DEPTH, KV_DIM, SEQ), lambda b: (b, 0, 0, 0)))
    return pl.pallas_call(
        functools.partial(_attn_prompt_body, layer=layer, nseq=nseq),
        out_shape=tuple(out_shape),
        grid=(nb // nseq,),
        in_specs=in_specs,
        out_specs=tuple(out_specs),
        compiler_params=pltpu.CompilerParams(vmem_limit_bytes=VMEM_LIMIT),
        name="attn_prompt",
    )(*args)


def _attn_sample_body(sink_ref, za_ref, cak_ref, cav_ref, cbk_ref, cbv_ref, cos_ref, sin_ref, gq_ref, gk_ref,
                      o_ref, qa_s, ka_s, va_s, qb_s, kb_s, vb_s, *, layer):
    t = DEC_SEQ
    cos = cos_ref[...]
    sin = sin_ref[...]
    qa_s[...] = (_rope(za_ref[:, A_Q], cos, sin) * QK_SCALE).astype(BF16)
    zpad = jnp.zeros((Q_BLK, KV_DIM), BF16)
    ka_s[0:Q_BLK] = zpad
    ka_s[Q_BLK + t:] = zpad
    va_s[0:Q_BLK] = zpad
    va_s[Q_BLK + t:] = zpad
    ka_s[Q_BLK:Q_BLK + t] = _rope(za_ref[:, A_K], cos, sin).astype(BF16)
    va_s[Q_BLK:Q_BLK + t] = za_ref[:, A_V].astype(BF16)
    bq = _head_rms(za_ref[:, B_Q], gq_ref[...], _head_ones(B_DIM))
    qb_s[...] = (_rope(bq, cos, sin) * QK_SCALE).astype(BF16)
    bk = _head_rms(za_ref[:, B_K], gk_ref[...], _head_ones(KV_DIM))
    kb_s[0:PAST_LEN] = cbk_ref[0].T.astype(BF16)
    kb_s[PAST_LEN:] = _rope(bk, cos, sin).astype(BF16)
    vb_s[0:PAST_LEN] = cbv_ref[0].T.astype(BF16)
    vb_s[PAST_LEN:] = za_ref[:, B_V].astype(BF16)

    kca = cak_ref[0].T.astype(BF16)
    vca = cav_ref[0].T.astype(BF16)

    ga = A_HEADS // A_KV_HEADS
    gb = B_HEADS // B_KV_HEADS
    qpos = lax.broadcasted_iota(jnp.int32, (ga * Q_BLK, 3 * Q_BLK), 0) & (Q_BLK - 1)
    kpos = lax.broadcasted_iota(jnp.int32, (ga * Q_BLK, 3 * Q_BLK), 1) - Q_BLK
    near = jnp.abs(kpos - qpos) <= WINDOW
    head_of_row = lax.shift_right_logical(lax.broadcasted_iota(jnp.int32, (ga * Q_BLK, 1), 0),
                                          jnp.int32(Q_BLK.bit_length() - 1))
    sinks = []
    for kv in range(A_KV_HEADS):
        col = jnp.full((ga * Q_BLK, 1), sink_ref[layer, kv * ga], F32)
        for g in range(1, ga):
            col = jnp.where(head_of_row == g, sink_ref[layer, kv * ga + g], col)
        sinks.append(col)

    def stack_heads(q, kv, group):
        return jnp.concatenate([q[:, (kv * group + g) * HEAD_DIM:(kv * group + g + 1) * HEAD_DIM] for g in range(group)], axis=0)

    def unstack_heads(outs, group, rows):
        return jnp.concatenate([o[g * rows:(g + 1) * rows] for o in outs for g in range(group)], axis=1)

    def a_block(n, carry):
        r0 = pl.multiple_of(_index32(n) * Q_BLK, Q_BLK)
        q = qa_s[pl.ds(r0, Q_BLK), :]
        kl = ka_s[pl.ds(r0, 3 * Q_BLK), :]
        vl = va_s[pl.ds(r0, 3 * Q_BLK), :]
        kabs = kpos + r0
        mask = near & (kabs >= 0) & (kabs < t)
        score_fns, finish_fns = [], []
        for kv in range(A_KV_HEADS):
            hs = slice(kv * HEAD_DIM, (kv + 1) * HEAD_DIM)

            def scores(kv=kv, hs=hs):
                qs = stack_heads(q, kv, ga)
                return [jnp.where(mask, _bdot_nt(qs, kl[:, hs]), NEG_INF), _bdot_nt(qs, kca[:, hs])]

            score_fns.append(scores)
            finish_fns.append(lambda sc, kv=kv, hs=hs: _softmax_pv(sc, [vl[:, hs], vca[:, hs]], sinks[kv]))
        outs = _attend_heads(score_fns, finish_fns, ahead=A_KV_HEADS)
        o_ref[pl.ds(r0, Q_BLK), 0:A_DIM] = unstack_heads(outs, ga, Q_BLK)
        return carry

    lax.fori_loop(0, t // Q_BLK, a_block, 0)

    def b_block(n, carry):
        r0 = pl.multiple_of(_index32(n) * B_QROWS, B_QROWS)
        q = qb_s[pl.ds(r0, B_QROWS), :]
        score_fns, finish_fns = [], []
        for kv in range(B_KV_HEADS):
            hs = slice(kv * HEAD_DIM, (kv + 1) * HEAD_DIM)
            score_fns.append(lambda kv=kv, hs=hs: [_bdot_nt(stack_heads(q, kv, gb), kb_s[:, hs])])
            finish_fns.append(lambda sc, hs=hs: _softmax_pv(sc, [vb_s[:, hs]]))
        outs = _attend_heads(score_fns, finish_fns)
        o_ref[pl.ds(r0, B_QROWS), A_DIM:ATT_OUT] = unstack_heads(outs, gb, B_QROWS)
        return carry

    lax.fori_loop(0, t // B_QROWS, b_block, 0)


def _attn_sample_call(za, sink, caches, cos, sin, gq, gk, layer):
    nb = za.shape[0] // DEC_SEQ
    t = DEC_SEQ
    cache_spec = pl.BlockSpec((1, None, KV_DIM, PAST_LEN), lambda b: (b, layer, 0, 0))
    return pl.pallas_call(
        functools.partial(_attn_sample_body, layer=layer),
        out_shape=jax.ShapeDtypeStruct((nb * t, ATT_OUT), F32),
        grid=(nb,),
        in_specs=[
            pl.BlockSpec(memory_space=pltpu.SMEM),
            pl.BlockSpec((t, ATT_COLS), lambda b: (b, 0)),
            cache_spec, cache_spec, cache_spec, cache_spec,
            pl.BlockSpec((t, LANES), lambda b: (0, 0)),
            pl.BlockSpec((t, LANES), lambda b: (0, 0)),
            _layer_spec((1, B_DIM), layer),
            _layer_spec((1, KV_DIM), layer),
        ],
        out_specs=pl.BlockSpec((t, ATT_OUT), lambda b: (b, 0)),
        scratch_shapes=[
            pltpu.VMEM((t, A_DIM), BF16),
            pltpu.VMEM((t + 2 * Q_BLK, KV_DIM), BF16),
            pltpu.VMEM((t + 2 * Q_BLK, KV_DIM), BF16),
            pltpu.VMEM((t, B_DIM), BF16),
            pltpu.VMEM((t + PAST_LEN, KV_DIM), BF16),
            pltpu.VMEM((t + PAST_LEN, KV_DIM), BF16),
        ],
        compiler_params=pltpu.CompilerParams(vmem_limit_bytes=VMEM_LIMIT),
        name="attn_sample",
    )(sink, za, *caches, cos, sin, gq, gk)


def _rwkv_body(*refs, t, nbatch, has_init, want_state, has_prev, ncast):
    refs = list(refs)
    zc_ref, cv_ref, wup_ref, aup_ref, gup_ref = refs[:5]
    pos = 5
    st0_ref = prev_ref = stout_ref = None
    if has_init:
        st0_ref = refs[pos]
        pos += 1
    if has_prev:
        prev_ref = refs[pos]
        pos += 1
    cast_src = refs[pos:pos + ncast]
    pos += ncast
    oc_ref = refs[pos]
    pos += 1
    if want_state:
        stout_ref = refs[pos]
        pos += 1
    cast_dst = refs[pos:pos + ncast]
    pos += ncast
    aa_s, ld_s, b_s, kt_s, bon_s, y_s, st_s = refs[pos:]

    _cast_blocks(cast_src, cast_dst)

    nchunks = t // CHUNK
    ones = _head_ones(C_DIM)
    k_k = cv_ref[0:1]
    k_a = cv_ref[1:2]
    r_k = cv_ref[2:3]
    ln_w = cv_ref[3:4]
    ln_b = cv_ref[4:5]
    w0h = (0.5 * cv_ref[5:6], 0.5 * cv_ref[6:7])
    a0h = (0.5 * cv_ref[7:8], 0.5 * cv_ref[8:9])
    wup_h = ((0.5 * wup_ref[0]).astype(BF16), (0.5 * wup_ref[1]).astype(BF16))
    aup_h = ((0.5 * aup_ref[0]).astype(BF16), (0.5 * aup_ref[1]).astype(BF16))
    c1 = 1.0 - 0.5 * k_a
    c2 = 0.5 * k_a
    half_rate = 0.5 * DECAY_RATE

    for rb in range(nbatch * t // RWKV_ROWS):
        rows = slice(rb * RWKV_ROWS, (rb + 1) * RWKV_ROWS)
        zr = zc_ref[rows, C_R]
        zk = zc_ref[rows, C_K]
        zv = zc_ref[rows, C_V]
        tw = jnp.tanh(zc_ref[rows, C_W])
        xa = zc_ref[rows, C_A]
        kkr = zk * k_k
        kk_half = kkr * (0.5 * lax.rsqrt(_head_sum(kkr * kkr, ones) + 1e-12))
        aa_s[rows] = -2.0 * kk_half
        kt_sum = None
        for d in range(2):
            th_w = jnp.tanh(w0h[d] + _bdot(tw, wup_h[d]))
            ld_s[d, rows] = th_w * (-half_rate) - half_rate
            th_a = jnp.tanh(a0h[d] + _bdot(xa, aup_h[d]))
            kt = zk * (c1 + c2 * th_a)
            kt_s[d, rows] = kt
            b_s[d, rows] = kk_half * th_a + kk_half
            kt_sum = kt if kt_sum is None else kt_sum + kt
        bon_s[rows] = _head_sum(zr * kt_sum * r_k, ones) * zv
        y_s[rows] = jnp.zeros((RWKV_ROWS, C_DIM), F32)

    ir = lax.broadcasted_iota(jnp.int32, (CHUNK, LANES), 0)
    ic = lax.broadcasted_iota(jnp.int32, (CHUNK, LANES), 1) & (HEAD_DIM - 1)
    eye = jnp.where(ir == ic, 1.0, 0.0).astype(F32)
    incl = (ic <= ir, ic >= ir)
    strict = (ic < ir, ic > ir)
    sr = lax.broadcasted_iota(jnp.int32, (CHUNK, CHUNK), 0)
    sc = lax.broadcasted_iota(jnp.int32, (CHUNK, CHUNK), 1)
    tri = tuple(jnp.where(m, 1.0, 0.0).astype(BF16) for m in (sc <= sr, sc >= sr))
    br = lax.broadcasted_iota(jnp.int32, (LANES, LANES), 0)
    bc = lax.broadcasted_iota(jnp.int32, (LANES, LANES), 1)
    diag_blocks = (br < HEAD_DIM) == (bc < HEAD_DIM)
    left_half = lax.broadcasted_iota(jnp.int32, (CHUNK, LANES), 1) < HEAD_DIM

    def level_mask(rows, cols, j):
        same_big = lax.shift_right_logical(rows, jnp.int32(j + 1)) == lax.shift_right_logical(cols, jnp.int32(j + 1))
        same_small = lax.shift_right_logical(rows, jnp.int32(j)) == lax.shift_right_logical(cols, jnp.int32(j))
        return same_big & jnp.logical_not(same_small)

    first_level = level_mask(ir, ic, 0)
    level_blocks = [diag_blocks & level_mask(br & (HEAD_DIM - 1), bc & (HEAD_DIM - 1), j) for j in range(1, CHUNK.bit_length() - 1)]

    def blockdiag(x2):
        x2 = x2.astype(BF16)
        return jnp.where(diag_blocks, jnp.concatenate([x2, x2], axis=0), jnp.zeros((LANES, LANES), BF16))

    npairs = C_HEADS // 2
    for bi in range(nbatch):
        for d in range(2):
            for p in range(npairs):
                if has_init:
                    st_s[bi, d, p] = jnp.concatenate([st0_ref[bi, d, 2 * p], st0_ref[bi, d, 2 * p + 1]], axis=1)
                else:
                    st_s[bi, d, p] = jnp.zeros((HEAD_DIM, LANES), F32)

    def chunk(c, carry):
        c = _index32(c)
        items = []
        for bi, d in [(bi, d) for bi in range(nbatch) for d in range(2)]:
            r0 = bi * t + ((nchunks - 1 - c) if d == 1 else c) * CHUNK
            rows = pl.ds(pl.multiple_of(r0, CHUNK), CHUNK)
            ld = ld_s[d, rows]
            cinc = _split_dot_left(tri[d], ld)
            cexc = cinc - ld
            ctot = cinc[0:1] if d == 1 else cinc[CHUNK - 1:CHUNK]
            e_ninc = jnp.exp(-cinc)
            e_rem = jnp.exp(ctot - cinc)
            e_tot = jnp.exp(ctot)
            bb = b_s[d, rows]
            kt = kt_s[d, rows]
            at = (aa_s[rows] * jnp.exp(cexc)).astype(BF16)
            rt = (zc_ref[rows, C_R] * jnp.exp(cinc)).astype(BF16)
            bt = (bb * e_ninc).astype(BF16)
            ktt = (kt * e_ninc).astype(BF16)
            bh = (bb * e_rem).astype(BF16)
            kh = (kt * e_rem).astype(BF16)
            vm = zc_ref[rows, C_V].astype(BF16)
            for p in range(npairs):
                ps = slice(p * LANES, (p + 1) * LANES)
                items.append(dict(bi=bi, d=d, p=p, rows=rows, e_tot=e_tot[:, ps], at=at[:, ps], rt=rt[:, ps],
                                  bt=bt[:, ps], kt=ktt[:, ps], bh=bh[:, ps], kh=kh[:, ps], vm=vm[:, ps]))
        for it in items:
            it["ar"] = jnp.concatenate([it["at"], it["rt"]], axis=0)
            it["vmd"] = blockdiag(it["vm"])
        for it in items:
            sbk = _bdot_nt(it["ar"], jnp.concatenate([blockdiag(it["bt"]), blockdiag(it["kt"])], axis=0))
            it["sb"] = sbk[:, 0:LANES]
            it["sk"] = sbk[:, LANES:2 * LANES]
        for it in items:
            d = it["d"]
            it["l"] = jnp.where(strict[d], it["sb"][0:CHUNK], 0.0)
            it["mak"] = jnp.where(strict[d], it["sk"][0:CHUNK], 0.0).astype(BF16)
            it["nr"] = jnp.concatenate([jnp.where(incl[d], it["sb"][CHUNK:], 0.0).astype(BF16),
                                        jnp.where(incl[d], it["sk"][CHUNK:], 0.0).astype(BF16)], axis=1)
            it["tm"] = eye + jnp.where(first_level, it["l"], 0.0)
            lb = it["l"].astype(BF16)
            it["l2"] = jnp.concatenate([lb, lb], axis=0)
        for lvl in level_blocks:
            for it in items:
                it["t1"] = _bdot(it["tm"], jnp.where(lvl, it["l2"], jnp.zeros((LANES, LANES), BF16)))
            for it in items:
                it["tm"] = it["tm"] + _bdot(it["t1"], blockdiag(it["tm"]))
        for it in items:
            it["mv"] = _bdot(it["mak"], it["vmd"])
        for it in items:
            it["tau"] = _bdot(it["tm"], jnp.concatenate([blockdiag(it["at"]), blockdiag(it["mv"])], axis=1))
        for it in items:
            it["s0"] = st_s[it["bi"], it["d"], it["p"]]
            it["x"] = _bdot_nt(jnp.concatenate([it["tau"][:, 0:LANES].astype(BF16), it["rt"]], axis=0),
                               blockdiag(it["s0"]))
        for it in items:
            it["u"] = (it["x"][0:CHUNK] + it["tau"][:, LANES:2 * LANES]).astype(BF16)
        for it in items:
            it["y"] = it["x"][CHUNK:] + _bdot(it["nr"], jnp.concatenate([blockdiag(it["u"]), it["vmd"]], axis=0))
        for it in items:
            full = _bdot_tn(jnp.concatenate([it["u"], it["vm"]], axis=0),
                            jnp.concatenate([it["bh"], it["kh"]], axis=0))
            st_s[it["bi"], it["d"], it["p"]] = (it["s0"] * it["e_tot"]
                                                + jnp.where(left_half, full[0:HEAD_DIM], full[HEAD_DIM:2 * HEAD_DIM]))
        for i in range(0, len(items), npairs):
            rows = items[i]["rows"]
            y_s[rows] = y_s[rows] + jnp.concatenate([it["y"] for it in items[i:i + npairs]], axis=1)
        return carry

    lax.fori_loop(0, nchunks, chunk, 0)
    if want_state:
        for bi in range(nbatch):
            if has_prev:
                stout_ref[bi, 0] = prev_ref[bi]
            for d in range(2):
                for p in range(npairs):
                    s2 = st_s[bi, d, p]
                    for q in range(2):
                        if has_prev:
                            stout_ref[bi, 1, d, 2 * p + q] = s2[:, q * HEAD_DIM:(q + 1) * HEAD_DIM]
                        else:
                            stout_ref[bi, d, 2 * p + q] = s2[:, q * HEAD_DIM:(q + 1) * HEAD_DIM]

    gup = gup_ref[...].astype(BF16)
    for rb in range(nbatch * t // RWKV_ROWS):
        rows = slice(rb * RWKV_ROWS, (rb + 1) * RWKV_ROWS)
        y = y_s[rows]
        mu = _head_sum(y, ones) * (1.0 / HEAD_DIM)
        yc = y - mu
        var = _head_sum(yc * yc, ones) * (1.0 / HEAD_DIM)
        yn = yc * lax.rsqrt(var + GN_EPS)
        g = _bdot(_sigmoid(zc_ref[rows, C_G]), gup)
        oc_ref[rows] = (yn * ln_w + ln_b + bon_s[rows]) * g


def _rwkv_call(zc, rvec, wup, aup, gup, layer, t, init_state, want_state, prev_state, cast_weights=()):
    nb = zc.shape[0] // t
    nbatch = min(RWKV_BATCH, nb)
    rows = nbatch * t
    has_init = init_state is not None
    has_prev = prev_state is not None
    st_shape = (2, C_HEADS, HEAD_DIM, HEAD_DIM)
    in_specs = [
        pl.BlockSpec((rows, RWKV_COLS), lambda b: (b, 0), pipeline_mode=pl.Buffered(1 if nb == nbatch else 2)),
        _layer_spec((16, C_DIM), layer),
        _layer_spec((2, W_RANK, C_DIM), layer),
        _layer_spec((2, A_RANK, C_DIM), layer),
        _layer_spec((G_RANK, C_DIM), layer),
    ]
    args = [zc, rvec, wup, aup, gup]
    if has_init:
        in_specs.append(pl.BlockSpec((nbatch, None) + st_shape, lambda b: (b, layer, 0, 0, 0, 0)))
        args.append(init_state)
    if has_prev:
        in_specs.append(pl.BlockSpec((nbatch,) + st_shape, lambda b: (b, 0, 0, 0, 0)))
        args.append(prev_state)
    steps = nb // nbatch
    cast_in, cast_shape, cast_out = _cast_specs(cast_weights, layer, steps)
    in_specs.extend(cast_in)
    args.extend(cast_weights)
    out_shape = [jax.ShapeDtypeStruct((nb * t, C_DIM), F32)]
    out_specs = [pl.BlockSpec((rows, C_DIM), lambda b: (b, 0))]
    if want_state and has_prev:
        out_shape.append(jax.ShapeDtypeStruct((nb, DEPTH) + st_shape, F32))
        out_specs.append(pl.BlockSpec((nbatch, DEPTH) + st_shape, lambda b: (b, 0, 0, 0, 0, 0)))
    elif want_state:
        out_shape.append(jax.ShapeDtypeStruct((nb,) + st_shape, F32))
        out_specs.append(pl.BlockSpec((nbatch,) + st_shape, lambda b: (b, 0, 0, 0, 0)))
    out_shape.extend(cast_shape)
    out_specs.extend(cast_out)
    return pl.pallas_call(
        functools.partial(_rwkv_body, t=t, nbatch=nbatch, has_init=has_init, want_state=want_state,
                          has_prev=has_prev, ncast=len(cast_weights)),
        out_shape=tuple(out_shape),
        grid=(steps,),
        in_specs=in_specs,
        out_specs=tuple(out_specs),
        scratch_shapes=[
            pltpu.VMEM((rows, C_DIM), F32),
            pltpu.VMEM((2, rows, C_DIM), F32),
            pltpu.VMEM((2, rows, C_DIM), F32),
            pltpu.VMEM((2, rows, C_DIM), F32),
            pltpu.VMEM((rows, C_DIM), F32),
            pltpu.VMEM((rows, C_DIM), F32),
            pltpu.VMEM((nbatch, 2, C_HEADS // 2, HEAD_DIM, LANES), F32),
        ],
        compiler_params=pltpu.CompilerParams(vmem_limit_bytes=VMEM_LIMIT),
        name="rwkv_prompt" if want_state else "rwkv_sample",
    )(*args)


def _post_tile(oa_ref, oc_ref, x_ref, mod, g_ref, wout_ref, wgu_ref, wdn_ref, xo_ref):
    gate1 = mod[:, 2 * D_MODEL:3 * D_MODEL]
    shift2 = mod[:, 3 * D_MODEL:4 * D_MODEL]
    scale2 = mod[:, 4 * D_MODEL:5 * D_MODEL]
    gate2 = mod[:, 5 * D_MODEL:6 * D_MODEL]
    sub = ROW_TILE // ROW_SPLIT
    parts = [slice(i * sub, (i + 1) * sub) for i in range(ROW_SPLIT)]
    o = [jnp.dot(jnp.concatenate([oa_ref[r, :].astype(BF16), oc_ref[r, :].astype(BF16)], axis=1), wout_ref[...],
                 preferred_element_type=F32) for r in parts]
    x1 = [x_ref[r, :] + gate1 * _rms(oi, g_ref[1:2]) for r, oi in zip(parts, o)]
    h2 = [(_rms(xi, g_ref[2:3]) * (1.0 + scale2) + shift2).astype(BF16) for xi in x1]

    def gate_up(step):
        (lo, hi), i = step
        g = jnp.dot(h2[i], wgu_ref[:, lo:hi], preferred_element_type=F32)
        u = jnp.dot(h2[i], wgu_ref[:, FF_DIM + lo:FF_DIM + hi], preferred_element_type=F32)
        return g, u

    steps = [(chunk, i) for chunk in FF_CHUNKS for i in range(ROW_SPLIT)]
    acc = [None] * ROW_SPLIT
    pending = gate_up(steps[0])
    for k, ((lo, hi), i) in enumerate(steps):
        g, u = pending
        if k + 1 < len(steps):
            pending = gate_up(steps[k + 1])
        act = (g * _sigmoid(g) * u).astype(BF16)
        part = jnp.dot(act, wdn_ref[lo:hi, :], preferred_element_type=F32)
        acc[i] = part if acc[i] is None else acc[i] + part
    for r, xi, ai in zip(parts, x1, acc):
        xo_ref[r, :] = xi + gate2 * _rms(ai, g_ref[3:4])


def _post_body(oap_ref, oas_ref, ocp_ref, ocs_ref, xp_ref, xs_ref, mod_ref, g_ref, wout_ref, wgu_ref, wdn_ref,
               xop_ref, xos_ref, *, n_first):
    step = pl.program_id(0)

    @pl.when(step < n_first)
    def _():
        _post_tile(oap_ref, ocp_ref, xp_ref, mod_ref[0:1], g_ref, wout_ref, wgu_ref, wdn_ref, xop_ref)

    @pl.when(step >= n_first)
    def _():
        _post_tile(oas_ref, ocs_ref, xs_ref, _latent_mod_row(mod_ref, step, n_first), g_ref, wout_ref, wgu_ref,
                   wdn_ref, xos_ref)


def _post_call(oatt_p, oatt_s, oc_p, oc_s, xp, xs, mods, gvec, wout_b, wgu_b, wdn_b, layer):
    n_first = xp.shape[0] // ROW_TILE
    n_second = xs.shape[0] // ROW_TILE
    x_first, x_second = _two_group_specs(D_MODEL, n_first)
    return pl.pallas_call(
        functools.partial(_post_body, n_first=n_first),
        out_shape=(jax.ShapeDtypeStruct(xp.shape, F32), jax.ShapeDtypeStruct(xs.shape, F32)),
        grid=(n_first + n_second,),
        in_specs=[
            *_two_group_specs(ATT_OUT, n_first),
            *_two_group_specs(C_DIM, n_first),
            x_first, x_second,
            _layer_spec((MOD_ROWS, MOD_COLS), layer),
            _layer_spec((8, D_MODEL), layer),
            _whole_spec((MIX_DIM, D_MODEL)),
            _whole_spec((D_MODEL, 2 * FF_DIM)),
            _whole_spec((FF_DIM, D_MODEL)),
        ],
        out_specs=(x_first, x_second),
        compiler_params=pltpu.CompilerParams(dimension_semantics=("arbitrary",), vmem_limit_bytes=VMEM_LIMIT),
        name="post_ffn",
    )(oatt_p, oatt_s, oc_p, oc_s, xp, xs, mods, gvec, wout_b, wgu_b, wdn_b)


def _rope_tables(t):
    pos = jnp.arange(t)
    row = (pos // GRID_W).astype(F32)
    col = (pos % GRID_W).astype(F32)
    freqs = ROPE_THETA ** (-jnp.arange(ROPE_PAIRS_AXIS, dtype=F32) / ROPE_PAIRS_AXIS)
    ang = jnp.concatenate([row[:, None] * freqs, col[:, None] * freqs], axis=-1)
    cos = jnp.tile(jnp.cos(ang), (1, 4))
    sin = jnp.sin(ang)
    sin = jnp.tile(jnp.concatenate([-sin, sin], axis=-1), (1, 2))
    return cos, sin


def _to_time_minor(cache):
    b, l, t = cache.shape[:3]
    return jnp.transpose(cache, (0, 1, 3, 4, 2)).reshape(b, l, KV_DIM, t)


def _from_time_minor(slab):
    b, l, _, t = slab.shape
    return jnp.transpose(slab.reshape(b, l, 2, HEAD_DIM, t), (0, 1, 4, 2, 3))


def kernel(x_prompt, x_sample, cache_a_k, cache_a_v, cache_b_k, cache_b_v, state_c, c, c_ctx, w_mod, b_mod, norm_mix_pre, norm_mix_post, norm_ffn_pre, norm_ffn_post, w_in, w_out, a_sink, b_q_norm, b_k_norm, c_w0, c_w_up, c_a0, c_a_up, c_g_up, c_k_k, c_k_a, c_r_k, c_ln_w, c_ln_b, w_gu, w_down):
    cvec = jnp.zeros((MOD_ROWS, D_MODEL), F32).at[0].set(c_ctx).at[1:1 + DEC_BATCH].set(c)
    mods = _mod_call(cvec, w_mod, b_mod)
    cos, sin = _rope_tables(DEC_SEQ)

    gvec = jnp.zeros((DEPTH, 8, D_MODEL), F32)
    gvec = gvec.at[:, 0].set(norm_mix_pre).at[:, 1].set(norm_mix_post)
    gvec = gvec.at[:, 2].set(norm_ffn_pre).at[:, 3].set(norm_ffn_post)
    rvec = jnp.zeros((DEPTH, 16, C_DIM), F32)
    rvec = rvec.at[:, 0].set(c_k_k).at[:, 1].set(c_k_a).at[:, 2].set(c_r_k.reshape(DEPTH, C_DIM))
    rvec = rvec.at[:, 3].set(c_ln_w).at[:, 4].set(c_ln_b)
    rvec = rvec.at[:, 5:7].set(c_w0).at[:, 7:9].set(c_a0)
    gq = jnp.tile(b_q_norm, (1, B_HEADS))[:, None, :]
    gk = jnp.tile(b_k_norm, (1, B_KV_HEADS))[:, None, :]
    caches = tuple(_to_time_minor(t) for t in (cache_a_k, cache_a_v, cache_b_k, cache_b_v))

    xp = x_prompt.reshape(BATCH * SEQ, D_MODEL)
    xs = x_sample.reshape(DEC_BATCH * DEC_SEQ, D_MODEL)
    kv_prev = st_prev = None
    for l in range(DEPTH):
        za_p, zc_p, za_s, zc_s, wout_b, wdn_b = _proj_in_call(xp, xs, mods, gvec, w_in, l, (w_out, w_down))
        oatt_p, *kv_prev = _attn_prompt_call(za_p, a_sink, gq, gk, l, kv_prev)
        if l == 0:
            (kv_prev,) = kv_prev
        oc_p, st_prev, wgu_b = _rwkv_call(zc_p, rvec, c_w_up, c_a_up, c_g_up, l, SEQ, None, True, st_prev,
                                          cast_weights=(w_gu,))
        oatt_s = _attn_sample_call(za_s, a_sink, caches, cos, sin, gq, gk, l)
        (oc_s,) = _rwkv_call(zc_s, rvec, c_w_up, c_a_up, c_g_up, l, DEC_SEQ, state_c, False, None)
        xp, xs = _post_call(oatt_p, oatt_s, oc_p, oc_s, xp, xs, mods, gvec, wout_b, wgu_b, wdn_b, l)

    new_caches = tuple(_from_time_minor(slab) for slab in kv_prev)
    return (xp.reshape(BATCH, SEQ, D_MODEL), xs.reshape(DEC_BATCH, DEC_SEQ, D_MODEL), *new_caches, st_prev)
```

```python
import functools
import math

import jax
import jax.numpy as jnp
from jax import lax
from jax.experimental import pallas as pl
from jax.experimental.pallas import tpu as pltpu

D_MODEL = 1024
BATCH = 32
SEQ = 256
DEPTH = 2
DEC_BATCH = 2
DEC_SEQ = 1024
PAST_LEN = 256
GRID_W = 64
HEAD_DIM = 64
A_HEADS = 4
A_KV_HEADS = 2
B_HEADS = 6
B_KV_HEADS = 2
C_HEADS = 6
A_DIM = A_HEADS * HEAD_DIM
B_DIM = B_HEADS * HEAD_DIM
C_DIM = C_HEADS * HEAD_DIM
MIX_DIM = A_DIM + B_DIM + C_DIM
WINDOW = 128
Q_BLK = 128
W_RANK = 64
A_RANK = 64
G_RANK = 128
FF_DIM = -(-8 * D_MODEL // (3 * 256)) * 256
ROPE_THETA = 10000.0
ROPE_PAIRS_AXIS = HEAD_DIM // 4
NORM_EPS = 1e-6
GN_EPS = 64e-5
NEG_INF = -1e30

LANES = 128
KV_DIM = 2 * HEAD_DIM
assert KV_DIM == LANES


def _col_ranges(widths):
    edges = [0]
    for w in widths:
        edges.append(edges[-1] + w)
    return tuple(slice(lo, hi) for lo, hi in zip(edges[:-1], edges[1:])), edges[-1]


(A_Q, A_K, A_V, B_Q, B_K, B_V), ATT_COLS = _col_ranges((A_DIM, KV_DIM, KV_DIM, B_DIM, KV_DIM, KV_DIM))
(C_R, C_K, C_V, C_W, C_A, C_G), RWKV_COLS = _col_ranges((C_DIM, C_DIM, C_DIM, W_RANK, A_RANK, G_RANK))
IN_COLS = ATT_COLS + RWKV_COLS
ATT_OUT = A_DIM + B_DIM
MOD_COLS = 6 * D_MODEL
MOD_ROWS = 8
N_CACHE = 4
CHUNK = 64
assert CHUNK == HEAD_DIM
QK_SCALE = HEAD_DIM ** -0.5
DECAY_RATE = math.exp(-0.5)
ROW_TILE = 512
ROW_SPLIT = 2
RWKV_ROWS = 512
RWKV_BATCH = 4
B_QROWS = 256
ATTN_BATCH = 4
MXU_TILE = 256
FF_CHUNKS = ((0, 4 * MXU_TILE), (4 * MXU_TILE, 8 * MXU_TILE), (8 * MXU_TILE, FF_DIM))
assert FF_DIM % MXU_TILE == 0
VMEM_LIMIT = 56 * 1024 * 1024

F32 = jnp.float32
BF16 = jnp.bfloat16

assert DEPTH == 2


def _bdot(a, b):
    return jnp.dot(a.astype(BF16), b.astype(BF16), preferred_element_type=F32)


def _bdot_nt(a, b):
    return lax.dot_general(a.astype(BF16), b.astype(BF16), (((1,), (1,)), ((), ())), preferred_element_type=F32)


def _bdot_tn(a, b):
    return lax.dot_general(a.astype(BF16), b.astype(BF16), (((0,), (0,)), ((), ())), preferred_element_type=F32)


def _split(x):
    hi = x.astype(BF16)
    lo = (x - hi.astype(F32)).astype(BF16)
    return hi, lo


def _split_dot_left(m, x):
    hi, lo = _split(x)
    return jnp.dot(m, hi, preferred_element_type=F32) + jnp.dot(m, lo, preferred_element_type=F32)


def _head_sum(x, ones):
    return jnp.dot(x.astype(BF16), ones, preferred_element_type=F32)


def _sigmoid(x):
    return 0.5 * jnp.tanh(0.5 * x) + 0.5


def _head_ones(n):
    r = lax.shift_right_logical(lax.broadcasted_iota(jnp.int32, (n, n), 0), jnp.int32(HEAD_DIM.bit_length() - 1))
    c = lax.shift_right_logical(lax.broadcasted_iota(jnp.int32, (n, n), 1), jnp.int32(HEAD_DIM.bit_length() - 1))
    return jnp.where(r == c, 1.0, 0.0).astype(BF16)


def _rms(x, g):
    ms = jnp.mean(x * x, axis=-1, keepdims=True)
    return x * lax.rsqrt(ms + NORM_EPS) * g


def _head_rms(x, g, ones):
    ms = _head_sum(x * x, ones) * (1.0 / HEAD_DIM)
    return x * lax.rsqrt(ms + NORM_EPS) * g


def _rope(x, cos, sin):
    t = x.shape[0]
    lane = lax.broadcasted_iota(jnp.int32, (t, LANES), 1)
    first = (lane & (HEAD_DIM // 2)) == 0
    outs = []
    for i in range(x.shape[1] // LANES):
        xs = x[:, i * LANES:(i + 1) * LANES]
        swapped = jnp.where(first, pltpu.roll(xs, LANES - HEAD_DIM // 2, axis=1), pltpu.roll(xs, HEAD_DIM // 2, axis=1))
        outs.append(xs * cos + swapped * sin)
    return outs[0] if len(outs) == 1 else jnp.concatenate(outs, axis=1)


def _softmax_pv(scores, values, sink=None):
    m = scores[0].max(axis=-1, keepdims=True)
    for s in scores[1:]:
        m = jnp.maximum(m, s.max(axis=-1, keepdims=True))
    if sink is not None:
        m = jnp.maximum(m, sink)
    denom = None
    acc = None
    for s, v in zip(scores, values):
        p = jnp.exp(s - m)
        d = p.sum(axis=-1, keepdims=True)
        o = jnp.dot(p.astype(BF16), v, preferred_element_type=F32)
        denom = d if denom is None else denom + d
        acc = o if acc is None else acc + o
    if sink is not None:
        denom = denom + jnp.exp(sink - m)
    return acc / denom


def _softmax_pv_t(scores_t, values_t, sink=None):
    m = scores_t[0].max(axis=0, keepdims=True)
    for s in scores_t[1:]:
        m = jnp.maximum(m, s.max(axis=0, keepdims=True))
    if sink is not None:
        m = jnp.maximum(m, sink)
    denom = None
    acc = None
    for s, vt in zip(scores_t, values_t):
        p = jnp.exp(s - m)
        d = p.sum(axis=0, keepdims=True)
        o = jnp.dot(vt, p.astype(BF16), preferred_element_type=F32)
        denom = d if denom is None else denom + d
        acc = o if acc is None else acc + o
    if sink is not None:
        denom = denom + jnp.exp(sink - m)
    return acc / denom


def _attend_heads(score_fns, finish_fns, ahead=1):
    n = len(score_fns)
    pending = [fn() for fn in score_fns[:ahead]]
    outs = []
    for i in range(n):
        if i + ahead < n:
            pending.append(score_fns[i + ahead]())
        outs.append(finish_fns[i](pending.pop(0)))
    return outs


def _index32(i):
    return jnp.asarray(i, jnp.int32)


def _layer_spec(shape, layer):
    zeros = (0,) * len(shape)
    return pl.BlockSpec((None,) + shape, lambda *_: (layer,) + zeros, pipeline_mode=pl.Buffered(1))


def _whole_spec(shape):
    zeros = (0,) * len(shape)
    return pl.BlockSpec(shape, lambda *_: zeros, pipeline_mode=pl.Buffered(1))


def _mod_body(c_ref, w_ref, b_ref, o_ref):
    cv = c_ref[...]
    s = cv * _sigmoid(cv)
    o_ref[0] = _bdot(s, w_ref[0]) + b_ref[0]


def _mod_call(cvec, w_mod, b_mod):
    bn = MOD_COLS // 4
    return pl.pallas_call(
        _mod_body,
        out_shape=jax.ShapeDtypeStruct((DEPTH, MOD_ROWS, MOD_COLS), F32),
        grid=(DEPTH, MOD_COLS // bn),
        in_specs=[
            pl.BlockSpec((MOD_ROWS, D_MODEL), lambda l, j: (0, 0)),
            pl.BlockSpec((1, D_MODEL, bn), lambda l, j: (l, 0, j)),
            pl.BlockSpec((1, 1, bn), lambda l, j: (l, 0, j)),
        ],
        out_specs=pl.BlockSpec((1, MOD_ROWS, bn), lambda l, j: (l, 0, j)),
        compiler_params=pltpu.CompilerParams(vmem_limit_bytes=VMEM_LIMIT),
        name="mod_vectors",
    )(cvec, w_mod, b_mod.reshape(DEPTH, 1, MOD_COLS))


def _cast_blocks(srcs, dsts):
    for src, dst in zip(srcs, dsts):
        dst[...] = src[...].astype(BF16)


def _cast_specs(weights, layer, steps):
    def block(i):
        return jnp.minimum(i, steps - 1)

    in_specs = [pl.BlockSpec((None, w.shape[1] // steps, w.shape[2]), lambda i: (layer, block(i), 0)) for w in weights]
    out_shape = [jax.ShapeDtypeStruct(w.shape[1:], BF16) for w in weights]
    out_specs = [pl.BlockSpec((w.shape[1] // steps, w.shape[2]), lambda i: (block(i), 0)) for w in weights]
    return in_specs, out_shape, out_specs


def _two_group_specs(cols, n_first):
    first = pl.BlockSpec((ROW_TILE, cols), lambda i: (jnp.minimum(i, n_first - 1), 0))
    second = pl.BlockSpec((ROW_TILE, cols), lambda i: (jnp.maximum(i - n_first, 0), 0))
    return first, second


def _latent_mod_row(mod_ref, step, n_first):
    row = 1 + (step - n_first) // (DEC_SEQ // ROW_TILE)
    return mod_ref[pl.ds(row, 1), :]


def _proj_in_body(xp_ref, xs_ref, mod_ref, g_ref, w_ref, *rest, n_first, ncast):
    cast_src = rest[:ncast]
    zap_ref, zcp_ref, zas_ref, zcs_ref = rest[ncast:ncast + 4]
    cast_dst = rest[ncast + 4:2 * ncast + 4]
    wb_s = rest[2 * ncast + 4]
    step = pl.program_id(0)

    @pl.when(step == 0)
    def _():
        wb_s[...] = w_ref[...].astype(BF16)

    def project(x_ref, mod, za_ref, zc_ref):
        shift1 = mod[:, 0:D_MODEL]
        scale1 = mod[:, D_MODEL:2 * D_MODEL]
        h = (_rms(x_ref[...], g_ref[0:1]) * (1.0 + scale1) + shift1).astype(BF16)
        z = jnp.dot(h, wb_s[...], preferred_element_type=F32)
        za_ref[...] = z[:, 0:ATT_COLS]
        zc_ref[...] = z[:, ATT_COLS:IN_COLS]

    @pl.when(step < n_first)
    def _():
        _cast_blocks(cast_src, cast_dst)
        project(xp_ref, mod_ref[0:1], zap_ref, zcp_ref)

    @pl.when(step >= n_first)
    def _():
        project(xs_ref, _latent_mod_row(mod_ref, step, n_first), zas_ref, zcs_ref)


def _proj_in_call(xp, xs, mods, gvec, w_in, layer, cast_weights):
    n_first = xp.shape[0] // ROW_TILE
    n_second = xs.shape[0] // ROW_TILE
    cast_in, cast_shape, cast_out = _cast_specs(cast_weights, layer, n_first)
    x_first, x_second = _two_group_specs(D_MODEL, n_first)
    za_first, za_second = _two_group_specs(ATT_COLS, n_first)
    zc_first, zc_second = _two_group_specs(RWKV_COLS, n_first)
    return pl.pallas_call(
        functools.partial(_proj_in_body, n_first=n_first, ncast=len(cast_weights)),
        out_shape=(jax.ShapeDtypeStruct((xp.shape[0], ATT_COLS), F32), jax.ShapeDtypeStruct((xp.shape[0], RWKV_COLS), F32),
                   jax.ShapeDtypeStruct((xs.shape[0], ATT_COLS), F32), jax.ShapeDtypeStruct((xs.shape[0], RWKV_COLS), F32),
                   *cast_shape),
        grid=(n_first + n_second,),
        in_specs=[
            x_first, x_second,
            _layer_spec((MOD_ROWS, MOD_COLS), layer),
            _layer_spec((8, D_MODEL), layer),
            _layer_spec((D_MODEL, IN_COLS), layer),
            *cast_in,
        ],
        out_specs=(za_first, zc_first, za_second, zc_second, *cast_out),
        scratch_shapes=[pltpu.VMEM((D_MODEL, IN_COLS), BF16)],
        compiler_params=pltpu.CompilerParams(dimension_semantics=("arbitrary",), vmem_limit_bytes=VMEM_LIMIT),
        name="proj_in",
    )(xp, xs, mods, gvec, w_in, *cast_weights)


def _attn_prompt_body(*refs, layer, nseq):
    sink_ref, za_ref, gq_ref, gk_ref = refs[:4]
    if layer == 0:
        o_ref, kv_ref = refs[4:]
    else:
        prev_ref, o_ref = refs[4:6]
        cache_refs = refs[6:]
    ones_q = _head_ones(B_DIM)
    ones_k = _head_ones(KV_DIM)
    score_fns, finish_fns = [], []
    for si in range(nseq):
        rows = slice(si * SEQ, (si + 1) * SEQ)
        aq = (za_ref[rows, A_Q] * QK_SCALE).astype(BF16)
        ak = za_ref[rows, A_K]
        bq = _head_rms(za_ref[rows, B_Q], gq_ref[...], ones_q)
        bk = _head_rms(za_ref[rows, B_K], gk_ref[...], ones_k)
        bqb = (bq * QK_SCALE).astype(BF16)
        akb = ak.astype(BF16)
        bkb = bk.astype(BF16)
        pieces_t = (ak.T, za_ref[rows, A_V].T, bk.T, za_ref[rows, B_V].T)
        avt = pieces_t[1].astype(BF16)
        bvt = pieces_t[3].astype(BF16)
        for j, piece_t in enumerate(pieces_t):
            if layer == 0:
                kv_ref[si, j] = piece_t
            else:
                cache_refs[j][si, 0] = prev_ref[si, j]
                cache_refs[j][si, 1] = piece_t

        for h in range(A_HEADS):
            kv = h // (A_HEADS // A_KV_HEADS)
            hs = slice(kv * HEAD_DIM, (kv + 1) * HEAD_DIM)
            score_fns.append(lambda h=h, hs=hs, akb=akb, aq=aq: [_bdot_nt(akb[:, hs], aq[:, h * HEAD_DIM:(h + 1) * HEAD_DIM])])
            finish_fns.append(lambda sc, h=h, hs=hs, avt=avt: _softmax_pv_t(sc, [avt[hs]], sink_ref[layer, h]))
        for h in range(B_HEADS):
            kv = h // (B_HEADS // B_KV_HEADS)
            hs = slice(kv * HEAD_DIM, (kv + 1) * HEAD_DIM)
            score_fns.append(lambda h=h, hs=hs, bkb=bkb, bqb=bqb: [_bdot_nt(bkb[:, hs], bqb[:, h * HEAD_DIM:(h + 1) * HEAD_DIM])])
            finish_fns.append(lambda sc, hs=hs, bvt=bvt: _softmax_pv_t(sc, [bvt[hs]]))
    outs = _attend_heads(score_fns, finish_fns, ahead=len(score_fns))
    heads = A_HEADS + B_HEADS
    for si in range(nseq):
        o_ref[si * SEQ:(si + 1) * SEQ, :] = jnp.concatenate(outs[si * heads:(si + 1) * heads], axis=0).T.astype(BF16)


def _attn_prompt_call(za, sink, gq, gk, layer, prev_kv):
    nb = za.shape[0] // SEQ
    nseq = ATTN_BATCH
    in_specs = [
        pl.BlockSpec(memory_space=pltpu.SMEM),
        pl.BlockSpec((nseq * SEQ, ATT_COLS), lambda b: (b, 0)),
        _layer_spec((1, B_DIM), layer),
        _layer_spec((1, KV_DIM), layer),
    ]
    args = [sink, za, gq, gk]
    out_shape = [jax.ShapeDtypeStruct((nb * SEQ, ATT_OUT), BF16)]
    out_specs = [pl.BlockSpec((nseq * SEQ, ATT_OUT), lambda b: (b, 0))]
    slab_spec = pl.BlockSpec((nseq, N_CACHE, KV_DIM, SEQ), lambda b: (b, 0, 0, 0))
    if layer == 0:
        out_shape.append(jax.ShapeDtypeStruct((nb, N_CACHE, KV_DIM, SEQ), F32))
        out_specs.append(slab_spec)
    else:
        in_specs.append(slab_spec)
        args.append(prev_kv)
        for _ in range(N_CACHE):
            out_shape.append(jax.ShapeDtypeStruct((nb, DEPTH, KV_DIM, SEQ), F32))
            out_specs.append(pl.BlockSpec((nseq, DEPTH, KV_DIM, SEQ), lambda b: (b, 0, 0, 0)))
    return pl.pallas_call(
        functools.partial(_attn_prompt_body, layer=layer, nseq=nseq),
        out_shape=tuple(out_shape),
        grid=(nb // nseq,),
        in_specs=in_specs,
        out_specs=tuple(out_specs),
        compiler_params=pltpu.CompilerParams(vmem_limit_bytes=VMEM_LIMIT),
        name="attn_prompt",
    )(*args)


def _attn_sample_body(sink_ref, za_ref, cak_ref, cav_ref, cbk_ref, cbv_ref, cos_ref, sin_ref, gq_ref, gk_ref,
                      o_ref, qa_s, ka_s, va_s, qb_s, kb_s, vb_s, *, layer):
    t = DEC_SEQ
    cos = cos_ref[...]
    sin = sin_ref[...]
    qa_s[...] = (_rope(za_ref[:, A_Q], cos, sin) * QK_SCALE).astype(BF16)
    zpad = jnp.zeros((Q_BLK, KV_DIM), BF16)
    ka_s[0:Q_BLK] = zpad
    ka_s[Q_BLK + t:] = zpad
    va_s[0:Q_BLK] = zpad
    va_s[Q_BLK + t:] = zpad
    ka_s[Q_BLK:Q_BLK + t] = _rope(za_ref[:, A_K], cos, sin).astype(BF16)
    va_s[Q_BLK:Q_BLK + t] = za_ref[:, A_V].astype(BF16)
    bq = _head_rms(za_ref[:, B_Q], gq_ref[...], _head_ones(B_DIM))
    qb_s[...] = (_rope(bq, cos, sin) * QK_SCALE).astype(BF16)
    bk = _head_rms(za_ref[:, B_K], gk_ref[...], _head_ones(KV_DIM))
    kb_s[0:PAST_LEN] = cbk_ref[0].T.astype(BF16)
    kb_s[PAST_LEN:] = _rope(bk, cos, sin).astype(BF16)
    vb_s[0:PAST_LEN] = cbv_ref[0].T.astype(BF16)
    vb_s[PAST_LEN:] = za_ref[:, B_V].astype(BF16)

    kca = cak_ref[0].T.astype(BF16)
    vca = cav_ref[0].T.astype(BF16)

    ga = A_HEADS // A_KV_HEADS
    gb = B_HEADS // B_KV_HEADS
    qpos = lax.broadcasted_iota(jnp.int32, (ga * Q_BLK, 3 * Q_BLK), 0) & (Q_BLK - 1)
    kpos = lax.broadcasted_iota(jnp.int32, (ga * Q_BLK, 3 * Q_BLK), 1) - Q_BLK
    near = jnp.abs(kpos - qpos) <= WINDOW
    head_of_row = lax.shift_right_logical(lax.broadcasted_iota(jnp.int32, (ga * Q_BLK, 1), 0),
                                          jnp.int32(Q_BLK.bit_length() - 1))
    sinks = []
    for kv in range(A_KV_HEADS):
        col = jnp.full((ga * Q_BLK, 1), sink_ref[layer, kv * ga], F32)
        for g in range(1, ga):
            col = jnp.where(head_of_row == g, sink_ref[layer, kv * ga + g], col)
        sinks.append(col)

    def stack_heads(q, kv, group):
        return jnp.concatenate([q[:, (kv * group + g) * HEAD_DIM:(kv * group + g + 1) * HEAD_DIM] for g in range(group)], axis=0)

    def unstack_heads(outs, group, rows):
        return jnp.concatenate([o[g * rows:(g + 1) * rows] for o in outs for g in range(group)], axis=1)

    def a_block(n, carry):
        r0 = pl.multiple_of(_index32(n) * Q_BLK, Q_BLK)
        q = qa_s[pl.ds(r0, Q_BLK), :]
        kl = ka_s[pl.ds(r0, 3 * Q_BLK), :]
        vl = va_s[pl.ds(r0, 3 * Q_BLK), :]
        kabs = kpos + r0
        mask = near & (kabs >= 0) & (kabs < t)
        score_fns, finish_fns = [], []
        for kv in range(A_KV_HEADS):
            hs = slice(kv * HEAD_DIM, (kv + 1) * HEAD_DIM)

            def scores(kv=kv, hs=hs):
                qs = stack_heads(q, kv, ga)
                return [jnp.where(mask, _bdot_nt(qs, kl[:, hs]), NEG_INF), _bdot_nt(qs, kca[:, hs])]

            score_fns.append(scores)
            finish_fns.append(lambda sc, kv=kv, hs=hs: _softmax_pv(sc, [vl[:, hs], vca[:, hs]], sinks[kv]))
        outs = _attend_heads(score_fns, finish_fns, ahead=A_KV_HEADS)
        o_ref[pl.ds(r0, Q_BLK), 0:A_DIM] = unstack_heads(outs, ga, Q_BLK).astype(BF16)
        return carry

    lax.fori_loop(0, t // Q_BLK, a_block, 0)

    def b_block(n, carry):
        r0 = pl.multiple_of(_index32(n) * B_QROWS, B_QROWS)
        q = qb_s[pl.ds(r0, B_QROWS), :]
        score_fns, finish_fns = [], []
        for kv in range(B_KV_HEADS):
            hs = slice(kv * HEAD_DIM, (kv + 1) * HEAD_DIM)
            score_fns.append(lambda kv=kv, hs=hs: [_bdot_nt(stack_heads(q, kv, gb), kb_s[:, hs])])
            finish_fns.append(lambda sc, hs=hs: _softmax_pv(sc, [vb_s[:, hs]]))
        outs = _attend_heads(score_fns, finish_fns)
        o_ref[pl.ds(r0, B_QROWS), A_DIM:ATT_OUT] = unstack_heads(outs, gb, B_QROWS).astype(BF16)
        return carry

    lax.fori_loop(0, t // B_QROWS, b_block, 0)


def _attn_sample_call(za, sink, caches, cos, sin, gq, gk, layer):
    nb = za.shape[0] // DEC_SEQ
    t = DEC_SEQ
    cache_spec = pl.BlockSpec((1, None, KV_DIM, PAST_LEN), lambda b: (b, layer, 0, 0))
    return pl.pallas_call(
        functools.partial(_attn_sample_body, layer=layer),
        out_shape=jax.ShapeDtypeStruct((nb * t, ATT_OUT), BF16),
        grid=(nb,),
        in_specs=[
            pl.BlockSpec(memory_space=pltpu.SMEM),
            pl.BlockSpec((t, ATT_COLS), lambda b: (b, 0)),
            cache_spec, cache_spec, cache_spec, cache_spec,
            pl.BlockSpec((t, LANES), lambda b: (0, 0)),
            pl.BlockSpec((t, LANES), lambda b: (0, 0)),
            _layer_spec((1, B_DIM), layer),
            _layer_spec((1, KV_DIM), layer),
        ],
        out_specs=pl.BlockSpec((t, ATT_OUT), lambda b: (b, 0)),
        scratch_shapes=[
            pltpu.VMEM((t, A_DIM), BF16),
            pltpu.VMEM((t + 2 * Q_BLK, KV_DIM), BF16),
            pltpu.VMEM((t + 2 * Q_BLK, KV_DIM), BF16),
            pltpu.VMEM((t, B_DIM), BF16),
            pltpu.VMEM((t + PAST_LEN, KV_DIM), BF16),
            pltpu.VMEM((t + PAST_LEN, KV_DIM), BF16),
        ],
        compiler_params=pltpu.CompilerParams(vmem_limit_bytes=VMEM_LIMIT),
        name="attn_sample",
    )(sink, za, *caches, cos, sin, gq, gk)


def _rwkv_body(*refs, t, nbatch, has_init, want_state, has_prev, ncast):
    refs = list(refs)
    zc_ref, cv_ref, wup_ref, aup_ref, gup_ref = refs[:5]
    pos = 5
    st0_ref = prev_ref = stout_ref = None
    if has_init:
        st0_ref = refs[pos]
        pos += 1
    if has_prev:
        prev_ref = refs[pos]
        pos += 1
    cast_src = refs[pos:pos + ncast]
    pos += ncast
    oc_ref = refs[pos]
    pos += 1
    if want_state:
        stout_ref = refs[pos]
        pos += 1
    cast_dst = refs[pos:pos + ncast]
    pos += ncast
    aa_s, ld_s, b_s, kt_s, bon_s, y_s, st_s = refs[pos:]

    _cast_blocks(cast_src, cast_dst)

    nchunks = t // CHUNK
    ones = _head_ones(C_DIM)
    k_k = cv_ref[0:1]
    k_a = cv_ref[1:2]
    r_k = cv_ref[2:3]
    ln_w = cv_ref[3:4]
    ln_b = cv_ref[4:5]
    w0h = (0.5 * cv_ref[5:6], 0.5 * cv_ref[6:7])
    a0h = (0.5 * cv_ref[7:8], 0.5 * cv_ref[8:9])
    wup_h = ((0.5 * wup_ref[0]).astype(BF16), (0.5 * wup_ref[1]).astype(BF16))
    aup_h = ((0.5 * aup_ref[0]).astype(BF16), (0.5 * aup_ref[1]).astype(BF16))
    c1 = 1.0 - 0.5 * k_a
    c2 = 0.5 * k_a
    half_rate = 0.5 * DECAY_RATE

    for rb in range(nbatch * t // RWKV_ROWS):
        rows = slice(rb * RWKV_ROWS, (rb + 1) * RWKV_ROWS)
        zr = zc_ref[rows, C_R]
        zk = zc_ref[rows, C_K]
        zv = zc_ref[rows, C_V]
        tw = jnp.tanh(zc_ref[rows, C_W])
        xa = zc_ref[rows, C_A]
        kkr = zk * k_k
        kk_half = kkr * (0.5 * lax.rsqrt(_head_sum(kkr * kkr, ones) + 1e-12))
        aa_s[rows] = -2.0 * kk_half
        kt_sum = None
        for d in range(2):
            th_w = jnp.tanh(w0h[d] + _bdot(tw, wup_h[d]))
            ld_s[d, rows] = th_w * (-half_rate) - half_rate
            th_a = jnp.tanh(a0h[d] + _bdot(xa, aup_h[d]))
            kt = zk * (c1 + c2 * th_a)
            kt_s[d, rows] = kt
            b_s[d, rows] = kk_half * th_a + kk_half
            kt_sum = kt if kt_sum is None else kt_sum + kt
        bon_s[rows] = _head_sum(zr * kt_sum * r_k, ones) * zv
        y_s[rows] = jnp.zeros((RWKV_ROWS, C_DIM), F32)

    ir = lax.broadcasted_iota(jnp.int32, (CHUNK, LANES), 0)
    ic = lax.broadcasted_iota(jnp.int32, (CHUNK, LANES), 1) & (HEAD_DIM - 1)
    eye = jnp.where(ir == ic, 1.0, 0.0).astype(F32)
    incl = (ic <= ir, ic >= ir)
    strict = (ic < ir, ic > ir)
    sr = lax.broadcasted_iota(jnp.int32, (CHUNK, CHUNK), 0)
    sc = lax.broadcasted_iota(jnp.int32, (CHUNK, CHUNK), 1)
    tri = tuple(jnp.where(m, 1.0, 0.0).astype(BF16) for m in (sc <= sr, sc >= sr))
    br = lax.broadcasted_iota(jnp.int32, (LANES, LANES), 0)
    bc = lax.broadcasted_iota(jnp.int32, (LANES, LANES), 1)
    diag_blocks = (br < HEAD_DIM) == (bc < HEAD_DIM)
    left_half = lax.broadcasted_iota(jnp.int32, (CHUNK, LANES), 1) < HEAD_DIM

    def level_mask(rows, cols, j):
        same_big = lax.shift_right_logical(rows, jnp.int32(j + 1)) == lax.shift_right_logical(cols, jnp.int32(j + 1))
        same_small = lax.shift_right_logical(rows, jnp.int32(j)) == lax.shift_right_logical(cols, jnp.int32(j))
        return same_big & jnp.logical_not(same_small)

    first_level = level_mask(ir, ic, 0)
    level_blocks = [diag_blocks & level_mask(br & (HEAD_DIM - 1), bc & (HEAD_DIM - 1), j) for j in range(1, CHUNK.bit_length() - 1)]

    def blockdiag(x2):
        x2 = x2.astype(BF16)
        return jnp.where(diag_blocks, jnp.concatenate([x2, x2], axis=0), jnp.zeros((LANES, LANES), BF16))

    npairs = C_HEADS // 2
    for bi in range(nbatch):
        for d in range(2):
            for p in range(npairs):
                if has_init:
                    st_s[bi, d, p] = jnp.concatenate([st0_ref[bi, d, 2 * p], st0_ref[bi, d, 2 * p + 1]], axis=1)
                else:
                    st_s[bi, d, p] = jnp.zeros((HEAD_DIM, LANES), F32)

    def chunk(c, carry):
        c = _index32(c)
        items = []
        for bi, d in [(bi, d) for bi in range(nbatch) for d in range(2)]:
            r0 = bi * t + ((nchunks - 1 - c) if d == 1 else c) * CHUNK
            rows = pl.ds(pl.multiple_of(r0, CHUNK), CHUNK)
            ld = ld_s[d, rows]
            cinc = _split_dot_left(tri[d], ld)
            cexc = cinc - ld
            ctot = cinc[0:1] if d == 1 else cinc[CHUNK - 1:CHUNK]
            e_ninc = jnp.exp(-cinc)
            e_rem = jnp.exp(ctot - cinc)
            e_tot = jnp.exp(ctot)
            bb = b_s[d, rows]
            kt = kt_s[d, rows]
            at = (aa_s[rows] * jnp.exp(cexc)).astype(BF16)
            rt = (zc_ref[rows, C_R] * jnp.exp(cinc)).astype(BF16)
            bt = (bb * e_ninc).astype(BF16)
            ktt = (kt * e_ninc).astype(BF16)
            bh = (bb * e_rem).astype(BF16)
            kh = (kt * e_rem).astype(BF16)
            vm = zc_ref[rows, C_V].astype(BF16)
            for p in range(npairs):
                ps = slice(p * LANES, (p + 1) * LANES)
                items.append(dict(bi=bi, d=d, p=p, rows=rows, e_tot=e_tot[:, ps], at=at[:, ps], rt=rt[:, ps],
                                  bt=bt[:, ps], kt=ktt[:, ps], bh=bh[:, ps], kh=kh[:, ps], vm=vm[:, ps]))
        for it in items:
            it["ar"] = jnp.concatenate([it["at"], it["rt"]], axis=0)
            it["vmd"] = blockdiag(it["vm"])
        for it in items:
            sbk = _bdot_nt(it["ar"], jnp.concatenate([blockdiag(it["bt"]), blockdiag(it["kt"])], axis=0))
            it["sb"] = sbk[:, 0:LANES]
            it["sk"] = sbk[:, LANES:2 * LANES]
        for it in items:
            d = it["d"]
            it["l"] = jnp.where(strict[d], it["sb"][0:CHUNK], 0.0)
            it["mak"] = jnp.where(strict[d], it["sk"][0:CHUNK], 0.0).astype(BF16)
            it["nr"] = jnp.concatenate([jnp.where(incl[d], it["sb"][CHUNK:], 0.0).astype(BF16),
                                        jnp.where(incl[d], it["sk"][CHUNK:], 0.0).astype(BF16)], axis=1)
            it["tm"] = eye + jnp.where(first_level, it["l"], 0.0)
            lb = it["l"].astype(BF16)
            it["l2"] = jnp.concatenate([lb, lb], axis=0)
        for lvl in level_blocks:
            for it in items:
                it["t1"] = _bdot(it["tm"], jnp.where(lvl, it["l2"], jnp.zeros((LANES, LANES), BF16)))
            for it in items:
                it["tm"] = it["tm"] + _bdot(it["t1"], blockdiag(it["tm"]))
        for it in items:
            it["mv"] = _bdot(it["mak"], it["vmd"])
        for it in items:
            it["tau"] = _bdot(it["tm"], jnp.concatenate([blockdiag(it["at"]), blockdiag(it["mv"])], axis=1))
        for it in items:
            it["s0"] = st_s[it["bi"], it["d"], it["p"]]
            it["x"] = _bdot_nt(jnp.concatenate([it["tau"][:, 0:LANES].astype(BF16), it["rt"]], axis=0),
                               blockdiag(it["s0"]))
        for it in items:
            it["u"] = (it["x"][0:CHUNK] + it["tau"][:, LANES:2 * LANES]).astype(BF16)
        for it in items:
            it["y"] = it["x"][CHUNK:] + _bdot(it["nr"], jnp.concatenate([blockdiag(it["u"]), it["vmd"]], axis=0))
        for it in items:
            full = _bdot_tn(jnp.concatenate([it["u"], it["vm"]], axis=0),
                            jnp.concatenate([it["bh"], it["kh"]], axis=0))
            st_s[it["bi"], it["d"], it["p"]] = (it["s0"] * it["e_tot"]
                                                + jnp.where(left_half, full[0:HEAD_DIM], full[HEAD_DIM:2 * HEAD_DIM]))
        for i in range(0, len(items), npairs):
            rows = items[i]["rows"]
            y_s[rows] = y_s[rows] + jnp.concatenate([it["y"] for it in items[i:i + npairs]], axis=1)
        return carry

    lax.fori_loop(0, nchunks, chunk, 0)
    if want_state:
        for bi in range(nbatch):
            if has_prev:
                stout_ref[bi, 0] = prev_ref[bi]
            for d in range(2):
                for p in range(npairs):
                    s2 = st_s[bi, d, p]
                    for q in range(2):
                        if has_prev:
                            stout_ref[bi, 1, d, 2 * p + q] = s2[:, q * HEAD_DIM:(q + 1) * HEAD_DIM]
                        else:
                            stout_ref[bi, d, 2 * p + q] = s2[:, q * HEAD_DIM:(q + 1) * HEAD_DIM]

    gup = gup_ref[...].astype(BF16)
    for rb in range(nbatch * t // RWKV_ROWS):
        rows = slice(rb * RWKV_ROWS, (rb + 1) * RWKV_ROWS)
        y = y_s[rows]
        mu = _head_sum(y, ones) * (1.0 / HEAD_DIM)
        yc = y - mu
        var = _head_sum(yc * yc, ones) * (1.0 / HEAD_DIM)
        yn = yc * lax.rsqrt(var + GN_EPS)
        g = _bdot(_sigmoid(zc_ref[rows, C_G]), gup)
        oc_ref[rows] = ((yn * ln_w + ln_b + bon_s[rows]) * g).astype(BF16)


def _rwkv_call(zc, rvec, wup, aup, gup, layer, t, init_state, want_state, prev_state, cast_weights=()):
    nb = zc.shape[0] // t
    nbatch = min(RWKV_BATCH, nb)
    rows = nbatch * t
    has_init = init_state is not None
    has_prev = prev_state is not None
    st_shape = (2, C_HEADS, HEAD_DIM, HEAD_DIM)
    in_specs = [
        pl.BlockSpec((rows, RWKV_COLS), lambda b: (b, 0), pipeline_mode=pl.Buffered(1 if nb == nbatch else 2)),
        _layer_spec((16, C_DIM), layer),
        _layer_spec((2, W_RANK, C_DIM), layer),
        _layer_spec((2, A_RANK, C_DIM), layer),
        _layer_spec((G_RANK, C_DIM), layer),
    ]
    args = [zc, rvec, wup, aup, gup]
    if has_init:
        in_specs.append(pl.BlockSpec((nbatch, None) + st_shape, lambda b: (b, layer, 0, 0, 0, 0)))
        args.append(init_state)
    if has_prev:
        in_specs.append(pl.BlockSpec((nbatch,) + st_shape, lambda b: (b, 0, 0, 0, 0)))
        args.append(prev_state)
    steps = nb // nbatch
    cast_in, cast_shape, cast_out = _cast_specs(cast_weights, layer, steps)
    in_specs.extend(cast_in)
    args.extend(cast_weights)
    out_shape = [jax.ShapeDtypeStruct((nb * t, C_DIM), BF16)]
    out_specs = [pl.BlockSpec((rows, C_DIM), lambda b: (b, 0))]
    if want_state and has_prev:
        out_shape.append(jax.ShapeDtypeStruct((nb, DEPTH) + st_shape, F32))
        out_specs.append(pl.BlockSpec((nbatch, DEPTH) + st_shape, lambda b: (b, 0, 0, 0, 0, 0)))
    elif want_state:
        out_shape.append(jax.ShapeDtypeStruct((nb,) + st_shape, F32))
        out_specs.append(pl.BlockSpec((nbatch,) + st_shape, lambda b: (b, 0, 0, 0, 0)))
    out_shape.extend(cast_shape)
    out_specs.extend(cast_out)
    return pl.pallas_call(
        functools.partial(_rwkv_body, t=t, nbatch=nbatch, has_init=has_init, want_state=want_state,
                          has_prev=has_prev, ncast=len(cast_weights)),
        out_shape=tuple(out_shape),
        grid=(steps,),
        in_specs=in_specs,
        out_specs=tuple(out_specs),
        scratch_shapes=[
            pltpu.VMEM((rows, C_DIM), F32),
            pltpu.VMEM((2, rows, C_DIM), F32),
            pltpu.VMEM((2, rows, C_DIM), F32),
            pltpu.VMEM((2, rows, C_DIM), F32),
            pltpu.VMEM((rows, C_DIM), F32),
            pltpu.VMEM((rows, C_DIM), F32),
            pltpu.VMEM((nbatch, 2, C_HEADS // 2, HEAD_DIM, LANES), F32),
        ],
        compiler_params=pltpu.CompilerParams(vmem_limit_bytes=VMEM_LIMIT),
        name="rwkv_prompt" if want_state else "rwkv_sample",
    )(*args)


def _post_tile(oa_ref, oc_ref, x_ref, mod, g_ref, wout_ref, wgu_ref, wdn_ref, xo_ref):
    gate1 = mod[:, 2 * D_MODEL:3 * D_MODEL]
    shift2 = mod[:, 3 * D_MODEL:4 * D_MODEL]
    scale2 = mod[:, 4 * D_MODEL:5 * D_MODEL]
    gate2 = mod[:, 5 * D_MODEL:6 * D_MODEL]
    sub = ROW_TILE // ROW_SPLIT
    parts = [slice(i * sub, (i + 1) * sub) for i in range(ROW_SPLIT)]
    o = [jnp.dot(jnp.concatenate([oa_ref[r, :], oc_ref[r, :]], axis=1), wout_ref[...],
                 preferred_element_type=F32) for r in parts]
    x1 = [x_ref[r, :] + gate1 * _rms(oi, g_ref[1:2]) for r, oi in zip(parts, o)]
    h2 = [(_rms(xi, g_ref[2:3]) * (1.0 + scale2) + shift2).astype(BF16) for xi in x1]

    def gate_up(step):
        (lo, hi), i = step
        g = jnp.dot(h2[i], wgu_ref[:, lo:hi], preferred_element_type=F32)
        u = jnp.dot(h2[i], wgu_ref[:, FF_DIM + lo:FF_DIM + hi], preferred_element_type=F32)
        return g, u

    steps = [(chunk, i) for chunk in FF_CHUNKS for i in range(ROW_SPLIT)]
    acc = [None] * ROW_SPLIT
    pending = gate_up(steps[0])
    for k, ((lo, hi), i) in enumerate(steps):
        g, u = pending
        if k + 1 < len(steps):
            pending = gate_up(steps[k + 1])
        act = (g * _sigmoid(g) * u).astype(BF16)
        part = jnp.dot(act, wdn_ref[lo:hi, :], preferred_element_type=F32)
        acc[i] = part if acc[i] is None else acc[i] + part
    for r, xi, ai in zip(parts, x1, acc):
        xo_ref[r, :] = xi + gate2 * _rms(ai, g_ref[3:4])


def _post_body(oap_ref, oas_ref, ocp_ref, ocs_ref, xp_ref, xs_ref, mod_ref, g_ref, wout_ref, wgu_ref, wdn_ref,
               xop_ref, xos_ref, *, n_first):
    step = pl.program_id(0)

    @pl.when(step < n_first)
    def _():
        _post_tile(oap_ref, ocp_ref, xp_ref, mod_ref[0:1], g_ref, wout_ref, wgu_ref, wdn_ref, xop_ref)

    @pl.when(step >= n_first)
    def _():
        _post_tile(oas_ref, ocs_ref, xs_ref, _latent_mod_row(mod_ref, step, n_first), g_ref, wout_ref, wgu_ref,
                   wdn_ref, xos_ref)


def _post_call(oatt_p, oatt_s, oc_p, oc_s, xp, xs, mods, gvec, wout_b, wgu_b, wdn_b, layer):
    n_first = xp.shape[0] // ROW_TILE
    n_second = xs.shape[0] // ROW_TILE
    x_first, x_second = _two_group_specs(D_MODEL, n_first)
    return pl.pallas_call(
        functools.partial(_post_body, n_first=n_first),
        out_shape=(jax.ShapeDtypeStruct(xp.shape, F32), jax.ShapeDtypeStruct(xs.shape, F32)),
        grid=(n_first + n_second,),
        in_specs=[
            *_two_group_specs(ATT_OUT, n_first),
            *_two_group_specs(C_DIM, n_first),
            x_first, x_second,
            _layer_spec((MOD_ROWS, MOD_COLS), layer),
            _layer_spec((8, D_MODEL), layer),
            _whole_spec((MIX_DIM, D_MODEL)),
            _whole_spec((D_MODEL, 2 * FF_DIM)),
            _whole_spec((FF_DIM, D_MODEL)),
        ],
        out_specs=(x_first, x_second),
        compiler_params=pltpu.CompilerParams(dimension_semantics=("arbitrary",), vmem_limit_bytes=VMEM_LIMIT),
        name="post_ffn",
    )(oatt_p, oatt_s, oc_p, oc_s, xp, xs, mods, gvec, wout_b, wgu_b, wdn_b)


def _rope_tables(t):
    pos = jnp.arange(t)
    row = (pos // GRID_W).astype(F32)
    col = (pos % GRID_W).astype(F32)
    freqs = ROPE_THETA ** (-jnp.arange(ROPE_PAIRS_AXIS, dtype=F32) / ROPE_PAIRS_AXIS)
    ang = jnp.concatenate([row[:, None] * freqs, col[:, None] * freqs], axis=-1)
    cos = jnp.tile(jnp.cos(ang), (1, 4))
    sin = jnp.sin(ang)
    sin = jnp.tile(jnp.concatenate([-sin, sin], axis=-1), (1, 2))
    return cos, sin


def _to_time_minor(cache):
    b, l, t = cache.shape[:3]
    return jnp.transpose(cache, (0, 1, 3, 4, 2)).reshape(b, l, KV_DIM, t)


def _from_time_minor(slab):
    b, l, _, t = slab.shape
    return jnp.transpose(slab.reshape(b, l, 2, HEAD_DIM, t), (0, 1, 4, 2, 3))


def kernel(x_prompt, x_sample, cache_a_k, cache_a_v, cache_b_k, cache_b_v, state_c, c, c_ctx, w_mod, b_mod, norm_mix_pre, norm_mix_post, norm_ffn_pre, norm_ffn_post, w_in, w_out, a_sink, b_q_norm, b_k_norm, c_w0, c_w_up, c_a0, c_a_up, c_g_up, c_k_k, c_k_a, c_r_k, c_ln_w, c_ln_b, w_gu, w_down):
    cvec = jnp.zeros((MOD_ROWS, D_MODEL), F32).at[0].set(c_ctx).at[1:1 + DEC_BATCH].set(c)
    mods = _mod_call(cvec, w_mod, b_mod)
    cos, sin = _rope_tables(DEC_SEQ)

    gvec = jnp.zeros((DEPTH, 8, D_MODEL), F32)
    gvec = gvec.at[:, 0].set(norm_mix_pre).at[:, 1].set(norm_mix_post)
    gvec = gvec.at[:, 2].set(norm_ffn_pre).at[:, 3].set(norm_ffn_post)
    rvec = jnp.zeros((DEPTH, 16, C_DIM), F32)
    rvec = rvec.at[:, 0].set(c_k_k).at[:, 1].set(c_k_a).at[:, 2].set(c_r_k.reshape(DEPTH, C_DIM))
    rvec = rvec.at[:, 3].set(c_ln_w).at[:, 4].set(c_ln_b)
    rvec = rvec.at[:, 5:7].set(c_w0).at[:, 7:9].set(c_a0)
    gq = jnp.tile(b_q_norm, (1, B_HEADS))[:, None, :]
    gk = jnp.tile(b_k_norm, (1, B_KV_HEADS))[:, None, :]
    caches = tuple(_to_time_minor(t) for t in (cache_a_k, cache_a_v, cache_b_k, cache_b_v))

    xp = x_prompt.reshape(BATCH * SEQ, D_MODEL)
    xs = x_sample.reshape(DEC_BATCH * DEC_SEQ, D_MODEL)
    kv_prev = st_prev = None
    for l in range(DEPTH):
        za_p, zc_p, za_s, zc_s, wout_b, wdn_b = _proj_in_call(xp, xs, mods, gvec, w_in, l, (w_out, w_down))
        oatt_p, *kv_prev = _attn_prompt_call(za_p, a_sink, gq, gk, l, kv_prev)
        if l == 0:
            (kv_prev,) = kv_prev
        oc_p, st_prev, wgu_b = _rwkv_call(zc_p, rvec, c_w_up, c_a_up, c_g_up, l, SEQ, None, True, st_prev,
                                          cast_weights=(w_gu,))
        oatt_s = _attn_sample_call(za_s, a_sink, caches, cos, sin, gq, gk, l)
        (oc_s,) = _rwkv_call(zc_s, rvec, c_w_up, c_a_up, c_g_up, l, DEC_SEQ, state_c, False, None)
        xp, xs = _post_call(oatt_p, oatt_s, oc_p, oc_s, xp, xs, mods, gvec, wout_b, wgu_b, wdn_b, l)

    new_caches = tuple(_from_time_minor(slab) for slab in kv_prev)
    return (xp.reshape(BATCH, SEQ, D_MODEL), xs.reshape(DEC_BATCH, DEC_SEQ, D_MODEL), *new_caches, st_prev)
```

```python
import functools
import math

import jax
import jax.numpy as jnp
from jax import lax
from jax.experimental import pallas as pl
from jax.experimental.pallas import tpu as pltpu

D_MODEL = 1024
BATCH = 32
SEQ = 256
DEPTH = 2
DEC_BATCH = 2
DEC_SEQ = 1024
PAST_LEN = 256
GRID_W = 64
HEAD_DIM = 64
A_HEADS = 4
A_KV_HEADS = 2
B_HEADS = 6
B_KV_HEADS = 2
C_HEADS = 6
A_DIM = A_HEADS * HEAD_DIM
B_DIM = B_HEADS * HEAD_DIM
C_DIM = C_HEADS * HEAD_DIM
MIX_DIM = A_DIM + B_DIM + C_DIM
WINDOW = 128
Q_BLK = 128
W_RANK = 64
A_RANK = 64
G_RANK = 128
FF_DIM = -(-8 * D_MODEL // (3 * 256)) * 256
ROPE_THETA = 10000.0
ROPE_PAIRS_AXIS = HEAD_DIM // 4
NORM_EPS = 1e-6
GN_EPS = 64e-5
NEG_INF = -1e30

LANES = 128
KV_DIM = 2 * HEAD_DIM
assert KV_DIM == LANES


def _col_ranges(widths):
    edges = [0]
    for w in widths:
        edges.append(edges[-1] + w)
    return tuple(slice(lo, hi) for lo, hi in zip(edges[:-1], edges[1:])), edges[-1]


(A_Q, A_K, A_V, B_Q, B_K, B_V), ATT_COLS = _col_ranges((A_DIM, KV_DIM, KV_DIM, B_DIM, KV_DIM, KV_DIM))
(C_R, C_K, C_V, C_W, C_A, C_G), RWKV_COLS = _col_ranges((C_DIM, C_DIM, C_DIM, W_RANK, A_RANK, G_RANK))
IN_COLS = ATT_COLS + RWKV_COLS
ATT_OUT = A_DIM + B_DIM
MOD_COLS = 6 * D_MODEL
MOD_ROWS = 8
N_CACHE = 4
CHUNK = 64
assert CHUNK == HEAD_DIM
QK_SCALE = HEAD_DIM ** -0.5
DECAY_RATE = math.exp(-0.5)
ROW_TILE = 512
ROW_SPLIT = 2
RWKV_ROWS = 512
RWKV_BATCH = 4
B_QROWS = 256
ATTN_BATCH = 4
MXU_TILE = 256
FF_CHUNKS = ((0, 4 * MXU_TILE), (4 * MXU_TILE, 8 * MXU_TILE), (8 * MXU_TILE, FF_DIM))
assert FF_DIM % MXU_TILE == 0
VMEM_LIMIT = 56 * 1024 * 1024

F32 = jnp.float32
BF16 = jnp.bfloat16

assert DEPTH == 2


def _bdot(a, b):
    return jnp.dot(a.astype(BF16), b.astype(BF16), preferred_element_type=F32)


def _bdot_nt(a, b):
    return lax.dot_general(a.astype(BF16), b.astype(BF16), (((1,), (1,)), ((), ())), preferred_element_type=F32)


def _bdot_tn(a, b):
    return lax.dot_general(a.astype(BF16), b.astype(BF16), (((0,), (0,)), ((), ())), preferred_element_type=F32)


def _split(x):
    hi = x.astype(BF16)
    lo = (x - hi.astype(F32)).astype(BF16)
    return hi, lo


def _split_dot_left(m, x):
    hi, lo = _split(x)
    return jnp.dot(m, hi, preferred_element_type=F32) + jnp.dot(m, lo, preferred_element_type=F32)


def _head_sum(x, ones):
    return jnp.dot(x.astype(BF16), ones, preferred_element_type=F32)


def _sigmoid(x):
    return 0.5 * jnp.tanh(0.5 * x) + 0.5


def _head_ones(n):
    r = lax.shift_right_logical(lax.broadcasted_iota(jnp.int32, (n, n), 0), jnp.int32(HEAD_DIM.bit_length() - 1))
    c = lax.shift_right_logical(lax.broadcasted_iota(jnp.int32, (n, n), 1), jnp.int32(HEAD_DIM.bit_length() - 1))
    return jnp.where(r == c, 1.0, 0.0).astype(BF16)


def _rms(x, g):
    ms = jnp.mean(x * x, axis=-1, keepdims=True)
    return x * lax.rsqrt(ms + NORM_EPS) * g


def _head_rms(x, g, ones):
    ms = _head_sum(x * x, ones) * (1.0 / HEAD_DIM)
    return x * lax.rsqrt(ms + NORM_EPS) * g


def _rope(x, cos, sin):
    t = x.shape[0]
    lane = lax.broadcasted_iota(jnp.int32, (t, LANES), 1)
    first = (lane & (HEAD_DIM // 2)) == 0
    outs = []
    for i in range(x.shape[1] // LANES):
        xs = x[:, i * LANES:(i + 1) * LANES]
        swapped = jnp.where(first, pltpu.roll(xs, LANES - HEAD_DIM // 2, axis=1), pltpu.roll(xs, HEAD_DIM // 2, axis=1))
        outs.append(xs * cos + swapped * sin)
    return outs[0] if len(outs) == 1 else jnp.concatenate(outs, axis=1)


def _softmax_pv(scores, values, sink=None):
    m = scores[0].max(axis=-1, keepdims=True)
    for s in scores[1:]:
        m = jnp.maximum(m, s.max(axis=-1, keepdims=True))
    if sink is not None:
        m = jnp.maximum(m, sink)
    denom = None
    acc = None
    for s, v in zip(scores, values):
        p = jnp.exp(s - m)
        d = p.sum(axis=-1, keepdims=True)
        o = jnp.dot(p.astype(BF16), v, preferred_element_type=F32)
        denom = d if denom is None else denom + d
        acc = o if acc is None else acc + o
    if sink is not None:
        denom = denom + jnp.exp(sink - m)
    return acc / denom


def _softmax_pv_t(scores_t, values_t, sink=None):
    m = scores_t[0].max(axis=0, keepdims=True)
    for s in scores_t[1:]:
        m = jnp.maximum(m, s.max(axis=0, keepdims=True))
    if sink is not None:
        m = jnp.maximum(m, sink)
    denom = None
    acc = None
    for s, vt in zip(scores_t, values_t):
        p = jnp.exp(s - m)
        d = p.sum(axis=0, keepdims=True)
        o = jnp.dot(vt, p.astype(BF16), preferred_element_type=F32)
        denom = d if denom is None else denom + d
        acc = o if acc is None else acc + o
    if sink is not None:
        denom = denom + jnp.exp(sink - m)
    return acc / denom


def _attend_heads(score_fns, finish_fns, ahead=1):
    n = len(score_fns)
    pending = [fn() for fn in score_fns[:ahead]]
    outs = []
    for i in range(n):
        if i + ahead < n:
            pending.append(score_fns[i + ahead]())
        outs.append(finish_fns[i](pending.pop(0)))
    return outs


def _index32(i):
    return jnp.asarray(i, jnp.int32)


def _layer_spec(shape, layer):
    zeros = (0,) * len(shape)
    return pl.BlockSpec((None,) + shape, lambda *_: (layer,) + zeros, pipeline_mode=pl.Buffered(1))


def _mod_body(c_ref, w_ref, b_ref, o_ref):
    cv = c_ref[...]
    s = cv * _sigmoid(cv)
    o_ref[0] = _bdot(s, w_ref[0]) + b_ref[0]


def _mod_call(cvec, w_mod, b_mod):
    bn = MOD_COLS // 4
    return pl.pallas_call(
        _mod_body,
        out_shape=jax.ShapeDtypeStruct((DEPTH, MOD_ROWS, MOD_COLS), F32),
        grid=(DEPTH, MOD_COLS // bn),
        in_specs=[
            pl.BlockSpec((MOD_ROWS, D_MODEL), lambda l, j: (0, 0)),
            pl.BlockSpec((1, D_MODEL, bn), lambda l, j: (l, 0, j)),
            pl.BlockSpec((1, 1, bn), lambda l, j: (l, 0, j)),
        ],
        out_specs=pl.BlockSpec((1, MOD_ROWS, bn), lambda l, j: (l, 0, j)),
        compiler_params=pltpu.CompilerParams(vmem_limit_bytes=VMEM_LIMIT),
        name="mod_vectors",
    )(cvec, w_mod, b_mod.reshape(DEPTH, 1, MOD_COLS))


def _cast_blocks(srcs, dsts):
    for src, dst in zip(srcs, dsts):
        dst[...] = src[...].astype(BF16)


def _cast_specs(weights, layer, steps):
    def block(i):
        return jnp.minimum(i, steps - 1)

    in_specs = [pl.BlockSpec((None, w.shape[1] // steps, w.shape[2]), lambda i: (layer, block(i), 0)) for w in weights]
    out_shape = [jax.ShapeDtypeStruct(w.shape[1:], BF16) for w in weights]
    out_specs = [pl.BlockSpec((w.shape[1] // steps, w.shape[2]), lambda i: (block(i), 0)) for w in weights]
    return in_specs, out_shape, out_specs


def _two_group_specs(cols, n_first):
    first = pl.BlockSpec((ROW_TILE, cols), lambda i: (jnp.minimum(i, n_first - 1), 0))
    second = pl.BlockSpec((ROW_TILE, cols), lambda i: (jnp.maximum(i - n_first, 0), 0))
    return first, second


def _latent_mod_row(mod_ref, step, n_first):
    row = 1 + (step - n_first) // (DEC_SEQ // ROW_TILE)
    return mod_ref[pl.ds(row, 1), :]


def _proj_in_body(xp_ref, xs_ref, mod_ref, g_ref, w_ref, *rest, n_first, ncast):
    cast_src = rest[:ncast]
    zap_ref, zcp_ref, zas_ref, zcs_ref = rest[ncast:ncast + 4]
    cast_dst = rest[ncast + 4:2 * ncast + 4]
    wb_s = rest[2 * ncast + 4]
    step = pl.program_id(0)

    @pl.when(step == 0)
    def _():
        wb_s[...] = w_ref[...].astype(BF16)

    def project(x_ref, mod, za_ref, zc_ref):
        shift1 = mod[:, 0:D_MODEL]
        scale1 = mod[:, D_MODEL:2 * D_MODEL]
        h = (_rms(x_ref[...], g_ref[0:1]) * (1.0 + scale1) + shift1).astype(BF16)
        z = jnp.dot(h, wb_s[...], preferred_element_type=F32)
        za_ref[...] = z[:, 0:ATT_COLS]
        zc_ref[...] = z[:, ATT_COLS:IN_COLS]

    @pl.when(step < n_first)
    def _():
        _cast_blocks(cast_src, cast_dst)
        project(xp_ref, mod_ref[0:1], zap_ref, zcp_ref)

    @pl.when(step >= n_first)
    def _():
        project(xs_ref, _latent_mod_row(mod_ref, step, n_first), zas_ref, zcs_ref)


def _proj_in_call(xp, xs, mods, gvec, w_in, layer, cast_weights):
    n_first = xp.shape[0] // ROW_TILE
    n_second = xs.shape[0] // ROW_TILE
    cast_in, cast_shape, cast_out = _cast_specs(cast_weights, layer, n_first)
    x_first, x_second = _two_group_specs(D_MODEL, n_first)
    za_first, za_second = _two_group_specs(ATT_COLS, n_first)
    zc_first, zc_second = _two_group_specs(RWKV_COLS, n_first)
    return pl.pallas_call(
        functools.partial(_proj_in_body, n_first=n_first, ncast=len(cast_weights)),
        out_shape=(jax.ShapeDtypeStruct((xp.shape[0], ATT_COLS), F32), jax.ShapeDtypeStruct((xp.shape[0], RWKV_COLS), F32),
                   jax.ShapeDtypeStruct((xs.shape[0], ATT_COLS), F32), jax.ShapeDtypeStruct((xs.shape[0], RWKV_COLS), F32),
                   *cast_shape),
        grid=(n_first + n_second,),
        in_specs=[
            x_first, x_second,
            _layer_spec((MOD_ROWS, MOD_COLS), layer),
            _layer_spec((8, D_MODEL), layer),
            _layer_spec((D_MODEL, IN_COLS), layer),
            *cast_in,
        ],
        out_specs=(za_first, zc_first, za_second, zc_second, *cast_out),
        scratch_shapes=[pltpu.VMEM((D_MODEL, IN_COLS), BF16)],
        compiler_params=pltpu.CompilerParams(dimension_semantics=("arbitrary",), vmem_limit_bytes=VMEM_LIMIT),
        name="proj_in",
    )(xp, xs, mods, gvec, w_in, *cast_weights)


def _attn_prompt_body(*refs, layer, nseq):
    sink_ref, za_ref, gq_ref, gk_ref = refs[:4]
    if layer == 0:
        o_ref, kv_ref = refs[4:]
    else:
        prev_ref, o_ref = refs[4:6]
        cache_refs = refs[6:]
    ones_q = _head_ones(B_DIM)
    ones_k = _head_ones(KV_DIM)
    score_fns, finish_fns = [], []
    for si in range(nseq):
        rows = slice(si * SEQ, (si + 1) * SEQ)
        aq = (za_ref[rows, A_Q] * QK_SCALE).astype(BF16)
        ak = za_ref[rows, A_K]
        bq = _head_rms(za_ref[rows, B_Q], gq_ref[...], ones_q)
        bk = _head_rms(za_ref[rows, B_K], gk_ref[...], ones_k)
        bqb = (bq * QK_SCALE).astype(BF16)
        akb = ak.astype(BF16)
        bkb = bk.astype(BF16)
        pieces_t = (ak.T, za_ref[rows, A_V].T, bk.T, za_ref[rows, B_V].T)
        avt = pieces_t[1].astype(BF16)
        bvt = pieces_t[3].astype(BF16)
        for j, piece_t in enumerate(pieces_t):
            if layer == 0:
                kv_ref[si, j] = piece_t
            else:
                cache_refs[j][si, 0] = prev_ref[si, j]
                cache_refs[j][si, 1] = piece_t

        for h in range(A_HEADS):
            kv = h // (A_HEADS // A_KV_HEADS)
            hs = slice(kv * HEAD_DIM, (kv + 1) * HEAD_DIM)
            score_fns.append(lambda h=h, hs=hs, akb=akb, aq=aq: [_bdot_nt(akb[:, hs], aq[:, h * HEAD_DIM:(h + 1) * HEAD_DIM])])
            finish_fns.append(lambda sc, h=h, hs=hs, avt=avt: _softmax_pv_t(sc, [avt[hs]], sink_ref[layer, h]))
        for h in range(B_HEADS):
            kv = h // (B_HEADS // B_KV_HEADS)
            hs = slice(kv * HEAD_DIM, (kv + 1) * HEAD_DIM)
            score_fns.append(lambda h=h, hs=hs, bkb=bkb, bqb=bqb: [_bdot_nt(bkb[:, hs], bqb[:, h * HEAD_DIM:(h + 1) * HEAD_DIM])])
            finish_fns.append(lambda sc, hs=hs, bvt=bvt: _softmax_pv_t(sc, [bvt[hs]]))
    outs = _attend_heads(score_fns, finish_fns, ahead=len(score_fns))
    heads = A_HEADS + B_HEADS
    for si in range(nseq):
        o_ref[si * SEQ:(si + 1) * SEQ, :] = jnp.concatenate(outs[si * heads:(si + 1) * heads], axis=0).T.astype(BF16)


def _attn_prompt_call(za, sink, gq, gk, layer, prev_kv):
    nb = za.shape[0] // SEQ
    nseq = ATTN_BATCH
    in_specs = [
        pl.BlockSpec(memory_space=pltpu.SMEM),
        pl.BlockSpec((nseq * SEQ, ATT_COLS), lambda b: (b, 0)),
        _layer_spec((1, B_DIM), layer),
        _layer_spec((1, KV_DIM), layer),
    ]
    args = [sink, za, gq, gk]
    out_shape = [jax.ShapeDtypeStruct((nb * SEQ, ATT_OUT), BF16)]
    out_specs = [pl.BlockSpec((nseq * SEQ, ATT_OUT), lambda b: (b, 0))]
    slab_spec = pl.BlockSpec((nseq, N_CACHE, KV_DIM, SEQ), lambda b: (b, 0, 0, 0))
    if layer == 0:
        out_shape.append(jax.ShapeDtypeStruct((nb, N_CACHE, KV_DIM, SEQ), F32))
        out_specs.append(slab_spec)
    else:
        in_specs.append(slab_spec)
        args.append(prev_kv)
        for _ in range(N_CACHE):
            out_shape.append(jax.ShapeDtypeStruct((nb, DEPTH, KV_DIM, SEQ), F32))
            out_specs.append(pl.BlockSpec((nseq, DEPTH, KV_DIM, SEQ), lambda b: (b, 0, 0, 0)))
    return pl.pallas_call(
        functools.partial(_attn_prompt_body, layer=layer, nseq=nseq),
        out_shape=tuple(out_shape),
        grid=(nb // nseq,),
        in_specs=in_specs,
        out_specs=tuple(out_specs),
        compiler_params=pltpu.CompilerParams(vmem_limit_bytes=VMEM_LIMIT),
        name="attn_prompt",
    )(*args)


def _attn_sample_body(sink_ref, za_ref, cak_ref, cav_ref, cbk_ref, cbv_ref, cos_ref, sin_ref, gq_ref, gk_ref,
                      o_ref, qa_s, ka_s, va_s, qb_s, kb_s, vb_s, *, layer):
    t = DEC_SEQ
    cos = cos_ref[...]
    sin = sin_ref[...]
    qa_s[...] = (_rope(za_ref[:, A_Q], cos, sin) * QK_SCALE).astype(BF16)
    zpad = jnp.zeros((Q_BLK, KV_DIM), BF16)
    ka_s[0:Q_BLK] = zpad
    ka_s[Q_BLK + t:] = zpad
    va_s[0:Q_BLK] = zpad
    va_s[Q_BLK + t:] = zpad
    ka_s[Q_BLK:Q_BLK + t] = _rope(za_ref[:, A_K], cos, sin).astype(BF16)
    va_s[Q_BLK:Q_BLK + t] = za_ref[:, A_V].astype(BF16)
    bq = _head_rms(za_ref[:, B_Q], gq_ref[...], _head_ones(B_DIM))
    qb_s[...] = (_rope(bq, cos, sin) * QK_SCALE).astype(BF16)
    bk = _head_rms(za_ref[:, B_K], gk_ref[...], _head_ones(KV_DIM))
    kb_s[0:PAST_LEN] = cbk_ref[0].T.astype(BF16)
    kb_s[PAST_LEN:] = _rope(bk, cos, sin).astype(BF16)
    vb_s[0:PAST_LEN] = cbv_ref[0].T.astype(BF16)
    vb_s[PAST_LEN:] = za_ref[:, B_V].astype(BF16)

    kca = cak_ref[0].T.astype(BF16)
    vca = cav_ref[0].T.astype(BF16)

    ga = A_HEADS // A_KV_HEADS
    gb = B_HEADS // B_KV_HEADS
    qpos = lax.broadcasted_iota(jnp.int32, (ga * Q_BLK, 3 * Q_BLK), 0) & (Q_BLK - 1)
    kpos = lax.broadcasted_iota(jnp.int32, (ga * Q_BLK, 3 * Q_BLK), 1) - Q_BLK
    near = jnp.abs(kpos - qpos) <= WINDOW
    head_of_row = lax.shift_right_logical(lax.broadcasted_iota(jnp.int32, (ga * Q_BLK, 1), 0),
                                          jnp.int32(Q_BLK.bit_length() - 1))
    sinks = []
    for kv in range(A_KV_HEADS):
        col = jnp.full((ga * Q_BLK, 1), sink_ref[layer, kv * ga], F32)
        for g in range(1, ga):
            col = jnp.where(head_of_row == g, sink_ref[layer, kv * ga + g], col)
        sinks.append(col)

    def stack_heads(q, kv, group):
        return jnp.concatenate([q[:, (kv * group + g) * HEAD_DIM:(kv * group + g + 1) * HEAD_DIM] for g in range(group)], axis=0)

    def unstack_heads(outs, group, rows):
        return jnp.concatenate([o[g * rows:(g + 1) * rows] for o in outs for g in range(group)], axis=1)

    def a_block(n, carry):
        r0 = pl.multiple_of(_index32(n) * Q_BLK, Q_BLK)
        q = qa_s[pl.ds(r0, Q_BLK), :]
        kl = ka_s[pl.ds(r0, 3 * Q_BLK), :]
        vl = va_s[pl.ds(r0, 3 * Q_BLK), :]
        kabs = kpos + r0
        mask = near & (kabs >= 0) & (kabs < t)
        score_fns, finish_fns = [], []
        for kv in range(A_KV_HEADS):
            hs = slice(kv * HEAD_DIM, (kv + 1) * HEAD_DIM)

            def scores(kv=kv, hs=hs):
                qs = stack_heads(q, kv, ga)
                return [jnp.where(mask, _bdot_nt(qs, kl[:, hs]), NEG_INF), _bdot_nt(qs, kca[:, hs])]

            score_fns.append(scores)
            finish_fns.append(lambda sc, kv=kv, hs=hs: _softmax_pv(sc, [vl[:, hs], vca[:, hs]], sinks[kv]))
        outs = _attend_heads(score_fns, finish_fns, ahead=A_KV_HEADS)
        o_ref[pl.ds(r0, Q_BLK), 0:A_DIM] = unstack_heads(outs, ga, Q_BLK).astype(BF16)
        return carry

    lax.fori_loop(0, t // Q_BLK, a_block, 0)

    def b_block(n, carry):
        r0 = pl.multiple_of(_index32(n) * B_QROWS, B_QROWS)
        q = qb_s[pl.ds(r0, B_QROWS), :]
        score_fns, finish_fns = [], []
        for kv in range(B_KV_HEADS):
            hs = slice(kv * HEAD_DIM, (kv + 1) * HEAD_DIM)
            score_fns.append(lambda kv=kv, hs=hs: [_bdot_nt(stack_heads(q, kv, gb), kb_s[:, hs])])
            finish_fns.append(lambda sc, hs=hs: _softmax_pv(sc, [vb_s[:, hs]]))
        outs = _attend_heads(score_fns, finish_fns)
        o_ref[pl.ds(r0, B_QROWS), A_DIM:ATT_OUT] = unstack_heads(outs, gb, B_QROWS).astype(BF16)
        return carry

    lax.fori_loop(0, t // B_QROWS, b_block, 0)


def _attn_sample_call(za, sink, caches, cos, sin, gq, gk, layer):
    nb = za.shape[0] // DEC_SEQ
    t = DEC_SEQ
    cache_spec = pl.BlockSpec((1, None, KV_DIM, PAST_LEN), lambda b: (b, layer, 0, 0))
    return pl.pallas_call(
        functools.partial(_attn_sample_body, layer=layer),
        out_shape=jax.ShapeDtypeStruct((nb * t, ATT_OUT), BF16),
        grid=(nb,),
        in_specs=[
            pl.BlockSpec(memory_space=pltpu.SMEM),
            pl.BlockSpec((t, ATT_COLS), lambda b: (b, 0)),
            cache_spec, cache_spec, cache_spec, cache_spec,
            pl.BlockSpec((t, LANES), lambda b: (0, 0)),
            pl.BlockSpec((t, LANES), lambda b: (0, 0)),
            _layer_spec((1, B_DIM), layer),
            _layer_spec((1, KV_DIM), layer),
        ],
        out_specs=pl.BlockSpec((t, ATT_OUT), lambda b: (b, 0)),
        scratch_shapes=[
            pltpu.VMEM((t, A_DIM), BF16),
            pltpu.VMEM((t + 2 * Q_BLK, KV_DIM), BF16),
            pltpu.VMEM((t + 2 * Q_BLK, KV_DIM), BF16),
            pltpu.VMEM((t, B_DIM), BF16),
            pltpu.VMEM((t + PAST_LEN, KV_DIM), BF16),
            pltpu.VMEM((t + PAST_LEN, KV_DIM), BF16),
        ],
        compiler_params=pltpu.CompilerParams(vmem_limit_bytes=VMEM_LIMIT),
        name="attn_sample",
    )(sink, za, *caches, cos, sin, gq, gk)


def _rwkv_body(*refs, t, nbatch, has_init, want_state, has_prev, ncast):
    refs = list(refs)
    zc_ref, cv_ref, wup_ref, aup_ref, gup_ref = refs[:5]
    pos = 5
    st0_ref = prev_ref = stout_ref = None
    if has_init:
        st0_ref = refs[pos]
        pos += 1
    if has_prev:
        prev_ref = refs[pos]
        pos += 1
    cast_src = refs[pos:pos + ncast]
    pos += ncast
    oc_ref = refs[pos]
    pos += 1
    if want_state:
        stout_ref = refs[pos]
        pos += 1
    cast_dst = refs[pos:pos + ncast]
    pos += ncast
    aa_s, ld_s, b_s, kt_s, bon_s, y_s, st_s = refs[pos:]

    _cast_blocks(cast_src, cast_dst)

    nchunks = t // CHUNK
    ones = _head_ones(C_DIM)
    k_k = cv_ref[0:1]
    k_a = cv_ref[1:2]
    r_k = cv_ref[2:3]
    ln_w = cv_ref[3:4]
    ln_b = cv_ref[4:5]
    w0h = (0.5 * cv_ref[5:6], 0.5 * cv_ref[6:7])
    a0h = (0.5 * cv_ref[7:8], 0.5 * cv_ref[8:9])
    wup_h = ((0.5 * wup_ref[0]).astype(BF16), (0.5 * wup_ref[1]).astype(BF16))
    aup_h = ((0.5 * aup_ref[0]).astype(BF16), (0.5 * aup_ref[1]).astype(BF16))
    c1 = 1.0 - 0.5 * k_a
    c2 = 0.5 * k_a
    half_rate = 0.5 * DECAY_RATE

    for rb in range(nbatch * t // RWKV_ROWS):
        rows = slice(rb * RWKV_ROWS, (rb + 1) * RWKV_ROWS)
        zr = zc_ref[rows, C_R]
        zk = zc_ref[rows, C_K]
        zv = zc_ref[rows, C_V]
        tw = jnp.tanh(zc_ref[rows, C_W])
        xa = zc_ref[rows, C_A]
        kkr = zk * k_k
        kk_half = kkr * (0.5 * lax.rsqrt(_head_sum(kkr * kkr, ones) + 1e-12))
        aa_s[rows] = -2.0 * kk_half
        kt_sum = None
        for d in range(2):
            th_w = jnp.tanh(w0h[d] + _bdot(tw, wup_h[d]))
            ld_s[d, rows] = th_w * (-half_rate) - half_rate
            th_a = jnp.tanh(a0h[d] + _bdot(xa, aup_h[d]))
            kt = zk * (c1 + c2 * th_a)
            kt_s[d, rows] = kt
            b_s[d, rows] = kk_half * th_a + kk_half
            kt_sum = kt if kt_sum is None else kt_sum + kt
        bon_s[rows] = _head_sum(zr * kt_sum * r_k, ones) * zv
        y_s[rows] = jnp.zeros((RWKV_ROWS, C_DIM), F32)

    ir = lax.broadcasted_iota(jnp.int32, (CHUNK, LANES), 0)
    ic = lax.broadcasted_iota(jnp.int32, (CHUNK, LANES), 1) & (HEAD_DIM - 1)
    eye = jnp.where(ir == ic, 1.0, 0.0).astype(F32)
    incl = (ic <= ir, ic >= ir)
    strict = (ic < ir, ic > ir)
    sr = lax.broadcasted_iota(jnp.int32, (CHUNK, CHUNK), 0)
    sc = lax.broadcasted_iota(jnp.int32, (CHUNK, CHUNK), 1)
    tri = tuple(jnp.where(m, 1.0, 0.0).astype(BF16) for m in (sc <= sr, sc >= sr))
    br = lax.broadcasted_iota(jnp.int32, (LANES, LANES), 0)
    bc = lax.broadcasted_iota(jnp.int32, (LANES, LANES), 1)
    diag_blocks = (br < HEAD_DIM) == (bc < HEAD_DIM)
    left_half = lax.broadcasted_iota(jnp.int32, (CHUNK, LANES), 1) < HEAD_DIM

    def level_mask(rows, cols, j):
        same_big = lax.shift_right_logical(rows, jnp.int32(j + 1)) == lax.shift_right_logical(cols, jnp.int32(j + 1))
        same_small = lax.shift_right_logical(rows, jnp.int32(j)) == lax.shift_right_logical(cols, jnp.int32(j))
        return same_big & jnp.logical_not(same_small)

    first_level = level_mask(ir, ic, 0)
    level_blocks = [diag_blocks & level_mask(br & (HEAD_DIM - 1), bc & (HEAD_DIM - 1), j) for j in range(1, CHUNK.bit_length() - 1)]

    def blockdiag(x2):
        x2 = x2.astype(BF16)
        return jnp.where(diag_blocks, jnp.concatenate([x2, x2], axis=0), jnp.zeros((LANES, LANES), BF16))

    npairs = C_HEADS // 2
    for bi in range(nbatch):
        for d in range(2):
            for p in range(npairs):
                if has_init:
                    st_s[bi, d, p] = jnp.concatenate([st0_ref[bi, d, 2 * p], st0_ref[bi, d, 2 * p + 1]], axis=1)
                else:
                    st_s[bi, d, p] = jnp.zeros((HEAD_DIM, LANES), F32)

    def chunk(c, carry):
        c = _index32(c)
        items = []
        for bi, d in [(bi, d) for bi in range(nbatch) for d in range(2)]:
            r0 = bi * t + ((nchunks - 1 - c) if d == 1 else c) * CHUNK
            rows = pl.ds(pl.multiple_of(r0, CHUNK), CHUNK)
            ld = ld_s[d, rows]
            cinc = _split_dot_left(tri[d], ld)
            cexc = cinc - ld
            ctot = cinc[0:1] if d == 1 else cinc[CHUNK - 1:CHUNK]
            e_ninc = jnp.exp(-cinc)
            e_rem = jnp.exp(ctot - cinc)
            e_tot = jnp.exp(ctot)
            bb = b_s[d, rows]
            kt = kt_s[d, rows]
            at = (aa_s[rows] * jnp.exp(cexc)).astype(BF16)
            rt = (zc_ref[rows, C_R] * jnp.exp(cinc)).astype(BF16)
            bt = (bb * e_ninc).astype(BF16)
            ktt = (kt * e_ninc).astype(BF16)
            bh = (bb * e_rem).astype(BF16)
            kh = (kt * e_rem).astype(BF16)
            vm = zc_ref[rows, C_V].astype(BF16)
            for p in range(npairs):
                ps = slice(p * LANES, (p + 1) * LANES)
                items.append(dict(bi=bi, d=d, p=p, rows=rows, e_tot=e_tot[:, ps], at=at[:, ps], rt=rt[:, ps],
                                  bt=bt[:, ps], kt=ktt[:, ps], bh=bh[:, ps], kh=kh[:, ps], vm=vm[:, ps]))
        for it in items:
            it["ar"] = jnp.concatenate([it["at"], it["rt"]], axis=0)
            it["vmd"] = blockdiag(it["vm"])
        for it in items:
            sbk = _bdot_nt(it["ar"], jnp.concatenate([blockdiag(it["bt"]), blockdiag(it["kt"])], axis=0))
            it["sb"] = sbk[:, 0:LANES]
            it["sk"] = sbk[:, LANES:2 * LANES]
        for it in items:
            d = it["d"]
            it["l"] = jnp.where(strict[d], it["sb"][0:CHUNK], 0.0)
            it["mak"] = jnp.where(strict[d], it["sk"][0:CHUNK], 0.0).astype(BF16)
            it["nr"] = jnp.concatenate([jnp.where(incl[d], it["sb"][CHUNK:], 0.0).astype(BF16),
                                        jnp.where(incl[d], it["sk"][CHUNK:], 0.0).astype(BF16)], axis=1)
            it["tm"] = eye + jnp.where(first_level, it["l"], 0.0)
            lb = it["l"].astype(BF16)
            it["l2"] = jnp.concatenate([lb, lb], axis=0)
        for lvl in level_blocks:
            for it in items:
                it["t1"] = _bdot(it["tm"], jnp.where(lvl, it["l2"], jnp.zeros((LANES, LANES), BF16)))
            for it in items:
                it["tm"] = it["tm"] + _bdot(it["t1"], blockdiag(it["tm"]))
        for it in items:
            it["mv"] = _bdot(it["mak"], it["vmd"])
        for it in items:
            it["tau"] = _bdot(it["tm"], jnp.concatenate([blockdiag(it["at"]), blockdiag(it["mv"])], axis=1))
        for it in items:
            it["s0"] = st_s[it["bi"], it["d"], it["p"]]
            it["x"] = _bdot_nt(jnp.concatenate([it["tau"][:, 0:LANES].astype(BF16), it["rt"]], axis=0),
                               blockdiag(it["s0"]))
        for it in items:
            it["u"] = (it["x"][0:CHUNK] + it["tau"][:, LANES:2 * LANES]).astype(BF16)
        for it in items:
            it["y"] = it["x"][CHUNK:] + _bdot(it["nr"], jnp.concatenate([blockdiag(it["u"]), it["vmd"]], axis=0))
        for it in items:
            full = _bdot_tn(jnp.concatenate([it["u"], it["vm"]], axis=0),
                            jnp.concatenate([it["bh"], it["kh"]], axis=0))
            st_s[it["bi"], it["d"], it["p"]] = (it["s0"] * it["e_tot"]
                                                + jnp.where(left_half, full[0:HEAD_DIM], full[HEAD_DIM:2 * HEAD_DIM]))
        for i in range(0, len(items), npairs):
            rows = items[i]["rows"]
            y_s[rows] = y_s[rows] + jnp.concatenate([it["y"] for it in items[i:i + npairs]], axis=1)
        return carry

    lax.fori_loop(0, nchunks, chunk, 0)
    if want_state:
        for bi in range(nbatch):
            if has_prev:
                stout_ref[bi, 0] = prev_ref[bi]
            for d in range(2):
                for p in range(npairs):
                    s2 = st_s[bi, d, p]
                    for q in range(2):
                        if has_prev:
                            stout_ref[bi, 1, d, 2 * p + q] = s2[:, q * HEAD_DIM:(q + 1) * HEAD_DIM]
                        else:
                            stout_ref[bi, d, 2 * p + q] = s2[:, q * HEAD_DIM:(q + 1) * HEAD_DIM]

    gup = gup_ref[...].astype(BF16)
    for rb in range(nbatch * t // RWKV_ROWS):
        rows = slice(rb * RWKV_ROWS, (rb + 1) * RWKV_ROWS)
        y = y_s[rows]
        mu = _head_sum(y, ones) * (1.0 / HEAD_DIM)
        yc = y - mu
        var = _head_sum(yc * yc, ones) * (1.0 / HEAD_DIM)
        yn = yc * lax.rsqrt(var + GN_EPS)
        g = _bdot(_sigmoid(zc_ref[rows, C_G]), gup)
        oc_ref[rows] = ((yn * ln_w + ln_b + bon_s[rows]) * g).astype(BF16)


def _rwkv_call(zc, rvec, wup, aup, gup, layer, t, init_state, want_state, prev_state, cast_weights=()):
    nb = zc.shape[0] // t
    nbatch = min(RWKV_BATCH, nb)
    rows = nbatch * t
    has_init = init_state is not None
    has_prev = prev_state is not None
    st_shape = (2, C_HEADS, HEAD_DIM, HEAD_DIM)
    in_specs = [
        pl.BlockSpec((rows, RWKV_COLS), lambda b: (b, 0), pipeline_mode=pl.Buffered(1 if nb == nbatch else 2)),
        _layer_spec((16, C_DIM), layer),
        _layer_spec((2, W_RANK, C_DIM), layer),
        _layer_spec((2, A_RANK, C_DIM), layer),
        _layer_spec((G_RANK, C_DIM), layer),
    ]
    args = [zc, rvec, wup, aup, gup]
    if has_init:
        in_specs.append(pl.BlockSpec((nbatch, None) + st_shape, lambda b: (b, layer, 0, 0, 0, 0)))
        args.append(init_state)
    if has_prev:
        in_specs.append(pl.BlockSpec((nbatch,) + st_shape, lambda b: (b, 0, 0, 0, 0)))
        args.append(prev_state)
    steps = nb // nbatch
    cast_in, cast_shape, cast_out = _cast_specs(cast_weights, layer, steps)
    in_specs.extend(cast_in)
    args.extend(cast_weights)
    out_shape = [jax.ShapeDtypeStruct((nb * t, C_DIM), BF16)]
    out_specs = [pl.BlockSpec((rows, C_DIM), lambda b: (b, 0))]
    if want_state and has_prev:
        out_shape.append(jax.ShapeDtypeStruct((nb, DEPTH) + st_shape, F32))
        out_specs.append(pl.BlockSpec((nbatch, DEPTH) + st_shape, lambda b: (b, 0, 0, 0, 0, 0)))
    elif want_state:
        out_shape.append(jax.ShapeDtypeStruct((nb,) + st_shape, F32))
        out_specs.append(pl.BlockSpec((nbatch,) + st_shape, lambda b: (b, 0, 0, 0, 0)))
    out_shape.extend(cast_shape)
    out_specs.extend(cast_out)
    return pl.pallas_call(
        functools.partial(_rwkv_body, t=t, nbatch=nbatch, has_init=has_init, want_state=want_state,
                          has_prev=has_prev, ncast=len(cast_weights)),
        out_shape=tuple(out_shape),
        grid=(steps,),
        in_specs=in_specs,
        out_specs=tuple(out_specs),
        scratch_shapes=[
            pltpu.VMEM((rows, C_DIM), F32),
            pltpu.VMEM((2, rows, C_DIM), F32),
            pltpu.VMEM((2, rows, C_DIM), F32),
            pltpu.VMEM((2, rows, C_DIM), F32),
            pltpu.VMEM((rows, C_DIM), F32),
            pltpu.VMEM((rows, C_DIM), F32),
            pltpu.VMEM((nbatch, 2, C_HEADS // 2, HEAD_DIM, LANES), F32),
        ],
        compiler_params=pltpu.CompilerParams(vmem_limit_bytes=VMEM_LIMIT),
        name="rwkv_prompt" if want_state else "rwkv_sample",
    )(*args)


def _weight_copies(wout_hbm, wgu_hbm, wdn_hbm, wout_s, wgu_s, wdn_s, sem):
    copies = {"out": pltpu.make_async_copy(wout_hbm, wout_s, sem.at[0])}
    n = 1
    for j, (lo, hi) in enumerate(FF_CHUNKS):
        for name, src, dst in (("gate", wgu_hbm.at[:, lo:hi], wgu_s.at[:, lo:hi]),
                               ("up", wgu_hbm.at[:, FF_DIM + lo:FF_DIM + hi], wgu_s.at[:, FF_DIM + lo:FF_DIM + hi]),
                               ("down", wdn_hbm.at[lo:hi, :], wdn_s.at[lo:hi, :])):
            copies[(name, j)] = pltpu.make_async_copy(src, dst, sem.at[n])
            n += 1
    return copies


N_WEIGHT_COPIES = 1 + 3 * len(FF_CHUNKS)


def _post_tile(oa_ref, oc_ref, x_ref, mod, g_ref, wout_ref, wgu_ref, wdn_ref, xo_ref, ready=None):
    def need(name):
        if ready is not None:
            ready(name)

    gate1 = mod[:, 2 * D_MODEL:3 * D_MODEL]
    shift2 = mod[:, 3 * D_MODEL:4 * D_MODEL]
    scale2 = mod[:, 4 * D_MODEL:5 * D_MODEL]
    gate2 = mod[:, 5 * D_MODEL:6 * D_MODEL]
    sub = ROW_TILE // ROW_SPLIT
    parts = [slice(i * sub, (i + 1) * sub) for i in range(ROW_SPLIT)]
    need("out")
    o = [jnp.dot(jnp.concatenate([oa_ref[r, :], oc_ref[r, :]], axis=1), wout_ref[...],
                 preferred_element_type=F32) for r in parts]
    x1 = [x_ref[r, :] + gate1 * _rms(oi, g_ref[1:2]) for r, oi in zip(parts, o)]
    h2 = [(_rms(xi, g_ref[2:3]) * (1.0 + scale2) + shift2).astype(BF16) for xi in x1]

    def gate_up(step):
        (lo, hi), i = step
        if i == 0:
            need(("gate", FF_CHUNKS.index((lo, hi))))
            need(("up", FF_CHUNKS.index((lo, hi))))
        g = jnp.dot(h2[i], wgu_ref[:, lo:hi], preferred_element_type=F32)
        u = jnp.dot(h2[i], wgu_ref[:, FF_DIM + lo:FF_DIM + hi], preferred_element_type=F32)
        return g, u

    steps = [(chunk, i) for chunk in FF_CHUNKS for i in range(ROW_SPLIT)]
    acc = [None] * ROW_SPLIT
    pending = gate_up(steps[0])
    for k, ((lo, hi), i) in enumerate(steps):
        g, u = pending
        if k + 1 < len(steps):
            pending = gate_up(steps[k + 1])
        act = (g * _sigmoid(g) * u).astype(BF16)
        if i == 0:
            need(("down", FF_CHUNKS.index((lo, hi))))
        part = jnp.dot(act, wdn_ref[lo:hi, :], preferred_element_type=F32)
        acc[i] = part if acc[i] is None else acc[i] + part
    for r, xi, ai in zip(parts, x1, acc):
        xo_ref[r, :] = xi + gate2 * _rms(ai, g_ref[3:4])


def _post_body(oap_ref, oas_ref, ocp_ref, ocs_ref, xp_ref, xs_ref, mod_ref, g_ref, wout_hbm, wgu_hbm, wdn_hbm,
               xop_ref, xos_ref, wout_ref, wgu_ref, wdn_ref, sem, *, n_first):
    step = pl.program_id(0)
    copies = _weight_copies(wout_hbm, wgu_hbm, wdn_hbm, wout_ref, wgu_ref, wdn_ref, sem)

    @pl.when(step == 0)
    def _():
        for cp in copies.values():
            cp.start()
        _post_tile(oap_ref, ocp_ref, xp_ref, mod_ref[0:1], g_ref, wout_ref, wgu_ref, wdn_ref, xop_ref,
                   ready=lambda name: copies[name].wait())

    @pl.when((step > 0) & (step < n_first))
    def _():
        _post_tile(oap_ref, ocp_ref, xp_ref, mod_ref[0:1], g_ref, wout_ref, wgu_ref, wdn_ref, xop_ref)

    @pl.when(step >= n_first)
    def _():
        _post_tile(oas_ref, ocs_ref, xs_ref, _latent_mod_row(mod_ref, step, n_first), g_ref, wout_ref, wgu_ref,
                   wdn_ref, xos_ref)


def _post_call(oatt_p, oatt_s, oc_p, oc_s, xp, xs, mods, gvec, wout_b, wgu_b, wdn_b, layer):
    n_first = xp.shape[0] // ROW_TILE
    n_second = xs.shape[0] // ROW_TILE
    x_first, x_second = _two_group_specs(D_MODEL, n_first)
    return pl.pallas_call(
        functools.partial(_post_body, n_first=n_first),
        out_shape=(jax.ShapeDtypeStruct(xp.shape, F32), jax.ShapeDtypeStruct(xs.shape, F32)),
        grid=(n_first + n_second,),
        in_specs=[
            *_two_group_specs(ATT_OUT, n_first),
            *_two_group_specs(C_DIM, n_first),
            x_first, x_second,
            _layer_spec((MOD_ROWS, MOD_COLS), layer),
            _layer_spec((8, D_MODEL), layer),
            pl.BlockSpec(memory_space=pl.ANY),
            pl.BlockSpec(memory_space=pl.ANY),
            pl.BlockSpec(memory_space=pl.ANY),
        ],
        out_specs=(x_first, x_second),
        scratch_shapes=[
            pltpu.VMEM((MIX_DIM, D_MODEL), BF16),
            pltpu.VMEM((D_MODEL, 2 * FF_DIM), BF16),
            pltpu.VMEM((FF_DIM, D_MODEL), BF16),
            pltpu.SemaphoreType.DMA((N_WEIGHT_COPIES,)),
        ],
        compiler_params=pltpu.CompilerParams(dimension_semantics=("arbitrary",), vmem_limit_bytes=VMEM_LIMIT),
        name="post_ffn",
    )(oatt_p, oatt_s, oc_p, oc_s, xp, xs, mods, gvec, wout_b, wgu_b, wdn_b)


def _rope_tables(t):
    pos = jnp.arange(t)
    row = (pos // GRID_W).astype(F32)
    col = (pos % GRID_W).astype(F32)
    freqs = ROPE_THETA ** (-jnp.arange(ROPE_PAIRS_AXIS, dtype=F32) / ROPE_PAIRS_AXIS)
    ang = jnp.concatenate([row[:, None] * freqs, col[:, None] * freqs], axis=-1)
    cos = jnp.tile(jnp.cos(ang), (1, 4))
    sin = jnp.sin(ang)
    sin = jnp.tile(jnp.concatenate([-sin, sin], axis=-1), (1, 2))
    return cos, sin


def _to_time_minor(cache):
    b, l, t = cache.shape[:3]
    return jnp.transpose(cache, (0, 1, 3, 4, 2)).reshape(b, l, KV_DIM, t)


def _from_time_minor(slab):
    b, l, _, t = slab.shape
    return jnp.transpose(slab.reshape(b, l, 2, HEAD_DIM, t), (0, 1, 4, 2, 3))


def kernel(x_prompt, x_sample, cache_a_k, cache_a_v, cache_b_k, cache_b_v, state_c, c, c_ctx, w_mod, b_mod, norm_mix_pre, norm_mix_post, norm_ffn_pre, norm_ffn_post, w_in, w_out, a_sink, b_q_norm, b_k_norm, c_w0, c_w_up, c_a0, c_a_up, c_g_up, c_k_k, c_k_a, c_r_k, c_ln_w, c_ln_b, w_gu, w_down):
    cvec = jnp.zeros((MOD_ROWS, D_MODEL), F32).at[0].set(c_ctx).at[1:1 + DEC_BATCH].set(c)
    mods = _mod_call(cvec, w_mod, b_mod)
    cos, sin = _rope_tables(DEC_SEQ)

    gvec = jnp.zeros((DEPTH, 8, D_MODEL), F32)
    gvec = gvec.at[:, 0].set(norm_mix_pre).at[:, 1].set(norm_mix_post)
    gvec = gvec.at[:, 2].set(norm_ffn_pre).at[:, 3].set(norm_ffn_post)
    rvec = jnp.zeros((DEPTH, 16, C_DIM), F32)
    rvec = rvec.at[:, 0].set(c_k_k).at[:, 1].set(c_k_a).at[:, 2].set(c_r_k.reshape(DEPTH, C_DIM))
    rvec = rvec.at[:, 3].set(c_ln_w).at[:, 4].set(c_ln_b)
    rvec = rvec.at[:, 5:7].set(c_w0).at[:, 7:9].set(c_a0)
    gq = jnp.tile(b_q_norm, (1, B_HEADS))[:, None, :]
    gk = jnp.tile(b_k_norm, (1, B_KV_HEADS))[:, None, :]
    caches = tuple(_to_time_minor(t) for t in (cache_a_k, cache_a_v, cache_b_k, cache_b_v))

    xp = x_prompt.reshape(BATCH * SEQ, D_MODEL)
    xs = x_sample.reshape(DEC_BATCH * DEC_SEQ, D_MODEL)
    kv_prev = st_prev = None
    for l in range(DEPTH):
        za_p, zc_p, za_s, zc_s, wout_b, wdn_b = _proj_in_call(xp, xs, mods, gvec, w_in, l, (w_out, w_down))
        oatt_p, *kv_prev = _attn_prompt_call(za_p, a_sink, gq, gk, l, kv_prev)
        if l == 0:
            (kv_prev,) = kv_prev
        oc_p, st_prev, wgu_b = _rwkv_call(zc_p, rvec, c_w_up, c_a_up, c_g_up, l, SEQ, None, True, st_prev,
                                          cast_weights=(w_gu,))
        oatt_s = _attn_sample_call(za_s, a_sink, caches, cos, sin, gq, gk, l)
        (oc_s,) = _rwkv_call(zc_s, rvec, c_w_up, c_a_up, c_g_up, l, DEC_SEQ, state_c, False, None)
        xp, xs = _post_call(oatt_p, oatt_s, oc_p, oc_s, xp, xs, mods, gvec, wout_b, wgu_b, wdn_b, l)

    new_caches = tuple(_from_time_minor(slab) for slab in kv_prev)
    return (xp.reshape(BATCH, SEQ, D_MODEL), xs.reshape(DEC_BATCH, DEC_SEQ, D_MODEL), *new_caches, st_prev)
```

```python
import functools
import math

import jax
import jax.numpy as jnp
from jax import lax
from jax.experimental import pallas as pl
from jax.experimental.pallas import tpu as pltpu

D_MODEL = 1024
BATCH = 32
SEQ = 256
DEPTH = 2
DEC_BATCH = 2
DEC_SEQ = 1024
PAST_LEN = 256
GRID_W = 64
HEAD_DIM = 64
A_HEADS = 4
A_KV_HEADS = 2
B_HEADS = 6
B_KV_HEADS = 2
C_HEADS = 6
A_DIM = A_HEADS * HEAD_DIM
B_DIM = B_HEADS * HEAD_DIM
C_DIM = C_HEADS * HEAD_DIM
MIX_DIM = A_DIM + B_DIM + C_DIM
WINDOW = 128
Q_BLK = 128
W_RANK = 64
A_RANK = 64
G_RANK = 128
FF_DIM = -(-8 * D_MODEL // (3 * 256)) * 256
ROPE_THETA = 10000.0
ROPE_PAIRS_AXIS = HEAD_DIM // 4
NORM_EPS = 1e-6
GN_EPS = 64e-5
NEG_INF = -1e30

LANES = 128
KV_DIM = 2 * HEAD_DIM
assert KV_DIM == LANES


def _col_ranges(widths):
    edges = [0]
    for w in widths:
        edges.append(edges[-1] + w)
    return tuple(slice(lo, hi) for lo, hi in zip(edges[:-1], edges[1:])), edges[-1]


(A_Q, A_K, A_V, B_Q, B_K, B_V), ATT_COLS = _col_ranges((A_DIM, KV_DIM, KV_DIM, B_DIM, KV_DIM, KV_DIM))
(C_R, C_K, C_V, C_W, C_A, C_G), RWKV_COLS = _col_ranges((C_DIM, C_DIM, C_DIM, W_RANK, A_RANK, G_RANK))
IN_COLS = ATT_COLS + RWKV_COLS
ATT_OUT = A_DIM + B_DIM
MOD_COLS = 6 * D_MODEL
MOD_ROWS = 8
N_CACHE = 4
CHUNK = 64
assert CHUNK == HEAD_DIM
QK_SCALE = HEAD_DIM ** -0.5
DECAY_RATE = math.exp(-0.5)
ROW_TILE = 512
ROW_SPLIT = 2
RWKV_ROWS = 512
CUMSUM_ROWS = 64
RWKV_BATCH = 4
B_QROWS = 256
ATTN_BATCH = 4
MXU_TILE = 256
FF_CHUNKS = ((0, 4 * MXU_TILE), (4 * MXU_TILE, 8 * MXU_TILE), (8 * MXU_TILE, FF_DIM))
assert FF_DIM % MXU_TILE == 0
VMEM_LIMIT = 56 * 1024 * 1024

F32 = jnp.float32
BF16 = jnp.bfloat16

assert DEPTH == 2


def _bdot(a, b):
    return jnp.dot(a.astype(BF16), b.astype(BF16), preferred_element_type=F32)


def _bdot_nt(a, b):
    return lax.dot_general(a.astype(BF16), b.astype(BF16), (((1,), (1,)), ((), ())), preferred_element_type=F32)


def _bdot_tn(a, b):
    return lax.dot_general(a.astype(BF16), b.astype(BF16), (((0,), (0,)), ((), ())), preferred_element_type=F32)


def _split(x):
    hi = x.astype(BF16)
    lo = (x - hi.astype(F32)).astype(BF16)
    return hi, lo


def _split_dot_left(m, x):
    hi, lo = _split(x)
    return jnp.dot(m, hi, preferred_element_type=F32) + jnp.dot(m, lo, preferred_element_type=F32)


def _head_sum(x, ones):
    return jnp.dot(x.astype(BF16), ones, preferred_element_type=F32)


def _sigmoid(x):
    return 0.5 * jnp.tanh(0.5 * x) + 0.5


def _head_ones(n):
    r = lax.shift_right_logical(lax.broadcasted_iota(jnp.int32, (n, n), 0), jnp.int32(HEAD_DIM.bit_length() - 1))
    c = lax.shift_right_logical(lax.broadcasted_iota(jnp.int32, (n, n), 1), jnp.int32(HEAD_DIM.bit_length() - 1))
    return jnp.where(r == c, 1.0, 0.0).astype(BF16)


def _rms(x, g):
    ms = jnp.mean(x * x, axis=-1, keepdims=True)
    return x * lax.rsqrt(ms + NORM_EPS) * g


def _head_rms(x, g, ones):
    ms = _head_sum(x * x, ones) * (1.0 / HEAD_DIM)
    return x * lax.rsqrt(ms + NORM_EPS) * g


def _rope(x, cos, sin):
    t = x.shape[0]
    lane = lax.broadcasted_iota(jnp.int32, (t, LANES), 1)
    first = (lane & (HEAD_DIM // 2)) == 0
    outs = []
    for i in range(x.shape[1] // LANES):
        xs = x[:, i * LANES:(i + 1) * LANES]
        swapped = jnp.where(first, pltpu.roll(xs, LANES - HEAD_DIM // 2, axis=1), pltpu.roll(xs, HEAD_DIM // 2, axis=1))
        outs.append(xs * cos + swapped * sin)
    return outs[0] if len(outs) == 1 else jnp.concatenate(outs, axis=1)


def _softmax_pv(scores, values, sink=None):
    m = scores[0].max(axis=-1, keepdims=True)
    for s in scores[1:]:
        m = jnp.maximum(m, s.max(axis=-1, keepdims=True))
    if sink is not None:
        m = jnp.maximum(m, sink)
    denom = None
    acc = None
    for s, v in zip(scores, values):
        p = jnp.exp(s - m)
        d = p.sum(axis=-1, keepdims=True)
        o = jnp.dot(p.astype(BF16), v, preferred_element_type=F32)
        denom = d if denom is None else denom + d
        acc = o if acc is None else acc + o
    if sink is not None:
        denom = denom + jnp.exp(sink - m)
    return acc / denom


def _softmax_pv_t(scores_t, values_t, sink=None):
    m = scores_t[0].max(axis=0, keepdims=True)
    for s in scores_t[1:]:
        m = jnp.maximum(m, s.max(axis=0, keepdims=True))
    if sink is not None:
        m = jnp.maximum(m, sink)
    denom = None
    acc = None
    for s, vt in zip(scores_t, values_t):
        p = jnp.exp(s - m)
        d = p.sum(axis=0, keepdims=True)
        o = jnp.dot(vt, p.astype(BF16), preferred_element_type=F32)
        denom = d if denom is None else denom + d
        acc = o if acc is None else acc + o
    if sink is not None:
        denom = denom + jnp.exp(sink - m)
    return acc / denom


def _attend_heads(score_fns, finish_fns, ahead=1):
    n = len(score_fns)
    pending = [fn() for fn in score_fns[:ahead]]
    outs = []
    for i in range(n):
        if i + ahead < n:
            pending.append(score_fns[i + ahead]())
        outs.append(finish_fns[i](pending.pop(0)))
    return outs


def _index32(i):
    return jnp.asarray(i, jnp.int32)


def _layer_spec(shape, layer):
    zeros = (0,) * len(shape)
    return pl.BlockSpec((None,) + shape, lambda *_: (layer,) + zeros, pipeline_mode=pl.Buffered(1))


def _whole_spec(shape):
    zeros = (0,) * len(shape)
    return pl.BlockSpec(shape, lambda *_: zeros, pipeline_mode=pl.Buffered(1))


def _mod_body(c_ref, w_ref, b_ref, o_ref):
    cv = c_ref[...]
    s = cv * _sigmoid(cv)
    o_ref[0] = _bdot(s, w_ref[0]) + b_ref[0]


def _mod_call(cvec, w_mod, b_mod):
    bn = MOD_COLS // 4
    return pl.pallas_call(
        _mod_body,
        out_shape=jax.ShapeDtypeStruct((DEPTH, MOD_ROWS, MOD_COLS), F32),
        grid=(DEPTH, MOD_COLS // bn),
        in_specs=[
            pl.BlockSpec((MOD_ROWS, D_MODEL), lambda l, j: (0, 0)),
            pl.BlockSpec((1, D_MODEL, bn), lambda l, j: (l, 0, j)),
            pl.BlockSpec((1, 1, bn), lambda l, j: (l, 0, j)),
        ],
        out_specs=pl.BlockSpec((1, MOD_ROWS, bn), lambda l, j: (l, 0, j)),
        compiler_params=pltpu.CompilerParams(vmem_limit_bytes=VMEM_LIMIT),
        name="mod_vectors",
    )(cvec, w_mod, b_mod.reshape(DEPTH, 1, MOD_COLS))


def _cast_blocks(srcs, dsts):
    for src, dst in zip(srcs, dsts):
        dst[...] = src[...].astype(BF16)


def _cast_specs(weights, layer, steps):
    def block(i):
        return jnp.minimum(i, steps - 1)

    in_specs = [pl.BlockSpec((None, w.shape[1] // steps, w.shape[2]), lambda i: (layer, block(i), 0)) for w in weights]
    out_shape = [jax.ShapeDtypeStruct(w.shape[1:], BF16) for w in weights]
    out_specs = [pl.BlockSpec((w.shape[1] // steps, w.shape[2]), lambda i: (block(i), 0)) for w in weights]
    return in_specs, out_shape, out_specs


def _two_group_specs(cols, n_first):
    first = pl.BlockSpec((ROW_TILE, cols), lambda i: (jnp.minimum(i, n_first - 1), 0))
    second = pl.BlockSpec((ROW_TILE, cols), lambda i: (jnp.maximum(i - n_first, 0), 0))
    return first, second


def _latent_mod_row(mod_ref, step, n_first):
    row = 1 + (step - n_first) // (DEC_SEQ // ROW_TILE)
    return mod_ref[pl.ds(row, 1), :]


def _proj_in_body(xp_ref, xs_ref, mod_ref, g_ref, w_ref, *rest, n_first, ncast):
    cast_src = rest[:ncast]
    zap_ref, zcp_ref, zas_ref, zcs_ref = rest[ncast:ncast + 4]
    cast_dst = rest[ncast + 4:2 * ncast + 4]
    wb_s = rest[2 * ncast + 4]
    step = pl.program_id(0)

    @pl.when(step == 0)
    def _():
        wb_s[...] = w_ref[...].astype(BF16)

    def project(x_ref, mod, za_ref, zc_ref):
        shift1 = mod[:, 0:D_MODEL]
        scale1 = mod[:, D_MODEL:2 * D_MODEL]
        h = (_rms(x_ref[...], g_ref[0:1]) * (1.0 + scale1) + shift1).astype(BF16)
        z = jnp.dot(h, wb_s[...], preferred_element_type=F32)
        za_ref[...] = z[:, 0:ATT_COLS]
        zc_ref[...] = z[:, ATT_COLS:IN_COLS]

    @pl.when(step < n_first)
    def _():
        _cast_blocks(cast_src, cast_dst)
        project(xp_ref, mod_ref[0:1], zap_ref, zcp_ref)

    @pl.when(step >= n_first)
    def _():
        project(xs_ref, _latent_mod_row(mod_ref, step, n_first), zas_ref, zcs_ref)


def _proj_in_call(xp, xs, mods, gvec, w_in, layer, cast_weights):
    n_first = xp.shape[0] // ROW_TILE
    n_second = xs.shape[0] // ROW_TILE
    cast_in, cast_shape, cast_out = _cast_specs(cast_weights, layer, n_first)
    x_first, x_second = _two_group_specs(D_MODEL, n_first)
    za_first, za_second = _two_group_specs(ATT_COLS, n_first)
    zc_first, zc_second = _two_group_specs(RWKV_COLS, n_first)
    return pl.pallas_call(
        functools.partial(_proj_in_body, n_first=n_first, ncast=len(cast_weights)),
        out_shape=(jax.ShapeDtypeStruct((xp.shape[0], ATT_COLS), F32), jax.ShapeDtypeStruct((xp.shape[0], RWKV_COLS), F32),
                   jax.ShapeDtypeStruct((xs.shape[0], ATT_COLS), F32), jax.ShapeDtypeStruct((xs.shape[0], RWKV_COLS), F32),
                   *cast_shape),
        grid=(n_first + n_second,),
        in_specs=[
            x_first, x_second,
            _layer_spec((MOD_ROWS, MOD_COLS), layer),
            _layer_spec((8, D_MODEL), layer),
            _layer_spec((D_MODEL, IN_COLS), layer),
            *cast_in,
        ],
        out_specs=(za_first, zc_first, za_second, zc_second, *cast_out),
        scratch_shapes=[pltpu.VMEM((D_MODEL, IN_COLS), BF16)],
        compiler_params=pltpu.CompilerParams(dimension_semantics=("arbitrary",), vmem_limit_bytes=VMEM_LIMIT),
        name="proj_in",
    )(xp, xs, mods, gvec, w_in, *cast_weights)


def _attn_prompt_body(*refs, layer, nseq):
    sink_ref, za_ref, gq_ref, gk_ref = refs[:4]
    if layer == 0:
        o_ref, kv_ref = refs[4:]
    else:
        prev_ref, o_ref = refs[4:6]
        cache_refs = refs[6:]
    ones_q = _head_ones(B_DIM)
    ones_k = _head_ones(KV_DIM)
    score_fns, finish_fns = [], []
    for si in range(nseq):
        rows = slice(si * SEQ, (si + 1) * SEQ)
        aq = (za_ref[rows, A_Q] * QK_SCALE).astype(BF16)
        ak = za_ref[rows, A_K]
        bq = _head_rms(za_ref[rows, B_Q], gq_ref[...], ones_q)
        bk = _head_rms(za_ref[rows, B_K], gk_ref[...], ones_k)
        bqb = (bq * QK_SCALE).astype(BF16)
        akb = ak.astype(BF16)
        bkb = bk.astype(BF16)
        pieces_t = (ak.T, za_ref[rows, A_V].T, bk.T, za_ref[rows, B_V].T)
        avt = pieces_t[1].astype(BF16)
        bvt = pieces_t[3].astype(BF16)
        for j, piece_t in enumerate(pieces_t):
            if layer == 0:
                kv_ref[si, j] = piece_t
            else:
                cache_refs[j][si, 0] = prev_ref[si, j]
                cache_refs[j][si, 1] = piece_t

        for h in range(A_HEADS):
            kv = h // (A_HEADS // A_KV_HEADS)
            hs = slice(kv * HEAD_DIM, (kv + 1) * HEAD_DIM)
            score_fns.append(lambda h=h, hs=hs, akb=akb, aq=aq: [_bdot_nt(akb[:, hs], aq[:, h * HEAD_DIM:(h + 1) * HEAD_DIM])])
            finish_fns.append(lambda sc, h=h, hs=hs, avt=avt: _softmax_pv_t(sc, [avt[hs]], sink_ref[layer, h]))
        for h in range(B_HEADS):
            kv = h // (B_HEADS // B_KV_HEADS)
            hs = slice(kv * HEAD_DIM, (kv + 1) * HEAD_DIM)
            score_fns.append(lambda h=h, hs=hs, bkb=bkb, bqb=bqb: [_bdot_nt(bkb[:, hs], bqb[:, h * HEAD_DIM:(h + 1) * HEAD_DIM])])
            finish_fns.append(lambda sc, hs=hs, bvt=bvt: _softmax_pv_t(sc, [bvt[hs]]))
    outs = _attend_heads(score_fns, finish_fns, ahead=len(score_fns))
    heads = A_HEADS + B_HEADS
    for si in range(nseq):
        o_ref[si * SEQ:(si + 1) * SEQ, :] = jnp.concatenate(outs[si * heads:(si + 1) * heads], axis=0).T.astype(BF16)


def _attn_prompt_call(za, sink, gq, gk, layer, prev_kv):
    nb = za.shape[0] // SEQ
    nseq = ATTN_BATCH
    in_specs = [
        pl.BlockSpec(memory_space=pltpu.SMEM),
        pl.BlockSpec((nseq * SEQ, ATT_COLS), lambda b: (b, 0)),
        _layer_spec((1, B_DIM), layer),
        _layer_spec((1, KV_DIM), layer),
    ]
    args = [sink, za, gq, gk]
    out_shape = [jax.ShapeDtypeStruct((nb * SEQ, ATT_OUT), BF16)]
    out_specs = [pl.BlockSpec((nseq * SEQ, ATT_OUT), lambda b: (b, 0))]
    slab_spec = pl.BlockSpec((nseq, N_CACHE, KV_DIM, SEQ), lambda b: (b, 0, 0, 0))
    if layer == 0:
        out_shape.append(jax.ShapeDtypeStruct((nb, N_CACHE, KV_DIM, SEQ), F32))
        out_specs.append(slab_spec)
    else:
        in_specs.append(slab_spec)
        args.append(prev_kv)
        for _ in range(N_CACHE):
            out_shape.append(jax.ShapeDtypeStruct((nb, DEPTH, KV_DIM, SEQ), F32))
            out_specs.append(pl.BlockSpec((nseq, DEPTH, KV_DIM, SEQ), lambda b: (b, 0, 0, 0)))
    return pl.pallas_call(
        functools.partial(_attn_prompt_body, layer=layer, nseq=nseq),
        out_shape=tuple(out_shape),
        grid=(nb // nseq,),
        in_specs=in_specs,
        out_specs=tuple(out_specs),
        compiler_params=pltpu.CompilerParams(vmem_limit_bytes=VMEM_LIMIT),
        name="attn_prompt",
    )(*args)


def _attn_sample_body(sink_ref, za_ref, cak_ref, cav_ref, cbk_ref, cbv_ref, cos_ref, sin_ref, gq_ref, gk_ref,
                      o_ref, qa_s, ka_s, va_s, qb_s, kb_s, vb_s, *, layer):
    t = DEC_SEQ
    cos = cos_ref[...]
    sin = sin_ref[...]
    qa_s[...] = (_rope(za_ref[:, A_Q], cos, sin) * QK_SCALE).astype(BF16)
    zpad = jnp.zeros((Q_BLK, KV_DIM), BF16)
    ka_s[0:Q_BLK] = zpad
    ka_s[Q_BLK + t:] = zpad
    va_s[0:Q_BLK] = zpad
    va_s[Q_BLK + t:] = zpad
    ka_s[Q_BLK:Q_BLK + t] = _rope(za_ref[:, A_K], cos, sin).astype(BF16)
    va_s[Q_BLK:Q_BLK + t] = za_ref[:, A_V].astype(BF16)
    bq = _head_rms(za_ref[:, B_Q], gq_ref[...], _head_ones(B_DIM))
    qb_s[...] = (_rope(bq, cos, sin) * QK_SCALE).astype(BF16)
    bk = _head_rms(za_ref[:, B_K], gk_ref[...], _head_ones(KV_DIM))
    kb_s[0:PAST_LEN] = cbk_ref[0].T.astype(BF16)
    kb_s[PAST_LEN:] = _rope(bk, cos, sin).astype(BF16)
    vb_s[0:PAST_LEN] = cbv_ref[0].T.astype(BF16)
    vb_s[PAST_LEN:] = za_ref[:, B_V].astype(BF16)

    kca = cak_ref[0].T.astype(BF16)
    vca = cav_ref[0].T.astype(BF16)

    ga = A_HEADS // A_KV_HEADS
    gb = B_HEADS // B_KV_HEADS
    qpos = lax.broadcasted_iota(jnp.int32, (ga * Q_BLK, 3 * Q_BLK), 0) & (Q_BLK - 1)
    kpos = lax.broadcasted_iota(jnp.int32, (ga * Q_BLK, 3 * Q_BLK), 1) - Q_BLK
    near = jnp.abs(kpos - qpos) <= WINDOW
    head_of_row = lax.shift_right_logical(lax.broadcasted_iota(jnp.int32, (ga * Q_BLK, 1), 0),
                                          jnp.int32(Q_BLK.bit_length() - 1))
    sinks = []
    for kv in range(A_KV_HEADS):
        col = jnp.full((ga * Q_BLK, 1), sink_ref[layer, kv * ga], F32)
        for g in range(1, ga):
            col = jnp.where(head_of_row == g, sink_ref[layer, kv * ga + g], col)
        sinks.append(col)

    def stack_heads(q, kv, group):
        return jnp.concatenate([q[:, (kv * group + g) * HEAD_DIM:(kv * group + g + 1) * HEAD_DIM] for g in range(group)], axis=0)

    def unstack_heads(outs, group, rows):
        return jnp.concatenate([o[g * rows:(g + 1) * rows] for o in outs for g in range(group)], axis=1)

    def a_block(n, carry):
        r0 = pl.multiple_of(_index32(n) * Q_BLK, Q_BLK)
        q = qa_s[pl.ds(r0, Q_BLK), :]
        kl = ka_s[pl.ds(r0, 3 * Q_BLK), :]
        vl = va_s[pl.ds(r0, 3 * Q_BLK), :]
        kabs = kpos + r0
        mask = near & (kabs >= 0) & (kabs < t)
        score_fns, finish_fns = [], []
        for kv in range(A_KV_HEADS):
            hs = slice(kv * HEAD_DIM, (kv + 1) * HEAD_DIM)

            def scores(kv=kv, hs=hs):
                qs = stack_heads(q, kv, ga)
                return [jnp.where(mask, _bdot_nt(qs, kl[:, hs]), NEG_INF), _bdot_nt(qs, kca[:, hs])]

            score_fns.append(scores)
            finish_fns.append(lambda sc, kv=kv, hs=hs: _softmax_pv(sc, [vl[:, hs], vca[:, hs]], sinks[kv]))
        outs = _attend_heads(score_fns, finish_fns, ahead=A_KV_HEADS)
        o_ref[pl.ds(r0, Q_BLK), 0:A_DIM] = unstack_heads(outs, ga, Q_BLK).astype(BF16)
        return carry

    lax.fori_loop(0, t // Q_BLK, a_block, 0)

    def b_block(n, carry):
        r0 = pl.multiple_of(_index32(n) * B_QROWS, B_QROWS)
        q = qb_s[pl.ds(r0, B_QROWS), :]
        score_fns, finish_fns = [], []
        for kv in range(B_KV_HEADS):
            hs = slice(kv * HEAD_DIM, (kv + 1) * HEAD_DIM)
            score_fns.append(lambda kv=kv, hs=hs: [_bdot_nt(stack_heads(q, kv, gb), kb_s[:, hs])])
            finish_fns.append(lambda sc, hs=hs: _softmax_pv(sc, [vb_s[:, hs]]))
        outs = _attend_heads(score_fns, finish_fns)
        o_ref[pl.ds(r0, B_QROWS), A_DIM:ATT_OUT] = unstack_heads(outs, gb, B_QROWS).astype(BF16)
        return carry

    lax.fori_loop(0, t // B_QROWS, b_block, 0)


def _attn_sample_call(za, sink, caches, cos, sin, gq, gk, layer):
    nb = za.shape[0] // DEC_SEQ
    t = DEC_SEQ
    cache_spec = pl.BlockSpec((1, None, KV_DIM, PAST_LEN), lambda b: (b, layer, 0, 0))
    return pl.pallas_call(
        functools.partial(_attn_sample_body, layer=layer),
        out_shape=jax.ShapeDtypeStruct((nb * t, ATT_OUT), BF16),
        grid=(nb,),
        in_specs=[
            pl.BlockSpec(memory_space=pltpu.SMEM),
            pl.BlockSpec((t, ATT_COLS), lambda b: (b, 0)),
            cache_spec, cache_spec, cache_spec, cache_spec,
            pl.BlockSpec((t, LANES), lambda b: (0, 0)),
            pl.BlockSpec((t, LANES), lambda b: (0, 0)),
            _layer_spec((1, B_DIM), layer),
            _layer_spec((1, KV_DIM), layer),
        ],
        out_specs=pl.BlockSpec((t, ATT_OUT), lambda b: (b, 0)),
        scratch_shapes=[
            pltpu.VMEM((t, A_DIM), BF16),
            pltpu.VMEM((t + 2 * Q_BLK, KV_DIM), BF16),
            pltpu.VMEM((t + 2 * Q_BLK, KV_DIM), BF16),
            pltpu.VMEM((t, B_DIM), BF16),
            pltpu.VMEM((t + PAST_LEN, KV_DIM), BF16),
            pltpu.VMEM((t + PAST_LEN, KV_DIM), BF16),
        ],
        compiler_params=pltpu.CompilerParams(vmem_limit_bytes=VMEM_LIMIT),
        name="attn_sample",
    )(sink, za, *caches, cos, sin, gq, gk)


def _rwkv_body(*refs, t, nbatch, has_init, want_state, has_prev, ncast):
    refs = list(refs)
    zc_ref, cv_ref, wup_ref, aup_ref, gup_ref = refs[:5]
    pos = 5
    st0_ref = prev_ref = stout_ref = None
    if has_init:
        st0_ref = refs[pos]
        pos += 1
    if has_prev:
        prev_ref = refs[pos]
        pos += 1
    cast_src = refs[pos:pos + ncast]
    pos += ncast
    oc_ref = refs[pos]
    pos += 1
    if want_state:
        stout_ref = refs[pos]
        pos += 1
    cast_dst = refs[pos:pos + ncast]
    pos += ncast
    aa_s, cinc_s, cexc_s, b_s, kt_s, bon_s, y_s, st_s = refs[pos:]

    _cast_blocks(cast_src, cast_dst)

    nchunks = t // CHUNK
    ones = _head_ones(C_DIM)
    k_k = cv_ref[0:1]
    k_a = cv_ref[1:2]
    r_k = cv_ref[2:3]
    ln_w = cv_ref[3:4]
    ln_b = cv_ref[4:5]
    w0h = (0.5 * cv_ref[5:6], 0.5 * cv_ref[6:7])
    a0h = (0.5 * cv_ref[7:8], 0.5 * cv_ref[8:9])
    wup_h = ((0.5 * wup_ref[0]).astype(BF16), (0.5 * wup_ref[1]).astype(BF16))
    aup_h = ((0.5 * aup_ref[0]).astype(BF16), (0.5 * aup_ref[1]).astype(BF16))
    c1 = 1.0 - 0.5 * k_a
    c2 = 0.5 * k_a
    half_rate = 0.5 * DECAY_RATE
    qr = lax.broadcasted_iota(jnp.int32, (CUMSUM_ROWS, CUMSUM_ROWS), 0)
    qc = lax.broadcasted_iota(jnp.int32, (CUMSUM_ROWS, CUMSUM_ROWS), 1)
    same_chunk = (lax.shift_right_logical(qr, jnp.int32(CHUNK.bit_length() - 1))
                  == lax.shift_right_logical(qc, jnp.int32(CHUNK.bit_length() - 1)))
    chunk_tri = tuple(jnp.where(same_chunk & m, 1.0, 0.0).astype(BF16) for m in (qc <= qr, qc >= qr))

    for rb in range(nbatch * t // RWKV_ROWS):
        rows = slice(rb * RWKV_ROWS, (rb + 1) * RWKV_ROWS)
        zr = zc_ref[rows, C_R]
        zk = zc_ref[rows, C_K]
        zv = zc_ref[rows, C_V]
        tw = jnp.tanh(zc_ref[rows, C_W])
        xa = zc_ref[rows, C_A]
        kkr = zk * k_k
        kk_half = kkr * (0.5 * lax.rsqrt(_head_sum(kkr * kkr, ones) + 1e-12))
        aa_s[rows] = -2.0 * kk_half
        kt_sum = None
        for d in range(2):
            th_w = jnp.tanh(w0h[d] + _bdot(tw, wup_h[d]))
            ld = th_w * (-half_rate) - half_rate
            for q in range(RWKV_ROWS // CUMSUM_ROWS):
                sub = slice(q * CUMSUM_ROWS, (q + 1) * CUMSUM_ROWS)
                dst = slice(rb * RWKV_ROWS + q * CUMSUM_ROWS, rb * RWKV_ROWS + (q + 1) * CUMSUM_ROWS)
                cinc = _split_dot_left(chunk_tri[d], ld[sub])
                cinc_s[d, dst] = cinc
                cexc_s[d, dst] = cinc - ld[sub]
            th_a = jnp.tanh(a0h[d] + _bdot(xa, aup_h[d]))
            kt = zk * (c1 + c2 * th_a)
            kt_s[d, rows] = kt
            b_s[d, rows] = kk_half * th_a + kk_half
            kt_sum = kt if kt_sum is None else kt_sum + kt
        bon_s[rows] = _head_sum(zr * kt_sum * r_k, ones) * zv
        y_s[rows] = jnp.zeros((RWKV_ROWS, C_DIM), F32)

    ir = lax.broadcasted_iota(jnp.int32, (CHUNK, LANES), 0)
    ic = lax.broadcasted_iota(jnp.int32, (CHUNK, LANES), 1) & (HEAD_DIM - 1)
    eye = jnp.where(ir == ic, 1.0, 0.0).astype(F32)
    incl = (ic <= ir, ic >= ir)
    strict = (ic < ir, ic > ir)
    br = lax.broadcasted_iota(jnp.int32, (LANES, LANES), 0)
    bc = lax.broadcasted_iota(jnp.int32, (LANES, LANES), 1)
    diag_blocks = (br < HEAD_DIM) == (bc < HEAD_DIM)
    left_half = lax.broadcasted_iota(jnp.int32, (CHUNK, LANES), 1) < HEAD_DIM

    def level_mask(rows, cols, j):
        same_big = lax.shift_right_logical(rows, jnp.int32(j + 1)) == lax.shift_right_logical(cols, jnp.int32(j + 1))
        same_small = lax.shift_right_logical(rows, jnp.int32(j)) == lax.shift_right_logical(cols, jnp.int32(j))
        return same_big & jnp.logical_not(same_small)

    first_level = level_mask(ir, ic, 0)
    level_blocks = [diag_blocks & level_mask(br & (HEAD_DIM - 1), bc & (HEAD_DIM - 1), j) for j in range(1, CHUNK.bit_length() - 1)]

    def blockdiag(x2):
        x2 = x2.astype(BF16)
        return jnp.where(diag_blocks, jnp.concatenate([x2, x2], axis=0), jnp.zeros((LANES, LANES), BF16))

    npairs = C_HEADS // 2
    for bi in range(nbatch):
        for d in range(2):
            for p in range(npairs):
                if has_init:
                    st_s[bi, d, p] = jnp.concatenate([st0_ref[bi, d, 2 * p], st0_ref[bi, d, 2 * p + 1]], axis=1)
                else:
                    st_s[bi, d, p] = jnp.zeros((HEAD_DIM, LANES), F32)

    def chunk(c, carry):
        c = _index32(c)
        items = []
        for bi, d in [(bi, d) for bi in range(nbatch) for d in range(2)]:
            r0 = bi * t + ((nchunks - 1 - c) if d == 1 else c) * CHUNK
            rows = pl.ds(pl.multiple_of(r0, CHUNK), CHUNK)
            cinc = cinc_s[d, rows]
            cexc = cexc_s[d, rows]
            ctot = cinc[0:1] if d == 1 else cinc[CHUNK - 1:CHUNK]
            e_ninc = jnp.exp(-cinc)
            e_rem = jnp.exp(ctot - cinc)
            e_tot = jnp.exp(ctot)
            bb = b_s[d, rows]
            kt = kt_s[d, rows]
            at = (aa_s[rows] * jnp.exp(cexc)).astype(BF16)
            rt = (zc_ref[rows, C_R] * jnp.exp(cinc)).astype(BF16)
            bt = (bb * e_ninc).astype(BF16)
            ktt = (kt * e_ninc).astype(BF16)
            bh = (bb * e_rem).astype(BF16)
            kh = (kt * e_rem).astype(BF16)
            vm = zc_ref[rows, C_V].astype(BF16)
            for p in range(npairs):
                ps = slice(p * LANES, (p + 1) * LANES)
                items.append(dict(bi=bi, d=d, p=p, rows=rows, e_tot=e_tot[:, ps], at=at[:, ps], rt=rt[:, ps],
                                  bt=bt[:, ps], kt=ktt[:, ps], bh=bh[:, ps], kh=kh[:, ps], vm=vm[:, ps]))
        for it in items:
            it["ar"] = jnp.concatenate([it["at"], it["rt"]], axis=0)
            it["vmd"] = blockdiag(it["vm"])
        for it in items:
            sbk = _bdot_nt(it["ar"], jnp.concatenate([blockdiag(it["bt"]), blockdiag(it["kt"])], axis=0))
            it["sb"] = sbk[:, 0:LANES]
            it["sk"] = sbk[:, LANES:2 * LANES]
        for it in items:
            d = it["d"]
            it["l"] = jnp.where(strict[d], it["sb"][0:CHUNK], 0.0)
            it["mak"] = jnp.where(strict[d], it["sk"][0:CHUNK], 0.0).astype(BF16)
            it["nr"] = jnp.concatenate([jnp.where(incl[d], it["sb"][CHUNK:], 0.0).astype(BF16),
                                        jnp.where(incl[d], it["sk"][CHUNK:], 0.0).astype(BF16)], axis=1)
            it["tm"] = eye + jnp.where(first_level, it["l"], 0.0)
            lb = it["l"].astype(BF16)
            it["l2"] = jnp.concatenate([lb, lb], axis=0)
        for lvl in level_blocks:
            for it in items:
                it["t1"] = _bdot(it["tm"], jnp.where(lvl, it["l2"], jnp.zeros((LANES, LANES), BF16)))
            for it in items:
                it["tm"] = it["tm"] + _bdot(it["t1"], blockdiag(it["tm"]))
        for it in items:
            it["mv"] = _bdot(it["mak"], it["vmd"])
        for it in items:
            it["tau"] = _bdot(it["tm"], jnp.concatenate([blockdiag(it["at"]), blockdiag(it["mv"])], axis=1))
        for it in items:
            it["s0"] = st_s[it["bi"], it["d"], it["p"]]
            it["x"] = _bdot_nt(jnp.concatenate([it["tau"][:, 0:LANES].astype(BF16), it["rt"]], axis=0),
                               blockdiag(it["s0"]))
        for it in items:
            it["u"] = (it["x"][0:CHUNK] + it["tau"][:, LANES:2 * LANES]).astype(BF16)
        for it in items:
            it["y"] = it["x"][CHUNK:] + _bdot(it["nr"], jnp.concatenate([blockdiag(it["u"]), it["vmd"]], axis=0))
        for it in items:
            full = _bdot_tn(jnp.concatenate([it["u"], it["vm"]], axis=0),
                            jnp.concatenate([it["bh"], it["kh"]], axis=0))
            st_s[it["bi"], it["d"], it["p"]] = (it["s0"] * it["e_tot"]
                                                + jnp.where(left_half, full[0:HEAD_DIM], full[HEAD_DIM:2 * HEAD_DIM]))
        for i in range(0, len(items), npairs):
            rows = items[i]["rows"]
            y_s[rows] = y_s[rows] + jnp.concatenate([it["y"] for it in items[i:i + npairs]], axis=1)
        return carry

    lax.fori_loop(0, nchunks, chunk, 0)
    if want_state:
        for bi in range(nbatch):
            if has_prev:
                stout_ref[bi, 0] = prev_ref[bi]
            for d in range(2):
                for p in range(npairs):
                    s2 = st_s[bi, d, p]
                    for q in range(2):
                        if has_prev:
                            stout_ref[bi, 1, d, 2 * p + q] = s2[:, q * HEAD_DIM:(q + 1) * HEAD_DIM]
                        else:
                            stout_ref[bi, d, 2 * p + q] = s2[:, q * HEAD_DIM:(q + 1) * HEAD_DIM]

    gup = gup_ref[...].astype(BF16)
    for rb in range(nbatch * t // RWKV_ROWS):
        rows = slice(rb * RWKV_ROWS, (rb + 1) * RWKV_ROWS)
        y = y_s[rows]
        mu = _head_sum(y, ones) * (1.0 / HEAD_DIM)
        yc = y - mu
        var = _head_sum(yc * yc, ones) * (1.0 / HEAD_DIM)
        yn = yc * lax.rsqrt(var + GN_EPS)
        g = _bdot(_sigmoid(zc_ref[rows, C_G]), gup)
        oc_ref[rows] = ((yn * ln_w + ln_b + bon_s[rows]) * g).astype(BF16)


def _rwkv_call(zc, rvec, wup, aup, gup, layer, t, init_state, want_state, prev_state, cast_weights=()):
    nb = zc.shape[0] // t
    nbatch = min(RWKV_BATCH, nb)
    rows = nbatch * t
    has_init = init_state is not None
    has_prev = prev_state is not None
    st_shape = (2, C_HEADS, HEAD_DIM, HEAD_DIM)
    in_specs = [
        pl.BlockSpec((rows, RWKV_COLS), lambda b: (b, 0), pipeline_mode=pl.Buffered(1 if nb == nbatch else 2)),
        _layer_spec((16, C_DIM), layer),
        _layer_spec((2, W_RANK, C_DIM), layer),
        _layer_spec((2, A_RANK, C_DIM), layer),
        _layer_spec((G_RANK, C_DIM), layer),
    ]
    args = [zc, rvec, wup, aup, gup]
    if has_init:
        in_specs.append(pl.BlockSpec((nbatch, None) + st_shape, lambda b: (b, layer, 0, 0, 0, 0)))
        args.append(init_state)
    if has_prev:
        in_specs.append(pl.BlockSpec((nbatch,) + st_shape, lambda b: (b, 0, 0, 0, 0)))
        args.append(prev_state)
    steps = nb // nbatch
    cast_in, cast_shape, cast_out = _cast_specs(cast_weights, layer, steps)
    in_specs.extend(cast_in)
    args.extend(cast_weights)
    out_shape = [jax.ShapeDtypeStruct((nb * t, C_DIM), BF16)]
    out_specs = [pl.BlockSpec((rows, C_DIM), lambda b: (b, 0))]
    if want_state and has_prev:
        out_shape.append(jax.ShapeDtypeStruct((nb, DEPTH) + st_shape, F32))
        out_specs.append(pl.BlockSpec((nbatch, DEPTH) + st_shape, lambda b: (b, 0, 0, 0, 0, 0)))
    elif want_state:
        out_shape.append(jax.ShapeDtypeStruct((nb,) + st_shape, F32))
        out_specs.append(pl.BlockSpec((nbatch,) + st_shape, lambda b: (b, 0, 0, 0, 0)))
    out_shape.extend(cast_shape)
    out_specs.extend(cast_out)
    return pl.pallas_call(
        functools.partial(_rwkv_body, t=t, nbatch=nbatch, has_init=has_init, want_state=want_state,
                          has_prev=has_prev, ncast=len(cast_weights)),
        out_shape=tuple(out_shape),
        grid=(steps,),
        in_specs=in_specs,
        out_specs=tuple(out_specs),
        scratch_shapes=[
            pltpu.VMEM((rows, C_DIM), F32),
            pltpu.VMEM((2, rows, C_DIM), F32),
            pltpu.VMEM((2, rows, C_DIM), F32),
            pltpu.VMEM((2, rows, C_DIM), F32),
            pltpu.VMEM((2, rows, C_DIM), F32),
            pltpu.VMEM((rows, C_DIM), F32),
            pltpu.VMEM((rows, C_DIM), F32),
            pltpu.VMEM((nbatch, 2, C_HEADS // 2, HEAD_DIM, LANES), F32),
        ],
        compiler_params=pltpu.CompilerParams(vmem_limit_bytes=VMEM_LIMIT),
        name="rwkv_prompt" if want_state else "rwkv_sample",
    )(*args)


def _post_tile(oa_ref, oc_ref, x_ref, mod, g_ref, wout_ref, wgu_ref, wdn_ref, xo_ref):
    gate1 = mod[:, 2 * D_MODEL:3 * D_MODEL]
    shift2 = mod[:, 3 * D_MODEL:4 * D_MODEL]
    scale2 = mod[:, 4 * D_MODEL:5 * D_MODEL]
    gate2 = mod[:, 5 * D_MODEL:6 * D_MODEL]
    sub = ROW_TILE // ROW_SPLIT
    parts = [slice(i * sub, (i + 1) * sub) for i in range(ROW_SPLIT)]
    o = [jnp.dot(jnp.concatenate([oa_ref[r, :], oc_ref[r, :]], axis=1), wout_ref[...],
                 preferred_element_type=F32) for r in parts]
    x1 = [x_ref[r, :] + gate1 * _rms(oi, g_ref[1:2]) for r, oi in zip(parts, o)]
    h2 = [(_rms(xi, g_ref[2:3]) * (1.0 + scale2) + shift2).astype(BF16) for xi in x1]

    def gate_up(step):
        (lo, hi), i = step
        g = jnp.dot(h2[i], wgu_ref[:, lo:hi], preferred_element_type=F32)
        u = jnp.dot(h2[i], wgu_ref[:, FF_DIM + lo:FF_DIM + hi], preferred_element_type=F32)
        return g, u

    steps = [(chunk, i) for chunk in FF_CHUNKS for i in range(ROW_SPLIT)]
    acc = [None] * ROW_SPLIT
    pending = gate_up(steps[0])
    for k, ((lo, hi), i) in enumerate(steps):
        g, u = pending
        if k + 1 < len(steps):
            pending = gate_up(steps[k + 1])
        act = (g * _sigmoid(g) * u).astype(BF16)
        part = jnp.dot(act, wdn_ref[lo:hi, :], preferred_element_type=F32)
        acc[i] = part if acc[i] is None else acc[i] + part
    for r, xi, ai in zip(parts, x1, acc):
        xo_ref[r, :] = xi + gate2 * _rms(ai, g_ref[3:4])


def _post_body(oap_ref, oas_ref, ocp_ref, ocs_ref, xp_ref, xs_ref, mod_ref, g_ref, wout_ref, wgu_ref, wdn_ref,
               xop_ref, xos_ref, *, n_first):
    step = pl.program_id(0)

    @pl.when(step < n_first)
    def _():
        _post_tile(oap_ref, ocp_ref, xp_ref, mod_ref[0:1], g_ref, wout_ref, wgu_ref, wdn_ref, xop_ref)

    @pl.when(step >= n_first)
    def _():
        _post_tile(oas_ref, ocs_ref, xs_ref, _latent_mod_row(mod_ref, step, n_first), g_ref, wout_ref, wgu_ref,
                   wdn_ref, xos_ref)


def _post_call(oatt_p, oatt_s, oc_p, oc_s, xp, xs, mods, gvec, wout_b, wgu_b, wdn_b, layer):
    n_first = xp.shape[0] // ROW_TILE
    n_second = xs.shape[0] // ROW_TILE
    x_first, x_second = _two_group_specs(D_MODEL, n_first)
    return pl.pallas_call(
        functools.partial(_post_body, n_first=n_first),
        out_shape=(jax.ShapeDtypeStruct(xp.shape, F32), jax.ShapeDtypeStruct(xs.shape, F32)),
        grid=(n_first + n_second,),
        in_specs=[
            *_two_group_specs(ATT_OUT, n_first),
            *_two_group_specs(C_DIM, n_first),
            x_first, x_second,
            _layer_spec((MOD_ROWS, MOD_COLS), layer),
            _layer_spec((8, D_MODEL), layer),
            _whole_spec((MIX_DIM, D_MODEL)),
            _whole_spec((D_MODEL, 2 * FF_DIM)),
            _whole_spec((FF_DIM, D_MODEL)),
        ],
        out_specs=(x_first, x_second),
        compiler_params=pltpu.CompilerParams(dimension_semantics=("arbitrary",), vmem_limit_bytes=VMEM_LIMIT),
        name="post_ffn",
    )(oatt_p, oatt_s, oc_p, oc_s, xp, xs, mods, gvec, wout_b, wgu_b, wdn_b)


def _rope_tables(t):
    pos = jnp.arange(t)
    row = (pos // GRID_W).astype(F32)
    col = (pos % GRID_W).astype(F32)
    freqs = ROPE_THETA ** (-jnp.arange(ROPE_PAIRS_AXIS, dtype=F32) / ROPE_PAIRS_AXIS)
    ang = jnp.concatenate([row[:, None] * freqs, col[:, None] * freqs], axis=-1)
    cos = jnp.tile(jnp.cos(ang), (1, 4))
    sin = jnp.sin(ang)
    sin = jnp.tile(jnp.concatenate([-sin, sin], axis=-1), (1, 2))
    return cos, sin


def _to_time_minor(cache):
    b, l, t = cache.shape[:3]
    return jnp.transpose(cache, (0, 1, 3, 4, 2)).reshape(b, l, KV_DIM, t)


def _from_time_minor(slab):
    b, l, _, t = slab.shape
    return jnp.transpose(slab.reshape(b, l, 2, HEAD_DIM, t), (0, 1, 4, 2, 3))


def kernel(x_prompt, x_sample, cache_a_k, cache_a_v, cache_b_k, cache_b_v, state_c, c, c_ctx, w_mod, b_mod, norm_mix_pre, norm_mix_post, norm_ffn_pre, norm_ffn_post, w_in, w_out, a_sink, b_q_norm, b_k_norm, c_w0, c_w_up, c_a0, c_a_up, c_g_up, c_k_k, c_k_a, c_r_k, c_ln_w, c_ln_b, w_gu, w_down):
    cvec = jnp.zeros((MOD_ROWS, D_MODEL), F32).at[0].set(c_ctx).at[1:1 + DEC_BATCH].set(c)
    mods = _mod_call(cvec, w_mod, b_mod)
    cos, sin = _rope_tables(DEC_SEQ)

    gvec = jnp.zeros((DEPTH, 8, D_MODEL), F32)
    gvec = gvec.at[:, 0].set(norm_mix_pre).at[:, 1].set(norm_mix_post)
    gvec = gvec.at[:, 2].set(norm_ffn_pre).at[:, 3].set(norm_ffn_post)
    rvec = jnp.zeros((DEPTH, 16, C_DIM), F32)
    rvec = rvec.at[:, 0].set(c_k_k).at[:, 1].set(c_k_a).at[:, 2].set(c_r_k.reshape(DEPTH, C_DIM))
    rvec = rvec.at[:, 3].set(c_ln_w).at[:, 4].set(c_ln_b)
    rvec = rvec.at[:, 5:7].set(c_w0).at[:, 7:9].set(c_a0)
    gq = jnp.tile(b_q_norm, (1, B_HEADS))[:, None, :]
    gk = jnp.tile(b_k_norm, (1, B_KV_HEADS))[:, None, :]
    caches = tuple(_to_time_minor(t) for t in (cache_a_k, cache_a_v, cache_b_k, cache_b_v))

    xp = x_prompt.reshape(BATCH * SEQ, D_MODEL)
    xs = x_sample.reshape(DEC_BATCH * DEC_SEQ, D_MODEL)
    kv_prev = st_prev = None
    for l in range(DEPTH):
        za_p, zc_p, za_s, zc_s, wout_b, wdn_b = _proj_in_call(xp, xs, mods, gvec, w_in, l, (w_out, w_down))
        oatt_p, *kv_prev = _attn_prompt_call(za_p, a_sink, gq, gk, l, kv_prev)
        if l == 0:
            (kv_prev,) = kv_prev
        oc_p, st_prev, wgu_b = _rwkv_call(zc_p, rvec, c_w_up, c_a_up, c_g_up, l, SEQ, None, True, st_prev,
                                          cast_weights=(w_gu,))
        oatt_s = _attn_sample_call(za_s, a_sink, caches, cos, sin, gq, gk, l)
        (oc_s,) = _rwkv_call(zc_s, rvec, c_w_up, c_a_up, c_g_up, l, DEC_SEQ, state_c, False, None)
        xp, xs = _post_call(oatt_p, oatt_s, oc_p, oc_s, xp, xs, mods, gvec, wout_b, wgu_b, wdn_b, l)

    new_caches = tuple(_from_time_minor(slab) for slab in kv_prev)
    return (xp.reshape(BATCH, SEQ, D_MODEL), xs.reshape(DEC_BATCH, DEC_SEQ, D_MODEL), *new_caches, st_prev)
```

```python
import functools
import math

import jax
import jax.numpy as jnp
from jax import lax
from jax.experimental import pallas as pl
from jax.experimental.pallas import tpu as pltpu

D_MODEL = 1024
BATCH = 32
SEQ = 256
DEPTH = 2
DEC_BATCH = 2
DEC_SEQ = 1024
PAST_LEN = 256
GRID_W = 64
HEAD_DIM = 64
A_HEADS = 4
A_KV_HEADS = 2
B_HEADS = 6
B_KV_HEADS = 2
C_HEADS = 6
A_DIM = A_HEADS * HEAD_DIM
B_DIM = B_HEADS * HEAD_DIM
C_DIM = C_HEADS * HEAD_DIM
MIX_DIM = A_DIM + B_DIM + C_DIM
WINDOW = 128
Q_BLK = 128
W_RANK = 64
A_RANK = 64
G_RANK = 128
FF_DIM = -(-8 * D_MODEL // (3 * 256)) * 256
ROPE_THETA = 10000.0
ROPE_PAIRS_AXIS = HEAD_DIM // 4
NORM_EPS = 1e-6
GN_EPS = 64e-5
NEG_INF = -1e30

LANES = 128
KV_DIM = 2 * HEAD_DIM
assert KV_DIM == LANES


def _col_ranges(widths):
    edges = [0]
    for w in widths:
        edges.append(edges[-1] + w)
    return tuple(slice(lo, hi) for lo, hi in zip(edges[:-1], edges[1:])), edges[-1]


(A_Q, A_K, A_V, B_Q, B_K, B_V), ATT_COLS = _col_ranges((A_DIM, KV_DIM, KV_DIM, B_DIM, KV_DIM, KV_DIM))
(C_R, C_K, C_V, C_W, C_A, C_G), RWKV_COLS = _col_ranges((C_DIM, C_DIM, C_DIM, W_RANK, A_RANK, G_RANK))
IN_COLS = ATT_COLS + RWKV_COLS
ATT_OUT = A_DIM + B_DIM
MOD_COLS = 6 * D_MODEL
MOD_ROWS = 8
N_CACHE = 4
CHUNK = 64
assert CHUNK == HEAD_DIM
QK_SCALE = HEAD_DIM ** -0.5
DECAY_RATE = math.exp(-0.5)
ROW_TILE = 512
X_SLOTS = 3
ROW_SPLIT = 2
RWKV_ROWS = 512
RWKV_BATCH = 4
B_QROWS = 256
ATTN_BATCH = 4
MXU_TILE = 256
FF_CHUNKS = ((0, 4 * MXU_TILE), (4 * MXU_TILE, 8 * MXU_TILE), (8 * MXU_TILE, FF_DIM))
assert FF_DIM % MXU_TILE == 0
VMEM_LIMIT = 56 * 1024 * 1024

F32 = jnp.float32
BF16 = jnp.bfloat16

assert DEPTH == 2


def _bdot(a, b):
    return jnp.dot(a.astype(BF16), b.astype(BF16), preferred_element_type=F32)


def _bdot_nt(a, b):
    return lax.dot_general(a.astype(BF16), b.astype(BF16), (((1,), (1,)), ((), ())), preferred_element_type=F32)


def _bdot_tn(a, b):
    return lax.dot_general(a.astype(BF16), b.astype(BF16), (((0,), (0,)), ((), ())), preferred_element_type=F32)


def _split(x):
    hi = x.astype(BF16)
    lo = (x - hi.astype(F32)).astype(BF16)
    return hi, lo


def _split_dot_left(m, x):
    hi, lo = _split(x)
    return jnp.dot(m, hi, preferred_element_type=F32) + jnp.dot(m, lo, preferred_element_type=F32)


def _head_sum(x, ones):
    return jnp.dot(x.astype(BF16), ones, preferred_element_type=F32)


def _sigmoid(x):
    return 0.5 * jnp.tanh(0.5 * x) + 0.5


def _head_ones(n):
    r = lax.shift_right_logical(lax.broadcasted_iota(jnp.int32, (n, n), 0), jnp.int32(HEAD_DIM.bit_length() - 1))
    c = lax.shift_right_logical(lax.broadcasted_iota(jnp.int32, (n, n), 1), jnp.int32(HEAD_DIM.bit_length() - 1))
    return jnp.where(r == c, 1.0, 0.0).astype(BF16)


def _rms(x, g):
    ms = jnp.mean(x * x, axis=-1, keepdims=True)
    return x * lax.rsqrt(ms + NORM_EPS) * g


def _head_rms(x, g, ones):
    ms = _head_sum(x * x, ones) * (1.0 / HEAD_DIM)
    return x * lax.rsqrt(ms + NORM_EPS) * g


def _rope(x, cos, sin):
    t = x.shape[0]
    lane = lax.broadcasted_iota(jnp.int32, (t, LANES), 1)
    first = (lane & (HEAD_DIM // 2)) == 0
    outs = []
    for i in range(x.shape[1] // LANES):
        xs = x[:, i * LANES:(i + 1) * LANES]
        swapped = jnp.where(first, pltpu.roll(xs, LANES - HEAD_DIM // 2, axis=1), pltpu.roll(xs, HEAD_DIM // 2, axis=1))
        outs.append(xs * cos + swapped * sin)
    return outs[0] if len(outs) == 1 else jnp.concatenate(outs, axis=1)


def _softmax_pv(scores, values, sink=None):
    m = scores[0].max(axis=-1, keepdims=True)
    for s in scores[1:]:
        m = jnp.maximum(m, s.max(axis=-1, keepdims=True))
    if sink is not None:
        m = jnp.maximum(m, sink)
    denom = None
    acc = None
    for s, v in zip(scores, values):
        p = jnp.exp(s - m)
        d = p.sum(axis=-1, keepdims=True)
        o = jnp.dot(p.astype(BF16), v, preferred_element_type=F32)
        denom = d if denom is None else denom + d
        acc = o if acc is None else acc + o
    if sink is not None:
        denom = denom + jnp.exp(sink - m)
    return acc / denom


def _softmax_pv_t(scores_t, values_t, sink=None):
    m = scores_t[0].max(axis=0, keepdims=True)
    for s in scores_t[1:]:
        m = jnp.maximum(m, s.max(axis=0, keepdims=True))
    if sink is not None:
        m = jnp.maximum(m, sink)
    denom = None
    acc = None
    for s, vt in zip(scores_t, values_t):
        p = jnp.exp(s - m)
        d = p.sum(axis=0, keepdims=True)
        o = jnp.dot(vt, p.astype(BF16), preferred_element_type=F32)
        denom = d if denom is None else denom + d
        acc = o if acc is None else acc + o
    if sink is not None:
        denom = denom + jnp.exp(sink - m)
    return acc / denom


def _attend_heads(score_fns, finish_fns, ahead=1):
    n = len(score_fns)
    pending = [fn() for fn in score_fns[:ahead]]
    outs = []
    for i in range(n):
        if i + ahead < n:
            pending.append(score_fns[i + ahead]())
        outs.append(finish_fns[i](pending.pop(0)))
    return outs


def _index32(i):
    return jnp.asarray(i, jnp.int32)


def _layer_spec(shape, layer):
    zeros = (0,) * len(shape)
    return pl.BlockSpec((None,) + shape, lambda *_: (layer,) + zeros, pipeline_mode=pl.Buffered(1))


def _whole_spec(shape):
    zeros = (0,) * len(shape)
    return pl.BlockSpec(shape, lambda *_: zeros, pipeline_mode=pl.Buffered(1))


def _mod_body(c_ref, w_ref, b_ref, o_ref):
    cv = c_ref[...]
    s = cv * _sigmoid(cv)
    o_ref[0] = _bdot(s, w_ref[0]) + b_ref[0]


def _mod_call(cvec, w_mod, b_mod):
    bn = MOD_COLS // 4
    return pl.pallas_call(
        _mod_body,
        out_shape=jax.ShapeDtypeStruct((DEPTH, MOD_ROWS, MOD_COLS), F32),
        grid=(DEPTH, MOD_COLS // bn),
        in_specs=[
            pl.BlockSpec((MOD_ROWS, D_MODEL), lambda l, j: (0, 0)),
            pl.BlockSpec((1, D_MODEL, bn), lambda l, j: (l, 0, j)),
            pl.BlockSpec((1, 1, bn), lambda l, j: (l, 0, j)),
        ],
        out_specs=pl.BlockSpec((1, MOD_ROWS, bn), lambda l, j: (l, 0, j)),
        compiler_params=pltpu.CompilerParams(vmem_limit_bytes=VMEM_LIMIT),
        name="mod_vectors",
    )(cvec, w_mod, b_mod.reshape(DEPTH, 1, MOD_COLS))


def _cast_blocks(srcs, dsts):
    for src, dst in zip(srcs, dsts):
        dst[...] = src[...].astype(BF16)


def _cast_specs(weights, layer, steps):
    def block(i):
        return jnp.minimum(i, steps - 1)

    in_specs = [pl.BlockSpec((None, w.shape[1] // steps, w.shape[2]), lambda i: (layer, block(i), 0)) for w in weights]
    out_shape = [jax.ShapeDtypeStruct(w.shape[1:], BF16) for w in weights]
    out_specs = [pl.BlockSpec((w.shape[1] // steps, w.shape[2]), lambda i: (block(i), 0)) for w in weights]
    return in_specs, out_shape, out_specs


def _two_group_specs(cols, n_first):
    first = pl.BlockSpec((ROW_TILE, cols), lambda i: (jnp.minimum(i, n_first - 1), 0))
    second = pl.BlockSpec((ROW_TILE, cols), lambda i: (jnp.maximum(i - n_first, 0), 0))
    return first, second


def _latent_mod_row(mod_ref, step, n_first):
    row = 1 + (step - n_first) // (DEC_SEQ // ROW_TILE)
    return mod_ref[pl.ds(row, 1), :]


def _x_tile_copy(x_hbm, tile, ring, sem, slot):
    return pltpu.make_async_copy(x_hbm.at[pl.ds(pl.multiple_of(tile * ROW_TILE, ROW_TILE), ROW_TILE)], ring.at[slot],
                                 sem.at[slot])


def _proj_in_body(xp_hbm, xs_hbm, mod_ref, g_ref, w_ref, *rest, n_first, n_steps, ncast):
    cast_src = rest[:ncast]
    zap_ref, zcp_ref, zas_ref, zcs_ref = rest[ncast:ncast + 4]
    cast_dst = rest[ncast + 4:2 * ncast + 4]
    wb_s, x_ring, x_sem = rest[2 * ncast + 4:]
    step = _index32(pl.program_id(0))

    def request(k):
        slot = lax.rem(k, jnp.int32(X_SLOTS))

        @pl.when(k < n_first)
        def _():
            _x_tile_copy(xp_hbm, k, x_ring, x_sem, slot).start()

        @pl.when((k >= n_first) & (k < n_steps))
        def _():
            _x_tile_copy(xs_hbm, k - n_first, x_ring, x_sem, slot).start()

    @pl.when(step == 0)
    def _():
        for k in range(X_SLOTS - 1):
            request(_index32(k))
        wb_s[...] = w_ref[...].astype(BF16)

    request(step + (X_SLOTS - 1))
    slot = lax.rem(step, jnp.int32(X_SLOTS))

    def project(mod, za_ref, zc_ref):
        shift1 = mod[:, 0:D_MODEL]
        scale1 = mod[:, D_MODEL:2 * D_MODEL]
        h = (_rms(x_ring[slot], g_ref[0:1]) * (1.0 + scale1) + shift1).astype(BF16)
        z = jnp.dot(h, wb_s[...], preferred_element_type=F32)
        za_ref[...] = z[:, 0:ATT_COLS]
        zc_ref[...] = z[:, ATT_COLS:IN_COLS]

    @pl.when(step < n_first)
    def _():
        _x_tile_copy(xp_hbm, step, x_ring, x_sem, slot).wait()
        _cast_blocks(cast_src, cast_dst)
        project(mod_ref[0:1], zap_ref, zcp_ref)

    @pl.when(step >= n_first)
    def _():
        _x_tile_copy(xs_hbm, step - n_first, x_ring, x_sem, slot).wait()
        project(_latent_mod_row(mod_ref, step, n_first), zas_ref, zcs_ref)


def _proj_in_call(xp, xs, mods, gvec, w_in, layer, cast_weights):
    n_first = xp.shape[0] // ROW_TILE
    n_second = xs.shape[0] // ROW_TILE
    cast_in, cast_shape, cast_out = _cast_specs(cast_weights, layer, n_first)
    za_first, za_second = _two_group_specs(ATT_COLS, n_first)
    zc_first, zc_second = _two_group_specs(RWKV_COLS, n_first)
    return pl.pallas_call(
        functools.partial(_proj_in_body, n_first=n_first, n_steps=n_first + n_second, ncast=len(cast_weights)),
        out_shape=(jax.ShapeDtypeStruct((xp.shape[0], ATT_COLS), F32), jax.ShapeDtypeStruct((xp.shape[0], RWKV_COLS), F32),
                   jax.ShapeDtypeStruct((xs.shape[0], ATT_COLS), F32), jax.ShapeDtypeStruct((xs.shape[0], RWKV_COLS), F32),
                   *cast_shape),
        grid=(n_first + n_second,),
        in_specs=[
            pl.BlockSpec(memory_space=pl.ANY), pl.BlockSpec(memory_space=pl.ANY),
            _layer_spec((MOD_ROWS, MOD_COLS), layer),
            _layer_spec((8, D_MODEL), layer),
            _layer_spec((D_MODEL, IN_COLS), layer),
            *cast_in,
        ],
        out_specs=(za_first, zc_first, za_second, zc_second, *cast_out),
        scratch_shapes=[pltpu.VMEM((D_MODEL, IN_COLS), BF16), pltpu.VMEM((X_SLOTS, ROW_TILE, D_MODEL), F32),
                        pltpu.SemaphoreType.DMA((X_SLOTS,))],
        compiler_params=pltpu.CompilerParams(dimension_semantics=("arbitrary",), vmem_limit_bytes=VMEM_LIMIT),
        name="proj_in",
    )(xp, xs, mods, gvec, w_in, *cast_weights)


def _attn_prompt_body(*refs, layer, nseq):
    sink_ref, za_ref, gq_ref, gk_ref = refs[:4]
    if layer == 0:
        o_ref, kv_ref = refs[4:]
    else:
        prev_ref, o_ref = refs[4:6]
        cache_refs = refs[6:]
    ones_q = _head_ones(B_DIM)
    ones_k = _head_ones(KV_DIM)
    score_fns, finish_fns = [], []
    for si in range(nseq):
        rows = slice(si * SEQ, (si + 1) * SEQ)
        aq = (za_ref[rows, A_Q] * QK_SCALE).astype(BF16)
        ak = za_ref[rows, A_K]
        bq = _head_rms(za_ref[rows, B_Q], gq_ref[...], ones_q)
        bk = _head_rms(za_ref[rows, B_K], gk_ref[...], ones_k)
        bqb = (bq * QK_SCALE).astype(BF16)
        akb = ak.astype(BF16)
        bkb = bk.astype(BF16)
        pieces_t = (ak.T, za_ref[rows, A_V].T, bk.T, za_ref[rows, B_V].T)
        avt = pieces_t[1].astype(BF16)
        bvt = pieces_t[3].astype(BF16)
        for j, piece_t in enumerate(pieces_t):
            if layer == 0:
                kv_ref[si, j] = piece_t
            else:
                cache_refs[j][si, 0] = prev_ref[si, j]
                cache_refs[j][si, 1] = piece_t

        for h in range(A_HEADS):
            kv = h // (A_HEADS // A_KV_HEADS)
            hs = slice(kv * HEAD_DIM, (kv + 1) * HEAD_DIM)
            score_fns.append(lambda h=h, hs=hs, akb=akb, aq=aq: [_bdot_nt(akb[:, hs], aq[:, h * HEAD_DIM:(h + 1) * HEAD_DIM])])
            finish_fns.append(lambda sc, h=h, hs=hs, avt=avt: _softmax_pv_t(sc, [avt[hs]], sink_ref[layer, h]))
        for h in range(B_HEADS):
            kv = h // (B_HEADS // B_KV_HEADS)
            hs = slice(kv * HEAD_DIM, (kv + 1) * HEAD_DIM)
            score_fns.append(lambda h=h, hs=hs, bkb=bkb, bqb=bqb: [_bdot_nt(bkb[:, hs], bqb[:, h * HEAD_DIM:(h + 1) * HEAD_DIM])])
            finish_fns.append(lambda sc, hs=hs, bvt=bvt: _softmax_pv_t(sc, [bvt[hs]]))
    outs = _attend_heads(score_fns, finish_fns, ahead=len(score_fns))
    heads = A_HEADS + B_HEADS
    for si in range(nseq):
        o_ref[si * SEQ:(si + 1) * SEQ, :] = jnp.concatenate(outs[si * heads:(si + 1) * heads], axis=0).T.astype(BF16)


def _attn_prompt_call(za, sink, gq, gk, layer, prev_kv):
    nb = za.shape[0] // SEQ
    nseq = ATTN_BATCH
    in_specs = [
        pl.BlockSpec(memory_space=pltpu.SMEM),
        pl.BlockSpec((nseq * SEQ, ATT_COLS), lambda b: (b, 0)),
        _layer_spec((1, B_DIM), layer),
        _layer_spec((1, KV_DIM), layer),
    ]
    args = [sink, za, gq, gk]
    out_shape = [jax.ShapeDtypeStruct((nb * SEQ, ATT_OUT), BF16)]
    out_specs = [pl.BlockSpec((nseq * SEQ, ATT_OUT), lambda b: (b, 0))]
    slab_spec = pl.BlockSpec((nseq, N_CACHE, KV_DIM, SEQ), lambda b: (b, 0, 0, 0))
    if layer == 0:
        out_shape.append(jax.ShapeDtypeStruct((nb, N_CACHE, KV_DIM, SEQ), F32))
        out_specs.append(slab_spec)
    else:
        in_specs.append(slab_spec)
        args.append(prev_kv)
        for _ in range(N_CACHE):
            out_shape.append(jax.ShapeDtypeStruct((nb, DEPTH, KV_DIM, SEQ), F32))
            out_specs.append(pl.BlockSpec((nseq, DEPTH, KV_DIM, SEQ), lambda b: (b, 0, 0, 0)))
    return pl.pallas_call(
        functools.partial(_attn_prompt_body, layer=layer, nseq=nseq),
        out_shape=tuple(out_shape),
        grid=(nb // nseq,),
        in_specs=in_specs,
        out_specs=tuple(out_specs),
        compiler_params=pltpu.CompilerParams(vmem_limit_bytes=VMEM_LIMIT),
        name="attn_prompt",
    )(*args)


def _attn_sample_body(sink_ref, za_ref, cak_ref, cav_ref, cbk_ref, cbv_ref, cos_ref, sin_ref, gq_ref, gk_ref,
                      o_ref, qa_s, ka_s, va_s, qb_s, kb_s, vb_s, *, layer):
    t = DEC_SEQ
    cos = cos_ref[...]
    sin = sin_ref[...]
    qa_s[...] = (_rope(za_ref[:, A_Q], cos, sin) * QK_SCALE).astype(BF16)
    zpad = jnp.zeros((Q_BLK, KV_DIM), BF16)
    ka_s[0:Q_BLK] = zpad
    ka_s[Q_BLK + t:] = zpad
    va_s[0:Q_BLK] = zpad
    va_s[Q_BLK + t:] = zpad
    ka_s[Q_BLK:Q_BLK + t] = _rope(za_ref[:, A_K], cos, sin).astype(BF16)
    va_s[Q_BLK:Q_BLK + t] = za_ref[:, A_V].astype(BF16)
    bq = _head_rms(za_ref[:, B_Q], gq_ref[...], _head_ones(B_DIM))
    qb_s[...] = (_rope(bq, cos, sin) * QK_SCALE).astype(BF16)
    bk = _head_rms(za_ref[:, B_K], gk_ref[...], _head_ones(KV_DIM))
    kb_s[0:PAST_LEN] = cbk_ref[0].T.astype(BF16)
    kb_s[PAST_LEN:] = _rope(bk, cos, sin).astype(BF16)
    vb_s[0:PAST_LEN] = cbv_ref[0].T.astype(BF16)
    vb_s[PAST_LEN:] = za_ref[:, B_V].astype(BF16)

    kca = cak_ref[0].T.astype(BF16)
    vca = cav_ref[0].T.astype(BF16)

    ga = A_HEADS // A_KV_HEADS
    gb = B_HEADS // B_KV_HEADS
    qpos = lax.broadcasted_iota(jnp.int32, (ga * Q_BLK, 3 * Q_BLK), 0) & (Q_BLK - 1)
    kpos = lax.broadcasted_iota(jnp.int32, (ga * Q_BLK, 3 * Q_BLK), 1) - Q_BLK
    near = jnp.abs(kpos - qpos) <= WINDOW
    head_of_row = lax.shift_right_logical(lax.broadcasted_iota(jnp.int32, (ga * Q_BLK, 1), 0),
                                          jnp.int32(Q_BLK.bit_length() - 1))
    sinks = []
    for kv in range(A_KV_HEADS):
        col = jnp.full((ga * Q_BLK, 1), sink_ref[layer, kv * ga], F32)
        for g in range(1, ga):
            col = jnp.where(head_of_row == g, sink_ref[layer, kv * ga + g], col)
        sinks.append(col)

    def stack_heads(q, kv, group):
        return jnp.concatenate([q[:, (kv * group + g) * HEAD_DIM:(kv * group + g + 1) * HEAD_DIM] for g in range(group)], axis=0)

    def unstack_heads(outs, group, rows):
        return jnp.concatenate([o[g * rows:(g + 1) * rows] for o in outs for g in range(group)], axis=1)

    def a_block(n, carry):
        r0 = pl.multiple_of(_index32(n) * Q_BLK, Q_BLK)
        q = qa_s[pl.ds(r0, Q_BLK), :]
        kl = ka_s[pl.ds(r0, 3 * Q_BLK), :]
        vl = va_s[pl.ds(r0, 3 * Q_BLK), :]
        kabs = kpos + r0
        mask = near & (kabs >= 0) & (kabs < t)
        score_fns, finish_fns = [], []
        for kv in range(A_KV_HEADS):
            hs = slice(kv * HEAD_DIM, (kv + 1) * HEAD_DIM)

            def scores(kv=kv, hs=hs):
                qs = stack_heads(q, kv, ga)
                return [jnp.where(mask, _bdot_nt(qs, kl[:, hs]), NEG_INF), _bdot_nt(qs, kca[:, hs])]

            score_fns.append(scores)
            finish_fns.append(lambda sc, kv=kv, hs=hs: _softmax_pv(sc, [vl[:, hs], vca[:, hs]], sinks[kv]))
        outs = _attend_heads(score_fns, finish_fns, ahead=A_KV_HEADS)
        o_ref[pl.ds(r0, Q_BLK), 0:A_DIM] = unstack_heads(outs, ga, Q_BLK).astype(BF16)
        return carry

    lax.fori_loop(0, t // Q_BLK, a_block, 0)

    def b_block(n, carry):
        r0 = pl.multiple_of(_index32(n) * B_QROWS, B_QROWS)
        q = qb_s[pl.ds(r0, B_QROWS), :]
        score_fns, finish_fns = [], []
        for kv in range(B_KV_HEADS):
            hs = slice(kv * HEAD_DIM, (kv + 1) * HEAD_DIM)
            score_fns.append(lambda kv=kv, hs=hs: [_bdot_nt(stack_heads(q, kv, gb), kb_s[:, hs])])
            finish_fns.append(lambda sc, hs=hs: _softmax_pv(sc, [vb_s[:, hs]]))
        outs = _attend_heads(score_fns, finish_fns)
        o_ref[pl.ds(r0, B_QROWS), A_DIM:ATT_OUT] = unstack_heads(outs, gb, B_QROWS).astype(BF16)
        return carry

    lax.fori_loop(0, t // B_QROWS, b_block, 0)


def _attn_sample_call(za, sink, caches, cos, sin, gq, gk, layer):
    nb = za.shape[0] // DEC_SEQ
    t = DEC_SEQ
    cache_spec = pl.BlockSpec((1, None, KV_DIM, PAST_LEN), lambda b: (b, layer, 0, 0))
    return pl.pallas_call(
        functools.partial(_attn_sample_body, layer=layer),
        out_shape=jax.ShapeDtypeStruct((nb * t, ATT_OUT), BF16),
        grid=(nb,),
        in_specs=[
            pl.BlockSpec(memory_space=pltpu.SMEM),
            pl.BlockSpec((t, ATT_COLS), lambda b: (b, 0)),
            cache_spec, cache_spec, cache_spec, cache_spec,
            pl.BlockSpec((t, LANES), lambda b: (0, 0)),
            pl.BlockSpec((t, LANES), lambda b: (0, 0)),
            _layer_spec((1, B_DIM), layer),
            _layer_spec((1, KV_DIM), layer),
        ],
        out_specs=pl.BlockSpec((t, ATT_OUT), lambda b: (b, 0)),
        scratch_shapes=[
            pltpu.VMEM((t, A_DIM), BF16),
            pltpu.VMEM((t + 2 * Q_BLK, KV_DIM), BF16),
            pltpu.VMEM((t + 2 * Q_BLK, KV_DIM), BF16),
            pltpu.VMEM((t, B_DIM), BF16),
            pltpu.VMEM((t + PAST_LEN, KV_DIM), BF16),
            pltpu.VMEM((t + PAST_LEN, KV_DIM), BF16),
        ],
        compiler_params=pltpu.CompilerParams(vmem_limit_bytes=VMEM_LIMIT),
        name="attn_sample",
    )(sink, za, *caches, cos, sin, gq, gk)


def _rwkv_body(*refs, t, nbatch, has_init, want_state, has_prev, ncast):
    refs = list(refs)
    zc_ref, cv_ref, wup_ref, aup_ref, gup_ref = refs[:5]
    pos = 5
    st0_ref = prev_ref = stout_ref = None
    if has_init:
        st0_ref = refs[pos]
        pos += 1
    if has_prev:
        prev_ref = refs[pos]
        pos += 1
    cast_src = refs[pos:pos + ncast]
    pos += ncast
    oc_ref = refs[pos]
    pos += 1
    if want_state:
        stout_ref = refs[pos]
        pos += 1
    cast_dst = refs[pos:pos + ncast]
    pos += ncast
    aa_s, ld_s, b_s, kt_s, bon_s, y_s, st_s = refs[pos:]

    _cast_blocks(cast_src, cast_dst)

    nchunks = t // CHUNK
    ones = _head_ones(C_DIM)
    k_k = cv_ref[0:1]
    k_a = cv_ref[1:2]
    r_k = cv_ref[2:3]
    ln_w = cv_ref[3:4]
    ln_b = cv_ref[4:5]
    w0h = (0.5 * cv_ref[5:6], 0.5 * cv_ref[6:7])
    a0h = (0.5 * cv_ref[7:8], 0.5 * cv_ref[8:9])
    wup_h = ((0.5 * wup_ref[0]).astype(BF16), (0.5 * wup_ref[1]).astype(BF16))
    aup_h = ((0.5 * aup_ref[0]).astype(BF16), (0.5 * aup_ref[1]).astype(BF16))
    c1 = 1.0 - 0.5 * k_a
    c2 = 0.5 * k_a
    half_rate = 0.5 * DECAY_RATE

    for rb in range(nbatch * t // RWKV_ROWS):
        rows = slice(rb * RWKV_ROWS, (rb + 1) * RWKV_ROWS)
        zr = zc_ref[rows, C_R]
        zk = zc_ref[rows, C_K]
        zv = zc_ref[rows, C_V]
        tw = jnp.tanh(zc_ref[rows, C_W])
        xa = zc_ref[rows, C_A]
        kkr = zk * k_k
        kk_half = kkr * (0.5 * lax.rsqrt(_head_sum(kkr * kkr, ones) + 1e-12))
        aa_s[rows] = -2.0 * kk_half
        kt_sum = None
        for d in range(2):
            th_w = jnp.tanh(w0h[d] + _bdot(tw, wup_h[d]))
            ld_s[d, rows] = th_w * (-half_rate) - half_rate
            th_a = jnp.tanh(a0h[d] + _bdot(xa, aup_h[d]))
            kt = zk * (c1 + c2 * th_a)
            kt_s[d, rows] = kt
            b_s[d, rows] = kk_half * th_a + kk_half
            kt_sum = kt if kt_sum is None else kt_sum + kt
        bon_s[rows] = _head_sum(zr * kt_sum * r_k, ones) * zv
        y_s[rows] = jnp.zeros((RWKV_ROWS, C_DIM), F32)

    ir = lax.broadcasted_iota(jnp.int32, (CHUNK, LANES), 0)
    ic = lax.broadcasted_iota(jnp.int32, (CHUNK, LANES), 1) & (HEAD_DIM - 1)
    eye = jnp.where(ir == ic, 1.0, 0.0).astype(F32)
    incl = (ic <= ir, ic >= ir)
    strict = (ic < ir, ic > ir)
    sr = lax.broadcasted_iota(jnp.int32, (CHUNK, CHUNK), 0)
    sc = lax.broadcasted_iota(jnp.int32, (CHUNK, CHUNK), 1)
    tri = tuple(jnp.where(m, 1.0, 0.0).astype(BF16) for m in (sc <= sr, sc >= sr))
    br = lax.broadcasted_iota(jnp.int32, (LANES, LANES), 0)
    bc = lax.broadcasted_iota(jnp.int32, (LANES, LANES), 1)
    diag_blocks = (br < HEAD_DIM) == (bc < HEAD_DIM)
    left_half = lax.broadcasted_iota(jnp.int32, (CHUNK, LANES), 1) < HEAD_DIM

    def level_mask(rows, cols, j):
        same_big = lax.shift_right_logical(rows, jnp.int32(j + 1)) == lax.shift_right_logical(cols, jnp.int32(j + 1))
        same_small = lax.shift_right_logical(rows, jnp.int32(j)) == lax.shift_right_logical(cols, jnp.int32(j))
        return same_big & jnp.logical_not(same_small)

    first_level = level_mask(ir, ic, 0)
    level_blocks = [diag_blocks & level_mask(br & (HEAD_DIM - 1), bc & (HEAD_DIM - 1), j) for j in range(1, CHUNK.bit_length() - 1)]

    def blockdiag(x2):
        x2 = x2.astype(BF16)
        return jnp.where(diag_blocks, jnp.concatenate([x2, x2], axis=0), jnp.zeros((LANES, LANES), BF16))

    npairs = C_HEADS // 2
    for bi in range(nbatch):
        for d in range(2):
            for p in range(npairs):
                if has_init:
                    st_s[bi, d, p] = jnp.concatenate([st0_ref[bi, d, 2 * p], st0_ref[bi, d, 2 * p + 1]], axis=1)
                else:
                    st_s[bi, d, p] = jnp.zeros((HEAD_DIM, LANES), F32)

    def chunk(c, carry):
        c = _index32(c)
        items = []
        for bi, d in [(bi, d) for bi in range(nbatch) for d in range(2)]:
            r0 = bi * t + ((nchunks - 1 - c) if d == 1 else c) * CHUNK
            rows = pl.ds(pl.multiple_of(r0, CHUNK), CHUNK)
            ld = ld_s[d, rows]
            cinc = _split_dot_left(tri[d], ld)
            cexc = cinc - ld
            ctot = cinc[0:1] if d == 1 else cinc[CHUNK - 1:CHUNK]
            e_ninc = jnp.exp(-cinc)
            e_rem = jnp.exp(ctot - cinc)
            e_tot = jnp.exp(ctot)
            bb = b_s[d, rows]
            kt = kt_s[d, rows]
            at = (aa_s[rows] * jnp.exp(cexc)).astype(BF16)
            rt = (zc_ref[rows, C_R] * jnp.exp(cinc)).astype(BF16)
            bt = (bb * e_ninc).astype(BF16)
            ktt = (kt * e_ninc).astype(BF16)
            bh = (bb * e_rem).astype(BF16)
            kh = (kt * e_rem).astype(BF16)
            vm = zc_ref[rows, C_V].astype(BF16)
            for p in range(npairs):
                ps = slice(p * LANES, (p + 1) * LANES)
                items.append(dict(bi=bi, d=d, p=p, rows=rows, e_tot=e_tot[:, ps], at=at[:, ps], rt=rt[:, ps],
                                  bt=bt[:, ps], kt=ktt[:, ps], bh=bh[:, ps], kh=kh[:, ps], vm=vm[:, ps]))
        for it in items:
            it["ar"] = jnp.concatenate([it["at"], it["rt"]], axis=0)
            it["vmd"] = blockdiag(it["vm"])
        for it in items:
            sbk = _bdot_nt(it["ar"], jnp.concatenate([blockdiag(it["bt"]), blockdiag(it["kt"])], axis=0))
            it["sb"] = sbk[:, 0:LANES]
            it["sk"] = sbk[:, LANES:2 * LANES]
        for it in items:
            d = it["d"]
            it["l"] = jnp.where(strict[d], it["sb"][0:CHUNK], 0.0)
            it["mak"] = jnp.where(strict[d], it["sk"][0:CHUNK], 0.0).astype(BF16)
            it["nr"] = jnp.concatenate([jnp.where(incl[d], it["sb"][CHUNK:], 0.0).astype(BF16),
                                        jnp.where(incl[d], it["sk"][CHUNK:], 0.0).astype(BF16)], axis=1)
            it["tm"] = eye + jnp.where(first_level, it["l"], 0.0)
            lb = it["l"].astype(BF16)
            it["l2"] = jnp.concatenate([lb, lb], axis=0)
        for lvl in level_blocks:
            for it in items:
                it["t1"] = _bdot(it["tm"], jnp.where(lvl, it["l2"], jnp.zeros((LANES, LANES), BF16)))
            for it in items:
                it["tm"] = it["tm"] + _bdot(it["t1"], blockdiag(it["tm"]))
        for it in items:
            it["mv"] = _bdot(it["mak"], it["vmd"])
        for it in items:
            it["tau"] = _bdot(it["tm"], jnp.concatenate([blockdiag(it["at"]), blockdiag(it["mv"])], axis=1))
        for it in items:
            it["s0"] = st_s[it["bi"], it["d"], it["p"]]
            it["x"] = _bdot_nt(jnp.concatenate([it["tau"][:, 0:LANES].astype(BF16), it["rt"]], axis=0),
                               blockdiag(it["s0"]))
        for it in items:
            it["u"] = (it["x"][0:CHUNK] + it["tau"][:, LANES:2 * LANES]).astype(BF16)
        for it in items:
            it["y"] = it["x"][CHUNK:] + _bdot(it["nr"], jnp.concatenate([blockdiag(it["u"]), it["vmd"]], axis=0))
        for it in items:
            full = _bdot_tn(jnp.concatenate([it["u"], it["vm"]], axis=0),
                            jnp.concatenate([it["bh"], it["kh"]], axis=0))
            st_s[it["bi"], it["d"], it["p"]] = (it["s0"] * it["e_tot"]
                                                + jnp.where(left_half, full[0:HEAD_DIM], full[HEAD_DIM:2 * HEAD_DIM]))
        for i in range(0, len(items), npairs):
            rows = items[i]["rows"]
            y_s[rows] = y_s[rows] + jnp.concatenate([it["y"] for it in items[i:i + npairs]], axis=1)
        return carry

    lax.fori_loop(0, nchunks, chunk, 0)
    if want_state:
        for bi in range(nbatch):
            if has_prev:
                stout_ref[bi, 0] = prev_ref[bi]
            for d in range(2):
                for p in range(npairs):
                    s2 = st_s[bi, d, p]
                    for q in range(2):
                        if has_prev:
                            stout_ref[bi, 1, d, 2 * p + q] = s2[:, q * HEAD_DIM:(q + 1) * HEAD_DIM]
                        else:
                            stout_ref[bi, d, 2 * p + q] = s2[:, q * HEAD_DIM:(q + 1) * HEAD_DIM]

    gup = gup_ref[...].astype(BF16)
    for rb in range(nbatch * t // RWKV_ROWS):
        rows = slice(rb * RWKV_ROWS, (rb + 1) * RWKV_ROWS)
        y = y_s[rows]
        mu = _head_sum(y, ones) * (1.0 / HEAD_DIM)
        yc = y - mu
        var = _head_sum(yc * yc, ones) * (1.0 / HEAD_DIM)
        yn = yc * lax.rsqrt(var + GN_EPS)
        g = _bdot(_sigmoid(zc_ref[rows, C_G]), gup)
        oc_ref[rows] = ((yn * ln_w + ln_b + bon_s[rows]) * g).astype(BF16)


def _rwkv_call(zc, rvec, wup, aup, gup, layer, t, init_state, want_state, prev_state, cast_weights=()):
    nb = zc.shape[0] // t
    nbatch = min(RWKV_BATCH, nb)
    rows = nbatch * t
    has_init = init_state is not None
    has_prev = prev_state is not None
    st_shape = (2, C_HEADS, HEAD_DIM, HEAD_DIM)
    in_specs = [
        pl.BlockSpec((rows, RWKV_COLS), lambda b: (b, 0), pipeline_mode=pl.Buffered(1 if nb == nbatch else 2)),
        _layer_spec((16, C_DIM), layer),
        _layer_spec((2, W_RANK, C_DIM), layer),
        _layer_spec((2, A_RANK, C_DIM), layer),
        _layer_spec((G_RANK, C_DIM), layer),
    ]
    args = [zc, rvec, wup, aup, gup]
    if has_init:
        in_specs.append(pl.BlockSpec((nbatch, None) + st_shape, lambda b: (b, layer, 0, 0, 0, 0)))
        args.append(init_state)
    if has_prev:
        in_specs.append(pl.BlockSpec((nbatch,) + st_shape, lambda b: (b, 0, 0, 0, 0)))
        args.append(prev_state)
    steps = nb // nbatch
    cast_in, cast_shape, cast_out = _cast_specs(cast_weights, layer, steps)
    in_specs.extend(cast_in)
    args.extend(cast_weights)
    out_shape = [jax.ShapeDtypeStruct((nb * t, C_DIM), BF16)]
    out_specs = [pl.BlockSpec((rows, C_DIM), lambda b: (b, 0))]
    if want_state and has_prev:
        out_shape.append(jax.ShapeDtypeStruct((nb, DEPTH) + st_shape, F32))
        out_specs.append(pl.BlockSpec((nbatch, DEPTH) + st_shape, lambda b: (b, 0, 0, 0, 0, 0)))
    elif want_state:
        out_shape.append(jax.ShapeDtypeStruct((nb,) + st_shape, F32))
        out_specs.append(pl.BlockSpec((nbatch,) + st_shape, lambda b: (b, 0, 0, 0, 0)))
    out_shape.extend(cast_shape)
    out_specs.extend(cast_out)
    return pl.pallas_call(
        functools.partial(_rwkv_body, t=t, nbatch=nbatch, has_init=has_init, want_state=want_state,
                          has_prev=has_prev, ncast=len(cast_weights)),
        out_shape=tuple(out_shape),
        grid=(steps,),
        in_specs=in_specs,
        out_specs=tuple(out_specs),
        scratch_shapes=[
            pltpu.VMEM((rows, C_DIM), F32),
            pltpu.VMEM((2, rows, C_DIM), F32),
            pltpu.VMEM((2, rows, C_DIM), F32),
            pltpu.VMEM((2, rows, C_DIM), F32),
            pltpu.VMEM((rows, C_DIM), F32),
            pltpu.VMEM((rows, C_DIM), F32),
            pltpu.VMEM((nbatch, 2, C_HEADS // 2, HEAD_DIM, LANES), F32),
        ],
        compiler_params=pltpu.CompilerParams(vmem_limit_bytes=VMEM_LIMIT),
        name="rwkv_prompt" if want_state else "rwkv_sample",
    )(*args)


def _post_tile(oa_ref, oc_ref, x_ref, mod, g_ref, wout_ref, wgu_ref, wdn_ref, xo_ref):
    gate1 = mod[:, 2 * D_MODEL:3 * D_MODEL]
    shift2 = mod[:, 3 * D_MODEL:4 * D_MODEL]
    scale2 = mod[:, 4 * D_MODEL:5 * D_MODEL]
    gate2 = mod[:, 5 * D_MODEL:6 * D_MODEL]
    sub = ROW_TILE // ROW_SPLIT
    parts = [slice(i * sub, (i + 1) * sub) for i in range(ROW_SPLIT)]
    o = [jnp.dot(jnp.concatenate([oa_ref[r, :], oc_ref[r, :]], axis=1), wout_ref[...],
                 preferred_element_type=F32) for r in parts]
    x1 = [x_ref[r, :] + gate1 * _rms(oi, g_ref[1:2]) for r, oi in zip(parts, o)]
    h2 = [(_rms(xi, g_ref[2:3]) * (1.0 + scale2) + shift2).astype(BF16) for xi in x1]

    def gate_up(step):
        (lo, hi), i = step
        g = jnp.dot(h2[i], wgu_ref[:, lo:hi], preferred_element_type=F32)
        u = jnp.dot(h2[i], wgu_ref[:, FF_DIM + lo:FF_DIM + hi], preferred_element_type=F32)
        return g, u

    steps = [(chunk, i) for chunk in FF_CHUNKS for i in range(ROW_SPLIT)]
    acc = [None] * ROW_SPLIT
    pending = gate_up(steps[0])
    for k, ((lo, hi), i) in enumerate(steps):
        g, u = pending
        if k + 1 < len(steps):
            pending = gate_up(steps[k + 1])
        act = (g * _sigmoid(g) * u).astype(BF16)
        part = jnp.dot(act, wdn_ref[lo:hi, :], preferred_element_type=F32)
        acc[i] = part if acc[i] is None else acc[i] + part
    for r, xi, ai in zip(parts, x1, acc):
        xo_ref[r, :] = xi + gate2 * _rms(ai, g_ref[3:4])


def _post_body(oap_ref, oas_ref, ocp_ref, ocs_ref, xp_ref, xs_ref, mod_ref, g_ref, wout_ref, wgu_ref, wdn_ref,
               xop_ref, xos_ref, *, n_first):
    step = pl.program_id(0)

    @pl.when(step < n_first)
    def _():
        _post_tile(oap_ref, ocp_ref, xp_ref, mod_ref[0:1], g_ref, wout_ref, wgu_ref, wdn_ref, xop_ref)

    @pl.when(step >= n_first)
    def _():
        _post_tile(oas_ref, ocs_ref, xs_ref, _latent_mod_row(mod_ref, step, n_first), g_ref, wout_ref, wgu_ref,
                   wdn_ref, xos_ref)


def _post_call(oatt_p, oatt_s, oc_p, oc_s, xp, xs, mods, gvec, wout_b, wgu_b, wdn_b, layer):
    n_first = xp.shape[0] // ROW_TILE
    n_second = xs.shape[0] // ROW_TILE
    x_first, x_second = _two_group_specs(D_MODEL, n_first)
    return pl.pallas_call(
        functools.partial(_post_body, n_first=n_first),
        out_shape=(jax.ShapeDtypeStruct(xp.shape, F32), jax.ShapeDtypeStruct(xs.shape, F32)),
        grid=(n_first + n_second,),
        in_specs=[
            *_two_group_specs(ATT_OUT, n_first),
            *_two_group_specs(C_DIM, n_first),
            x_first, x_second,
            _layer_spec((MOD_ROWS, MOD_COLS), layer),
            _layer_spec((8, D_MODEL), layer),
            _whole_spec((MIX_DIM, D_MODEL)),
            _whole_spec((D_MODEL, 2 * FF_DIM)),
            _whole_spec((FF_DIM, D_MODEL)),
        ],
        out_specs=(x_first, x_second),
        compiler_params=pltpu.CompilerParams(dimension_semantics=("arbitrary",), vmem_limit_bytes=VMEM_LIMIT),
        name="post_ffn",
    )(oatt_p, oatt_s, oc_p, oc_s, xp, xs, mods, gvec, wout_b, wgu_b, wdn_b)


def _rope_tables(t):
    pos = jnp.arange(t)
    row = (pos // GRID_W).astype(F32)
    col = (pos % GRID_W).astype(F32)
    freqs = ROPE_THETA ** (-jnp.arange(ROPE_PAIRS_AXIS, dtype=F32) / ROPE_PAIRS_AXIS)
    ang = jnp.concatenate([row[:, None] * freqs, col[:, None] * freqs], axis=-1)
    cos = jnp.tile(jnp.cos(ang), (1, 4))
    sin = jnp.sin(ang)
    sin = jnp.tile(jnp.concatenate([-sin, sin], axis=-1), (1, 2))
    return cos, sin


def _to_time_minor(cache):
    b, l, t = cache.shape[:3]
    return jnp.transpose(cache, (0, 1, 3, 4, 2)).reshape(b, l, KV_DIM, t)


def _from_time_minor(slab):
    b, l, _, t = slab.shape
    return jnp.transpose(slab.reshape(b, l, 2, HEAD_DIM, t), (0, 1, 4, 2, 3))


def kernel(x_prompt, x_sample, cache_a_k, cache_a_v, cache_b_k, cache_b_v, state_c, c, c_ctx, w_mod, b_mod, norm_mix_pre, norm_mix_post, norm_ffn_pre, norm_ffn_post, w_in, w_out, a_sink, b_q_norm, b_k_norm, c_w0, c_w_up, c_a0, c_a_up, c_g_up, c_k_k, c_k_a, c_r_k, c_ln_w, c_ln_b, w_gu, w_down):
    cvec = jnp.zeros((MOD_ROWS, D_MODEL), F32).at[0].set(c_ctx).at[1:1 + DEC_BATCH].set(c)
    mods = _mod_call(cvec, w_mod, b_mod)
    cos, sin = _rope_tables(DEC_SEQ)

    gvec = jnp.zeros((DEPTH, 8, D_MODEL), F32)
    gvec = gvec.at[:, 0].set(norm_mix_pre).at[:, 1].set(norm_mix_post)
    gvec = gvec.at[:, 2].set(norm_ffn_pre).at[:, 3].set(norm_ffn_post)
    rvec = jnp.zeros((DEPTH, 16, C_DIM), F32)
    rvec = rvec.at[:, 0].set(c_k_k).at[:, 1].set(c_k_a).at[:, 2].set(c_r_k.reshape(DEPTH, C_DIM))
    rvec = rvec.at[:, 3].set(c_ln_w).at[:, 4].set(c_ln_b)
    rvec = rvec.at[:, 5:7].set(c_w0).at[:, 7:9].set(c_a0)
    gq = jnp.tile(b_q_norm, (1, B_HEADS))[:, None, :]
    gk = jnp.tile(b_k_norm, (1, B_KV_HEADS))[:, None, :]
    caches = tuple(_to_time_minor(t) for t in (cache_a_k, cache_a_v, cache_b_k, cache_b_v))

    xp = x_prompt.reshape(BATCH * SEQ, D_MODEL)
    xs = x_sample.reshape(DEC_BATCH * DEC_SEQ, D_MODEL)
    kv_prev = st_prev = None
    for l in range(DEPTH):
        za_p, zc_p, za_s, zc_s, wout_b, wdn_b = _proj_in_call(xp, xs, mods, gvec, w_in, l, (w_out, w_down))
        oatt_p, *kv_prev = _attn_prompt_call(za_p, a_sink, gq, gk, l, kv_prev)
        if l == 0:
            (kv_prev,) = kv_prev
        oc_p, st_prev, wgu_b = _rwkv_call(zc_p, rvec, c_w_up, c_a_up, c_g_up, l, SEQ, None, True, st_prev,
                                          cast_weights=(w_gu,))
        oatt_s = _attn_sample_call(za_s, a_sink, caches, cos, sin, gq, gk, l)
        (oc_s,) = _rwkv_call(zc_s, rvec, c_w_up, c_a_up, c_g_up, l, DEC_SEQ, state_c, False, None)
        xp, xs = _post_call(oatt_p, oatt_s, oc_p, oc_s, xp, xs, mods, gvec, wout_b, wgu_b, wdn_b, l)

    new_caches = tuple(_from_time_minor(slab) for slab in kv_prev)
    return (xp.reshape(BATCH, SEQ, D_MODEL), xs.reshape(DEC_BATCH, DEC_SEQ, D_MODEL), *new_caches, st_prev)
```

```python
import functools
import math

import jax
import jax.numpy as jnp
from jax import lax
from jax.experimental import pallas as pl
from jax.experimental.pallas import tpu as pltpu

D_MODEL = 1024
BATCH = 32
SEQ = 256
DEPTH = 2
DEC_BATCH = 2
DEC_SEQ = 1024
PAST_LEN = 256
GRID_W = 64
HEAD_DIM = 64
A_HEADS = 4
A_KV_HEADS = 2
B_HEADS = 6
B_KV_HEADS = 2
C_HEADS = 6
A_DIM = A_HEADS * HEAD_DIM
B_DIM = B_HEADS * HEAD_DIM
C_DIM = C_HEADS * HEAD_DIM
MIX_DIM = A_DIM + B_DIM + C_DIM
WINDOW = 128
Q_BLK = 128
W_RANK = 64
A_RANK = 64
G_RANK = 128
FF_DIM = -(-8 * D_MODEL // (3 * 256)) * 256
ROPE_THETA = 10000.0
ROPE_PAIRS_AXIS = HEAD_DIM // 4
NORM_EPS = 1e-6
GN_EPS = 64e-5
NEG_INF = -1e30

LANES = 128
KV_DIM = 2 * HEAD_DIM
assert KV_DIM == LANES


def _col_ranges(widths):
    edges = [0]
    for w in widths:
        edges.append(edges[-1] + w)
    return tuple(slice(lo, hi) for lo, hi in zip(edges[:-1], edges[1:])), edges[-1]


(A_Q, A_K, A_V, B_Q, B_K, B_V), ATT_COLS = _col_ranges((A_DIM, KV_DIM, KV_DIM, B_DIM, KV_DIM, KV_DIM))
(C_R, C_K, C_V, C_W, C_A, C_G), RWKV_COLS = _col_ranges((C_DIM, C_DIM, C_DIM, W_RANK, A_RANK, G_RANK))
IN_COLS = ATT_COLS + RWKV_COLS
ATT_OUT = A_DIM + B_DIM
MOD_COLS = 6 * D_MODEL
MOD_ROWS = 8
N_CACHE = 4
CHUNK = 64
assert CHUNK == HEAD_DIM
QK_SCALE = HEAD_DIM ** -0.5
DECAY_RATE = math.exp(-0.5)
ROW_TILE = 512
X_SLOTS = 3
ROW_SPLIT = 2
RWKV_ROWS = 512
RWKV_BATCH = 4
B_QROWS = 256
ATTN_BATCH = 4
MXU_TILE = 256
FF_CHUNKS = ((0, 4 * MXU_TILE), (4 * MXU_TILE, 8 * MXU_TILE), (8 * MXU_TILE, FF_DIM))
assert FF_DIM % MXU_TILE == 0
VMEM_LIMIT = 56 * 1024 * 1024

F32 = jnp.float32
BF16 = jnp.bfloat16

assert DEPTH == 2


def _bdot(a, b):
    return jnp.dot(a.astype(BF16), b.astype(BF16), preferred_element_type=F32)


def _bdot_nt(a, b):
    return lax.dot_general(a.astype(BF16), b.astype(BF16), (((1,), (1,)), ((), ())), preferred_element_type=F32)


def _bdot_tn(a, b):
    return lax.dot_general(a.astype(BF16), b.astype(BF16), (((0,), (0,)), ((), ())), preferred_element_type=F32)


def _split(x):
    hi = x.astype(BF16)
    lo = (x - hi.astype(F32)).astype(BF16)
    return hi, lo


def _split_dot_left(m, x):
    hi, lo = _split(x)
    return jnp.dot(m, hi, preferred_element_type=F32) + jnp.dot(m, lo, preferred_element_type=F32)


def _head_sum(x, ones):
    return jnp.dot(x.astype(BF16), ones, preferred_element_type=F32)


def _sigmoid(x):
    return 0.5 * jnp.tanh(0.5 * x) + 0.5


def _head_ones(n):
    r = lax.shift_right_logical(lax.broadcasted_iota(jnp.int32, (n, n), 0), jnp.int32(HEAD_DIM.bit_length() - 1))
    c = lax.shift_right_logical(lax.broadcasted_iota(jnp.int32, (n, n), 1), jnp.int32(HEAD_DIM.bit_length() - 1))
    return jnp.where(r == c, 1.0, 0.0).astype(BF16)


def _rms(x, g):
    ms = jnp.mean(x * x, axis=-1, keepdims=True)
    return x * lax.rsqrt(ms + NORM_EPS) * g


def _head_rms(x, g, ones):
    ms = _head_sum(x * x, ones) * (1.0 / HEAD_DIM)
    return x * lax.rsqrt(ms + NORM_EPS) * g


def _rope(x, cos, sin):
    t = x.shape[0]
    lane = lax.broadcasted_iota(jnp.int32, (t, LANES), 1)
    first = (lane & (HEAD_DIM // 2)) == 0
    outs = []
    for i in range(x.shape[1] // LANES):
        xs = x[:, i * LANES:(i + 1) * LANES]
        swapped = jnp.where(first, pltpu.roll(xs, LANES - HEAD_DIM // 2, axis=1), pltpu.roll(xs, HEAD_DIM // 2, axis=1))
        outs.append(xs * cos + swapped * sin)
    return outs[0] if len(outs) == 1 else jnp.concatenate(outs, axis=1)


def _softmax_pv(scores, values, sink=None):
    m = scores[0].max(axis=-1, keepdims=True)
    for s in scores[1:]:
        m = jnp.maximum(m, s.max(axis=-1, keepdims=True))
    if sink is not None:
        m = jnp.maximum(m, sink)
    denom = None
    acc = None
    for s, v in zip(scores, values):
        p = jnp.exp(s - m)
        d = p.sum(axis=-1, keepdims=True)
        o = jnp.dot(p.astype(BF16), v, preferred_element_type=F32)
        denom = d if denom is None else denom + d
        acc = o if acc is None else acc + o
    if sink is not None:
        denom = denom + jnp.exp(sink - m)
    return acc / denom


def _softmax_pv_t(scores_t, values_t, sink=None):
    m = scores_t[0].max(axis=0, keepdims=True)
    for s in scores_t[1:]:
        m = jnp.maximum(m, s.max(axis=0, keepdims=True))
    if sink is not None:
        m = jnp.maximum(m, sink)
    denom = None
    acc = None
    for s, vt in zip(scores_t, values_t):
        p = jnp.exp(s - m)
        d = p.sum(axis=0, keepdims=True)
        o = jnp.dot(vt, p.astype(BF16), preferred_element_type=F32)
        denom = d if denom is None else denom + d
        acc = o if acc is None else acc + o
    if sink is not None:
        denom = denom + jnp.exp(sink - m)
    return acc / denom


def _attend_heads(score_fns, finish_fns, ahead=1):
    n = len(score_fns)
    pending = [fn() for fn in score_fns[:ahead]]
    outs = []
    for i in range(n):
        if i + ahead < n:
            pending.append(score_fns[i + ahead]())
        outs.append(finish_fns[i](pending.pop(0)))
    return outs


def _index32(i):
    return jnp.asarray(i, jnp.int32)


def _layer_spec(shape, layer):
    zeros = (0,) * len(shape)
    return pl.BlockSpec((None,) + shape, lambda *_: (layer,) + zeros, pipeline_mode=pl.Buffered(1))


def _whole_spec(shape):
    zeros = (0,) * len(shape)
    return pl.BlockSpec(shape, lambda *_: zeros, pipeline_mode=pl.Buffered(1))


def _mod_body(c_ref, w_ref, b_ref, o_ref):
    cv = c_ref[...]
    s = cv * _sigmoid(cv)
    o_ref[0] = _bdot(s, w_ref[0]) + b_ref[0]


def _mod_call(cvec, w_mod, b_mod):
    bn = MOD_COLS // 4
    return pl.pallas_call(
        _mod_body,
        out_shape=jax.ShapeDtypeStruct((DEPTH, MOD_ROWS, MOD_COLS), F32),
        grid=(DEPTH, MOD_COLS // bn),
        in_specs=[
            pl.BlockSpec((MOD_ROWS, D_MODEL), lambda l, j: (0, 0)),
            pl.BlockSpec((1, D_MODEL, bn), lambda l, j: (l, 0, j)),
            pl.BlockSpec((1, 1, bn), lambda l, j: (l, 0, j)),
        ],
        out_specs=pl.BlockSpec((1, MOD_ROWS, bn), lambda l, j: (l, 0, j)),
        compiler_params=pltpu.CompilerParams(vmem_limit_bytes=VMEM_LIMIT),
        name="mod_vectors",
    )(cvec, w_mod, b_mod.reshape(DEPTH, 1, MOD_COLS))


def _cast_blocks(srcs, dsts):
    for src, dst in zip(srcs, dsts):
        dst[...] = src[...].astype(BF16)


def _cast_specs(weights, layer, steps):
    def block(i):
        return jnp.minimum(i, steps - 1)

    in_specs = [pl.BlockSpec((None, w.shape[1] // steps, w.shape[2]), lambda i: (layer, block(i), 0)) for w in weights]
    out_shape = [jax.ShapeDtypeStruct(w.shape[1:], BF16) for w in weights]
    out_specs = [pl.BlockSpec((w.shape[1] // steps, w.shape[2]), lambda i: (block(i), 0)) for w in weights]
    return in_specs, out_shape, out_specs


def _two_group_specs(cols, n_first):
    first = pl.BlockSpec((ROW_TILE, cols), lambda i: (jnp.minimum(i, n_first - 1), 0))
    second = pl.BlockSpec((ROW_TILE, cols), lambda i: (jnp.maximum(i - n_first, 0), 0))
    return first, second


def _latent_mod_row(mod_ref, step, n_first):
    row = 1 + (step - n_first) // (DEC_SEQ // ROW_TILE)
    return mod_ref[pl.ds(row, 1), :]


def _x_tile_copy(x_hbm, tile, ring, sem, slot):
    return pltpu.make_async_copy(x_hbm.at[pl.ds(pl.multiple_of(tile * ROW_TILE, ROW_TILE), ROW_TILE)], ring.at[slot],
                                 sem.at[slot])


def _proj_in_body(xp_hbm, xs_hbm, mod_ref, g_ref, w_ref, *rest, n_first, n_steps, ncast):
    cast_src = rest[:ncast]
    zap_ref, zcp_ref, zas_ref, zcs_ref = rest[ncast:ncast + 4]
    cast_dst = rest[ncast + 4:2 * ncast + 4]
    wb_s, x_ring, x_sem = rest[2 * ncast + 4:]
    step = _index32(pl.program_id(0))

    def request(k):
        slot = lax.rem(k, jnp.int32(X_SLOTS))

        @pl.when(k < n_first)
        def _():
            _x_tile_copy(xp_hbm, k, x_ring, x_sem, slot).start()

        @pl.when((k >= n_first) & (k < n_steps))
        def _():
            _x_tile_copy(xs_hbm, k - n_first, x_ring, x_sem, slot).start()

    @pl.when(step == 0)
    def _():
        for k in range(X_SLOTS - 1):
            request(_index32(k))
        wb_s[...] = w_ref[...].astype(BF16)

    request(step + (X_SLOTS - 1))
    slot = lax.rem(step, jnp.int32(X_SLOTS))

    def project(mod, za_ref, zc_ref):
        shift1 = mod[:, 0:D_MODEL]
        scale1 = mod[:, D_MODEL:2 * D_MODEL]
        h = (_rms(x_ring[slot], g_ref[0:1]) * (1.0 + scale1) + shift1).astype(BF16)
        z = jnp.dot(h, wb_s[...], preferred_element_type=F32)
        za_ref[...] = z[:, 0:ATT_COLS]
        zc_ref[...] = z[:, ATT_COLS:IN_COLS]

    @pl.when(step < n_first)
    def _():
        _x_tile_copy(xp_hbm, step, x_ring, x_sem, slot).wait()
        _cast_blocks(cast_src, cast_dst)
        project(mod_ref[0:1], zap_ref, zcp_ref)

    @pl.when(step >= n_first)
    def _():
        _x_tile_copy(xs_hbm, step - n_first, x_ring, x_sem, slot).wait()
        project(_latent_mod_row(mod_ref, step, n_first), zas_ref, zcs_ref)


def _proj_in_call(xp, xs, mods, gvec, w_in, layer, cast_weights):
    n_first = xp.shape[0] // ROW_TILE
    n_second = xs.shape[0] // ROW_TILE
    cast_in, cast_shape, cast_out = _cast_specs(cast_weights, layer, n_first)
    za_first, za_second = _two_group_specs(ATT_COLS, n_first)
    zc_first, zc_second = _two_group_specs(RWKV_COLS, n_first)
    return pl.pallas_call(
        functools.partial(_proj_in_body, n_first=n_first, n_steps=n_first + n_second, ncast=len(cast_weights)),
        out_shape=(jax.ShapeDtypeStruct((xp.shape[0], ATT_COLS), F32), jax.ShapeDtypeStruct((xp.shape[0], RWKV_COLS), F32),
                   jax.ShapeDtypeStruct((xs.shape[0], ATT_COLS), F32), jax.ShapeDtypeStruct((xs.shape[0], RWKV_COLS), F32),
                   *cast_shape),
        grid=(n_first + n_second,),
        in_specs=[
            pl.BlockSpec(memory_space=pl.ANY), pl.BlockSpec(memory_space=pl.ANY),
            _layer_spec((MOD_ROWS, MOD_COLS), layer),
            _layer_spec((8, D_MODEL), layer),
            _layer_spec((D_MODEL, IN_COLS), layer),
            *cast_in,
        ],
        out_specs=(za_first, zc_first, za_second, zc_second, *cast_out),
        scratch_shapes=[pltpu.VMEM((D_MODEL, IN_COLS), BF16), pltpu.VMEM((X_SLOTS, ROW_TILE, D_MODEL), F32),
                        pltpu.SemaphoreType.DMA((X_SLOTS,))],
        compiler_params=pltpu.CompilerParams(dimension_semantics=("arbitrary",), vmem_limit_bytes=VMEM_LIMIT),
        name="proj_in",
    )(xp, xs, mods, gvec, w_in, *cast_weights)


def _attn_prompt_body(*refs, layer, nseq):
    sink_ref, za_ref, gq_ref, gk_ref = refs[:4]
    o_ref, *cache_refs = refs[4 + (N_CACHE if layer else 0):]
    ones_q = _head_ones(B_DIM)
    ones_k = _head_ones(KV_DIM)
    score_fns, finish_fns = [], []
    for si in range(nseq):
        rows = slice(si * SEQ, (si + 1) * SEQ)
        aq = (za_ref[rows, A_Q] * QK_SCALE).astype(BF16)
        ak = za_ref[rows, A_K]
        bq = _head_rms(za_ref[rows, B_Q], gq_ref[...], ones_q)
        bk = _head_rms(za_ref[rows, B_K], gk_ref[...], ones_k)
        bqb = (bq * QK_SCALE).astype(BF16)
        akb = ak.astype(BF16)
        bkb = bk.astype(BF16)
        pieces_t = (ak.T, za_ref[rows, A_V].T, bk.T, za_ref[rows, B_V].T)
        avt = pieces_t[1].astype(BF16)
        bvt = pieces_t[3].astype(BF16)
        for j, piece_t in enumerate(pieces_t):
            if layer == 0:
                cache_refs[j][si, 0] = piece_t
                cache_refs[j][si, 1:] = jnp.zeros((DEPTH - 1, KV_DIM, SEQ), F32)
            else:
                cache_refs[j][si] = piece_t

        for h in range(A_HEADS):
            kv = h // (A_HEADS // A_KV_HEADS)
            hs = slice(kv * HEAD_DIM, (kv + 1) * HEAD_DIM)
            score_fns.append(lambda h=h, hs=hs, akb=akb, aq=aq: [_bdot_nt(akb[:, hs], aq[:, h * HEAD_DIM:(h + 1) * HEAD_DIM])])
            finish_fns.append(lambda sc, h=h, hs=hs, avt=avt: _softmax_pv_t(sc, [avt[hs]], sink_ref[layer, h]))
        for h in range(B_HEADS):
            kv = h // (B_HEADS // B_KV_HEADS)
            hs = slice(kv * HEAD_DIM, (kv + 1) * HEAD_DIM)
            score_fns.append(lambda h=h, hs=hs, bkb=bkb, bqb=bqb: [_bdot_nt(bkb[:, hs], bqb[:, h * HEAD_DIM:(h + 1) * HEAD_DIM])])
            finish_fns.append(lambda sc, hs=hs, bvt=bvt: _softmax_pv_t(sc, [bvt[hs]]))
    outs = _attend_heads(score_fns, finish_fns, ahead=len(score_fns))
    heads = A_HEADS + B_HEADS
    for si in range(nseq):
        o_ref[si * SEQ:(si + 1) * SEQ, :] = jnp.concatenate(outs[si * heads:(si + 1) * heads], axis=0).T.astype(BF16)


def _attn_prompt_call(za, sink, gq, gk, layer, prev_kv):
    nb = za.shape[0] // SEQ
    nseq = ATTN_BATCH
    in_specs = [
        pl.BlockSpec(memory_space=pltpu.SMEM),
        pl.BlockSpec((nseq * SEQ, ATT_COLS), lambda b: (b, 0)),
        _layer_spec((1, B_DIM), layer),
        _layer_spec((1, KV_DIM), layer),
    ]
    args = [sink, za, gq, gk]
    out_shape = [jax.ShapeDtypeStruct((nb * SEQ, ATT_OUT), BF16)]
    out_specs = [pl.BlockSpec((nseq * SEQ, ATT_OUT), lambda b: (b, 0))]
    out_shape += [jax.ShapeDtypeStruct((nb, DEPTH, KV_DIM, SEQ), F32)] * N_CACHE
    aliases = {}
    if layer == 0:
        out_specs += [pl.BlockSpec((nseq, DEPTH, KV_DIM, SEQ), lambda b: (b, 0, 0, 0))] * N_CACHE
    else:
        out_specs += [pl.BlockSpec((nseq, None, KV_DIM, SEQ), lambda b: (b, layer, 0, 0))] * N_CACHE
        aliases = {len(args) + j: 1 + j for j in range(N_CACHE)}
        in_specs += [pl.BlockSpec(memory_space=pl.ANY)] * N_CACHE
        args += list(prev_kv)
    return pl.pallas_call(
        functools.partial(_attn_prompt_body, layer=layer, nseq=nseq),
        out_shape=tuple(out_shape),
        grid=(nb // nseq,),
        in_specs=in_specs,
        out_specs=tuple(out_specs),
        input_output_aliases=aliases,
        compiler_params=pltpu.CompilerParams(vmem_limit_bytes=VMEM_LIMIT),
        name="attn_prompt",
    )(*args)


def _attn_sample_body(sink_ref, za_ref, cak_ref, cav_ref, cbk_ref, cbv_ref, cos_ref, sin_ref, gq_ref, gk_ref,
                      o_ref, qa_s, ka_s, va_s, qb_s, kb_s, vb_s, *, layer):
    t = DEC_SEQ
    cos = cos_ref[...]
    sin = sin_ref[...]
    qa_s[...] = (_rope(za_ref[:, A_Q], cos, sin) * QK_SCALE).astype(BF16)
    zpad = jnp.zeros((Q_BLK, KV_DIM), BF16)
    ka_s[0:Q_BLK] = zpad
    ka_s[Q_BLK + t:] = zpad
    va_s[0:Q_BLK] = zpad
    va_s[Q_BLK + t:] = zpad
    ka_s[Q_BLK:Q_BLK + t] = _rope(za_ref[:, A_K], cos, sin).astype(BF16)
    va_s[Q_BLK:Q_BLK + t] = za_ref[:, A_V].astype(BF16)
    bq = _head_rms(za_ref[:, B_Q], gq_ref[...], _head_ones(B_DIM))
    qb_s[...] = (_rope(bq, cos, sin) * QK_SCALE).astype(BF16)
    bk = _head_rms(za_ref[:, B_K], gk_ref[...], _head_ones(KV_DIM))
    kb_s[0:PAST_LEN] = cbk_ref[0].T.astype(BF16)
    kb_s[PAST_LEN:] = _rope(bk, cos, sin).astype(BF16)
    vb_s[0:PAST_LEN] = cbv_ref[0].T.astype(BF16)
    vb_s[PAST_LEN:] = za_ref[:, B_V].astype(BF16)

    kca = cak_ref[0].T.astype(BF16)
    vca = cav_ref[0].T.astype(BF16)

    ga = A_HEADS // A_KV_HEADS
    gb = B_HEADS // B_KV_HEADS
    qpos = lax.broadcasted_iota(jnp.int32, (ga * Q_BLK, 3 * Q_BLK), 0) & (Q_BLK - 1)
    kpos = lax.broadcasted_iota(jnp.int32, (ga * Q_BLK, 3 * Q_BLK), 1) - Q_BLK
    near = jnp.abs(kpos - qpos) <= WINDOW
    head_of_row = lax.shift_right_logical(lax.broadcasted_iota(jnp.int32, (ga * Q_BLK, 1), 0),
                                          jnp.int32(Q_BLK.bit_length() - 1))
    sinks = []
    for kv in range(A_KV_HEADS):
        col = jnp.full((ga * Q_BLK, 1), sink_ref[layer, kv * ga], F32)
        for g in range(1, ga):
            col = jnp.where(head_of_row == g, sink_ref[layer, kv * ga + g], col)
        sinks.append(col)

    def stack_heads(q, kv, group):
        return jnp.concatenate([q[:, (kv * group + g) * HEAD_DIM:(kv * group + g + 1) * HEAD_DIM] for g in range(group)], axis=0)

    def unstack_heads(outs, group, rows):
        return jnp.concatenate([o[g * rows:(g + 1) * rows] for o in outs for g in range(group)], axis=1)

    def a_block(n, carry):
        r0 = pl.multiple_of(_index32(n) * Q_BLK, Q_BLK)
        q = qa_s[pl.ds(r0, Q_BLK), :]
        kl = ka_s[pl.ds(r0, 3 * Q_BLK), :]
        vl = va_s[pl.ds(r0, 3 * Q_BLK), :]
        kabs = kpos + r0
        mask = near & (kabs >= 0) & (kabs < t)
        score_fns, finish_fns = [], []
        for kv in range(A_KV_HEADS):
            hs = slice(kv * HEAD_DIM, (kv + 1) * HEAD_DIM)

            def scores(kv=kv, hs=hs):
                qs = stack_heads(q, kv, ga)
                return [jnp.where(mask, _bdot_nt(qs, kl[:, hs]), NEG_INF), _bdot_nt(qs, kca[:, hs])]

            score_fns.append(scores)
            finish_fns.append(lambda sc, kv=kv, hs=hs: _softmax_pv(sc, [vl[:, hs], vca[:, hs]], sinks[kv]))
        outs = _attend_heads(score_fns, finish_fns, ahead=A_KV_HEADS)
        o_ref[pl.ds(r0, Q_BLK), 0:A_DIM] = unstack_heads(outs, ga, Q_BLK).astype(BF16)
        return carry

    lax.fori_loop(0, t // Q_BLK, a_block, 0)

    def b_block(n, carry):
        r0 = pl.multiple_of(_index32(n) * B_QROWS, B_QROWS)
        q = qb_s[pl.ds(r0, B_QROWS), :]
        score_fns, finish_fns = [], []
        for kv in range(B_KV_HEADS):
            hs = slice(kv * HEAD_DIM, (kv + 1) * HEAD_DIM)
            score_fns.append(lambda kv=kv, hs=hs: [_bdot_nt(stack_heads(q, kv, gb), kb_s[:, hs])])
            finish_fns.append(lambda sc, hs=hs: _softmax_pv(sc, [vb_s[:, hs]]))
        outs = _attend_heads(score_fns, finish_fns)
        o_ref[pl.ds(r0, B_QROWS), A_DIM:ATT_OUT] = unstack_heads(outs, gb, B_QROWS).astype(BF16)
        return carry

    lax.fori_loop(0, t // B_QROWS, b_block, 0)


def _attn_sample_call(za, sink, caches, cos, sin, gq, gk, layer):
    nb = za.shape[0] // DEC_SEQ
    t = DEC_SEQ
    cache_spec = pl.BlockSpec((1, None, KV_DIM, PAST_LEN), lambda b: (b, layer, 0, 0))
    return pl.pallas_call(
        functools.partial(_attn_sample_body, layer=layer),
        out_shape=jax.ShapeDtypeStruct((nb * t, ATT_OUT), BF16),
        grid=(nb,),
        in_specs=[
            pl.BlockSpec(memory_space=pltpu.SMEM),
            pl.BlockSpec((t, ATT_COLS), lambda b: (b, 0)),
            cache_spec, cache_spec, cache_spec, cache_spec,
            pl.BlockSpec((t, LANES), lambda b: (0, 0)),
            pl.BlockSpec((t, LANES), lambda b: (0, 0)),
            _layer_spec((1, B_DIM), layer),
            _layer_spec((1, KV_DIM), layer),
        ],
        out_specs=pl.BlockSpec((t, ATT_OUT), lambda b: (b, 0)),
        scratch_shapes=[
            pltpu.VMEM((t, A_DIM), BF16),
            pltpu.VMEM((t + 2 * Q_BLK, KV_DIM), BF16),
            pltpu.VMEM((t + 2 * Q_BLK, KV_DIM), BF16),
            pltpu.VMEM((t, B_DIM), BF16),
            pltpu.VMEM((t + PAST_LEN, KV_DIM), BF16),
            pltpu.VMEM((t + PAST_LEN, KV_DIM), BF16),
        ],
        compiler_params=pltpu.CompilerParams(vmem_limit_bytes=VMEM_LIMIT),
        name="attn_sample",
    )(sink, za, *caches, cos, sin, gq, gk)


def _rwkv_body(*refs, t, nbatch, has_init, want_state, has_prev, ncast):
    refs = list(refs)
    zc_ref, cv_ref, wup_ref, aup_ref, gup_ref = refs[:5]
    pos = 5
    st0_ref = stout_ref = None
    if has_init:
        st0_ref = refs[pos]
        pos += 1
    if has_prev:
        pos += 1
    cast_src = refs[pos:pos + ncast]
    pos += ncast
    oc_ref = refs[pos]
    pos += 1
    if want_state:
        stout_ref = refs[pos]
        pos += 1
    cast_dst = refs[pos:pos + ncast]
    pos += ncast
    aa_s, ld_s, b_s, kt_s, bon_s, y_s, st_s = refs[pos:]

    _cast_blocks(cast_src, cast_dst)

    nchunks = t // CHUNK
    ones = _head_ones(C_DIM)
    k_k = cv_ref[0:1]
    k_a = cv_ref[1:2]
    r_k = cv_ref[2:3]
    ln_w = cv_ref[3:4]
    ln_b = cv_ref[4:5]
    w0h = (0.5 * cv_ref[5:6], 0.5 * cv_ref[6:7])
    a0h = (0.5 * cv_ref[7:8], 0.5 * cv_ref[8:9])
    wup_h = ((0.5 * wup_ref[0]).astype(BF16), (0.5 * wup_ref[1]).astype(BF16))
    aup_h = ((0.5 * aup_ref[0]).astype(BF16), (0.5 * aup_ref[1]).astype(BF16))
    c1 = 1.0 - 0.5 * k_a
    c2 = 0.5 * k_a
    half_rate = 0.5 * DECAY_RATE

    for rb in range(nbatch * t // RWKV_ROWS):
        rows = slice(rb * RWKV_ROWS, (rb + 1) * RWKV_ROWS)
        zr = zc_ref[rows, C_R]
        zk = zc_ref[rows, C_K]
        zv = zc_ref[rows, C_V]
        tw = jnp.tanh(zc_ref[rows, C_W])
        xa = zc_ref[rows, C_A]
        kkr = zk * k_k
        kk_half = kkr * (0.5 * lax.rsqrt(_head_sum(kkr * kkr, ones) + 1e-12))
        aa_s[rows] = -2.0 * kk_half
        kt_sum = None
        for d in range(2):
            th_w = jnp.tanh(w0h[d] + _bdot(tw, wup_h[d]))
            ld_s[d, rows] = th_w * (-half_rate) - half_rate
            th_a = jnp.tanh(a0h[d] + _bdot(xa, aup_h[d]))
            kt = zk * (c1 + c2 * th_a)
            kt_s[d, rows] = kt
            b_s[d, rows] = kk_half * th_a + kk_half
            kt_sum = kt if kt_sum is None else kt_sum + kt
        bon_s[rows] = _head_sum(zr * kt_sum * r_k, ones) * zv
        y_s[rows] = jnp.zeros((RWKV_ROWS, C_DIM), F32)

    ir = lax.broadcasted_iota(jnp.int32, (CHUNK, LANES), 0)
    ic = lax.broadcasted_iota(jnp.int32, (CHUNK, LANES), 1) & (HEAD_DIM - 1)
    eye = jnp.where(ir == ic, 1.0, 0.0).astype(F32)
    incl = (ic <= ir, ic >= ir)
    strict = (ic < ir, ic > ir)
    sr = lax.broadcasted_iota(jnp.int32, (CHUNK, CHUNK), 0)
    sc = lax.broadcasted_iota(jnp.int32, (CHUNK, CHUNK), 1)
    tri = tuple(jnp.where(m, 1.0, 0.0).astype(BF16) for m in (sc <= sr, sc >= sr))
    br = lax.broadcasted_iota(jnp.int32, (LANES, LANES), 0)
    bc = lax.broadcasted_iota(jnp.int32, (LANES, LANES), 1)
    diag_blocks = (br < HEAD_DIM) == (bc < HEAD_DIM)
    left_half = lax.broadcasted_iota(jnp.int32, (CHUNK, LANES), 1) < HEAD_DIM

    def level_mask(rows, cols, j):
        same_big = lax.shift_right_logical(rows, jnp.int32(j + 1)) == lax.shift_right_logical(cols, jnp.int32(j + 1))
        same_small = lax.shift_right_logical(rows, jnp.int32(j)) == lax.shift_right_logical(cols, jnp.int32(j))
        return same_big & jnp.logical_not(same_small)

    first_level = level_mask(ir, ic, 0)
    level_blocks = [diag_blocks & level_mask(br & (HEAD_DIM - 1), bc & (HEAD_DIM - 1), j) for j in range(1, CHUNK.bit_length() - 1)]

    def blockdiag(x2):
        x2 = x2.astype(BF16)
        return jnp.where(diag_blocks, jnp.concatenate([x2, x2], axis=0), jnp.zeros((LANES, LANES), BF16))

    npairs = C_HEADS // 2
    for bi in range(nbatch):
        for d in range(2):
            for p in range(npairs):
                if has_init:
                    st_s[bi, d, p] = jnp.concatenate([st0_ref[bi, d, 2 * p], st0_ref[bi, d, 2 * p + 1]], axis=1)
                else:
                    st_s[bi, d, p] = jnp.zeros((HEAD_DIM, LANES), F32)

    def chunk(c, carry):
        c = _index32(c)
        items = []
        for bi, d in [(bi, d) for bi in range(nbatch) for d in range(2)]:
            r0 = bi * t + ((nchunks - 1 - c) if d == 1 else c) * CHUNK
            rows = pl.ds(pl.multiple_of(r0, CHUNK), CHUNK)
            ld = ld_s[d, rows]
            cinc = _split_dot_left(tri[d], ld)
            cexc = cinc - ld
            ctot = cinc[0:1] if d == 1 else cinc[CHUNK - 1:CHUNK]
            e_ninc = jnp.exp(-cinc)
            e_rem = jnp.exp(ctot - cinc)
            e_tot = jnp.exp(ctot)
            bb = b_s[d, rows]
            kt = kt_s[d, rows]
            at = (aa_s[rows] * jnp.exp(cexc)).astype(BF16)
            rt = (zc_ref[rows, C_R] * jnp.exp(cinc)).astype(BF16)
            bt = (bb * e_ninc).astype(BF16)
            ktt = (kt * e_ninc).astype(BF16)
            bh = (bb * e_rem).astype(BF16)
            kh = (kt * e_rem).astype(BF16)
            vm = zc_ref[rows, C_V].astype(BF16)
            for p in range(npairs):
                ps = slice(p * LANES, (p + 1) * LANES)
                items.append(dict(bi=bi, d=d, p=p, rows=rows, e_tot=e_tot[:, ps], at=at[:, ps], rt=rt[:, ps],
                                  bt=bt[:, ps], kt=ktt[:, ps], bh=bh[:, ps], kh=kh[:, ps], vm=vm[:, ps]))
        for it in items:
            it["ar"] = jnp.concatenate([it["at"], it["rt"]], axis=0)
            it["vmd"] = blockdiag(it["vm"])
        for it in items:
            sbk = _bdot_nt(it["ar"], jnp.concatenate([blockdiag(it["bt"]), blockdiag(it["kt"])], axis=0))
            it["sb"] = sbk[:, 0:LANES]
            it["sk"] = sbk[:, LANES:2 * LANES]
        for it in items:
            d = it["d"]
            it["l"] = jnp.where(strict[d], it["sb"][0:CHUNK], 0.0)
            it["mak"] = jnp.where(strict[d], it["sk"][0:CHUNK], 0.0).astype(BF16)
            it["nr"] = jnp.concatenate([jnp.where(incl[d], it["sb"][CHUNK:], 0.0).astype(BF16),
                                        jnp.where(incl[d], it["sk"][CHUNK:], 0.0).astype(BF16)], axis=1)
            it["tm"] = eye + jnp.where(first_level, it["l"], 0.0)
            lb = it["l"].astype(BF16)
            it["l2"] = jnp.concatenate([lb, lb], axis=0)
        for lvl in level_blocks:
            for it in items:
                it["t1"] = _bdot(it["tm"], jnp.where(lvl, it["l2"], jnp.zeros((LANES, LANES), BF16)))
            for it in items:
                it["tm"] = it["tm"] + _bdot(it["t1"], blockdiag(it["tm"]))
        for it in items:
            it["mv"] = _bdot(it["mak"], it["vmd"])
        for it in items:
            it["tau"] = _bdot(it["tm"], jnp.concatenate([blockdiag(it["at"]), blockdiag(it["mv"])], axis=1))
        for it in items:
            it["s0"] = st_s[it["bi"], it["d"], it["p"]]
            it["x"] = _bdot_nt(jnp.concatenate([it["tau"][:, 0:LANES].astype(BF16), it["rt"]], axis=0),
                               blockdiag(it["s0"]))
        for it in items:
            it["u"] = (it["x"][0:CHUNK] + it["tau"][:, LANES:2 * LANES]).astype(BF16)
        for it in items:
            it["y"] = it["x"][CHUNK:] + _bdot(it["nr"], jnp.concatenate([blockdiag(it["u"]), it["vmd"]], axis=0))
        for it in items:
            full = _bdot_tn(jnp.concatenate([it["u"], it["vm"]], axis=0),
                            jnp.concatenate([it["bh"], it["kh"]], axis=0))
            st_s[it["bi"], it["d"], it["p"]] = (it["s0"] * it["e_tot"]
                                                + jnp.where(left_half, full[0:HEAD_DIM], full[HEAD_DIM:2 * HEAD_DIM]))
        for i in range(0, len(items), npairs):
            rows = items[i]["rows"]
            y_s[rows] = y_s[rows] + jnp.concatenate([it["y"] for it in items[i:i + npairs]], axis=1)
        return carry

    lax.fori_loop(0, nchunks, chunk, 0)
    if want_state:
        for bi in range(nbatch):
            for d in range(2):
                for p in range(npairs):
                    s2 = st_s[bi, d, p]
                    for q in range(2):
                        if has_prev:
                            stout_ref[bi, d, 2 * p + q] = s2[:, q * HEAD_DIM:(q + 1) * HEAD_DIM]
                        else:
                            stout_ref[bi, 0, d, 2 * p + q] = s2[:, q * HEAD_DIM:(q + 1) * HEAD_DIM]
            if not has_prev:
                stout_ref[bi, 1:] = jnp.zeros((DEPTH - 1, 2, C_HEADS, HEAD_DIM, HEAD_DIM), F32)

    gup = gup_ref[...].astype(BF16)
    for rb in range(nbatch * t // RWKV_ROWS):
        rows = slice(rb * RWKV_ROWS, (rb + 1) * RWKV_ROWS)
        y = y_s[rows]
        mu = _head_sum(y, ones) * (1.0 / HEAD_DIM)
        yc = y - mu
        var = _head_sum(yc * yc, ones) * (1.0 / HEAD_DIM)
        yn = yc * lax.rsqrt(var + GN_EPS)
        g = _bdot(_sigmoid(zc_ref[rows, C_G]), gup)
        oc_ref[rows] = ((yn * ln_w + ln_b + bon_s[rows]) * g).astype(BF16)


def _rwkv_call(zc, rvec, wup, aup, gup, layer, t, init_state, want_state, prev_state, cast_weights=()):
    nb = zc.shape[0] // t
    nbatch = min(RWKV_BATCH, nb)
    rows = nbatch * t
    has_init = init_state is not None
    has_prev = prev_state is not None
    st_shape = (2, C_HEADS, HEAD_DIM, HEAD_DIM)
    in_specs = [
        pl.BlockSpec((rows, RWKV_COLS), lambda b: (b, 0), pipeline_mode=pl.Buffered(1 if nb == nbatch else 2)),
        _layer_spec((16, C_DIM), layer),
        _layer_spec((2, W_RANK, C_DIM), layer),
        _layer_spec((2, A_RANK, C_DIM), layer),
        _layer_spec((G_RANK, C_DIM), layer),
    ]
    args = [zc, rvec, wup, aup, gup]
    if has_init:
        in_specs.append(pl.BlockSpec((nbatch, None) + st_shape, lambda b: (b, layer, 0, 0, 0, 0)))
        args.append(init_state)
    aliases = {}
    if has_prev:
        aliases = {len(args): 1}
        in_specs.append(pl.BlockSpec(memory_space=pl.ANY))
        args.append(prev_state)
    steps = nb // nbatch
    cast_in, cast_shape, cast_out = _cast_specs(cast_weights, layer, steps)
    in_specs.extend(cast_in)
    args.extend(cast_weights)
    out_shape = [jax.ShapeDtypeStruct((nb * t, C_DIM), BF16)]
    out_specs = [pl.BlockSpec((rows, C_DIM), lambda b: (b, 0))]
    if want_state:
        out_shape.append(jax.ShapeDtypeStruct((nb, DEPTH) + st_shape, F32))
        if has_prev:
            out_specs.append(pl.BlockSpec((nbatch, None) + st_shape, lambda b: (b, layer, 0, 0, 0, 0)))
        else:
            out_specs.append(pl.BlockSpec((nbatch, DEPTH) + st_shape, lambda b: (b, 0, 0, 0, 0, 0)))
    out_shape.extend(cast_shape)
    out_specs.extend(cast_out)
    return pl.pallas_call(
        functools.partial(_rwkv_body, t=t, nbatch=nbatch, has_init=has_init, want_state=want_state,
                          has_prev=has_prev, ncast=len(cast_weights)),
        out_shape=tuple(out_shape),
        grid=(steps,),
        in_specs=in_specs,
        out_specs=tuple(out_specs),
        input_output_aliases=aliases,
        scratch_shapes=[
            pltpu.VMEM((rows, C_DIM), F32),
            pltpu.VMEM((2, rows, C_DIM), F32),
            pltpu.VMEM((2, rows, C_DIM), F32),
            pltpu.VMEM((2, rows, C_DIM), F32),
            pltpu.VMEM((rows, C_DIM), F32),
            pltpu.VMEM((rows, C_DIM), F32),
            pltpu.VMEM((nbatch, 2, C_HEADS // 2, HEAD_DIM, LANES), F32),
        ],
        compiler_params=pltpu.CompilerParams(vmem_limit_bytes=VMEM_LIMIT),
        name="rwkv_prompt" if want_state else "rwkv_sample",
    )(*args)


def _post_tile(oa_ref, oc_ref, x_ref, mod, g_ref, wout_ref, wgu_ref, wdn_ref, xo_ref):
    gate1 = mod[:, 2 * D_MODEL:3 * D_MODEL]
    shift2 = mod[:, 3 * D_MODEL:4 * D_MODEL]
    scale2 = mod[:, 4 * D_MODEL:5 * D_MODEL]
    gate2 = mod[:, 5 * D_MODEL:6 * D_MODEL]
    sub = ROW_TILE // ROW_SPLIT
    parts = [slice(i * sub, (i + 1) * sub) for i in range(ROW_SPLIT)]
    o = [jnp.dot(jnp.concatenate([oa_ref[r, :], oc_ref[r, :]], axis=1), wout_ref[...],
                 preferred_element_type=F32) for r in parts]
    x1 = [x_ref[r, :] + gate1 * _rms(oi, g_ref[1:2]) for r, oi in zip(parts, o)]
    h2 = [(_rms(xi, g_ref[2:3]) * (1.0 + scale2) + shift2).astype(BF16) for xi in x1]

    def gate_up(step):
        (lo, hi), i = step
        g = jnp.dot(h2[i], wgu_ref[:, lo:hi], preferred_element_type=F32)
        u = jnp.dot(h2[i], wgu_ref[:, FF_DIM + lo:FF_DIM + hi], preferred_element_type=F32)
        return g, u

    steps = [(chunk, i) for chunk in FF_CHUNKS for i in range(ROW_SPLIT)]
    acc = [None] * ROW_SPLIT
    pending = gate_up(steps[0])
    for k, ((lo, hi), i) in enumerate(steps):
        g, u = pending
        if k + 1 < len(steps):
            pending = gate_up(steps[k + 1])
        act = (g * _sigmoid(g) * u).astype(BF16)
        part = jnp.dot(act, wdn_ref[lo:hi, :], preferred_element_type=F32)
        acc[i] = part if acc[i] is None else acc[i] + part
    for r, xi, ai in zip(parts, x1, acc):
        xo_ref[r, :] = xi + gate2 * _rms(ai, g_ref[3:4])


def _post_body(oap_ref, oas_ref, ocp_ref, ocs_ref, xp_ref, xs_ref, mod_ref, g_ref, wout_ref, wgu_ref, wdn_ref,
               xop_ref, xos_ref, *, n_first):
    step = pl.program_id(0)

    @pl.when(step < n_first)
    def _():
        _post_tile(oap_ref, ocp_ref, xp_ref, mod_ref[0:1], g_ref, wout_ref, wgu_ref, wdn_ref, xop_ref)

    @pl.when(step >= n_first)
    def _():
        _post_tile(oas_ref, ocs_ref, xs_ref, _latent_mod_row(mod_ref, step, n_first), g_ref, wout_ref, wgu_ref,
                   wdn_ref, xos_ref)


def _post_call(oatt_p, oatt_s, oc_p, oc_s, xp, xs, mods, gvec, wout_b, wgu_b, wdn_b, layer):
    n_first = xp.shape[0] // ROW_TILE
    n_second = xs.shape[0] // ROW_TILE
    x_first, x_second = _two_group_specs(D_MODEL, n_first)
    return pl.pallas_call(
        functools.partial(_post_body, n_first=n_first),
        out_shape=(jax.ShapeDtypeStruct(xp.shape, F32), jax.ShapeDtypeStruct(xs.shape, F32)),
        grid=(n_first + n_second,),
        in_specs=[
            *_two_group_specs(ATT_OUT, n_first),
            *_two_group_specs(C_DIM, n_first),
            x_first, x_second,
            _layer_spec((MOD_ROWS, MOD_COLS), layer),
            _layer_spec((8, D_MODEL), layer),
            _whole_spec((MIX_DIM, D_MODEL)),
            _whole_spec((D_MODEL, 2 * FF_DIM)),
            _whole_spec((FF_DIM, D_MODEL)),
        ],
        out_specs=(x_first, x_second),
        compiler_params=pltpu.CompilerParams(dimension_semantics=("arbitrary",), vmem_limit_bytes=VMEM_LIMIT),
        name="post_ffn",
    )(oatt_p, oatt_s, oc_p, oc_s, xp, xs, mods, gvec, wout_b, wgu_b, wdn_b)


def _rope_tables(t):
    pos = jnp.arange(t)
    row = (pos // GRID_W).astype(F32)
    col = (pos % GRID_W).astype(F32)
    freqs = ROPE_THETA ** (-jnp.arange(ROPE_PAIRS_AXIS, dtype=F32) / ROPE_PAIRS_AXIS)
    ang = jnp.concatenate([row[:, None] * freqs, col[:, None] * freqs], axis=-1)
    cos = jnp.tile(jnp.cos(ang), (1, 4))
    sin = jnp.sin(ang)
    sin = jnp.tile(jnp.concatenate([-sin, sin], axis=-1), (1, 2))
    return cos, sin


def _to_time_minor(cache):
    b, l, t = cache.shape[:3]
    return jnp.transpose(cache, (0, 1, 3, 4, 2)).reshape(b, l, KV_DIM, t)


def _from_time_minor(slab):
    b, l, _, t = slab.shape
    return jnp.transpose(slab.reshape(b, l, 2, HEAD_DIM, t), (0, 1, 4, 2, 3))


def kernel(x_prompt, x_sample, cache_a_k, cache_a_v, cache_b_k, cache_b_v, state_c, c, c_ctx, w_mod, b_mod, norm_mix_pre, norm_mix_post, norm_ffn_pre, norm_ffn_post, w_in, w_out, a_sink, b_q_norm, b_k_norm, c_w0, c_w_up, c_a0, c_a_up, c_g_up, c_k_k, c_k_a, c_r_k, c_ln_w, c_ln_b, w_gu, w_down):
    cvec = jnp.zeros((MOD_ROWS, D_MODEL), F32).at[0].set(c_ctx).at[1:1 + DEC_BATCH].set(c)
    mods = _mod_call(cvec, w_mod, b_mod)
    cos, sin = _rope_tables(DEC_SEQ)

    gvec = jnp.zeros((DEPTH, 8, D_MODEL), F32)
    gvec = gvec.at[:, 0].set(norm_mix_pre).at[:, 1].set(norm_mix_post)
    gvec = gvec.at[:, 2].set(norm_ffn_pre).at[:, 3].set(norm_ffn_post)
    rvec = jnp.zeros((DEPTH, 16, C_DIM), F32)
    rvec = rvec.at[:, 0].set(c_k_k).at[:, 1].set(c_k_a).at[:, 2].set(c_r_k.reshape(DEPTH, C_DIM))
    rvec = rvec.at[:, 3].set(c_ln_w).at[:, 4].set(c_ln_b)
    rvec = rvec.at[:, 5:7].set(c_w0).at[:, 7:9].set(c_a0)
    gq = jnp.tile(b_q_norm, (1, B_HEADS))[:, None, :]
    gk = jnp.tile(b_k_norm, (1, B_KV_HEADS))[:, None, :]
    caches = tuple(_to_time_minor(t) for t in (cache_a_k, cache_a_v, cache_b_k, cache_b_v))

    xp = x_prompt.reshape(BATCH * SEQ, D_MODEL)
    xs = x_sample.reshape(DEC_BATCH * DEC_SEQ, D_MODEL)
    kv_prev = st_prev = None
    for l in range(DEPTH):
        za_p, zc_p, za_s, zc_s, wout_b, wdn_b = _proj_in_call(xp, xs, mods, gvec, w_in, l, (w_out, w_down))
        oatt_p, *kv_prev = _attn_prompt_call(za_p, a_sink, gq, gk, l, kv_prev)
        oc_p, st_prev, wgu_b = _rwkv_call(zc_p, rvec, c_w_up, c_a_up, c_g_up, l, SEQ, None, True, st_prev,
                                          cast_weights=(w_gu,))
        oatt_s = _attn_sample_call(za_s, a_sink, caches, cos, sin, gq, gk, l)
        (oc_s,) = _rwkv_call(zc_s, rvec, c_w_up, c_a_up, c_g_up, l, DEC_SEQ, state_c, False, None)
        xp, xs = _post_call(oatt_p, oatt_s, oc_p, oc_s, xp, xs, mods, gvec, wout_b, wgu_b, wdn_b, l)

    new_caches = tuple(_from_time_minor(slab) for slab in kv_prev)
    return (xp.reshape(BATCH, SEQ, D_MODEL), xs.reshape(DEC_BATCH, DEC_SEQ, D_MODEL), *new_caches, st_prev)
```

```python
import functools
import math

import jax
import jax.numpy as jnp
from jax import lax
from jax.experimental import pallas as pl
from jax.experimental.pallas import tpu as pltpu

D_MODEL = 1024
BATCH = 32
SEQ = 256
DEPTH = 2
DEC_BATCH = 2
DEC_SEQ = 1024
PAST_LEN = 256
GRID_W = 64
HEAD_DIM = 64
A_HEADS = 4
A_KV_HEADS = 2
B_HEADS = 6
B_KV_HEADS = 2
C_HEADS = 6
A_DIM = A_HEADS * HEAD_DIM
B_DIM = B_HEADS * HEAD_DIM
C_DIM = C_HEADS * HEAD_DIM
MIX_DIM = A_DIM + B_DIM + C_DIM
WINDOW = 128
Q_BLK = 128
W_RANK = 64
A_RANK = 64
G_RANK = 128
FF_DIM = -(-8 * D_MODEL // (3 * 256)) * 256
ROPE_THETA = 10000.0
ROPE_PAIRS_AXIS = HEAD_DIM // 4
NORM_EPS = 1e-6
GN_EPS = 64e-5
NEG_INF = -1e30

LANES = 128
KV_DIM = 2 * HEAD_DIM
assert KV_DIM == LANES


def _col_ranges(widths):
    edges = [0]
    for w in widths:
        edges.append(edges[-1] + w)
    return tuple(slice(lo, hi) for lo, hi in zip(edges[:-1], edges[1:])), edges[-1]


(A_Q, A_K, A_V, B_Q, B_K, B_V), ATT_COLS = _col_ranges((A_DIM, KV_DIM, KV_DIM, B_DIM, KV_DIM, KV_DIM))
(C_R, C_K, C_V, C_W, C_A, C_G), RWKV_COLS = _col_ranges((C_DIM, C_DIM, C_DIM, W_RANK, A_RANK, G_RANK))
IN_COLS = ATT_COLS + RWKV_COLS
ATT_OUT = A_DIM + B_DIM
MOD_COLS = 6 * D_MODEL
MOD_ROWS = 8
N_CACHE = 4
CHUNK = 64
assert CHUNK == HEAD_DIM
QK_SCALE = HEAD_DIM ** -0.5
DECAY_RATE = math.exp(-0.5)
ROW_TILE = 512
X_SLOTS = 3
ROW_SPLIT = 2
RWKV_ROWS = 512
RWKV_BATCH = 4
B_QROWS = 256
ATTN_BATCH = 4
MXU_TILE = 256
FF_CHUNKS = ((0, 4 * MXU_TILE), (4 * MXU_TILE, 8 * MXU_TILE), (8 * MXU_TILE, FF_DIM))
assert FF_DIM % MXU_TILE == 0
VMEM_LIMIT = 56 * 1024 * 1024

F32 = jnp.float32
BF16 = jnp.bfloat16

assert DEPTH == 2


def _bdot(a, b):
    return jnp.dot(a.astype(BF16), b.astype(BF16), preferred_element_type=F32)


def _bdot_nt(a, b):
    return lax.dot_general(a.astype(BF16), b.astype(BF16), (((1,), (1,)), ((), ())), preferred_element_type=F32)


def _bdot_tn(a, b):
    return lax.dot_general(a.astype(BF16), b.astype(BF16), (((0,), (0,)), ((), ())), preferred_element_type=F32)


def _split(x):
    hi = x.astype(BF16)
    lo = (x - hi.astype(F32)).astype(BF16)
    return hi, lo


def _split_dot_left(m, x):
    hi, lo = _split(x)
    return jnp.dot(m, hi, preferred_element_type=F32) + jnp.dot(m, lo, preferred_element_type=F32)


def _head_sum(x, ones):
    return jnp.dot(x.astype(BF16), ones, preferred_element_type=F32)


def _sigmoid(x):
    return 0.5 * jnp.tanh(0.5 * x) + 0.5


def _head_ones(n):
    r = lax.shift_right_logical(lax.broadcasted_iota(jnp.int32, (n, n), 0), jnp.int32(HEAD_DIM.bit_length() - 1))
    c = lax.shift_right_logical(lax.broadcasted_iota(jnp.int32, (n, n), 1), jnp.int32(HEAD_DIM.bit_length() - 1))
    return jnp.where(r == c, 1.0, 0.0).astype(BF16)


def _rms(x, g):
    ms = jnp.mean(x * x, axis=-1, keepdims=True)
    return x * lax.rsqrt(ms + NORM_EPS) * g


def _head_rms(x, g, ones):
    ms = _head_sum(x * x, ones) * (1.0 / HEAD_DIM)
    return x * lax.rsqrt(ms + NORM_EPS) * g


def _rope(x, cos, sin):
    t = x.shape[0]
    lane = lax.broadcasted_iota(jnp.int32, (t, LANES), 1)
    first = (lane & (HEAD_DIM // 2)) == 0
    outs = []
    for i in range(x.shape[1] // LANES):
        xs = x[:, i * LANES:(i + 1) * LANES]
        swapped = jnp.where(first, pltpu.roll(xs, LANES - HEAD_DIM // 2, axis=1), pltpu.roll(xs, HEAD_DIM // 2, axis=1))
        outs.append(xs * cos + swapped * sin)
    return outs[0] if len(outs) == 1 else jnp.concatenate(outs, axis=1)


def _softmax_pv(scores, values, sink=None):
    m = scores[0].max(axis=-1, keepdims=True)
    for s in scores[1:]:
        m = jnp.maximum(m, s.max(axis=-1, keepdims=True))
    if sink is not None:
        m = jnp.maximum(m, sink)
    denom = None
    acc = None
    for s, v in zip(scores, values):
        p = jnp.exp(s - m)
        d = p.sum(axis=-1, keepdims=True)
        o = jnp.dot(p.astype(BF16), v, preferred_element_type=F32)
        denom = d if denom is None else denom + d
        acc = o if acc is None else acc + o
    if sink is not None:
        denom = denom + jnp.exp(sink - m)
    return acc / denom


def _softmax_pv_t(scores_t, values_t, sink=None):
    m = scores_t[0].max(axis=0, keepdims=True)
    for s in scores_t[1:]:
        m = jnp.maximum(m, s.max(axis=0, keepdims=True))
    if sink is not None:
        m = jnp.maximum(m, sink)
    denom = None
    acc = None
    for s, vt in zip(scores_t, values_t):
        p = jnp.exp(s - m)
        d = p.sum(axis=0, keepdims=True)
        o = jnp.dot(vt, p.astype(BF16), preferred_element_type=F32)
        denom = d if denom is None else denom + d
        acc = o if acc is None else acc + o
    if sink is not None:
        denom = denom + jnp.exp(sink - m)
    return acc / denom


def _attend_heads(score_fns, finish_fns, ahead=1):
    n = len(score_fns)
    pending = [fn() for fn in score_fns[:ahead]]
    outs = []
    for i in range(n):
        if i + ahead < n:
            pending.append(score_fns[i + ahead]())
        outs.append(finish_fns[i](pending.pop(0)))
    return outs


def _index32(i):
    return jnp.asarray(i, jnp.int32)


def _layer_spec(shape, layer):
    zeros = (0,) * len(shape)
    return pl.BlockSpec((None,) + shape, lambda *_: (layer,) + zeros, pipeline_mode=pl.Buffered(1))


def _whole_spec(shape):
    zeros = (0,) * len(shape)
    return pl.BlockSpec(shape, lambda *_: zeros, pipeline_mode=pl.Buffered(1))


def _mod_body(c_ref, w_ref, b_ref, o_ref):
    cv = c_ref[...]
    s = cv * _sigmoid(cv)
    o_ref[0] = _bdot(s, w_ref[0]) + b_ref[0]


def _mod_call(cvec, w_mod, b_mod):
    bn = MOD_COLS // 4
    return pl.pallas_call(
        _mod_body,
        out_shape=jax.ShapeDtypeStruct((DEPTH, MOD_ROWS, MOD_COLS), F32),
        grid=(DEPTH, MOD_COLS // bn),
        in_specs=[
            pl.BlockSpec((MOD_ROWS, D_MODEL), lambda l, j: (0, 0)),
            pl.BlockSpec((1, D_MODEL, bn), lambda l, j: (l, 0, j)),
            pl.BlockSpec((1, 1, bn), lambda l, j: (l, 0, j)),
        ],
        out_specs=pl.BlockSpec((1, MOD_ROWS, bn), lambda l, j: (l, 0, j)),
        compiler_params=pltpu.CompilerParams(vmem_limit_bytes=VMEM_LIMIT),
        name="mod_vectors",
    )(cvec, w_mod, b_mod.reshape(DEPTH, 1, MOD_COLS))


def _cast_blocks(srcs, dsts):
    for src, dst in zip(srcs, dsts):
        dst[...] = src[...].astype(BF16)


def _cast_specs(weights, layer, steps):
    def block(i):
        return jnp.minimum(i, steps - 1)

    in_specs = [pl.BlockSpec((None, w.shape[1] // steps, w.shape[2]), lambda i: (layer, block(i), 0)) for w in weights]
    out_shape = [jax.ShapeDtypeStruct(w.shape[1:], BF16) for w in weights]
    out_specs = [pl.BlockSpec((w.shape[1] // steps, w.shape[2]), lambda i: (block(i), 0)) for w in weights]
    return in_specs, out_shape, out_specs


def _two_group_specs(cols, n_first):
    first = pl.BlockSpec((ROW_TILE, cols), lambda i: (jnp.minimum(i, n_first - 1), 0))
    second = pl.BlockSpec((ROW_TILE, cols), lambda i: (jnp.maximum(i - n_first, 0), 0))
    return first, second


def _latent_mod_row(mod_ref, step, n_first):
    row = 1 + (step - n_first) // (DEC_SEQ // ROW_TILE)
    return mod_ref[pl.ds(row, 1), :]


def _x_tile_copy(x_hbm, tile, ring, sem, slot):
    return pltpu.make_async_copy(x_hbm.at[pl.ds(pl.multiple_of(tile * ROW_TILE, ROW_TILE), ROW_TILE)], ring.at[slot],
                                 sem.at[slot])


def _proj_in_body(xp_hbm, xs_hbm, mod_ref, g_ref, w_ref, *rest, n_first, n_steps, ncast):
    cast_src = rest[:ncast]
    zap_ref, zcp_ref, zas_ref, zcs_ref = rest[ncast:ncast + 4]
    cast_dst = rest[ncast + 4:2 * ncast + 4]
    wb_s, x_ring, x_sem = rest[2 * ncast + 4:]
    step = _index32(pl.program_id(0))

    def request(k):
        slot = lax.rem(k, jnp.int32(X_SLOTS))

        @pl.when(k < n_first)
        def _():
            _x_tile_copy(xp_hbm, k, x_ring, x_sem, slot).start()

        @pl.when((k >= n_first) & (k < n_steps))
        def _():
            _x_tile_copy(xs_hbm, k - n_first, x_ring, x_sem, slot).start()

    @pl.when(step == 0)
    def _():
        for k in range(X_SLOTS - 1):
            request(_index32(k))
        wb_s[...] = w_ref[...].astype(BF16)

    request(step + (X_SLOTS - 1))
    slot = lax.rem(step, jnp.int32(X_SLOTS))

    def project(mod, za_ref, zc_ref):
        shift1 = mod[:, 0:D_MODEL]
        scale1 = mod[:, D_MODEL:2 * D_MODEL]
        h = (_rms(x_ring[slot], g_ref[0:1]) * (1.0 + scale1) + shift1).astype(BF16)
        z = jnp.dot(h, wb_s[...], preferred_element_type=F32)
        za_ref[...] = z[:, 0:ATT_COLS]
        zc_ref[...] = z[:, ATT_COLS:IN_COLS]

    @pl.when(step < n_first)
    def _():
        _x_tile_copy(xp_hbm, step, x_ring, x_sem, slot).wait()
        _cast_blocks(cast_src, cast_dst)
        project(mod_ref[0:1], zap_ref, zcp_ref)

    @pl.when(step >= n_first)
    def _():
        _x_tile_copy(xs_hbm, step - n_first, x_ring, x_sem, slot).wait()
        project(_latent_mod_row(mod_ref, step, n_first), zas_ref, zcs_ref)


def _proj_in_call(xp, xs, mods, gvec, w_in, layer, cast_weights):
    n_first = xp.shape[0] // ROW_TILE
    n_second = xs.shape[0] // ROW_TILE
    cast_in, cast_shape, cast_out = _cast_specs(cast_weights, layer, n_first)
    za_first, za_second = _two_group_specs(ATT_COLS, n_first)
    zc_first, zc_second = _two_group_specs(RWKV_COLS, n_first)
    return pl.pallas_call(
        functools.partial(_proj_in_body, n_first=n_first, n_steps=n_first + n_second, ncast=len(cast_weights)),
        out_shape=(jax.ShapeDtypeStruct((xp.shape[0], ATT_COLS), F32), jax.ShapeDtypeStruct((xp.shape[0], RWKV_COLS), F32),
                   jax.ShapeDtypeStruct((xs.shape[0], ATT_COLS), F32), jax.ShapeDtypeStruct((xs.shape[0], RWKV_COLS), F32),
                   *cast_shape),
        grid=(n_first + n_second,),
        in_specs=[
            pl.BlockSpec(memory_space=pl.ANY), pl.BlockSpec(memory_space=pl.ANY),
            _layer_spec((MOD_ROWS, MOD_COLS), layer),
            _layer_spec((8, D_MODEL), layer),
            _layer_spec((D_MODEL, IN_COLS), layer),
            *cast_in,
        ],
        out_specs=(za_first, zc_first, za_second, zc_second, *cast_out),
        scratch_shapes=[pltpu.VMEM((D_MODEL, IN_COLS), BF16), pltpu.VMEM((X_SLOTS, ROW_TILE, D_MODEL), F32),
                        pltpu.SemaphoreType.DMA((X_SLOTS,))],
        compiler_params=pltpu.CompilerParams(dimension_semantics=("arbitrary",), vmem_limit_bytes=VMEM_LIMIT),
        name="proj_in",
    )(xp, xs, mods, gvec, w_in, *cast_weights)


def _attn_prompt_body(*refs, layer, nseq):
    sink_ref, za_ref, gq_ref, gk_ref = refs[:4]
    o_ref, *cache_refs = refs[4 + (N_CACHE if layer else 0):]
    ones_q = _head_ones(B_DIM)
    ones_k = _head_ones(KV_DIM)
    score_fns, finish_fns = [], []
    for si in range(nseq):
        rows = slice(si * SEQ, (si + 1) * SEQ)
        aq = (za_ref[rows, A_Q] * QK_SCALE).astype(BF16)
        ak = za_ref[rows, A_K]
        bq = _head_rms(za_ref[rows, B_Q], gq_ref[...], ones_q)
        bk = _head_rms(za_ref[rows, B_K], gk_ref[...], ones_k)
        bqb = (bq * QK_SCALE).astype(BF16)
        akb = ak.astype(BF16)
        bkb = bk.astype(BF16)
        pieces_t = (ak.T, za_ref[rows, A_V].T, bk.T, za_ref[rows, B_V].T)
        avt = pieces_t[1].astype(BF16)
        bvt = pieces_t[3].astype(BF16)
        for j, piece_t in enumerate(pieces_t):
            if layer == 0:
                cache_refs[j][si, 0] = piece_t
                cache_refs[j][si, 1:] = jnp.zeros((DEPTH - 1, KV_DIM, SEQ), F32)
            else:
                cache_refs[j][si] = piece_t

        for h in range(A_HEADS):
            kv = h // (A_HEADS // A_KV_HEADS)
            hs = slice(kv * HEAD_DIM, (kv + 1) * HEAD_DIM)
            score_fns.append(lambda h=h, hs=hs, akb=akb, aq=aq: [_bdot_nt(akb[:, hs], aq[:, h * HEAD_DIM:(h + 1) * HEAD_DIM])])
            finish_fns.append(lambda sc, h=h, hs=hs, avt=avt: _softmax_pv_t(sc, [avt[hs]], sink_ref[layer, h]))
        for h in range(B_HEADS):
            kv = h // (B_HEADS // B_KV_HEADS)
            hs = slice(kv * HEAD_DIM, (kv + 1) * HEAD_DIM)
            score_fns.append(lambda h=h, hs=hs, bkb=bkb, bqb=bqb: [_bdot_nt(bkb[:, hs], bqb[:, h * HEAD_DIM:(h + 1) * HEAD_DIM])])
            finish_fns.append(lambda sc, hs=hs, bvt=bvt: _softmax_pv_t(sc, [bvt[hs]]))
    outs = _attend_heads(score_fns, finish_fns, ahead=len(score_fns))
    heads = A_HEADS + B_HEADS
    for si in range(nseq):
        o_ref[si * SEQ:(si + 1) * SEQ, :] = jnp.concatenate(outs[si * heads:(si + 1) * heads], axis=0).T.astype(BF16)


def _attn_prompt_call(za, sink, gq, gk, layer, prev_kv):
    nb = za.shape[0] // SEQ
    nseq = ATTN_BATCH
    in_specs = [
        pl.BlockSpec(memory_space=pltpu.SMEM),
        pl.BlockSpec((nseq * SEQ, ATT_COLS), lambda b: (b, 0)),
        _layer_spec((1, B_DIM), layer),
        _layer_spec((1, KV_DIM), layer),
    ]
    args = [sink, za, gq, gk]
    out_shape = [jax.ShapeDtypeStruct((nb * SEQ, ATT_OUT), BF16)]
    out_specs = [pl.BlockSpec((nseq * SEQ, ATT_OUT), lambda b: (b, 0))]
    out_shape += [jax.ShapeDtypeStruct((nb, DEPTH, KV_DIM, SEQ), F32)] * N_CACHE
    aliases = {}
    if layer == 0:
        out_specs += [pl.BlockSpec((nseq, DEPTH, KV_DIM, SEQ), lambda b: (b, 0, 0, 0))] * N_CACHE
    else:
        out_specs += [pl.BlockSpec((nseq, None, KV_DIM, SEQ), lambda b: (b, layer, 0, 0))] * N_CACHE
        aliases = {len(args) + j: 1 + j for j in range(N_CACHE)}
        in_specs += [pl.BlockSpec(memory_space=pl.ANY)] * N_CACHE
        args += list(prev_kv)
    return pl.pallas_call(
        functools.partial(_attn_prompt_body, layer=layer, nseq=nseq),
        out_shape=tuple(out_shape),
        grid=(nb // nseq,),
        in_specs=in_specs,
        out_specs=tuple(out_specs),
        input_output_aliases=aliases,
        compiler_params=pltpu.CompilerParams(vmem_limit_bytes=VMEM_LIMIT),
        name="attn_prompt",
    )(*args)


def _attn_sample_body(sink_ref, za_ref, cak_ref, cav_ref, cbk_ref, cbv_ref, cos_ref, sin_ref, gq_ref, gk_ref,
                      o_ref, qa_s, ka_s, va_s, qb_s, kb_s, vb_s, *, layer):
    t = DEC_SEQ
    cos = cos_ref[...]
    sin = sin_ref[...]
    qa_s[...] = (_rope(za_ref[:, A_Q], cos, sin) * QK_SCALE).astype(BF16)
    zpad = jnp.zeros((Q_BLK, KV_DIM), BF16)
    ka_s[0:Q_BLK] = zpad
    ka_s[Q_BLK + t:] = zpad
    va_s[0:Q_BLK] = zpad
    va_s[Q_BLK + t:] = zpad
    ka_s[Q_BLK:Q_BLK + t] = _rope(za_ref[:, A_K], cos, sin).astype(BF16)
    va_s[Q_BLK:Q_BLK + t] = za_ref[:, A_V].astype(BF16)
    bq = _head_rms(za_ref[:, B_Q], gq_ref[...], _head_ones(B_DIM))
    qb_s[...] = (_rope(bq, cos, sin) * QK_SCALE).astype(BF16)
    bk = _head_rms(za_ref[:, B_K], gk_ref[...], _head_ones(KV_DIM))
    kb_s[0:PAST_LEN] = cbk_ref[0].T.astype(BF16)
    kb_s[PAST_LEN:] = _rope(bk, cos, sin).astype(BF16)
    vb_s[0:PAST_LEN] = cbv_ref[0].T.astype(BF16)
    vb_s[PAST_LEN:] = za_ref[:, B_V].astype(BF16)

    kca = cak_ref[0].T.astype(BF16)
    vca = cav_ref[0].T.astype(BF16)

    ga = A_HEADS // A_KV_HEADS
    gb = B_HEADS // B_KV_HEADS
    qpos = lax.broadcasted_iota(jnp.int32, (ga * Q_BLK, 3 * Q_BLK), 0) & (Q_BLK - 1)
    kpos = lax.broadcasted_iota(jnp.int32, (ga * Q_BLK, 3 * Q_BLK), 1) - Q_BLK
    near = jnp.abs(kpos - qpos) <= WINDOW
    head_of_row = lax.shift_right_logical(lax.broadcasted_iota(jnp.int32, (ga * Q_BLK, 1), 0),
                                          jnp.int32(Q_BLK.bit_length() - 1))
    sinks = []
    for kv in range(A_KV_HEADS):
        col = jnp.full((ga * Q_BLK, 1), sink_ref[layer, kv * ga], F32)
        for g in range(1, ga):
            col = jnp.where(head_of_row == g, sink_ref[layer, kv * ga + g], col)
        sinks.append(col)

    def stack_heads(q, kv, group):
        return jnp.concatenate([q[:, (kv * group + g) * HEAD_DIM:(kv * group + g + 1) * HEAD_DIM] for g in range(group)], axis=0)

    def unstack_heads(outs, group, rows):
        return jnp.concatenate([o[g * rows:(g + 1) * rows] for o in outs for g in range(group)], axis=1)

    def a_block(n, carry):
        r0 = pl.multiple_of(_index32(n) * Q_BLK, Q_BLK)
        q = qa_s[pl.ds(r0, Q_BLK), :]
        kl = ka_s[pl.ds(r0, 3 * Q_BLK), :]
        vl = va_s[pl.ds(r0, 3 * Q_BLK), :]
        kabs = kpos + r0
        mask = near & (kabs >= 0) & (kabs < t)
        score_fns, finish_fns = [], []
        for kv in range(A_KV_HEADS):
            hs = slice(kv * HEAD_DIM, (kv + 1) * HEAD_DIM)

            def scores(kv=kv, hs=hs):
                qs = stack_heads(q, kv, ga)
                return [jnp.where(mask, _bdot_nt(qs, kl[:, hs]), NEG_INF), _bdot_nt(qs, kca[:, hs])]

            score_fns.append(scores)
            finish_fns.append(lambda sc, kv=kv, hs=hs: _softmax_pv(sc, [vl[:, hs], vca[:, hs]], sinks[kv]))
        outs = _attend_heads(score_fns, finish_fns, ahead=A_KV_HEADS)
        o_ref[pl.ds(r0, Q_BLK), 0:A_DIM] = unstack_heads(outs, ga, Q_BLK).astype(BF16)
        return carry

    lax.fori_loop(0, t // Q_BLK, a_block, 0)

    def b_block(n, carry):
        r0 = pl.multiple_of(_index32(n) * B_QROWS, B_QROWS)
        q = qb_s[pl.ds(r0, B_QROWS), :]
        score_fns, finish_fns = [], []
        for kv in range(B_KV_HEADS):
            hs = slice(kv * HEAD_DIM, (kv + 1) * HEAD_DIM)
            score_fns.append(lambda kv=kv, hs=hs: [_bdot_nt(stack_heads(q, kv, gb), kb_s[:, hs])])
            finish_fns.append(lambda sc, hs=hs: _softmax_pv(sc, [vb_s[:, hs]]))
        outs = _attend_heads(score_fns, finish_fns)
        o_ref[pl.ds(r0, B_QROWS), A_DIM:ATT_OUT] = unstack_heads(outs, gb, B_QROWS).astype(BF16)
        return carry

    lax.fori_loop(0, t // B_QROWS, b_block, 0)


def _attn_sample_call(za, sink, caches, cos, sin, gq, gk, layer):
    nb = za.shape[0] // DEC_SEQ
    t = DEC_SEQ
    cache_spec = pl.BlockSpec((1, None, KV_DIM, PAST_LEN), lambda b: (b, layer, 0, 0))
    return pl.pallas_call(
        functools.partial(_attn_sample_body, layer=layer),
        out_shape=jax.ShapeDtypeStruct((nb * t, ATT_OUT), BF16),
        grid=(nb,),
        in_specs=[
            pl.BlockSpec(memory_space=pltpu.SMEM),
            pl.BlockSpec((t, ATT_COLS), lambda b: (b, 0)),
            cache_spec, cache_spec, cache_spec, cache_spec,
            pl.BlockSpec((t, LANES), lambda b: (0, 0)),
            pl.BlockSpec((t, LANES), lambda b: (0, 0)),
            _layer_spec((1, B_DIM), layer),
            _layer_spec((1, KV_DIM), layer),
        ],
        out_specs=pl.BlockSpec((t, ATT_OUT), lambda b: (b, 0)),
        scratch_shapes=[
            pltpu.VMEM((t, A_DIM), BF16),
            pltpu.VMEM((t + 2 * Q_BLK, KV_DIM), BF16),
            pltpu.VMEM((t + 2 * Q_BLK, KV_DIM), BF16),
            pltpu.VMEM((t, B_DIM), BF16),
            pltpu.VMEM((t + PAST_LEN, KV_DIM), BF16),
            pltpu.VMEM((t + PAST_LEN, KV_DIM), BF16),
        ],
        compiler_params=pltpu.CompilerParams(vmem_limit_bytes=VMEM_LIMIT),
        name="attn_sample",
    )(sink, za, *caches, cos, sin, gq, gk)


def _rwkv_body(*refs, t, nbatch, has_init, want_state, has_prev, ncast, chunked_in):
    refs = list(refs)
    zc_ref, cv_ref, wup_ref, aup_ref, gup_ref = refs[:5]
    pos = 5
    st0_ref = stout_ref = None
    if has_init:
        st0_ref = refs[pos]
        pos += 1
    if has_prev:
        pos += 1
    cast_src = refs[pos:pos + ncast]
    pos += ncast
    oc_ref = refs[pos]
    pos += 1
    if want_state:
        stout_ref = refs[pos]
        pos += 1
    cast_dst = refs[pos:pos + ncast]
    pos += ncast
    aa_s, ld_s, b_s, kt_s, bon_s, y_s, st_s = refs[pos:pos + 7]
    nblocks = nbatch * t // RWKV_ROWS
    zc_copies = ()
    if chunked_in:
        zc_hbm, (zc_ref, zc_sem) = zc_ref, refs[pos + 7:]
        zc_copies = tuple(pltpu.make_async_copy(zc_hbm.at[pl.ds(rb * RWKV_ROWS, RWKV_ROWS)],
                                                zc_ref.at[pl.ds(rb * RWKV_ROWS, RWKV_ROWS)], zc_sem.at[rb])
                          for rb in range(nblocks))
        for cp in zc_copies:
            cp.start()

    _cast_blocks(cast_src, cast_dst)

    nchunks = t // CHUNK
    ones = _head_ones(C_DIM)
    k_k = cv_ref[0:1]
    k_a = cv_ref[1:2]
    r_k = cv_ref[2:3]
    ln_w = cv_ref[3:4]
    ln_b = cv_ref[4:5]
    w0h = (0.5 * cv_ref[5:6], 0.5 * cv_ref[6:7])
    a0h = (0.5 * cv_ref[7:8], 0.5 * cv_ref[8:9])
    wup_h = ((0.5 * wup_ref[0]).astype(BF16), (0.5 * wup_ref[1]).astype(BF16))
    aup_h = ((0.5 * aup_ref[0]).astype(BF16), (0.5 * aup_ref[1]).astype(BF16))
    c1 = 1.0 - 0.5 * k_a
    c2 = 0.5 * k_a
    half_rate = 0.5 * DECAY_RATE

    for rb in range(nblocks):
        rows = slice(rb * RWKV_ROWS, (rb + 1) * RWKV_ROWS)
        if chunked_in:
            zc_copies[rb].wait()
        zr = zc_ref[rows, C_R]
        zk = zc_ref[rows, C_K]
        zv = zc_ref[rows, C_V]
        tw = jnp.tanh(zc_ref[rows, C_W])
        xa = zc_ref[rows, C_A]
        kkr = zk * k_k
        kk_half = kkr * (0.5 * lax.rsqrt(_head_sum(kkr * kkr, ones) + 1e-12))
        aa_s[rows] = -2.0 * kk_half
        kt_sum = None
        for d in range(2):
            th_w = jnp.tanh(w0h[d] + _bdot(tw, wup_h[d]))
            ld_s[d, rows] = th_w * (-half_rate) - half_rate
            th_a = jnp.tanh(a0h[d] + _bdot(xa, aup_h[d]))
            kt = zk * (c1 + c2 * th_a)
            kt_s[d, rows] = kt
            b_s[d, rows] = kk_half * th_a + kk_half
            kt_sum = kt if kt_sum is None else kt_sum + kt
        bon_s[rows] = _head_sum(zr * kt_sum * r_k, ones) * zv
        y_s[rows] = jnp.zeros((RWKV_ROWS, C_DIM), F32)

    ir = lax.broadcasted_iota(jnp.int32, (CHUNK, LANES), 0)
    ic = lax.broadcasted_iota(jnp.int32, (CHUNK, LANES), 1) & (HEAD_DIM - 1)
    eye = jnp.where(ir == ic, 1.0, 0.0).astype(F32)
    incl = (ic <= ir, ic >= ir)
    strict = (ic < ir, ic > ir)
    sr = lax.broadcasted_iota(jnp.int32, (CHUNK, CHUNK), 0)
    sc = lax.broadcasted_iota(jnp.int32, (CHUNK, CHUNK), 1)
    tri = tuple(jnp.where(m, 1.0, 0.0).astype(BF16) for m in (sc <= sr, sc >= sr))
    br = lax.broadcasted_iota(jnp.int32, (LANES, LANES), 0)
    bc = lax.broadcasted_iota(jnp.int32, (LANES, LANES), 1)
    diag_blocks = (br < HEAD_DIM) == (bc < HEAD_DIM)
    left_half = lax.broadcasted_iota(jnp.int32, (CHUNK, LANES), 1) < HEAD_DIM

    def level_mask(rows, cols, j):
        same_big = lax.shift_right_logical(rows, jnp.int32(j + 1)) == lax.shift_right_logical(cols, jnp.int32(j + 1))
        same_small = lax.shift_right_logical(rows, jnp.int32(j)) == lax.shift_right_logical(cols, jnp.int32(j))
        return same_big & jnp.logical_not(same_small)

    first_level = level_mask(ir, ic, 0)
    level_blocks = [diag_blocks & level_mask(br & (HEAD_DIM - 1), bc & (HEAD_DIM - 1), j) for j in range(1, CHUNK.bit_length() - 1)]

    def blockdiag(x2):
        x2 = x2.astype(BF16)
        return jnp.where(diag_blocks, jnp.concatenate([x2, x2], axis=0), jnp.zeros((LANES, LANES), BF16))

    npairs = C_HEADS // 2
    for bi in range(nbatch):
        for d in range(2):
            for p in range(npairs):
                if has_init:
                    st_s[bi, d, p] = jnp.concatenate([st0_ref[bi, d, 2 * p], st0_ref[bi, d, 2 * p + 1]], axis=1)
                else:
                    st_s[bi, d, p] = jnp.zeros((HEAD_DIM, LANES), F32)

    def chunk(c, carry):
        c = _index32(c)
        items = []
        for bi, d in [(bi, d) for bi in range(nbatch) for d in range(2)]:
            r0 = bi * t + ((nchunks - 1 - c) if d == 1 else c) * CHUNK
            rows = pl.ds(pl.multiple_of(r0, CHUNK), CHUNK)
            ld = ld_s[d, rows]
            cinc = _split_dot_left(tri[d], ld)
            cexc = cinc - ld
            ctot = cinc[0:1] if d == 1 else cinc[CHUNK - 1:CHUNK]
            e_ninc = jnp.exp(-cinc)
            e_rem = jnp.exp(ctot - cinc)
            e_tot = jnp.exp(ctot)
            bb = b_s[d, rows]
            kt = kt_s[d, rows]
            at = (aa_s[rows] * jnp.exp(cexc)).astype(BF16)
            rt = (zc_ref[rows, C_R] * jnp.exp(cinc)).astype(BF16)
            bt = (bb * e_ninc).astype(BF16)
            ktt = (kt * e_ninc).astype(BF16)
            bh = (bb * e_rem).astype(BF16)
            kh = (kt * e_rem).astype(BF16)
            vm = zc_ref[rows, C_V].astype(BF16)
            for p in range(npairs):
                ps = slice(p * LANES, (p + 1) * LANES)
                items.append(dict(bi=bi, d=d, p=p, rows=rows, e_tot=e_tot[:, ps], at=at[:, ps], rt=rt[:, ps],
                                  bt=bt[:, ps], kt=ktt[:, ps], bh=bh[:, ps], kh=kh[:, ps], vm=vm[:, ps]))
        for it in items:
            it["ar"] = jnp.concatenate([it["at"], it["rt"]], axis=0)
            it["vmd"] = blockdiag(it["vm"])
        for it in items:
            sbk = _bdot_nt(it["ar"], jnp.concatenate([blockdiag(it["bt"]), blockdiag(it["kt"])], axis=0))
            it["sb"] = sbk[:, 0:LANES]
            it["sk"] = sbk[:, LANES:2 * LANES]
        for it in items:
            d = it["d"]
            it["l"] = jnp.where(strict[d], it["sb"][0:CHUNK], 0.0)
            it["mak"] = jnp.where(strict[d], it["sk"][0:CHUNK], 0.0).astype(BF16)
            it["nr"] = jnp.concatenate([jnp.where(incl[d], it["sb"][CHUNK:], 0.0).astype(BF16),
                                        jnp.where(incl[d], it["sk"][CHUNK:], 0.0).astype(BF16)], axis=1)
            it["tm"] = eye + jnp.where(first_level, it["l"], 0.0)
            lb = it["l"].astype(BF16)
            it["l2"] = jnp.concatenate([lb, lb], axis=0)
        for lvl in level_blocks:
            for it in items:
                it["t1"] = _bdot(it["tm"], jnp.where(lvl, it["l2"], jnp.zeros((LANES, LANES), BF16)))
            for it in items:
                it["tm"] = it["tm"] + _bdot(it["t1"], blockdiag(it["tm"]))
        for it in items:
            it["mv"] = _bdot(it["mak"], it["vmd"])
        for it in items:
            it["tau"] = _bdot(it["tm"], jnp.concatenate([blockdiag(it["at"]), blockdiag(it["mv"])], axis=1))
        for it in items:
            it["s0"] = st_s[it["bi"], it["d"], it["p"]]
            it["x"] = _bdot_nt(jnp.concatenate([it["tau"][:, 0:LANES].astype(BF16), it["rt"]], axis=0),
                               blockdiag(it["s0"]))
        for it in items:
            it["u"] = (it["x"][0:CHUNK] + it["tau"][:, LANES:2 * LANES]).astype(BF16)
        for it in items:
            it["y"] = it["x"][CHUNK:] + _bdot(it["nr"], jnp.concatenate([blockdiag(it["u"]), it["vmd"]], axis=0))
        for it in items:
            full = _bdot_tn(jnp.concatenate([it["u"], it["vm"]], axis=0),
                            jnp.concatenate([it["bh"], it["kh"]], axis=0))
            st_s[it["bi"], it["d"], it["p"]] = (it["s0"] * it["e_tot"]
                                                + jnp.where(left_half, full[0:HEAD_DIM], full[HEAD_DIM:2 * HEAD_DIM]))
        for i in range(0, len(items), npairs):
            rows = items[i]["rows"]
            y_s[rows] = y_s[rows] + jnp.concatenate([it["y"] for it in items[i:i + npairs]], axis=1)
        return carry

    lax.fori_loop(0, nchunks, chunk, 0)
    if want_state:
        for bi in range(nbatch):
            for d in range(2):
                for p in range(npairs):
                    s2 = st_s[bi, d, p]
                    for q in range(2):
                        if has_prev:
                            stout_ref[bi, d, 2 * p + q] = s2[:, q * HEAD_DIM:(q + 1) * HEAD_DIM]
                        else:
                            stout_ref[bi, 0, d, 2 * p + q] = s2[:, q * HEAD_DIM:(q + 1) * HEAD_DIM]
            if not has_prev:
                stout_ref[bi, 1:] = jnp.zeros((DEPTH - 1, 2, C_HEADS, HEAD_DIM, HEAD_DIM), F32)

    gup = gup_ref[...].astype(BF16)
    for rb in range(nbatch * t // RWKV_ROWS):
        rows = slice(rb * RWKV_ROWS, (rb + 1) * RWKV_ROWS)
        y = y_s[rows]
        mu = _head_sum(y, ones) * (1.0 / HEAD_DIM)
        yc = y - mu
        var = _head_sum(yc * yc, ones) * (1.0 / HEAD_DIM)
        yn = yc * lax.rsqrt(var + GN_EPS)
        g = _bdot(_sigmoid(zc_ref[rows, C_G]), gup)
        oc_ref[rows] = ((yn * ln_w + ln_b + bon_s[rows]) * g).astype(BF16)


def _rwkv_call(zc, rvec, wup, aup, gup, layer, t, init_state, want_state, prev_state, cast_weights=()):
    nb = zc.shape[0] // t
    nbatch = min(RWKV_BATCH, nb)
    rows = nbatch * t
    has_init = init_state is not None
    has_prev = prev_state is not None
    chunked_in = nb == nbatch
    st_shape = (2, C_HEADS, HEAD_DIM, HEAD_DIM)
    in_specs = [
        pl.BlockSpec(memory_space=pl.ANY) if chunked_in else pl.BlockSpec((rows, RWKV_COLS), lambda b: (b, 0)),
        _layer_spec((16, C_DIM), layer),
        _layer_spec((2, W_RANK, C_DIM), layer),
        _layer_spec((2, A_RANK, C_DIM), layer),
        _layer_spec((G_RANK, C_DIM), layer),
    ]
    args = [zc, rvec, wup, aup, gup]
    if has_init:
        in_specs.append(pl.BlockSpec((nbatch, None) + st_shape, lambda b: (b, layer, 0, 0, 0, 0)))
        args.append(init_state)
    aliases = {}
    if has_prev:
        aliases = {len(args): 1}
        in_specs.append(pl.BlockSpec(memory_space=pl.ANY))
        args.append(prev_state)
    steps = nb // nbatch
    cast_in, cast_shape, cast_out = _cast_specs(cast_weights, layer, steps)
    in_specs.extend(cast_in)
    args.extend(cast_weights)
    out_shape = [jax.ShapeDtypeStruct((nb * t, C_DIM), BF16)]
    out_specs = [pl.BlockSpec((rows, C_DIM), lambda b: (b, 0))]
    if want_state:
        out_shape.append(jax.ShapeDtypeStruct((nb, DEPTH) + st_shape, F32))
        if has_prev:
            out_specs.append(pl.BlockSpec((nbatch, None) + st_shape, lambda b: (b, layer, 0, 0, 0, 0)))
        else:
            out_specs.append(pl.BlockSpec((nbatch, DEPTH) + st_shape, lambda b: (b, 0, 0, 0, 0, 0)))
    out_shape.extend(cast_shape)
    out_specs.extend(cast_out)
    return pl.pallas_call(
        functools.partial(_rwkv_body, t=t, nbatch=nbatch, has_init=has_init, want_state=want_state,
                          has_prev=has_prev, ncast=len(cast_weights), chunked_in=chunked_in),
        out_shape=tuple(out_shape),
        grid=(steps,),
        in_specs=in_specs,
        out_specs=tuple(out_specs),
        input_output_aliases=aliases,
        scratch_shapes=[
            pltpu.VMEM((rows, C_DIM), F32),
            pltpu.VMEM((2, rows, C_DIM), F32),
            pltpu.VMEM((2, rows, C_DIM), F32),
            pltpu.VMEM((2, rows, C_DIM), F32),
            pltpu.VMEM((rows, C_DIM), F32),
            pltpu.VMEM((rows, C_DIM), F32),
            pltpu.VMEM((nbatch, 2, C_HEADS // 2, HEAD_DIM, LANES), F32),
        ] + ([pltpu.VMEM((rows, RWKV_COLS), F32), pltpu.SemaphoreType.DMA((rows // RWKV_ROWS,))] if chunked_in else []),
        compiler_params=pltpu.CompilerParams(vmem_limit_bytes=VMEM_LIMIT),
        name="rwkv_prompt" if want_state else "rwkv_sample",
    )(*args)


def _post_tile(oa_ref, oc_ref, x_ref, mod, g_ref, wout_ref, wgu_ref, wdn_ref, xo_ref):
    gate1 = mod[:, 2 * D_MODEL:3 * D_MODEL]
    shift2 = mod[:, 3 * D_MODEL:4 * D_MODEL]
    scale2 = mod[:, 4 * D_MODEL:5 * D_MODEL]
    gate2 = mod[:, 5 * D_MODEL:6 * D_MODEL]
    sub = ROW_TILE // ROW_SPLIT
    parts = [slice(i * sub, (i + 1) * sub) for i in range(ROW_SPLIT)]
    o = [jnp.dot(jnp.concatenate([oa_ref[r, :], oc_ref[r, :]], axis=1), wout_ref[...],
                 preferred_element_type=F32) for r in parts]
    x1 = [x_ref[r, :] + gate1 * _rms(oi, g_ref[1:2]) for r, oi in zip(parts, o)]
    h2 = [(_rms(xi, g_ref[2:3]) * (1.0 + scale2) + shift2).astype(BF16) for xi in x1]

    def gate_up(step):
        (lo, hi), i = step
        g = jnp.dot(h2[i], wgu_ref[:, lo:hi], preferred_element_type=F32)
        u = jnp.dot(h2[i], wgu_ref[:, FF_DIM + lo:FF_DIM + hi], preferred_element_type=F32)
        return g, u

    steps = [(chunk, i) for chunk in FF_CHUNKS for i in range(ROW_SPLIT)]
    acc = [None] * ROW_SPLIT
    pending = gate_up(steps[0])
    for k, ((lo, hi), i) in enumerate(steps):
        g, u = pending
        if k + 1 < len(steps):
            pending = gate_up(steps[k + 1])
        act = (g * _sigmoid(g) * u).astype(BF16)
        part = jnp.dot(act, wdn_ref[lo:hi, :], preferred_element_type=F32)
        acc[i] = part if acc[i] is None else acc[i] + part
    for r, xi, ai in zip(parts, x1, acc):
        xo_ref[r, :] = xi + gate2 * _rms(ai, g_ref[3:4])


def _post_body(oap_ref, oas_ref, ocp_ref, ocs_ref, xp_ref, xs_ref, mod_ref, g_ref, wout_ref, wgu_ref, wdn_ref,
               xop_ref, xos_ref, *, n_first):
    step = pl.program_id(0)

    @pl.when(step < n_first)
    def _():
        _post_tile(oap_ref, ocp_ref, xp_ref, mod_ref[0:1], g_ref, wout_ref, wgu_ref, wdn_ref, xop_ref)

    @pl.when(step >= n_first)
    def _():
        _post_tile(oas_ref, ocs_ref, xs_ref, _latent_mod_row(mod_ref, step, n_first), g_ref, wout_ref, wgu_ref,
                   wdn_ref, xos_ref)


def _post_call(oatt_p, oatt_s, oc_p, oc_s, xp, xs, mods, gvec, wout_b, wgu_b, wdn_b, layer):
    n_first = xp.shape[0] // ROW_TILE
    n_second = xs.shape[0] // ROW_TILE
    x_first, x_second = _two_group_specs(D_MODEL, n_first)
    return pl.pallas_call(
        functools.partial(_post_body, n_first=n_first),
        out_shape=(jax.ShapeDtypeStruct(xp.shape, F32), jax.ShapeDtypeStruct(xs.shape, F32)),
        grid=(n_first + n_second,),
        in_specs=[
            *_two_group_specs(ATT_OUT, n_first),
            *_two_group_specs(C_DIM, n_first),
            x_first, x_second,
            _layer_spec((MOD_ROWS, MOD_COLS), layer),
            _layer_spec((8, D_MODEL), layer),
            _whole_spec((MIX_DIM, D_MODEL)),
            _whole_spec((D_MODEL, 2 * FF_DIM)),
            _whole_spec((FF_DIM, D_MODEL)),
        ],
        out_specs=(x_first, x_second),
        compiler_params=pltpu.CompilerParams(dimension_semantics=("arbitrary",), vmem_limit_bytes=VMEM_LIMIT),
        name="post_ffn",
    )(oatt_p, oatt_s, oc_p, oc_s, xp, xs, mods, gvec, wout_b, wgu_b, wdn_b)


def _rope_tables(t):
    pos = jnp.arange(t)
    row = (pos // GRID_W).astype(F32)
    col = (pos % GRID_W).astype(F32)
    freqs = ROPE_THETA ** (-jnp.arange(ROPE_PAIRS_AXIS, dtype=F32) / ROPE_PAIRS_AXIS)
    ang = jnp.concatenate([row[:, None] * freqs, col[:, None] * freqs], axis=-1)
    cos = jnp.tile(jnp.cos(ang), (1, 4))
    sin = jnp.sin(ang)
    sin = jnp.tile(jnp.concatenate([-sin, sin], axis=-1), (1, 2))
    return cos, sin


def _to_time_minor(cache):
    b, l, t = cache.shape[:3]
    return jnp.transpose(cache, (0, 1, 3, 4, 2)).reshape(b, l, KV_DIM, t)


def _from_time_minor(slab):
    b, l, _, t = slab.shape
    return jnp.transpose(slab.reshape(b, l, 2, HEAD_DIM, t), (0, 1, 4, 2, 3))


def kernel(x_prompt, x_sample, cache_a_k, cache_a_v, cache_b_k, cache_b_v, state_c, c, c_ctx, w_mod, b_mod, norm_mix_pre, norm_mix_post, norm_ffn_pre, norm_ffn_post, w_in, w_out, a_sink, b_q_norm, b_k_norm, c_w0, c_w_up, c_a0, c_a_up, c_g_up, c_k_k, c_k_a, c_r_k, c_ln_w, c_ln_b, w_gu, w_down):
    cvec = jnp.zeros((MOD_ROWS, D_MODEL), F32).at[0].set(c_ctx).at[1:1 + DEC_BATCH].set(c)
    mods = _mod_call(cvec, w_mod, b_mod)
    cos, sin = _rope_tables(DEC_SEQ)

    gvec = jnp.zeros((DEPTH, 8, D_MODEL), F32)
    gvec = gvec.at[:, 0].set(norm_mix_pre).at[:, 1].set(norm_mix_post)
    gvec = gvec.at[:, 2].set(norm_ffn_pre).at[:, 3].set(norm_ffn_post)
    rvec = jnp.zeros((DEPTH, 16, C_DIM), F32)
    rvec = rvec.at[:, 0].set(c_k_k).at[:, 1].set(c_k_a).at[:, 2].set(c_r_k.reshape(DEPTH, C_DIM))
    rvec = rvec.at[:, 3].set(c_ln_w).at[:, 4].set(c_ln_b)
    rvec = rvec.at[:, 5:7].set(c_w0).at[:, 7:9].set(c_a0)
    gq = jnp.tile(b_q_norm, (1, B_HEADS))[:, None, :]
    gk = jnp.tile(b_k_norm, (1, B_KV_HEADS))[:, None, :]
    caches = tuple(_to_time_minor(t) for t in (cache_a_k, cache_a_v, cache_b_k, cache_b_v))

    xp = x_prompt.reshape(BATCH * SEQ, D_MODEL)
    xs = x_sample.reshape(DEC_BATCH * DEC_SEQ, D_MODEL)
    kv_prev = st_prev = None
    for l in range(DEPTH):
        za_p, zc_p, za_s, zc_s, wout_b, wdn_b = _proj_in_call(xp, xs, mods, gvec, w_in, l, (w_out, w_down))
        oatt_p, *kv_prev = _attn_prompt_call(za_p, a_sink, gq, gk, l, kv_prev)
        oc_p, st_prev, wgu_b = _rwkv_call(zc_p, rvec, c_w_up, c_a_up, c_g_up, l, SEQ, None, True, st_prev,
                                          cast_weights=(w_gu,))
        oatt_s = _attn_sample_call(za_s, a_sink, caches, cos, sin, gq, gk, l)
        (oc_s,) = _rwkv_call(zc_s, rvec, c_w_up, c_a_up, c_g_up, l, DEC_SEQ, state_c, False, None)
        xp, xs = _post_call(oatt_p, oatt_s, oc_p, oc_s, xp, xs, mods, gvec, wout_b, wgu_b, wdn_b, l)

    new_caches = tuple(_from_time_minor(slab) for slab in kv_prev)
    return (xp.reshape(BATCH, SEQ, D_MODEL), xs.reshape(DEC_BATCH, DEC_SEQ, D_MODEL), *new_caches, st_prev)
```
